```python
import math
import jax, jax.numpy as jnp
from jax import lax
import numpy as np

D_MODEL = 1024
BATCH = 8
SEQ = 4096
DEPTH = 4

GRID_W = 64
CTX_LEN = 256

RWKV_HEADS = 8
RWKV_HEAD_DIM = 64
RWKV_WIDTH = RWKV_HEADS * RWKV_HEAD_DIM
DECAY_LORA = 64
AAA_LORA = 64
GN_EPS = 64e-5
NA_HEADS = 8
NA_HEAD_DIM = 64
NA_WIDTH = NA_HEADS * NA_HEAD_DIM
NA_WIN_R = 8
NA_WIN_C = 16
DIFF_HEADS = 4
DIFF_QK_DIM = 64
DIFF_V_DIM = 2 * DIFF_QK_DIM
DIFF_WIDTH = DIFF_HEADS * DIFF_V_DIM
Q_BLOCK = 128
ROPE_THETA = 10000.0
SUBLN_EPS = 1e-5
N_BRANCH = 3
BRANCH_WIDTH = 512
RMS_EPS = 1e-6
NEG_INF = -1e30

RWKV_SHIFT_WIDTH = 3 * RWKV_WIDTH + 2 * DECAY_LORA + 2 * AAA_LORA
DIFF_QKV_WIDTH = 2 * (DIFF_HEADS * 2 * DIFF_QK_DIM) + DIFF_WIDTH
IN_SIZES = (RWKV_SHIFT_WIDTH, RWKV_WIDTH,
            3 * NA_WIDTH, NA_WIDTH,
            DIFF_QKV_WIDTH, DIFF_WIDTH,
            N_BRANCH * D_MODEL)
D_IN = sum(IN_SIZES)

kernel_name = "hybrid_rwkv7_natten_diffattn_parallel_block"


def _split(z, sizes):
    return jnp.split(z, [int(i) for i in np.cumsum(sizes)[:-1]], axis=-1)


def _rmsnorm(x, g, eps=RMS_EPS):
    xf = x.astype(jnp.float32)
    y = xf * lax.rsqrt(jnp.mean(xf * xf, axis=-1, keepdims=True) + eps)
    return (y * g.astype(jnp.float32)).astype(x.dtype)


def _token_shift(u, mu_prev, mu_next):
    zeros = jnp.zeros_like(u[:, :1])
    u_prev = jnp.concatenate([zeros, u[:, :-1]], axis=1)
    u_next = jnp.concatenate([u[:, 1:], zeros], axis=1)
    return u + mu_prev * (u_prev - u) + mu_next * (u_next - u)


def _rwkv_scan(r, decay, k, v, kk, a, s0, reverse, with_out):
    seq = lambda t: jnp.moveaxis(t, 1, 0)
    xs = (seq(decay), seq(k), seq(v), seq(-kk), seq(kk * a), seq(r) if with_out else None)

    def step(s, inp):
        w_t, k_t, v_t, a_t, b_t, r_t = inp
        sa = jnp.einsum('bhij,bhj->bhi', s, a_t)
        s = s * w_t[:, :, None, :] + sa[..., None] * b_t[:, :, None, :] + v_t[..., None] * k_t[:, :, None, :]
        y = jnp.einsum('bhij,bhj->bhi', s, r_t) if with_out else None
        return s, y

    s_fin, ys = lax.scan(step, s0, xs, reverse=reverse)
    return s_fin, (jnp.moveaxis(ys, 0, 1) if with_out else None)


def _rwkv_branch(u_x, u_c, k_k, k_a, r_k, w0, w_up, a0, a_up, ln_g, ln_b, with_ctx_out):
    f32 = jnp.float32

    def prep(u):
        B, T, _ = u.shape
        r, k, v, wf, wb, af, ab = _split(u.astype(f32), (RWKV_WIDTH, RWKV_WIDTH, RWKV_WIDTH,
                                                         DECAY_LORA, DECAY_LORA, AAA_LORA, AAA_LORA))
        hd = lambda t: t.reshape(B, T, RWKV_HEADS, RWKV_HEAD_DIM)
        kk = hd(k * k_k)
        kk = kk / jnp.maximum(jnp.sqrt(jnp.sum(kk * kk, axis=-1, keepdims=True)), 1e-12)
        dirs = []
        for d, (wd, ad) in enumerate(((wf, af), (wb, ab))):
            w_log = -jax.nn.softplus(-(w0[d] + jnp.tanh(wd) @ w_up[d])) - 0.5
            a = jax.nn.sigmoid(a0[d] + ad @ a_up[d])
            k_dir = k * (1.0 + (a - 1.0) * k_a)
            dirs.append((hd(jnp.exp(-jnp.exp(w_log))), hd(k_dir), hd(a)))
        return hd(r), hd(v), kk, dirs

    def readout(y, r, v, k_dirs):
        B, T = y.shape[:2]
        mu = jnp.mean(y, axis=-1, keepdims=True)
        var = jnp.mean(jnp.square(y - mu), axis=-1, keepdims=True)
        yn = ((y - mu) * lax.rsqrt(var + GN_EPS)).reshape(B, T, RWKV_WIDTH) * ln_g + ln_b
        bonus = sum(jnp.sum(r * kd * r_k, axis=-1, keepdims=True) * v for kd in k_dirs)
        return yn + bonus.reshape(B, T, RWKV_WIDTH)

    r_c, v_c, kk_c, dirs_c = prep(u_c)
    r_x, v_x, kk_x, dirs_x = prep(u_x)
    s0 = jnp.zeros((u_x.shape[0], RWKV_HEADS, RWKV_HEAD_DIM, RWKV_HEAD_DIM), f32)
    ys_x, ys_c = [], []
    for d, reverse in enumerate((False, True)):
        dec_c, k_c, a_c = dirs_c[d]
        s_ctx, y_cd = _rwkv_scan(r_c, dec_c, k_c, v_c, kk_c, a_c, s0, reverse, with_ctx_out)
        dec_x, k_x, a_x = dirs_x[d]
        _, y_xd = _rwkv_scan(r_x, dec_x, k_x, v_x, kk_x, a_x, s_ctx, reverse, True)
        ys_x.append(y_xd)
        ys_c.append(y_cd)
    o_x = readout(ys_x[0] + ys_x[1], r_x, v_x, [dirs_x[0][1], dirs_x[1][1]])
    o_c = readout(ys_c[0] + ys_c[1], r_c, v_c, [dirs_c[0][1], dirs_c[1][1]]) if with_ctx_out else None
    return o_x, o_c


def _na_branch(qkv_x, qkv_c, rpb, with_ctx_out):
    f32 = jnp.float32
    B, S, _ = qkv_x.shape
    rows = S // GRID_W
    wr = min(NA_WIN_R, rows)
    scale = NA_HEAD_DIM ** -0.5

    def heads(t):
        return t.reshape(t.shape[0], t.shape[1], NA_HEADS, NA_HEAD_DIM).transpose(0, 2, 1, 3)

    q, k, v = (heads(t) for t in jnp.split(qkv_x, 3, axis=-1))
    qc, kc, vc = (heads(t) for t in jnp.split(qkv_c, 3, axis=-1))
    grid = lambda t: t.reshape(B, NA_HEADS, rows, GRID_W, NA_HEAD_DIM)
    q, k, v = grid(q) * scale, grid(k), grid(v)

    r_idx = jnp.arange(rows)
    r_start = jnp.clip(r_idx - wr // 2, 0, rows - wr)
    key_rows = r_start[:, None] + jnp.arange(wr)[None, :]
    kg = k[:, :, key_rows]
    vg = v[:, :, key_rows]
    c_idx = jnp.arange(GRID_W)
    c_start = jnp.clip(c_idx - NA_WIN_C // 2, 0, GRID_W - NA_WIN_C)
    col_ok = (c_idx[None, :] >= c_start[:, None]) & (c_idx[None, :] < c_start[:, None] + NA_WIN_C)
    dr = key_rows - r_idx[:, None] + (NA_WIN_R - 1)
    dc = jnp.clip(c_idx[None, :] - c_idx[:, None], -(NA_WIN_C - 1), NA_WIN_C - 1) + (NA_WIN_C - 1)
    bias = rpb[:, dr[:, None, :, None], dc[None, :, None, :]].astype(f32)

    s_nb = jnp.einsum('bhrqd,bhrwkd->bhrqwk', q, kg).astype(f32) + bias
    s_nb = jnp.where(col_ok[:, None, :], s_nb, NEG_INF)
    s_c = jnp.einsum('bhrqd,bhld->bhrql', q, kc).astype(f32)
    m = jnp.maximum(jnp.max(s_nb, axis=(-2, -1)), jnp.max(s_c, axis=-1))[..., None]
    p_nb = jnp.exp(s_nb - m[..., None])
    p_c = jnp.exp(s_c - m)
    denom = jnp.sum(p_nb, axis=(-2, -1)) + jnp.sum(p_c, axis=-1)
    o = (jnp.einsum('bhrqwk,bhrwkd->bhrqd', p_nb, vg.astype(f32))
         + jnp.einsum('bhrql,bhld->bhrqd', p_c, vc.astype(f32))) / denom[..., None]
    o_x = o.reshape(B, NA_HEADS, S, NA_HEAD_DIM).transpose(0, 2, 1, 3).reshape(B, S, NA_WIDTH)
    o_c = None
    if with_ctx_out:
        pc = jax.nn.softmax(jnp.einsum('bhqd,bhkd->bhqk', qc * scale, kc).astype(f32), axis=-1)
        oc = jnp.einsum('bhqk,bhkd->bhqd', pc, vc.astype(f32))
        o_c = oc.transpose(0, 2, 1, 3).reshape(B, qkv_c.shape[1], NA_WIDTH)
    return o_x, o_c


def _rope_tables(n_tokens):
    t = jnp.arange(n_tokens, dtype=jnp.int32)
    row = (t // GRID_W).astype(jnp.float32)
    col = (t % GRID_W).astype(jnp.float32)
    axis_dim = DIFF_QK_DIM // 2
    inv = ROPE_THETA ** (-jnp.arange(0, axis_dim, 2, dtype=jnp.float32) / axis_dim)
    ar = row[:, None] * inv
    ac = col[:, None] * inv
    ang = jnp.concatenate([ar, ar, ac, ac], axis=-1)
    return jnp.cos(ang), jnp.sin(ang)


def _apply_rope_2d(x, cos, sin):
    xf = x.astype(jnp.float32)
    xs = xf.reshape(xf.shape[:-1] + (2, 2, DIFF_QK_DIM // 4))
    rot = jnp.stack([-xs[..., 1, :], xs[..., 0, :]], axis=-2).reshape(xf.shape)
    cb = cos[None, :, None, None, :]
    sb = sin[None, :, None, None, :]
    return (xf * cb + rot * sb).astype(x.dtype)


def _diff_branch(z_x, z_c, lam_q, lam_k, subln_g, lambda_init, cos, sin, with_ctx_out):
    f32 = jnp.float32
    qk_w = DIFF_HEADS * 2 * DIFF_QK_DIM
    scale = DIFF_QK_DIM ** -0.5

    def split(z):
        B, T, _ = z.shape
        q, k, v = _split(z, (qk_w, qk_w, DIFF_WIDTH))
        q = q.reshape(B, T, DIFF_HEADS, 2, DIFF_QK_DIM)
        k = k.reshape(B, T, DIFF_HEADS, 2, DIFF_QK_DIM)
        v = v.reshape(B, T, DIFF_HEADS, DIFF_V_DIM)
        return q, k, v

    qx, kx, vx = split(z_x)
    qc, kc, vc = split(z_c)
    qx = _apply_rope_2d(qx, cos, sin)
    kx = _apply_rope_2d(kx, cos, sin)
    to_h = lambda t: t.transpose(0, 2, 3, 1, 4)
    qx, kx, qc, kc = to_h(qx), to_h(kx), to_h(qc), to_h(kc)
    vx, vc = vx.transpose(0, 2, 1, 3), vc.transpose(0, 2, 1, 3)
    lam = (jnp.exp(jnp.sum(lam_q[0] * lam_k[0]).astype(f32))
           - jnp.exp(jnp.sum(lam_q[1] * lam_k[1]).astype(f32)) + lambda_init)

    def attend(qb, keys, vals):
        s = jnp.einsum('bhmqd,bhmkd->bhmqk', qb, keys).astype(f32) * scale
        p = jax.nn.softmax(s, axis=-1)
        w = p[:, :, 0] - lam * p[:, :, 1]
        return jnp.einsum('bhqk,bhkv->bhqv', w, vals.astype(f32))

    def post(o):
        o = o.transpose(0, 2, 1, 3)
        o = _rmsnorm(o, subln_g, SUBLN_EPS) * (1.0 - lambda_init)
        return o.reshape(o.shape[0], o.shape[1], DIFF_WIDTH)

    B, H, _, S, d = qx.shape
    keys = jnp.concatenate([kx, kc], axis=3)
    vals = jnp.concatenate([vx, vc], axis=2)
    nblk = S // Q_BLOCK
    qb = jnp.moveaxis(qx.reshape(B, H, 2, nblk, Q_BLOCK, d), 3, 0)
    ob = lax.map(lambda blk: attend(blk, keys, vals), qb)
    o_x = post(jnp.moveaxis(ob, 0, 2).reshape(B, H, S, DIFF_V_DIM))
    o_c = post(attend(qc, kc, vc)) if with_ctx_out else None
    return o_x, o_c


def _mixer(hx, hc, w_in, mu, k_k, k_a, r_k, w0, w_up, a0, a_up, ln_g, ln_b, rpb,
           lam_q, lam_k, subln_g, w_branch, w_out, lambda_init, cos, sin, with_ctx_out):
    zx = hx @ w_in
    zc = hc @ w_in
    rw_x, rg_x, na_x, ng_x, df_x, dg_x, mg_x = _split(zx, IN_SIZES)
    rw_c, rg_c, na_c, ng_c, df_c, dg_c, mg_c = _split(zc, IN_SIZES)
    rw_x = _token_shift(rw_x, mu[0], mu[1])
    rw_c = _token_shift(rw_c, mu[0], mu[1])
    o_rw_x, o_rw_c = _rwkv_branch(rw_x, rw_c, k_k, k_a, r_k, w0, w_up, a0, a_up, ln_g, ln_b, with_ctx_out)
    o_na_x, o_na_c = _na_branch(na_x, na_c, rpb, with_ctx_out)
    o_df_x, o_df_c = _diff_branch(df_x, df_c, lam_q, lam_k, subln_g, lambda_init, cos, sin, with_ctx_out)

    def merge(outs, gates, mg):
        o = jnp.stack([ob.astype(g.dtype) * jax.nn.silu(g) for ob, g in zip(outs, gates)], axis=2)
        yb = jnp.einsum('btnc,ncd->btnd', o, w_branch)
        gb = jax.nn.sigmoid(mg.reshape(mg.shape[:2] + (N_BRANCH, D_MODEL)))
        return jnp.einsum('btd,de->bte', jnp.sum(gb * yb, axis=2), w_out).astype(hx.dtype)

    y_x = merge((o_rw_x, o_na_x, o_df_x), (rg_x, ng_x, dg_x), mg_x)
    y_c = merge((o_rw_c, o_na_c, o_df_c), (rg_c, ng_c, dg_c), mg_c) if with_ctx_out else None
    return y_x, y_c


def setup_inputs(seed: int = 0) -> dict:
    key = jax.random.key(seed)
    ks = jax.random.split(key, 26)
    f32 = jnp.float32
    D, L = D_MODEL, DEPTH
    nrm = lambda k, shape, s: jax.random.normal(k, shape, f32) * s
    return {
        "x": nrm(ks[0], (BATCH, SEQ, D), 1.0),
        "c": nrm(ks[1], (BATCH, D), 1.0),
        "ctx": nrm(ks[2], (BATCH, CTX_LEN, D), 1.0),
        "c_ctx": nrm(ks[3], (D,), 1.0),
        "w_mod": nrm(ks[4], (L, D, 3 * D), 0.5 * D ** -0.5),
        "b_mod": nrm(ks[5], (L, 3 * D), 0.02),
        "g_pre": 1.0 + nrm(ks[6], (L, D), 0.05),
        "g_post": 1.0 + nrm(ks[7], (L, D), 0.05),
        "w_in": nrm(ks[8], (L, D, D_IN), D ** -0.5),
        "shift_mu": jax.random.uniform(ks[9], (L, 2, RWKV_SHIFT_WIDTH), f32, 0.0, 0.5),
        "k_k": 0.85 + nrm(ks[10], (L, RWKV_WIDTH), 0.05),
        "k_a": 1.0 + nrm(ks[11], (L, RWKV_WIDTH), 0.05),
        "r_k": nrm(ks[12], (L, RWKV_HEADS, RWKV_HEAD_DIM), 0.1),
        "w0": jax.random.uniform(ks[13], (L, 2, RWKV_WIDTH), f32, -6.0, -1.0),
        "w_up": nrm(ks[14], (L, 2, DECAY_LORA, RWKV_WIDTH), 0.1),
        "a0": nrm(ks[15], (L, 2, RWKV_WIDTH), 0.1),
        "a_up": nrm(ks[16], (L, 2, AAA_LORA, RWKV_WIDTH), 0.5 * AAA_LORA ** -0.5),
        "ln_x_g": 1.0 + nrm(ks[17], (L, RWKV_WIDTH), 0.05),
        "ln_x_b": nrm(ks[18], (L, RWKV_WIDTH), 0.02),
        "rpb": nrm(ks[19], (L, NA_HEADS, 2 * NA_WIN_R - 1, 2 * NA_WIN_C - 1), 0.1),
        "lam_q": nrm(ks[20], (L, 2, DIFF_QK_DIM), 0.1),
        "lam_k": nrm(ks[21], (L, 2, DIFF_QK_DIM), 0.1),
        "diff_subln": 1.0 + nrm(ks[22], (L, DIFF_V_DIM), 0.05),
        "w_branch": nrm(ks[23], (L, N_BRANCH, BRANCH_WIDTH, D), BRANCH_WIDTH ** -0.5),
        "w_out": nrm(ks[24], (L, D, D), D ** -0.5),
    }


def reference(x, c, ctx, c_ctx, w_mod, b_mod, g_pre, g_post, w_in, shift_mu, k_k, k_a, r_k,
              w0, w_up, a0, a_up, ln_x_g, ln_x_b, rpb, lam_q, lam_k, diff_subln, w_branch, w_out):
    S = x.shape[1]
    cos, sin = _rope_tables(S)
    hc = ctx
    for l in range(DEPTH):
        last = l == DEPTH - 1
        lambda_init = 0.8 - 0.6 * math.exp(-0.3 * l)
        mod_x = jax.nn.silu(c) @ w_mod[l] + b_mod[l]
        mod_c = jax.nn.silu(c_ctx) @ w_mod[l] + b_mod[l]
        sh_x, sc_x, gt_x = jnp.split(mod_x[:, None, :], 3, axis=-1)
        sh_c, sc_c, gt_c = jnp.split(mod_c, 3, axis=-1)
        hx = _rmsnorm(x, g_pre[l]) * (1.0 + sc_x) + sh_x
        hcn = _rmsnorm(hc, g_pre[l]) * (1.0 + sc_c) + sh_c
        y_x, y_c = _mixer(hx, hcn, w_in[l], shift_mu[l], k_k[l], k_a[l], r_k[l], w0[l], w_up[l],
                          a0[l], a_up[l], ln_x_g[l], ln_x_b[l], rpb[l], lam_q[l], lam_k[l],
                          diff_subln[l], w_branch[l], w_out[l], lambda_init, cos, sin,
                          not last)
        x = (x + gt_x * _rmsnorm(y_x, g_post[l])).astype(x.dtype)
        if not last:
            hc = (hc + gt_c * _rmsnorm(y_c, g_post[l])).astype(hc.dtype)
    return x
```

```python
import functools
import math

import jax
import jax.numpy as jnp
import numpy as np
from jax import lax
from jax.experimental import pallas as pl
from jax.experimental.pallas import tpu as pltpu

F32 = jnp.float32
BF16 = jnp.bfloat16

DEPTH = 4
GRID_W = 64
RWKV_HEADS = 8
HEAD_DIM = 64
RWKV_WIDTH = 512
LORA = 64
GN_EPS = 64e-5
NA_HEADS = 8
NA_WIN_R = 8
NA_WIN_C = 16
DIFF_HEADS = 4
DIFF_QK_DIM = 64
DIFF_V_DIM = 128
ROPE_THETA = 10000.0
SUBLN_EPS = 1e-5
RMS_EPS = 1e-6
NEG_INF = -1e30
BRANCH_WIDTH = 512
N_BRANCH = 3

RWKV_SHIFT_WIDTH = 3 * RWKV_WIDTH + 4 * LORA
RWKV_PAD_WIDTH = 2048
COL_MG, COL_NA, COL_DF, COL_RW, COL_RG, COL_NG, COL_DG = 0, 3072, 4608, 6144, 8192, 8704, 9216
Z_WIDTH = 9728
IN_SIZES = (RWKV_SHIFT_WIDTH, 512, 1536, 512, 1536, 512, 3072)

VMEM_LIMIT = 48 * 1024 * 1024


def _cparams(sem):
    return pltpu.CompilerParams(dimension_semantics=sem, vmem_limit_bytes=VMEM_LIMIT)


def _sigmoid(x):
    return 1.0 / (1.0 + jnp.exp(-x))


def _silu(x):
    return x * _sigmoid(x)


def _bdot(a, b):
    return jnp.dot(a.astype(BF16), b.astype(BF16), preferred_element_type=F32)


def _mod_kernel(c_ref, w_ref, b_ref, o_ref):
    o_ref[0] = _bdot(_silu(c_ref[...]), w_ref[0]) + b_ref[0]


def _modulation(cvec, w_mod, b_mod):
    L, D, N = w_mod.shape
    R = cvec.shape[0]
    tn = 1024
    return pl.pallas_call(
        _mod_kernel,
        grid=(L, N // tn),
        in_specs=[pl.BlockSpec((R, D), lambda l, n: (0, 0)),
                  pl.BlockSpec((1, D, tn), lambda l, n: (l, 0, n)),
                  pl.BlockSpec((1, 1, tn), lambda l, n: (l, 0, n))],
        out_specs=pl.BlockSpec((1, R, tn), lambda l, n: (l, 0, n)),
        out_shape=jax.ShapeDtypeStruct((L, R, N), F32),
        compiler_params=_cparams(("parallel", "parallel")),
        name="modulation",
    )(cvec, w_mod, b_mod.reshape(L, 1, N))


def _inproj_kernel(x_ref, mod_ref, g_ref, w_ref, o_ref, h_ref):
    @pl.when(pl.program_id(2) == 0)
    def _():
        x = x_ref[0]
        y = x * lax.rsqrt(jnp.mean(x * x, axis=-1, keepdims=True) + RMS_EPS) * g_ref[...]
        h_ref[...] = (y * (1.0 + mod_ref[0, 1:2, :]) + mod_ref[0, 0:1, :]).astype(BF16)

    o_ref[0] = jnp.dot(h_ref[...], w_ref[...], preferred_element_type=F32)


def _inproj(x, mod, g_pre, w_perm, tm):
    B, T, D = x.shape
    N = w_perm.shape[1]
    tn = 512
    per_batch = mod.shape[0] > 1
    return pl.pallas_call(
        _inproj_kernel,
        grid=(B, T // tm, N // tn),
        in_specs=[pl.BlockSpec((1, tm, D), lambda b, m, n: (b, m, 0)),
                  pl.BlockSpec((1, 3, D), (lambda b, m, n: (b, 0, 0)) if per_batch else (lambda b, m, n: (0, 0, 0))),
                  pl.BlockSpec((1, D), lambda b, m, n: (0, 0)),
                  pl.BlockSpec((D, tn), lambda b, m, n: (0, n))],
        out_specs=pl.BlockSpec((1, tm, tn), lambda b, m, n: (b, m, n)),
        out_shape=jax.ShapeDtypeStruct((B, T, N), F32),
        scratch_shapes=[pltpu.VMEM((tm, D), BF16)],
        compiler_params=_cparams(("parallel", "parallel", "arbitrary")),
        name="inproj",
    )(x, mod, g_pre.reshape(1, D), w_perm)


def _merge_kernel(x_ref, mod_ref, gpost_ref, orw_ref, ona_ref, odf_ref, rg_ref, ng_ref, dg_ref, mg_ref,
                  wb_ref, wo_ref, o_ref):
    D = x_ref.shape[-1]
    acc = None
    for n, (ob_ref, gate_ref) in enumerate(((orw_ref, rg_ref), (ona_ref, ng_ref), (odf_ref, dg_ref))):
        yb = _bdot(ob_ref[0] * _silu(gate_ref[0]), wb_ref[n])
        term = _sigmoid(mg_ref[0, :, n * D:(n + 1) * D]) * yb
        acc = term if acc is None else acc + term
    y = _bdot(acc, wo_ref[...])
    yn = y * lax.rsqrt(jnp.mean(y * y, axis=-1, keepdims=True) + RMS_EPS) * gpost_ref[...]
    o_ref[0] = x_ref[0] + mod_ref[0, 2:3, :] * yn


def _merge(x, mod, g_post, o_rw, o_na, o_df, z, w_branch, w_out, tm):
    B, T, D = x.shape
    per_batch = mod.shape[0] > 1
    bw = BRANCH_WIDTH
    row = lambda b, m: (b, m, 0)
    col = lambda c: (lambda b, m: (b, m, c))
    return pl.pallas_call(
        _merge_kernel,
        grid=(B, T // tm),
        in_specs=[pl.BlockSpec((1, tm, D), row),
                  pl.BlockSpec((1, 3, D), (lambda b, m: (b, 0, 0)) if per_batch else (lambda b, m: (0, 0, 0))),
                  pl.BlockSpec((1, D), lambda b, m: (0, 0)),
                  pl.BlockSpec((1, tm, bw), row), pl.BlockSpec((1, tm, bw), row), pl.BlockSpec((1, tm, bw), row),
                  pl.BlockSpec((1, tm, bw), col(COL_RG // bw)),
                  pl.BlockSpec((1, tm, bw), col(COL_NG // bw)),
                  pl.BlockSpec((1, tm, bw), col(COL_DG // bw)),
                  pl.BlockSpec((1, tm, N_BRANCH * D), col(COL_MG // (N_BRANCH * D))),
                  pl.BlockSpec((N_BRANCH, bw, D), lambda b, m: (0, 0, 0)),
                  pl.BlockSpec((D, D), lambda b, m: (0, 0))],
        out_specs=pl.BlockSpec((1, tm, D), row),
        out_shape=jax.ShapeDtypeStruct((B, T, D), F32),
        compiler_params=_cparams(("parallel", "parallel")),
        name="merge",
    )(x, mod, g_post.reshape(1, D), o_rw, o_na, o_df, z, z, z, z, w_branch, w_out)


def _qkv_prep_kernel(*refs, scale, rope):
    if rope:
        q_ref, k_ref, v_ref, cos_ref, sa_ref, sb_ref, qo_ref, ko_ref, vo_ref = refs
    else:
        q_ref, k_ref, v_ref, qo_ref, ko_ref, vo_ref = refs
    q, k = q_ref[0], k_ref[0]
    if rope:
        w = q.shape[-1]
        cos, sa, sb = cos_ref[...], sa_ref[...], sb_ref[...]
        rot = lambda t: t * cos + pltpu.roll(t, w - 16, 1) * sa + pltpu.roll(t, 16, 1) * sb
        q, k = rot(q), rot(k)
    qo_ref[0] = (q * scale).astype(BF16)
    ko_ref[0] = k.astype(BF16)
    vo_ref[0] = v_ref[0].astype(BF16)


def _qkv_prep(z, col, scale, tables, tm):
    B, T, _ = z.shape
    w = 512
    blk = lambda j: pl.BlockSpec((1, tm, w), lambda b, m: (b, m, col // w + j))
    in_specs = [blk(0), blk(1), blk(2)]
    args = [z, z, z]
    if tables is not None:
        in_specs += [pl.BlockSpec((tm, w), lambda b, m: (m, 0))] * 3
        args += list(tables)
    out = jax.ShapeDtypeStruct((B, T, w), BF16)
    return pl.pallas_call(
        functools.partial(_qkv_prep_kernel, scale=scale, rope=tables is not None),
        grid=(B, T // tm),
        in_specs=in_specs,
        out_specs=[pl.BlockSpec((1, tm, w), lambda b, m: (b, m, 0))] * 3,
        out_shape=[out, out, out],
        compiler_params=_cparams(("parallel", "parallel")),
        name="qkv_prep",
    )(*args)


def _rope_tables(n_tokens):
    t = np.arange(n_tokens)
    axis_dim = DIFF_QK_DIM // 2
    inv = ROPE_THETA ** (-np.arange(0, axis_dim, 2, dtype=np.float32) / axis_dim)
    ar = (t // GRID_W).astype(np.float32)[:, None] * inv
    ac = (t % GRID_W).astype(np.float32)[:, None] * inv
    ang = jnp.asarray(np.concatenate([ar, ar, ac, ac], axis=-1).astype(np.float32))
    cos, sin = jnp.cos(ang), jnp.sin(ang)
    first = (np.arange(DIFF_QK_DIM) % 32) < 16
    sin_a = jnp.where(first, -sin, 0.0)
    sin_b = jnp.where(first, 0.0, sin)
    tile = lambda a: jnp.tile(a, (1, 8))
    return tile(cos), tile(sin_a), tile(sin_b)


def _dot_nt(a, b):
    return lax.dot_general(a, b, (((1,), (1,)), ((), ())), preferred_element_type=F32)


def _diff_attn_kernel(q_ref, k_ref, v_ref, lq_ref, lk_ref, g_ref, o_ref, *, tk, lambda_init):
    tq = q_ref.shape[1]
    nkv = k_ref.shape[1] // tk
    q = q_ref[0]
    lo = lax.broadcasted_iota(jnp.int32, (1, 128), 1) < DIFF_QK_DIM
    zero = jnp.zeros_like(q)
    q1 = jnp.where(lo, q, zero)
    q2 = jnp.where(lo, zero, q)

    def body(j, carry):
        m1, l1, a1, m2, l2, a2 = carry
        k = k_ref[0, pl.ds(pl.multiple_of(j * tk, tk), tk), :]
        v = v_ref[0, pl.ds(pl.multiple_of(j * tk, tk), tk), :]

        def upd(qm, m, l, a):
            s = _dot_nt(qm, k)
            mn = jnp.maximum(m, jnp.max(s, axis=-1, keepdims=True))
            al = jnp.exp(m - mn)
            p = jnp.exp(s - mn)
            return (mn, al * l + jnp.sum(p, axis=-1, keepdims=True),
                    al * a + jnp.dot(p.astype(BF16), v, preferred_element_type=F32))

        m1, l1, a1 = upd(q1, m1, l1, a1)
        m2, l2, a2 = upd(q2, m2, l2, a2)
        return m1, l1, a1, m2, l2, a2

    neg = jnp.full((tq, 1), NEG_INF, F32)
    z1 = jnp.zeros((tq, 1), F32)
    za = jnp.zeros((tq, DIFF_V_DIM), F32)
    m1, l1, a1, m2, l2, a2 = lax.fori_loop(0, nkv, body, (neg, z1, za, neg, z1, za))
    lqk = lq_ref[...] * lk_ref[...]
    e = jnp.exp(jnp.sum(lqk, axis=-1, keepdims=True))
    lam = e[0:1] - e[1:2] + lambda_init
    o = a1 / l1 - lam * (a2 / l2)
    o = o * lax.rsqrt(jnp.mean(o * o, axis=-1, keepdims=True) + SUBLN_EPS) * g_ref[...]
    o_ref[0] = o * (1.0 - lambda_init)


def _diff_attn(q, k, v, lam_q, lam_k, subln_g, lambda_init, tq, tk):
    B, T, _ = q.shape
    Tk = k.shape[1]
    kv = pl.BlockSpec((1, Tk, 128), lambda b, h, m: (b, 0, h))
    small = lambda r, c: pl.BlockSpec((r, c), lambda b, h, m: (0, 0))
    return pl.pallas_call(
        functools.partial(_diff_attn_kernel, tk=tk, lambda_init=lambda_init),
        grid=(B, DIFF_HEADS, T // tq),
        in_specs=[pl.BlockSpec((1, tq, 128), lambda b, h, m: (b, m, h)), kv, kv,
                  small(2, DIFF_QK_DIM), small(2, DIFF_QK_DIM), small(1, DIFF_V_DIM)],
        out_specs=pl.BlockSpec((1, tq, 128), lambda b, h, m: (b, m, h)),
        out_shape=jax.ShapeDtypeStruct((B, T, DIFF_HEADS * DIFF_V_DIM), F32),
        compiler_params=_cparams(("parallel", "parallel", "parallel")),
        name="diff_attn",
    )(q, k, v, lam_q, lam_k, subln_g.reshape(1, DIFF_V_DIM))


def _na_bias_tables(rpb):
    c_idx = np.arange(GRID_W)
    c_start = np.clip(c_idx - NA_WIN_C // 2, 0, GRID_W - NA_WIN_C)
    col_ok = (c_idx[None, :] >= c_start[:, None]) & (c_idx[None, :] < c_start[:, None] + NA_WIN_C)
    dc = np.clip(c_idx[None, :] - c_idx[:, None], -(NA_WIN_C - 1), NA_WIN_C - 1) + (NA_WIN_C - 1)
    off = np.arange(NA_WIN_R)
    dr = np.arange(NA_WIN_R)[None, :] - off[:, None] + (NA_WIN_R - 1)
    b = rpb[:, dr[:, None, :, None], dc[None, :, None, :]]
    b = jnp.where(col_ok[None, None, :, None, :], b, NEG_INF)
    return b.reshape(rpb.shape[0], NA_WIN_R, GRID_W, NA_WIN_R * GRID_W)


def _na_kernel(q_ref, k_ref, v_ref, kc_ref, vc_ref, bias_ref, o_ref, *, rq, rows):
    i = pl.program_id(2)
    lo = lax.broadcasted_iota(jnp.int32, (1, 128), 1) < HEAD_DIM
    kc, vc = kc_ref[0], vc_ref[0]
    win = NA_WIN_R * GRID_W

    def body(rr, _):
        r = i * rq + rr
        r_start = jnp.clip(r - NA_WIN_R // 2, 0, rows - NA_WIN_R)
        off = r - r_start
        q = q_ref[0, pl.ds(pl.multiple_of(rr * GRID_W, GRID_W), GRID_W), :]
        kw = k_ref[0, pl.ds(pl.multiple_of(r_start * GRID_W, GRID_W), win), :]
        vw = v_ref[0, pl.ds(pl.multiple_of(r_start * GRID_W, GRID_W), win), :]
        zero = jnp.zeros_like(q)
        outs = []
        for hl in range(2):
            qh = jnp.where(lo, q, zero) if hl == 0 else jnp.where(lo, zero, q)
            s_nb = _dot_nt(qh, kw) + bias_ref[hl, off]
            s_c = _dot_nt(qh, kc)
            m = jnp.maximum(jnp.max(s_nb, axis=-1, keepdims=True), jnp.max(s_c, axis=-1, keepdims=True))
            p_nb = jnp.exp(s_nb - m)
            p_c = jnp.exp(s_c - m)
            den = jnp.sum(p_nb, axis=-1, keepdims=True) + jnp.sum(p_c, axis=-1, keepdims=True)
            o = (jnp.dot(p_nb.astype(BF16), vw, preferred_element_type=F32)
                 + jnp.dot(p_c.astype(BF16), vc, preferred_element_type=F32))
            outs.append(o / den)
        o_ref[0, pl.ds(pl.multiple_of(rr * GRID_W, GRID_W), GRID_W), :] = jnp.where(lo, outs[0], outs[1])
        return 0

    lax.fori_loop(0, rq, body, 0)


def _na_attn(q, k, v, kc, vc, bias, rq):
    B, S, _ = q.shape
    C = kc.shape[1]
    rows = S // GRID_W
    full = lambda t: pl.BlockSpec((1, t, 128), lambda b, p, m: (b, 0, p))
    return pl.pallas_call(
        functools.partial(_na_kernel, rq=rq, rows=rows),
        grid=(B, NA_HEADS // 2, rows // rq),
        in_specs=[pl.BlockSpec((1, rq * GRID_W, 128), lambda b, p, m: (b, m, p)),
                  full(S), full(S), full(C), full(C),
                  pl.BlockSpec((2, NA_WIN_R, GRID_W, NA_WIN_R * GRID_W), lambda b, p, m: (p, 0, 0, 0))],
        out_specs=pl.BlockSpec((1, rq * GRID_W, 128), lambda b, p, m: (b, m, p)),
        out_shape=jax.ShapeDtypeStruct((B, S, NA_HEADS * HEAD_DIM), F32),
        compiler_params=_cparams(("parallel", "parallel", "parallel")),
        name="na_attn",
    )(q, k, v, kc, vc, bias)


def _ctx_attn_kernel(q_ref, k_ref, v_ref, o_ref):
    lo = lax.broadcasted_iota(jnp.int32, (1, 128), 1) < HEAD_DIM
    q, k, v = q_ref[0], k_ref[0], v_ref[0]
    zero = jnp.zeros_like(q)
    outs = []
    for hl in range(2):
        qh = jnp.where(lo, q, zero) if hl == 0 else jnp.where(lo, zero, q)
        s = _dot_nt(qh, k)
        p = jnp.exp(s - jnp.max(s, axis=-1, keepdims=True))
        o = jnp.dot(p.astype(BF16), v, preferred_element_type=F32)
        outs.append(o / jnp.sum(p, axis=-1, keepdims=True))
    o_ref[0] = jnp.where(lo, outs[0], outs[1])


def _ctx_attn(q, k, v):
    B, C, _ = q.shape
    blk = pl.BlockSpec((1, C, 128), lambda b, p: (b, 0, p))
    return pl.pallas_call(
        _ctx_attn_kernel,
        grid=(B, NA_HEADS // 2),
        in_specs=[blk, blk, blk],
        out_specs=blk,
        out_shape=jax.ShapeDtypeStruct((B, C, NA_HEADS * HEAD_DIM), F32),
        compiler_params=_cparams(("parallel", "parallel")),
        name="ctx_attn",
    )(q, k, v)


PL_R, PL_V, PL_KK = 0, 1, 2
PL_LOGW, PL_KDIR, PL_B = 3, 4, 5
N_PLANES = 9
RWKV_CHUNK = 64


def _split3(x):
    hi = x.astype(BF16)
    r1 = x - hi.astype(F32)
    mid = r1.astype(BF16)
    lo = (r1 - mid.astype(F32)).astype(BF16)
    return hi, mid, lo


def _dot_exact_rhs(m, x):
    hi, mid, lo = _split3(x)
    mb = m.astype(BF16)
    d = lambda t: jnp.dot(mb, t, preferred_element_type=F32)
    return d(hi) + d(mid) + d(lo)


def _dot_exact_lhs(x, m):
    hi, mid, lo = _split3(x)
    mb = m.astype(BF16)
    d = lambda t: jnp.dot(t, mb, preferred_element_type=F32)
    return d(hi) + d(mid) + d(lo)


def _head_ones(n):
    r = lax.broadcasted_iota(jnp.int32, (n, n), 0) // HEAD_DIM
    c = lax.broadcasted_iota(jnp.int32, (n, n), 1) // HEAD_DIM
    return (r == c).astype(F32)


def _rwkv_prep_kernel(zc_ref, zp_ref, zn_ref, mu_ref, kk_ref, ka_ref, w0_ref, a0_ref, wup_ref, aup_ref, o_ref):
    tm = zc_ref.shape[1]
    m = pl.program_id(1)
    u = zc_ref[0]
    prev_row = jnp.where(m > 0, zp_ref[0, 7:8, :], 0.0)
    next_row = jnp.where(m < pl.num_programs(1) - 1, zn_ref[0, 0:1, :], 0.0)
    rows = lax.broadcasted_iota(jnp.int32, (tm, 1), 0)
    u_prev = jnp.where(rows == 0, prev_row, pltpu.roll(u, 1, 0))
    u_next = jnp.where(rows == tm - 1, next_row, pltpu.roll(u, tm - 1, 0))
    u = u + mu_ref[0:1, :] * (u_prev - u) + mu_ref[1:2, :] * (u_next - u)

    W = RWKV_WIDTH
    r, k, v = u[:, 0:W], u[:, W:2 * W], u[:, 2 * W:3 * W]
    lw = jnp.tanh(u[:, 3 * W:3 * W + 2 * LORA]).astype(BF16)
    la = u[:, 3 * W + 2 * LORA:3 * W + 4 * LORA].astype(BF16)
    ones = _head_ones(W)
    kk = k * kk_ref[...]
    ss = _dot_exact_lhs(kk * kk, ones)
    kk = kk / jnp.maximum(jnp.sqrt(ss), 1e-12)
    o_ref[0, :, PL_R * W:(PL_R + 1) * W] = r
    o_ref[0, :, PL_V * W:(PL_V + 1) * W] = v
    o_ref[0, :, PL_KK * W:(PL_KK + 1) * W] = kk
    for d in range(2):
        xw = w0_ref[d:d + 1, :] + jnp.dot(lw, wup_ref[d], preferred_element_type=F32)
        logw = -math.exp(-0.5) * _sigmoid(xw)
        a = _sigmoid(a0_ref[d:d + 1, :] + jnp.dot(la, aup_ref[d], preferred_element_type=F32))
        kdir = k * (1.0 + (a - 1.0) * ka_ref[...])
        base = 3 * d
        o_ref[0, :, (PL_LOGW + base) * W:(PL_LOGW + base + 1) * W] = logw
        o_ref[0, :, (PL_KDIR + base) * W:(PL_KDIR + base + 1) * W] = kdir
        o_ref[0, :, (PL_B + base) * W:(PL_B + base + 1) * W] = kk * a


def _rwkv_prep(z, mu_pad, k_k, k_a, w0, a0, wup2, aup2, tm):
    B, T, _ = z.shape
    wz = RWKV_PAD_WIDTH
    cb = COL_RW // wz
    nb = tm // 8
    W = RWKV_WIDTH
    const = lambda shape: pl.BlockSpec(shape, lambda b, m: (0,) * len(shape))
    return pl.pallas_call(
        _rwkv_prep_kernel,
        grid=(B, T // tm),
        in_specs=[pl.BlockSpec((1, tm, wz), lambda b, m: (b, m, cb)),
                  pl.BlockSpec((1, 8, wz), lambda b, m: (b, jnp.maximum(m * nb - 1, 0), cb)),
                  pl.BlockSpec((1, 8, wz), lambda b, m: (b, jnp.minimum((m + 1) * nb, T // 8 - 1), cb)),
                  const((2, wz)), const((1, W)), const((1, W)), const((2, W)), const((2, W)),
                  const((2, 2 * LORA, W)), const((2, 2 * LORA, W))],
        out_specs=pl.BlockSpec((1, tm, N_PLANES * W), lambda b, m: (b, m, 0)),
        out_shape=jax.ShapeDtypeStruct((B, T, N_PLANES * W), F32),
        compiler_params=_cparams(("parallel", "parallel")),
        name="rwkv_prep",
    )(z, z, z, mu_pad, k_k.reshape(1, W), k_a.reshape(1, W), w0, a0, wup2, aup2)


def _rwkv_scan_kernel(pf_ref, pb_ref, s0_ref, yf_ref, yb_ref, s_ref):
    C = RWKV_CHUNK
    W = RWKV_WIDTH

    @pl.when(pl.program_id(1) == 0)
    def _():
        s_ref[...] = s0_ref[...]

    lo = lax.broadcasted_iota(jnp.int32, (1, 128), 1) < HEAD_DIM
    row = lax.broadcasted_iota(jnp.int32, (2 * C, 2 * C), 0)
    col = lax.broadcasted_iota(jnp.int32, (2 * C, 2 * C), 1)
    same = (row // C) == (col // C)
    eye = (row == col).astype(F32)
    tri_r = lax.broadcasted_iota(jnp.int32, (C, C), 0)
    tri_c = lax.broadcasted_iota(jnp.int32, (C, C), 1)

    def stack(x):
        return jnp.concatenate([jnp.where(lo, x, 0.0), jnp.where(lo, 0.0, x)], axis=0)

    for d, (p_ref, y_ref) in enumerate(((pf_ref, yf_ref), (pb_ref, yb_ref))):
        before = (col < row) if d == 0 else (col > row)
        strict = same & before
        incl = same & (before | (row == col))
        cum = ((tri_c <= tri_r) if d == 0 else (tri_c >= tri_r)).astype(F32)
        last = C - 1 if d == 0 else 0
        for p in range(W // 128):
            get = lambda plane: p_ref[0, :, plane * W + p * 128:plane * W + (p + 1) * 128]
            r, v, kk = get(PL_R), get(PL_V), get(PL_KK)
            logw, kdir, bb = get(PL_LOGW + 3 * d), get(PL_KDIR + 3 * d), get(PL_B + 3 * d)
            c = _dot_exact_rhs(cum, logw)
            c_last = c[last:last + 1, :]
            e_neg = jnp.exp(-c)
            a_s = stack(-kk * jnp.exp(c - logw))
            r_s = stack(r * jnp.exp(c))
            b_s = stack(bb * e_neg)
            k_s = stack(kdir * e_neg)
            e_end = jnp.exp(c_last - c)
            bend_s = stack(bb * e_end)
            kend_s = stack(kdir * e_end)
            v_s = stack(v)
            a_ab = jnp.where(strict, _dot_nt_b(a_s, b_s), 0.0)
            a_ak = jnp.where(strict, _dot_nt_b(a_s, k_s), 0.0)
            a_rb = jnp.where(incl, _dot_nt_b(r_s, b_s), 0.0)
            a_rk = jnp.where(incl, _dot_nt_b(r_s, k_s), 0.0)
            t = eye + a_ab
            pw = a_ab
            for _ in range(5):
                pw = _bdot(pw, pw)
                t = t + _bdot(t, pw)
            s = s_ref[0, d, p]
            u_s = _bdot(t, _dot_nt_b(a_s, s) + _bdot(a_ak, v_s))
            y_s = _dot_nt_b(r_s, s) + _bdot(a_rb, u_s) + _bdot(a_rk, v_s)
            y_ref[0, :, p * 128:(p + 1) * 128] = y_s[:C] + y_s[C:]
            s_ref[0, d, p] = s * jnp.exp(c_last) + _dot_tn_b(u_s, bend_s) + _dot_tn_b(v_s, kend_s)


def _dot_nt_b(a, b):
    return _dot_nt(a.astype(BF16), b.astype(BF16))


def _dot_tn_b(a, b):
    return lax.dot_general(a.astype(BF16), b.astype(BF16), (((0,), (0,)), ((), ())), preferred_element_type=F32)


def _rwkv_scan(planes, s0):
    B, T, _ = planes.shape
    nc = T // RWKV_CHUNK
    W = RWKV_WIDTH
    st = pl.BlockSpec((1, 2, W // 128, 128, 128), lambda b, i: (b, 0, 0, 0, 0))
    y = jax.ShapeDtypeStruct((B, T, W), F32)
    return pl.pallas_call(
        _rwkv_scan_kernel,
        grid=(B, nc),
        in_specs=[pl.BlockSpec((1, RWKV_CHUNK, N_PLANES * W), lambda b, i: (b, i, 0)),
                  pl.BlockSpec((1, RWKV_CHUNK, N_PLANES * W), lambda b, i: (b, nc - 1 - i, 0)),
                  st],
        out_specs=[pl.BlockSpec((1, RWKV_CHUNK, W), lambda b, i: (b, i, 0)),
                   pl.BlockSpec((1, RWKV_CHUNK, W), lambda b, i: (b, nc - 1 - i, 0)),
                   st],
        out_shape=[y, y, jax.ShapeDtypeStruct(s0.shape, F32)],
        compiler_params=_cparams(("parallel", "arbitrary")),
        name="rwkv_scan",
    )(planes, planes, s0)


def _rwkv_readout_kernel(yf_ref, yb_ref, r_ref, v_ref, kf_ref, kb_ref, g_ref, b_ref, rk_ref, o_ref):
    ones = _head_ones(RWKV_WIDTH)
    y = yf_ref[0] + yb_ref[0]
    inv_n = 1.0 / HEAD_DIM
    mu = _dot_exact_lhs(y, ones) * inv_n
    yc = y - mu
    var = _dot_exact_lhs(yc * yc, ones) * inv_n
    yn = yc * lax.rsqrt(var + GN_EPS) * g_ref[...] + b_ref[...]
    bonus = _dot_exact_lhs(r_ref[0] * (kf_ref[0] + kb_ref[0]) * rk_ref[...], ones)
    o_ref[0] = yn + bonus * v_ref[0]


def _rwkv_readout(y_f, y_b, planes, ln_g, ln_b, r_k, tm):
    B, T, W = y_f.shape
    row = pl.BlockSpec((1, tm, W), lambda b, m: (b, m, 0))
    plane = lambda j: pl.BlockSpec((1, tm, W), lambda b, m: (b, m, j))
    vec = pl.BlockSpec((1, W), lambda b, m: (0, 0))
    return pl.pallas_call(
        _rwkv_readout_kernel,
        grid=(B, T // tm),
        in_specs=[row, row, plane(PL_R), plane(PL_V), plane(PL_KDIR), plane(PL_KDIR + 3), vec, vec, vec],
        out_specs=row,
        out_shape=jax.ShapeDtypeStruct((B, T, W), F32),
        compiler_params=_cparams(("parallel", "parallel")),
        name="rwkv_readout",
    )(y_f, y_b, planes, planes, planes, planes, ln_g.reshape(1, W), ln_b.reshape(1, W), r_k.reshape(1, W))


def _layer_weights(l, w_in, shift_mu, w_up, a_up, w_branch, w_out):
    rw, rg, na, ng, df, dg, mg = jnp.split(w_in[l], [int(i) for i in np.cumsum(IN_SIZES)[:-1]], axis=-1)
    pad = jnp.zeros((w_in.shape[1], RWKV_PAD_WIDTH - RWKV_SHIFT_WIDTH), w_in.dtype)
    w_perm = jnp.concatenate([mg, na, df, rw, pad, rg, ng, dg], axis=-1).astype(BF16)
    mu_pad = jnp.pad(shift_mu[l], ((0, 0), (0, RWKV_PAD_WIDTH - RWKV_SHIFT_WIDTH)))
    zl = jnp.zeros((LORA, RWKV_WIDTH), F32)
    wup2 = jnp.stack([jnp.concatenate([w_up[l, 0], zl]), jnp.concatenate([zl, w_up[l, 1]])]).astype(BF16)
    aup2 = jnp.stack([jnp.concatenate([a_up[l, 0], zl]), jnp.concatenate([zl, a_up[l, 1]])]).astype(BF16)
    return w_perm, mu_pad, wup2, aup2, w_branch[l].astype(BF16), w_out[l].astype(BF16)


def kernel(x, c, ctx, c_ctx, w_mod, b_mod, g_pre, g_post, w_in, shift_mu, k_k, k_a, r_k, w0, w_up, a0, a_up,
           ln_x_g, ln_x_b, rpb, lam_q, lam_k, diff_subln, w_branch, w_out):
    B, S, D = x.shape
    C = ctx.shape[1]
    depth = w_in.shape[0]
    tables = _rope_tables(S)
    rows_pad = 16
    cvec = jnp.zeros((rows_pad, D), F32).at[:B].set(c).at[B].set(c_ctx)
    mod = _modulation(cvec, w_mod, b_mod)
    hc = ctx
    tm_x = min(1024, S)
    for l in range(depth):
        last = l == depth - 1
        lambda_init = 0.8 - 0.6 * math.exp(-0.3 * l)
        w_perm, mu_pad, wup2, aup2, wb, wo = _layer_weights(l, w_in, shift_mu, w_up, a_up, w_branch, w_out)
        mod_x = mod[l, :B].reshape(B, 3, D)
        mod_c = mod[l, B:B + 1].reshape(1, 3, D)
        z_x = _inproj(x, mod_x, g_pre[l], w_perm, tm_x)
        z_c = _inproj(hc, mod_c, g_pre[l], w_perm, C)

        pl_c = _rwkv_prep(z_c, mu_pad, k_k[l], k_a[l], w0[l], a0[l], wup2, aup2, min(256, C))
        pl_x = _rwkv_prep(z_x, mu_pad, k_k[l], k_a[l], w0[l], a0[l], wup2, aup2, 256)
        s_zero = jnp.zeros((B, 2, RWKV_WIDTH // 128, 128, 128), F32)
        yf_c, yb_c, s_ctx = _rwkv_scan(pl_c, s_zero)
        yf_x, yb_x, _ = _rwkv_scan(pl_x, s_ctx)
        r_k_flat = r_k[l].reshape(RWKV_WIDTH)
        o_rw_x = _rwkv_readout(yf_x, yb_x, pl_x, ln_x_g[l], ln_x_b[l], r_k_flat, 256)

        scale = HEAD_DIM ** -0.5
        nq_x, nk_x, nv_x = _qkv_prep(z_x, COL_NA, scale, None, 512)
        nq_c, nk_c, nv_c = _qkv_prep(z_c, COL_NA, scale, None, C)
        o_na_x = _na_attn(nq_x, nk_x, nv_x, nk_c, nv_c, _na_bias_tables(rpb[l]), 8)

        dq_x, dk_x, dv_x = _qkv_prep(z_x, COL_DF, scale, tables, 512)
        dq_c, dk_c, dv_c = _qkv_prep(z_c, COL_DF, scale, None, C)
        keys = jnp.concatenate([dk_c, dk_x], axis=1)
        vals = jnp.concatenate([dv_c, dv_x], axis=1)
        o_df_x = _diff_attn(dq_x, keys, vals, lam_q[l], lam_k[l], diff_subln[l], lambda_init, 512, C)

        x = _merge(x, mod_x, g_post[l], o_rw_x, o_na_x, o_df_x, z_x, wb, wo, 256)
        if not last:
            o_rw_c = _rwkv_readout(yf_c, yb_c, pl_c, ln_x_g[l], ln_x_b[l], r_k_flat, min(256, C))
            o_na_c = _ctx_attn(nq_c, nk_c, nv_c)
            o_df_c = _diff_attn(dq_c, dk_c, dv_c, lam_q[l], lam_k[l], diff_subln[l], lambda_init, C, C)
            hc = _merge(hc, mod_c, g_post[l], o_rw_c, o_na_c, o_df_c, z_c, wb, wo, min(256, C))
    return x
```

```python
import functools
import math

import jax
import jax.numpy as jnp
import numpy as np
from jax import lax
from jax.experimental import pallas as pl
from jax.experimental.pallas import tpu as pltpu

F32 = jnp.float32
BF16 = jnp.bfloat16

DEPTH = 4
GRID_W = 64
RWKV_HEADS = 8
HEAD_DIM = 64
RWKV_WIDTH = 512
LORA = 64
GN_EPS = 64e-5
NA_HEADS = 8
NA_WIN_R = 8
NA_WIN_C = 16
DIFF_HEADS = 4
DIFF_QK_DIM = 64
DIFF_V_DIM = 128
ROPE_THETA = 10000.0
SUBLN_EPS = 1e-5
RMS_EPS = 1e-6
NEG_INF = -1e30
BRANCH_WIDTH = 512
N_BRANCH = 3

RWKV_SHIFT_WIDTH = 3 * RWKV_WIDTH + 4 * LORA
RWKV_PAD_WIDTH = 2048
COL_MG, COL_NA, COL_DF, COL_RW, COL_RG, COL_NG, COL_DG = 0, 3072, 4608, 6144, 8192, 8704, 9216
Z_WIDTH = 9728
IN_SIZES = (RWKV_SHIFT_WIDTH, 512, 1536, 512, 1536, 512, 3072)

VMEM_LIMIT = 48 * 1024 * 1024


def _cparams(sem):
    return pltpu.CompilerParams(dimension_semantics=sem, vmem_limit_bytes=VMEM_LIMIT)


def _sigmoid(x):
    return 1.0 / (1.0 + jnp.exp(-x))


def _silu(x):
    return x * _sigmoid(x)


def _bdot(a, b):
    return jnp.dot(a.astype(BF16), b.astype(BF16), preferred_element_type=F32)


def _mod_kernel(c_ref, w_ref, b_ref, o_ref):
    o_ref[0] = _bdot(_silu(c_ref[...]), w_ref[0]) + b_ref[0]


def _modulation(cvec, w_mod, b_mod):
    L, D, N = w_mod.shape
    R = cvec.shape[0]
    tn = 1024
    return pl.pallas_call(
        _mod_kernel,
        grid=(L, N // tn),
        in_specs=[pl.BlockSpec((R, D), lambda l, n: (0, 0)),
                  pl.BlockSpec((1, D, tn), lambda l, n: (l, 0, n)),
                  pl.BlockSpec((1, 1, tn), lambda l, n: (l, 0, n))],
        out_specs=pl.BlockSpec((1, R, tn), lambda l, n: (l, 0, n)),
        out_shape=jax.ShapeDtypeStruct((L, R, N), F32),
        compiler_params=_cparams(("parallel", "parallel")),
        name="modulation",
    )(cvec, w_mod, b_mod.reshape(L, 1, N))


def _inproj_kernel(x_ref, mod_ref, g_ref, w_ref, o_ref, h_ref):
    @pl.when(pl.program_id(2) == 0)
    def _():
        x = x_ref[0]
        y = x * lax.rsqrt(jnp.mean(x * x, axis=-1, keepdims=True) + RMS_EPS) * g_ref[...]
        h_ref[...] = (y * (1.0 + mod_ref[0, 1:2, :]) + mod_ref[0, 0:1, :]).astype(BF16)

    o_ref[0] = jnp.dot(h_ref[...], w_ref[...], preferred_element_type=F32)


def _inproj(x, mod, g_pre, w_perm, tm):
    B, T, D = x.shape
    N = w_perm.shape[1]
    tn = 512
    per_batch = mod.shape[0] > 1
    return pl.pallas_call(
        _inproj_kernel,
        grid=(B, T // tm, N // tn),
        in_specs=[pl.BlockSpec((1, tm, D), lambda b, m, n: (b, m, 0)),
                  pl.BlockSpec((1, 3, D), (lambda b, m, n: (b, 0, 0)) if per_batch else (lambda b, m, n: (0, 0, 0))),
                  pl.BlockSpec((1, D), lambda b, m, n: (0, 0)),
                  pl.BlockSpec((D, tn), lambda b, m, n: (0, n))],
        out_specs=pl.BlockSpec((1, tm, tn), lambda b, m, n: (b, m, n)),
        out_shape=jax.ShapeDtypeStruct((B, T, N), F32),
        scratch_shapes=[pltpu.VMEM((tm, D), BF16)],
        compiler_params=_cparams(("parallel", "parallel", "arbitrary")),
        name="inproj",
    )(x, mod, g_pre.reshape(1, D), w_perm)


def _merge_kernel(x_ref, mod_ref, gpost_ref, orw_ref, ona_ref, odf_ref, rg_ref, ng_ref, dg_ref, mg_ref,
                  wb_ref, wo_ref, o_ref):
    D = x_ref.shape[-1]
    acc = None
    for n, (ob_ref, gate_ref) in enumerate(((orw_ref, rg_ref), (ona_ref, ng_ref), (odf_ref, dg_ref))):
        yb = _bdot(ob_ref[0] * _silu(gate_ref[0]), wb_ref[n])
        term = _sigmoid(mg_ref[0, :, n * D:(n + 1) * D]) * yb
        acc = term if acc is None else acc + term
    y = _bdot(acc, wo_ref[...])
    yn = y * lax.rsqrt(jnp.mean(y * y, axis=-1, keepdims=True) + RMS_EPS) * gpost_ref[...]
    o_ref[0] = x_ref[0] + mod_ref[0, 2:3, :] * yn


def _merge(x, mod, g_post, o_rw, o_na, o_df, z, w_branch, w_out, tm):
    B, T, D = x.shape
    per_batch = mod.shape[0] > 1
    bw = BRANCH_WIDTH
    row = lambda b, m: (b, m, 0)
    col = lambda c: (lambda b, m: (b, m, c))
    return pl.pallas_call(
        _merge_kernel,
        grid=(B, T // tm),
        in_specs=[pl.BlockSpec((1, tm, D), row),
                  pl.BlockSpec((1, 3, D), (lambda b, m: (b, 0, 0)) if per_batch else (lambda b, m: (0, 0, 0))),
                  pl.BlockSpec((1, D), lambda b, m: (0, 0)),
                  pl.BlockSpec((1, tm, bw), row), pl.BlockSpec((1, tm, bw), row), pl.BlockSpec((1, tm, bw), row),
                  pl.BlockSpec((1, tm, bw), col(COL_RG // bw)),
                  pl.BlockSpec((1, tm, bw), col(COL_NG // bw)),
                  pl.BlockSpec((1, tm, bw), col(COL_DG // bw)),
                  pl.BlockSpec((1, tm, N_BRANCH * D), col(COL_MG // (N_BRANCH * D))),
                  pl.BlockSpec((N_BRANCH, bw, D), lambda b, m: (0, 0, 0)),
                  pl.BlockSpec((D, D), lambda b, m: (0, 0))],
        out_specs=pl.BlockSpec((1, tm, D), row),
        out_shape=jax.ShapeDtypeStruct((B, T, D), F32),
        compiler_params=_cparams(("parallel", "parallel")),
        name="merge",
    )(x, mod, g_post.reshape(1, D), o_rw, o_na, o_df, z, z, z, z, w_branch, w_out)


def _qkv_prep_kernel(*refs, scale, rope):
    if rope:
        q_ref, k_ref, v_ref, cos_ref, sa_ref, sb_ref, qo_ref, ko_ref, vo_ref = refs
    else:
        q_ref, k_ref, v_ref, qo_ref, ko_ref, vo_ref = refs
    q, k = q_ref[0], k_ref[0]
    if rope:
        w = q.shape[-1]
        cos, sa, sb = cos_ref[...], sa_ref[...], sb_ref[...]
        rot = lambda t: t * cos + pltpu.roll(t, w - 16, 1) * sa + pltpu.roll(t, 16, 1) * sb
        q, k = rot(q), rot(k)
    qo_ref[0] = (q * scale).astype(BF16)
    ko_ref[0] = k.astype(BF16)
    vo_ref[0] = v_ref[0].astype(BF16)


def _qkv_prep(z, col, scale, tables, tm):
    B, T, _ = z.shape
    w = 512
    blk = lambda j: pl.BlockSpec((1, tm, w), lambda b, m: (b, m, col // w + j))
    in_specs = [blk(0), blk(1), blk(2)]
    args = [z, z, z]
    if tables is not None:
        in_specs += [pl.BlockSpec((tm, w), lambda b, m: (m, 0))] * 3
        args += list(tables)
    out = jax.ShapeDtypeStruct((B, T, w), BF16)
    return pl.pallas_call(
        functools.partial(_qkv_prep_kernel, scale=scale, rope=tables is not None),
        grid=(B, T // tm),
        in_specs=in_specs,
        out_specs=[pl.BlockSpec((1, tm, w), lambda b, m: (b, m, 0))] * 3,
        out_shape=[out, out, out],
        compiler_params=_cparams(("parallel", "parallel")),
        name="qkv_prep",
    )(*args)


def _rope_tables(n_tokens):
    t = np.arange(n_tokens)
    axis_dim = DIFF_QK_DIM // 2
    inv = ROPE_THETA ** (-np.arange(0, axis_dim, 2, dtype=np.float32) / axis_dim)
    ar = (t // GRID_W).astype(np.float32)[:, None] * inv
    ac = (t % GRID_W).astype(np.float32)[:, None] * inv
    ang = jnp.asarray(np.concatenate([ar, ar, ac, ac], axis=-1).astype(np.float32))
    cos, sin = jnp.cos(ang), jnp.sin(ang)
    first = (np.arange(DIFF_QK_DIM) % 32) < 16
    sin_a = jnp.where(first, -sin, 0.0)
    sin_b = jnp.where(first, 0.0, sin)
    tile = lambda a: jnp.tile(a, (1, 8))
    return tile(cos), tile(sin_a), tile(sin_b)


def _dot_nt(a, b):
    return lax.dot_general(a, b, (((1,), (1,)), ((), ())), preferred_element_type=F32)


def _diff_attn_kernel(q_ref, k_ref, v_ref, lq_ref, lk_ref, g_ref, o_ref, *, tk, lambda_init):
    tq = q_ref.shape[1]
    nkv = k_ref.shape[1] // tk
    q = q_ref[0]
    lo = lax.broadcasted_iota(jnp.int32, (1, 128), 1) < DIFF_QK_DIM
    zero = jnp.zeros_like(q)
    q1 = jnp.where(lo, q, zero)
    q2 = jnp.where(lo, zero, q)

    def body(j, carry):
        m1, l1, a1, m2, l2, a2 = carry
        k = k_ref[0, pl.ds(pl.multiple_of(j * tk, tk), tk), :]
        v = v_ref[0, pl.ds(pl.multiple_of(j * tk, tk), tk), :]

        def upd(qm, m, l, a):
            s = _dot_nt(qm, k)
            mn = jnp.maximum(m, jnp.max(s, axis=-1, keepdims=True))
            al = jnp.exp(m - mn)
            p = jnp.exp(s - mn)
            return (mn, al * l + jnp.sum(p, axis=-1, keepdims=True),
                    al * a + jnp.dot(p.astype(BF16), v, preferred_element_type=F32))

        m1, l1, a1 = upd(q1, m1, l1, a1)
        m2, l2, a2 = upd(q2, m2, l2, a2)
        return m1, l1, a1, m2, l2, a2

    neg = jnp.full((tq, 1), NEG_INF, F32)
    z1 = jnp.zeros((tq, 1), F32)
    za = jnp.zeros((tq, DIFF_V_DIM), F32)
    m1, l1, a1, m2, l2, a2 = lax.fori_loop(0, nkv, body, (neg, z1, za, neg, z1, za))
    lqk = lq_ref[...] * lk_ref[...]
    e = jnp.exp(jnp.sum(lqk, axis=-1, keepdims=True))
    lam = e[0:1] - e[1:2] + lambda_init
    o = a1 / l1 - lam * (a2 / l2)
    o = o * lax.rsqrt(jnp.mean(o * o, axis=-1, keepdims=True) + SUBLN_EPS) * g_ref[...]
    o_ref[0] = o * (1.0 - lambda_init)


def _diff_attn(q, k, v, lam_q, lam_k, subln_g, lambda_init, tq, tk):
    B, T, _ = q.shape
    Tk = k.shape[1]
    kv = pl.BlockSpec((1, Tk, 128), lambda b, h, m: (b, 0, h))
    small = lambda r, c: pl.BlockSpec((r, c), lambda b, h, m: (0, 0))
    return pl.pallas_call(
        functools.partial(_diff_attn_kernel, tk=tk, lambda_init=lambda_init),
        grid=(B, DIFF_HEADS, T // tq),
        in_specs=[pl.BlockSpec((1, tq, 128), lambda b, h, m: (b, m, h)), kv, kv,
                  small(2, DIFF_QK_DIM), small(2, DIFF_QK_DIM), small(1, DIFF_V_DIM)],
        out_specs=pl.BlockSpec((1, tq, 128), lambda b, h, m: (b, m, h)),
        out_shape=jax.ShapeDtypeStruct((B, T, DIFF_HEADS * DIFF_V_DIM), F32),
        compiler_params=_cparams(("parallel", "parallel", "parallel")),
        name="diff_attn",
    )(q, k, v, lam_q, lam_k, subln_g.reshape(1, DIFF_V_DIM))


def _na_bias_tables(rpb):
    c_idx = np.arange(GRID_W)
    c_start = np.clip(c_idx - NA_WIN_C // 2, 0, GRID_W - NA_WIN_C)
    col_ok = (c_idx[None, :] >= c_start[:, None]) & (c_idx[None, :] < c_start[:, None] + NA_WIN_C)
    dc = np.clip(c_idx[None, :] - c_idx[:, None], -(NA_WIN_C - 1), NA_WIN_C - 1) + (NA_WIN_C - 1)
    H = rpb.shape[0]
    e = jnp.take(rpb, jnp.asarray(dc.reshape(-1)), axis=2).reshape(H, 2 * NA_WIN_R - 1, GRID_W, GRID_W)
    e = jnp.where(col_ok, e, NEG_INF)
    tabs = [e[:, NA_WIN_R - 1 - off:2 * NA_WIN_R - 1 - off].transpose(0, 2, 1, 3).reshape(H, GRID_W, NA_WIN_R * GRID_W)
            for off in range(NA_WIN_R)]
    return jnp.stack(tabs, axis=1)


def _na_kernel(q_ref, k_ref, v_ref, kc_ref, vc_ref, bias_ref, o_ref, *, rq, rows):
    i = pl.program_id(2)
    lo = lax.broadcasted_iota(jnp.int32, (1, 128), 1) < HEAD_DIM
    kc, vc = kc_ref[0], vc_ref[0]
    win = NA_WIN_R * GRID_W

    def body(rr, _):
        r = i * rq + rr
        r_start = jnp.clip(r - NA_WIN_R // 2, 0, rows - NA_WIN_R)
        off = r - r_start
        q = q_ref[0, pl.ds(pl.multiple_of(rr * GRID_W, GRID_W), GRID_W), :]
        kw = k_ref[0, pl.ds(pl.multiple_of(r_start * GRID_W, GRID_W), win), :]
        vw = v_ref[0, pl.ds(pl.multiple_of(r_start * GRID_W, GRID_W), win), :]
        zero = jnp.zeros_like(q)
        outs = []
        for hl in range(2):
            qh = jnp.where(lo, q, zero) if hl == 0 else jnp.where(lo, zero, q)
            s_nb = _dot_nt(qh, kw) + bias_ref[hl, off]
            s_c = _dot_nt(qh, kc)
            m = jnp.maximum(jnp.max(s_nb, axis=-1, keepdims=True), jnp.max(s_c, axis=-1, keepdims=True))
            p_nb = jnp.exp(s_nb - m)
            p_c = jnp.exp(s_c - m)
            den = jnp.sum(p_nb, axis=-1, keepdims=True) + jnp.sum(p_c, axis=-1, keepdims=True)
            o = (jnp.dot(p_nb.astype(BF16), vw, preferred_element_type=F32)
                 + jnp.dot(p_c.astype(BF16), vc, preferred_element_type=F32))
            outs.append(o / den)
        o_ref[0, pl.ds(pl.multiple_of(rr * GRID_W, GRID_W), GRID_W), :] = jnp.where(lo, outs[0], outs[1])
        return 0

    lax.fori_loop(0, rq, body, 0)


def _na_attn(q, k, v, kc, vc, bias, rq):
    B, S, _ = q.shape
    C = kc.shape[1]
    rows = S // GRID_W
    full = lambda t: pl.BlockSpec((1, t, 128), lambda b, p, m: (b, 0, p))
    return pl.pallas_call(
        functools.partial(_na_kernel, rq=rq, rows=rows),
        grid=(B, NA_HEADS // 2, rows // rq),
        in_specs=[pl.BlockSpec((1, rq * GRID_W, 128), lambda b, p, m: (b, m, p)),
                  full(S), full(S), full(C), full(C),
                  pl.BlockSpec((2, NA_WIN_R, GRID_W, NA_WIN_R * GRID_W), lambda b, p, m: (p, 0, 0, 0))],
        out_specs=pl.BlockSpec((1, rq * GRID_W, 128), lambda b, p, m: (b, m, p)),
        out_shape=jax.ShapeDtypeStruct((B, S, NA_HEADS * HEAD_DIM), F32),
        compiler_params=_cparams(("parallel", "parallel", "parallel")),
        name="na_attn",
    )(q, k, v, kc, vc, bias)


def _ctx_attn_kernel(q_ref, k_ref, v_ref, o_ref):
    lo = lax.broadcasted_iota(jnp.int32, (1, 128), 1) < HEAD_DIM
    q, k, v = q_ref[0], k_ref[0], v_ref[0]
    zero = jnp.zeros_like(q)
    outs = []
    for hl in range(2):
        qh = jnp.where(lo, q, zero) if hl == 0 else jnp.where(lo, zero, q)
        s = _dot_nt(qh, k)
        p = jnp.exp(s - jnp.max(s, axis=-1, keepdims=True))
        o = jnp.dot(p.astype(BF16), v, preferred_element_type=F32)
        outs.append(o / jnp.sum(p, axis=-1, keepdims=True))
    o_ref[0] = jnp.where(lo, outs[0], outs[1])


def _ctx_attn(q, k, v):
    B, C, _ = q.shape
    blk = pl.BlockSpec((1, C, 128), lambda b, p: (b, 0, p))
    return pl.pallas_call(
        _ctx_attn_kernel,
        grid=(B, NA_HEADS // 2),
        in_specs=[blk, blk, blk],
        out_specs=blk,
        out_shape=jax.ShapeDtypeStruct((B, C, NA_HEADS * HEAD_DIM), F32),
        compiler_params=_cparams(("parallel", "parallel")),
        name="ctx_attn",
    )(q, k, v)


PL_R, PL_V, PL_KK = 0, 1, 2
PL_LOGW, PL_KDIR, PL_B = 3, 4, 5
N_PLANES = 9
RWKV_CHUNK = 64


def _split3(x):
    hi = x.astype(BF16)
    r1 = x - hi.astype(F32)
    mid = r1.astype(BF16)
    lo = (r1 - mid.astype(F32)).astype(BF16)
    return hi, mid, lo


def _dot_exact_rhs(m, x):
    hi, mid, lo = _split3(x)
    mb = m.astype(BF16)
    d = lambda t: jnp.dot(mb, t, preferred_element_type=F32)
    return d(hi) + d(mid) + d(lo)


def _dot_exact_lhs(x, m):
    hi, mid, lo = _split3(x)
    mb = m.astype(BF16)
    d = lambda t: jnp.dot(t, mb, preferred_element_type=F32)
    return d(hi) + d(mid) + d(lo)


def _head_ones(n):
    r = lax.broadcasted_iota(jnp.int32, (n, n), 0) // HEAD_DIM
    c = lax.broadcasted_iota(jnp.int32, (n, n), 1) // HEAD_DIM
    return (r == c).astype(F32)


def _rwkv_prep_kernel(zc_ref, zp_ref, zn_ref, mu_ref, kk_ref, ka_ref, w0_ref, a0_ref, wup_ref, aup_ref, o_ref):
    tm = zc_ref.shape[1]
    m = pl.program_id(1)
    u = zc_ref[0]
    prev_row = jnp.where(m > 0, zp_ref[0, 7:8, :], 0.0)
    next_row = jnp.where(m < pl.num_programs(1) - 1, zn_ref[0, 0:1, :], 0.0)
    rows = lax.broadcasted_iota(jnp.int32, (tm, 1), 0)
    u_prev = jnp.where(rows == 0, prev_row, pltpu.roll(u, 1, 0))
    u_next = jnp.where(rows == tm - 1, next_row, pltpu.roll(u, tm - 1, 0))
    u = u + mu_ref[0:1, :] * (u_prev - u) + mu_ref[1:2, :] * (u_next - u)

    W = RWKV_WIDTH
    r, k, v = u[:, 0:W], u[:, W:2 * W], u[:, 2 * W:3 * W]
    lw = jnp.tanh(u[:, 3 * W:3 * W + 2 * LORA]).astype(BF16)
    la = u[:, 3 * W + 2 * LORA:3 * W + 4 * LORA].astype(BF16)
    ones = _head_ones(W)
    kk = k * kk_ref[...]
    ss = _dot_exact_lhs(kk * kk, ones)
    kk = kk / jnp.maximum(jnp.sqrt(ss), 1e-12)
    o_ref[0, :, PL_R * W:(PL_R + 1) * W] = r
    o_ref[0, :, PL_V * W:(PL_V + 1) * W] = v
    o_ref[0, :, PL_KK * W:(PL_KK + 1) * W] = kk
    for d in range(2):
        xw = w0_ref[d:d + 1, :] + jnp.dot(lw, wup_ref[d], preferred_element_type=F32)
        logw = -math.exp(-0.5) * _sigmoid(xw)
        a = _sigmoid(a0_ref[d:d + 1, :] + jnp.dot(la, aup_ref[d], preferred_element_type=F32))
        kdir = k * (1.0 + (a - 1.0) * ka_ref[...])
        base = 3 * d
        o_ref[0, :, (PL_LOGW + base) * W:(PL_LOGW + base + 1) * W] = logw
        o_ref[0, :, (PL_KDIR + base) * W:(PL_KDIR + base + 1) * W] = kdir
        o_ref[0, :, (PL_B + base) * W:(PL_B + base + 1) * W] = kk * a


def _rwkv_prep(z, mu_pad, k_k, k_a, w0, a0, wup2, aup2, tm):
    B, T, _ = z.shape
    wz = RWKV_PAD_WIDTH
    cb = COL_RW // wz
    nb = tm // 8
    W = RWKV_WIDTH
    const = lambda shape: pl.BlockSpec(shape, lambda b, m: (0,) * len(shape))
    return pl.pallas_call(
        _rwkv_prep_kernel,
        grid=(B, T // tm),
        in_specs=[pl.BlockSpec((1, tm, wz), lambda b, m: (b, m, cb)),
                  pl.BlockSpec((1, 8, wz), lambda b, m: (b, jnp.maximum(m * nb - 1, 0), cb)),
                  pl.BlockSpec((1, 8, wz), lambda b, m: (b, jnp.minimum((m + 1) * nb, T // 8 - 1), cb)),
                  const((2, wz)), const((1, W)), const((1, W)), const((2, W)), const((2, W)),
                  const((2, 2 * LORA, W)), const((2, 2 * LORA, W))],
        out_specs=pl.BlockSpec((1, tm, N_PLANES * W), lambda b, m: (b, m, 0)),
        out_shape=jax.ShapeDtypeStruct((B, T, N_PLANES * W), F32),
        compiler_params=_cparams(("parallel", "parallel")),
        name="rwkv_prep",
    )(z, z, z, mu_pad, k_k.reshape(1, W), k_a.reshape(1, W), w0, a0, wup2, aup2)


def _rwkv_scan_kernel(pf_ref, pb_ref, s0_ref, yf_ref, yb_ref, s_ref):
    C = RWKV_CHUNK
    W = RWKV_WIDTH
    C2 = 2 * C

    @pl.when(pl.program_id(1) == 0)
    def _():
        s_ref[...] = s0_ref[...]

    lo = lax.broadcasted_iota(jnp.int32, (1, 128), 1) < HEAD_DIM
    row = lax.broadcasted_iota(jnp.int32, (C2, C2), 0)
    col = lax.broadcasted_iota(jnp.int32, (C2, C2), 1)
    same = (row // C) == (col // C)
    eye = (row == col).astype(F32)
    tri_r = lax.broadcasted_iota(jnp.int32, (C, C), 0)
    tri_c = lax.broadcasted_iota(jnp.int32, (C, C), 1)
    levels = [((row >> k) == (col >> k)) & ((row >> (k - 1)) != (col >> (k - 1))) for k in range(1, 7)]
    p_refs, y_refs = (pf_ref, pb_ref), (yf_ref, yb_ref)
    groups = [(d, p) for d in range(2) for p in range(W // 128)]
    ng = len(groups)
    bf = lambda t: t.astype(BF16)

    def stack(x):
        return jnp.concatenate([jnp.where(lo, x, 0.0), jnp.where(lo, 0.0, x)], axis=0)

    c_dir = []
    for d in range(2):
        cum = ((tri_c <= tri_r) if d == 0 else (tri_c >= tri_r)).astype(F32)
        logw_all = p_refs[d][0, :, (PL_LOGW + 3 * d) * W:(PL_LOGW + 3 * d + 1) * W]
        c_dir.append(_dot_exact_rhs(cum, logw_all))

    ar, bk, ends, vs, decay_end = [], [], [], [], []
    for d, p in groups:
        get = lambda plane: p_refs[d][0, :, plane * W + p * 128:plane * W + (p + 1) * 128]
        r, v, kk = get(PL_R), get(PL_V), get(PL_KK)
        logw, kdir, bb = get(PL_LOGW + 3 * d), get(PL_KDIR + 3 * d), get(PL_B + 3 * d)
        c = c_dir[d][:, p * 128:(p + 1) * 128]
        last = C - 1 if d == 0 else 0
        c_last = c[last:last + 1, :]
        e_neg = jnp.exp(-c)
        e_end = jnp.exp(c_last - c)
        ar.append(bf(jnp.concatenate([stack(-kk * jnp.exp(c - logw)), stack(r * jnp.exp(c))], axis=0)))
        bk.append(bf(jnp.concatenate([stack(bb * e_neg), stack(kdir * e_neg)], axis=0)))
        ends.append(bf(jnp.concatenate([stack(bb * e_end), stack(kdir * e_end)], axis=0)))
        vs.append(bf(stack(v)))
        decay_end.append(jnp.exp(c_last))

    amat = [_dot_nt(ar[g], bk[g]) for g in range(ng)]
    a_ab, a_kr, a_rb = [], [], []
    for g, (d, p) in enumerate(groups):
        before = (col < row) if d == 0 else (col > row)
        strict = same & before
        incl = same & (before | (row == col))
        m = amat[g]
        a_ab.append(jnp.where(strict, m[:C2, :C2], 0.0))
        a_kr.append(bf(jnp.concatenate([jnp.where(strict, m[:C2, C2:], 0.0),
                                        jnp.where(incl, m[C2:, C2:], 0.0)], axis=0)))
        a_rb.append(bf(jnp.where(incl, m[C2:, :C2], 0.0)))

    t = [eye + jnp.where(levels[0], a, 0.0) for a in a_ab]
    for lvl in levels[1:]:
        ta = [jnp.dot(bf(t[g]), bf(jnp.where(lvl, a_ab[g], 0.0)), preferred_element_type=F32) for g in range(ng)]
        t = [t[g] + jnp.dot(bf(ta[g]), bf(t[g]), preferred_element_type=F32) for g in range(ng)]

    s_old = [s_ref[0, d, p] for d, p in groups]
    from_s = [_dot_nt(ar[g], bf(s_old[g])) for g in range(ng)]
    from_v = [jnp.dot(a_kr[g], vs[g], preferred_element_type=F32) for g in range(ng)]
    u = [jnp.dot(bf(t[g]), bf(from_s[g][:C2] + from_v[g][:C2]), preferred_element_type=F32) for g in range(ng)]
    for g, (d, p) in enumerate(groups):
        y_s = from_s[g][C2:] + from_v[g][C2:] + jnp.dot(a_rb[g], bf(u[g]), preferred_element_type=F32)
        y_refs[d][0, :, p * 128:(p + 1) * 128] = y_s[:C] + y_s[C:]
    for g, (d, p) in enumerate(groups):
        uv = jnp.concatenate([bf(u[g]), vs[g]], axis=0)
        s_ref[0, d, p] = s_old[g] * decay_end[g] + lax.dot_general(
            uv, ends[g], (((0,), (0,)), ((), ())), preferred_element_type=F32)


def _rwkv_scan(planes, s0):
    B, T, _ = planes.shape
    nc = T // RWKV_CHUNK
    W = RWKV_WIDTH
    st = pl.BlockSpec((1, 2, W // 128, 128, 128), lambda b, i: (b, 0, 0, 0, 0))
    y = jax.ShapeDtypeStruct((B, T, W), F32)
    return pl.pallas_call(
        _rwkv_scan_kernel,
        grid=(B, nc),
        in_specs=[pl.BlockSpec((1, RWKV_CHUNK, N_PLANES * W), lambda b, i: (b, i, 0)),
                  pl.BlockSpec((1, RWKV_CHUNK, N_PLANES * W), lambda b, i: (b, nc - 1 - i, 0)),
                  st],
        out_specs=[pl.BlockSpec((1, RWKV_CHUNK, W), lambda b, i: (b, i, 0)),
                   pl.BlockSpec((1, RWKV_CHUNK, W), lambda b, i: (b, nc - 1 - i, 0)),
                   st],
        out_shape=[y, y, jax.ShapeDtypeStruct(s0.shape, F32)],
        compiler_params=_cparams(("parallel", "arbitrary")),
        name="rwkv_scan",
    )(planes, planes, s0)


def _rwkv_readout_kernel(yf_ref, yb_ref, r_ref, v_ref, kf_ref, kb_ref, g_ref, b_ref, rk_ref, o_ref):
    ones = _head_ones(RWKV_WIDTH)
    y = yf_ref[0] + yb_ref[0]
    inv_n = 1.0 / HEAD_DIM
    mu = _dot_exact_lhs(y, ones) * inv_n
    yc = y - mu
    var = _dot_exact_lhs(yc * yc, ones) * inv_n
    yn = yc * lax.rsqrt(var + GN_EPS) * g_ref[...] + b_ref[...]
    bonus = _dot_exact_lhs(r_ref[0] * (kf_ref[0] + kb_ref[0]) * rk_ref[...], ones)
    o_ref[0] = yn + bonus * v_ref[0]


def _rwkv_readout(y_f, y_b, planes, ln_g, ln_b, r_k, tm):
    B, T, W = y_f.shape
    row = pl.BlockSpec((1, tm, W), lambda b, m: (b, m, 0))
    plane = lambda j: pl.BlockSpec((1, tm, W), lambda b, m: (b, m, j))
    vec = pl.BlockSpec((1, W), lambda b, m: (0, 0))
    return pl.pallas_call(
        _rwkv_readout_kernel,
        grid=(B, T // tm),
        in_specs=[row, row, plane(PL_R), plane(PL_V), plane(PL_KDIR), plane(PL_KDIR + 3), vec, vec, vec],
        out_specs=row,
        out_shape=jax.ShapeDtypeStruct((B, T, W), F32),
        compiler_params=_cparams(("parallel", "parallel")),
        name="rwkv_readout",
    )(y_f, y_b, planes, planes, planes, planes, ln_g.reshape(1, W), ln_b.reshape(1, W), r_k.reshape(1, W))


def _layer_weights(l, w_in, shift_mu, w_up, a_up, w_branch, w_out):
    rw, rg, na, ng, df, dg, mg = jnp.split(w_in[l], [int(i) for i in np.cumsum(IN_SIZES)[:-1]], axis=-1)
    pad = jnp.zeros((w_in.shape[1], RWKV_PAD_WIDTH - RWKV_SHIFT_WIDTH), w_in.dtype)
    w_perm = jnp.concatenate([mg, na, df, rw, pad, rg, ng, dg], axis=-1).astype(BF16)
    mu_pad = jnp.pad(shift_mu[l], ((0, 0), (0, RWKV_PAD_WIDTH - RWKV_SHIFT_WIDTH)))
    zl = jnp.zeros((LORA, RWKV_WIDTH), F32)
    wup2 = jnp.stack([jnp.concatenate([w_up[l, 0], zl]), jnp.concatenate([zl, w_up[l, 1]])]).astype(BF16)
    aup2 = jnp.stack([jnp.concatenate([a_up[l, 0], zl]), jnp.concatenate([zl, a_up[l, 1]])]).astype(BF16)
    return w_perm, mu_pad, wup2, aup2, w_branch[l].astype(BF16), w_out[l].astype(BF16)


def kernel(x, c, ctx, c_ctx, w_mod, b_mod, g_pre, g_post, w_in, shift_mu, k_k, k_a, r_k, w0, w_up, a0, a_up,
           ln_x_g, ln_x_b, rpb, lam_q, lam_k, diff_subln, w_branch, w_out):
    B, S, D = x.shape
    C = ctx.shape[1]
    depth = w_in.shape[0]
    tables = _rope_tables(S)
    rows_pad = 16
    cvec = jnp.zeros((rows_pad, D), F32).at[:B].set(c).at[B].set(c_ctx)
    mod = _modulation(cvec, w_mod, b_mod)
    hc = ctx
    tm_x = min(1024, S)
    for l in range(depth):
        last = l == depth - 1
        lambda_init = 0.8 - 0.6 * math.exp(-0.3 * l)
        w_perm, mu_pad, wup2, aup2, wb, wo = _layer_weights(l, w_in, shift_mu, w_up, a_up, w_branch, w_out)
        mod_x = mod[l, :B].reshape(B, 3, D)
        mod_c = mod[l, B:B + 1].reshape(1, 3, D)
        z_x = _inproj(x, mod_x, g_pre[l], w_perm, tm_x)
        z_c = _inproj(hc, mod_c, g_pre[l], w_perm, C)

        pl_c = _rwkv_prep(z_c, mu_pad, k_k[l], k_a[l], w0[l], a0[l], wup2, aup2, min(256, C))
        pl_x = _rwkv_prep(z_x, mu_pad, k_k[l], k_a[l], w0[l], a0[l], wup2, aup2, 256)
        s_zero = jnp.zeros((B, 2, RWKV_WIDTH // 128, 128, 128), F32)
        yf_c, yb_c, s_ctx = _rwkv_scan(pl_c, s_zero)
        yf_x, yb_x, _ = _rwkv_scan(pl_x, s_ctx)
        r_k_flat = r_k[l].reshape(RWKV_WIDTH)
        o_rw_x = _rwkv_readout(yf_x, yb_x, pl_x, ln_x_g[l], ln_x_b[l], r_k_flat, 256)

        scale = HEAD_DIM ** -0.5
        nq_x, nk_x, nv_x = _qkv_prep(z_x, COL_NA, scale, None, 512)
        nq_c, nk_c, nv_c = _qkv_prep(z_c, COL_NA, scale, None, C)
        o_na_x = _na_attn(nq_x, nk_x, nv_x, nk_c, nv_c, _na_bias_tables(rpb[l]), 8)

        dq_x, dk_x, dv_x = _qkv_prep(z_x, COL_DF, scale, tables, 512)
        dq_c, dk_c, dv_c = _qkv_prep(z_c, COL_DF, scale, None, C)
        keys = jnp.concatenate([dk_c, dk_x], axis=1)
        vals = jnp.concatenate([dv_c, dv_x], axis=1)
        o_df_x = _diff_attn(dq_x, keys, vals, lam_q[l], lam_k[l], diff_subln[l], lambda_init, 512, C)

        x = _merge(x, mod_x, g_post[l], o_rw_x, o_na_x, o_df_x, z_x, wb, wo, 256)
        if not last:
            o_rw_c = _rwkv_readout(yf_c, yb_c, pl_c, ln_x_g[l], ln_x_b[l], r_k_flat, min(256, C))
            o_na_c = _ctx_attn(nq_c, nk_c, nv_c)
            o_df_c = _diff_attn(dq_c, dk_c, dv_c, lam_q[l], lam_k[l], diff_subln[l], lambda_init, C, C)
            hc = _merge(hc, mod_c, g_post[l], o_rw_c, o_na_c, o_df_c, z_c, wb, wo, min(256, C))
    return x
```

```python
import functools
import math

import jax
import jax.numpy as jnp
import numpy as np
from jax import lax
from jax.experimental import pallas as pl
from jax.experimental.pallas import tpu as pltpu

F32 = jnp.float32
BF16 = jnp.bfloat16

DEPTH = 4
GRID_W = 64
RWKV_HEADS = 8
HEAD_DIM = 64
RWKV_WIDTH = 512
LORA = 64
GN_EPS = 64e-5
NA_HEADS = 8
NA_WIN_R = 8
NA_WIN_C = 16
DIFF_HEADS = 4
DIFF_QK_DIM = 64
DIFF_V_DIM = 128
ROPE_THETA = 10000.0
SUBLN_EPS = 1e-5
RMS_EPS = 1e-6
NEG_INF = -1e30
LOG2E = math.log2(math.e)
BRANCH_WIDTH = 512
N_BRANCH = 3

RWKV_SHIFT_WIDTH = 3 * RWKV_WIDTH + 4 * LORA
RWKV_PAD_WIDTH = 2048
COL_MG, COL_NA, COL_DF, COL_RW, COL_RG, COL_NG, COL_DG = 0, 3072, 4608, 6144, 8192, 8704, 9216
Z_WIDTH = 9728
IN_SIZES = (RWKV_SHIFT_WIDTH, 512, 1536, 512, 1536, 512, 3072)

VMEM_LIMIT = 48 * 1024 * 1024


def _cparams(sem):
    return pltpu.CompilerParams(dimension_semantics=sem, vmem_limit_bytes=VMEM_LIMIT)


def _sigmoid(x):
    return 1.0 / (1.0 + jnp.exp(-x))


def _silu(x):
    return x * _sigmoid(x)


def _bdot(a, b):
    return jnp.dot(a.astype(BF16), b.astype(BF16), preferred_element_type=F32)


def _mod_kernel(c_ref, w_ref, b_ref, o_ref):
    o_ref[0] = _bdot(_silu(c_ref[...]), w_ref[0]) + b_ref[0]


def _modulation(cvec, w_mod, b_mod):
    L, D, N = w_mod.shape
    R = cvec.shape[0]
    tn = 1024
    return pl.pallas_call(
        _mod_kernel,
        grid=(L, N // tn),
        in_specs=[pl.BlockSpec((R, D), lambda l, n: (0, 0)),
                  pl.BlockSpec((1, D, tn), lambda l, n: (l, 0, n)),
                  pl.BlockSpec((1, 1, tn), lambda l, n: (l, 0, n))],
        out_specs=pl.BlockSpec((1, R, tn), lambda l, n: (l, 0, n)),
        out_shape=jax.ShapeDtypeStruct((L, R, N), F32),
        compiler_params=_cparams(("parallel", "parallel")),
        name="modulation",
    )(cvec, w_mod, b_mod.reshape(L, 1, N))


def _inproj_kernel(x_ref, mod_ref, g_ref, w_ref, o_ref, h_ref):
    @pl.when(pl.program_id(2) == 0)
    def _():
        x = x_ref[0]
        y = x * lax.rsqrt(jnp.mean(x * x, axis=-1, keepdims=True) + RMS_EPS) * g_ref[...]
        h_ref[...] = (y * (1.0 + mod_ref[0, 1:2, :]) + mod_ref[0, 0:1, :]).astype(BF16)

    o_ref[0] = jnp.dot(h_ref[...], w_ref[...], preferred_element_type=F32)


def _inproj(x, mod, g_pre, w_perm, tm):
    B, T, D = x.shape
    N = w_perm.shape[1]
    tn = 512
    per_batch = mod.shape[0] > 1
    return pl.pallas_call(
        _inproj_kernel,
        grid=(B, T // tm, N // tn),
        in_specs=[pl.BlockSpec((1, tm, D), lambda b, m, n: (b, m, 0)),
                  pl.BlockSpec((1, 3, D), (lambda b, m, n: (b, 0, 0)) if per_batch else (lambda b, m, n: (0, 0, 0))),
                  pl.BlockSpec((1, D), lambda b, m, n: (0, 0)),
                  pl.BlockSpec((D, tn), lambda b, m, n: (0, n))],
        out_specs=pl.BlockSpec((1, tm, tn), lambda b, m, n: (b, m, n)),
        out_shape=jax.ShapeDtypeStruct((B, T, N), F32),
        scratch_shapes=[pltpu.VMEM((tm, D), BF16)],
        compiler_params=_cparams(("parallel", "parallel", "arbitrary")),
        name="inproj",
    )(x, mod, g_pre.reshape(1, D), w_perm)


def _merge_kernel(x_ref, mod_ref, gpost_ref, orw_ref, ona_ref, odf_ref, rg_ref, ng_ref, dg_ref, mg_ref,
                  wb_ref, wo_ref, o_ref):
    D = x_ref.shape[-1]
    acc = None
    for n, (ob_ref, gate_ref) in enumerate(((orw_ref, rg_ref), (ona_ref, ng_ref), (odf_ref, dg_ref))):
        yb = _bdot(ob_ref[0] * _silu(gate_ref[0]), wb_ref[n])
        term = _sigmoid(mg_ref[0, :, n * D:(n + 1) * D]) * yb
        acc = term if acc is None else acc + term
    y = _bdot(acc, wo_ref[...])
    yn = y * lax.rsqrt(jnp.mean(y * y, axis=-1, keepdims=True) + RMS_EPS) * gpost_ref[...]
    o_ref[0] = x_ref[0] + mod_ref[0, 2:3, :] * yn


def _merge(x, mod, g_post, o_rw, o_na, o_df, z, w_branch, w_out, tm):
    B, T, D = x.shape
    per_batch = mod.shape[0] > 1
    bw = BRANCH_WIDTH
    row = lambda b, m: (b, m, 0)
    col = lambda c: (lambda b, m: (b, m, c))
    return pl.pallas_call(
        _merge_kernel,
        grid=(B, T // tm),
        in_specs=[pl.BlockSpec((1, tm, D), row),
                  pl.BlockSpec((1, 3, D), (lambda b, m: (b, 0, 0)) if per_batch else (lambda b, m: (0, 0, 0))),
                  pl.BlockSpec((1, D), lambda b, m: (0, 0)),
                  pl.BlockSpec((1, tm, bw), row), pl.BlockSpec((1, tm, bw), row), pl.BlockSpec((1, tm, bw), row),
                  pl.BlockSpec((1, tm, bw), col(COL_RG // bw)),
                  pl.BlockSpec((1, tm, bw), col(COL_NG // bw)),
                  pl.BlockSpec((1, tm, bw), col(COL_DG // bw)),
                  pl.BlockSpec((1, tm, N_BRANCH * D), col(COL_MG // (N_BRANCH * D))),
                  pl.BlockSpec((N_BRANCH, bw, D), lambda b, m: (0, 0, 0)),
                  pl.BlockSpec((D, D), lambda b, m: (0, 0))],
        out_specs=pl.BlockSpec((1, tm, D), row),
        out_shape=jax.ShapeDtypeStruct((B, T, D), F32),
        compiler_params=_cparams(("parallel", "parallel")),
        name="merge",
    )(x, mod, g_post.reshape(1, D), o_rw, o_na, o_df, z, z, z, z, w_branch, w_out)


def _qkv_prep_kernel(*refs, scale, rope):
    if rope:
        q_ref, k_ref, v_ref, cos_ref, sa_ref, sb_ref, qo_ref, ko_ref, vo_ref = refs
    else:
        q_ref, k_ref, v_ref, qo_ref, ko_ref, vo_ref = refs
    q, k = q_ref[0], k_ref[0]
    if rope:
        w = q.shape[-1]
        cos, sa, sb = cos_ref[...], sa_ref[...], sb_ref[...]
        rot = lambda t: t * cos + pltpu.roll(t, w - 16, 1) * sa + pltpu.roll(t, 16, 1) * sb
        q, k = rot(q), rot(k)
    qo_ref[0] = (q * scale).astype(BF16)
    ko_ref[0] = k.astype(BF16)
    vo_ref[0] = v_ref[0].astype(BF16)


def _qkv_prep(z, col, scale, tables, tm):
    B, T, _ = z.shape
    w = 512
    blk = lambda j: pl.BlockSpec((1, tm, w), lambda b, m: (b, m, col // w + j))
    in_specs = [blk(0), blk(1), blk(2)]
    args = [z, z, z]
    if tables is not None:
        in_specs += [pl.BlockSpec((tm, w), lambda b, m: (m, 0))] * 3
        args += list(tables)
    out = jax.ShapeDtypeStruct((B, T, w), BF16)
    return pl.pallas_call(
        functools.partial(_qkv_prep_kernel, scale=scale, rope=tables is not None),
        grid=(B, T // tm),
        in_specs=in_specs,
        out_specs=[pl.BlockSpec((1, tm, w), lambda b, m: (b, m, 0))] * 3,
        out_shape=[out, out, out],
        compiler_params=_cparams(("parallel", "parallel")),
        name="qkv_prep",
    )(*args)


def _rope_tables(n_tokens):
    t = np.arange(n_tokens)
    axis_dim = DIFF_QK_DIM // 2
    inv = ROPE_THETA ** (-np.arange(0, axis_dim, 2, dtype=np.float32) / axis_dim)
    ar = (t // GRID_W).astype(np.float32)[:, None] * inv
    ac = (t % GRID_W).astype(np.float32)[:, None] * inv
    ang = jnp.asarray(np.concatenate([ar, ar, ac, ac], axis=-1).astype(np.float32))
    cos, sin = jnp.cos(ang), jnp.sin(ang)
    first = (np.arange(DIFF_QK_DIM) % 32) < 16
    sin_a = jnp.where(first, -sin, 0.0)
    sin_b = jnp.where(first, 0.0, sin)
    tile = lambda a: jnp.tile(a, (1, 8))
    return tile(cos), tile(sin_a), tile(sin_b)


def _dot_nt(a, b):
    return lax.dot_general(a, b, (((1,), (1,)), ((), ())), preferred_element_type=F32)


SOFTMAX_ROWS = 128


def _diff_attn_kernel(q_ref, k_ref, v_ref, lq_ref, lk_ref, g_ref, o_ref, s0, s1, p0, p1, al0, al1, m_scr, acc_scr,
                      *, t_first, tk, lambda_init):
    tq = q_ref.shape[1]
    n_main = (k_ref.shape[1] - t_first) // tk
    q = q_ref[0]
    lo = lax.broadcasted_iota(jnp.int32, (1, 128), 1) < DIFF_QK_DIM
    zero = jnp.zeros_like(q)
    qq = jnp.concatenate([jnp.where(lo, q, zero), jnp.where(lo, zero, q)], axis=0)

    n_sub = 2 * tq // SOFTMAX_ROWS
    sub = lambda i: slice(i * SOFTMAX_ROWS, (i + 1) * SOFTMAX_ROWS)

    def scores(start, size, s_scr):
        s_scr[:, :size] = _dot_nt(qq, k_ref[0, pl.ds(start, size), :])

    def softmax(size, first, s_scr, p_scr, al_scr):
        w = min(128, size)
        cols = [slice(c * w, (c + 1) * w) for c in range(size // w)]
        for i in range(n_sub):
            mx = s_scr[sub(i), cols[0]]
            for c in cols[1:]:
                mx = jnp.maximum(mx, s_scr[sub(i), c])
            mn = jnp.broadcast_to(jnp.max(mx, axis=-1, keepdims=True), (SOFTMAX_ROWS, 128))
            if not first:
                mo = m_scr[sub(i), :]
                mn = jnp.maximum(mo, mn)
                al_scr[sub(i), :] = jnp.exp2(mo - mn)
            m_scr[sub(i), :] = mn
        for i in range(n_sub):
            mn = m_scr[sub(i), :w]
            for c in cols:
                p_scr[sub(i), c] = jnp.exp2(s_scr[sub(i), c] - mn).astype(BF16)

    def accumulate(start, size, first, p_scr, al_scr):
        v = v_ref[0, pl.ds(start, size), :]
        v_ext = jnp.concatenate([v, jnp.ones((size, 128), BF16)], axis=1)
        pv = jnp.dot(p_scr[:, :size], v_ext, preferred_element_type=F32)
        if first:
            acc_scr[...] = pv
        else:
            al = al_scr[...]
            acc_scr[...] = acc_scr[...] * jnp.concatenate([al, al], axis=1) + pv

    scores(0, t_first, s0)
    softmax(t_first, True, s0, p0, al0)
    accumulate(0, t_first, True, p0, al0)
    if n_main:
        main = lambda j: pl.multiple_of(t_first + j * tk, 64)
        scores(main(0), tk, s0)
        p1[...] = jnp.zeros_like(p1)
        al1[...] = jnp.ones_like(al1)

        def body(t, _):
            j = 2 * t
            scores(main(j + 1), tk, s1)
            softmax(tk, False, s0, p0, al0)
            accumulate(main(jnp.maximum(j - 1, 0)), tk, False, p1, al1)
            scores(main(jnp.minimum(j + 2, n_main - 1)), tk, s0)
            softmax(tk, False, s1, p1, al1)
            accumulate(main(j), tk, False, p0, al0)
            return 0

        lax.fori_loop(0, n_main // 2, body, 0)
        accumulate(main(n_main - 1), tk, False, p1, al1)
    lqk = lq_ref[...] * lk_ref[...]
    e = jnp.exp(jnp.sum(lqk, axis=-1, keepdims=True))
    lam = e[0:1] - e[1:2] + lambda_init
    o_maps = acc_scr[:, :DIFF_V_DIM] / acc_scr[:, DIFF_V_DIM:]
    o = o_maps[:tq] - lam * o_maps[tq:]
    o = o * lax.rsqrt(jnp.mean(o * o, axis=-1, keepdims=True) + SUBLN_EPS) * g_ref[...]
    o_ref[0] = o * (1.0 - lambda_init)


def _diff_attn(q, k, v, lam_q, lam_k, subln_g, lambda_init, tq, t_first, tk):
    B, T, _ = q.shape
    Tk = k.shape[1]
    assert (Tk - t_first) % (2 * tk) == 0 and (2 * tq) % SOFTMAX_ROWS == 0
    wmax = max(t_first, tk if Tk > t_first else 0)
    kv = pl.BlockSpec((1, Tk, 128), lambda b, h, m: (b, 0, h))
    small = lambda r, c: pl.BlockSpec((r, c), lambda b, h, m: (0, 0))
    stat = pltpu.VMEM((2 * tq, 128), F32)
    s_buf = pltpu.VMEM((2 * tq, wmax), F32)
    p_buf = pltpu.VMEM((2 * tq, wmax), BF16)
    return pl.pallas_call(
        functools.partial(_diff_attn_kernel, t_first=t_first, tk=tk, lambda_init=lambda_init),
        grid=(B, DIFF_HEADS, T // tq),
        in_specs=[pl.BlockSpec((1, tq, 128), lambda b, h, m: (b, m, h)), kv, kv,
                  small(2, DIFF_QK_DIM), small(2, DIFF_QK_DIM), small(1, DIFF_V_DIM)],
        out_specs=pl.BlockSpec((1, tq, 128), lambda b, h, m: (b, m, h)),
        out_shape=jax.ShapeDtypeStruct((B, T, DIFF_HEADS * DIFF_V_DIM), F32),
        scratch_shapes=[s_buf, s_buf, p_buf, p_buf, stat, stat, stat, pltpu.VMEM((2 * tq, 2 * DIFF_V_DIM), F32)],
        compiler_params=_cparams(("parallel", "parallel", "parallel")),
        name="diff_attn",
    )(q, k, v, lam_q, lam_k, subln_g.reshape(1, DIFF_V_DIM))


def _na_bias_tables(rpb):
    c_idx = np.arange(GRID_W)
    c_start = np.clip(c_idx - NA_WIN_C // 2, 0, GRID_W - NA_WIN_C)
    col_ok = (c_idx[None, :] >= c_start[:, None]) & (c_idx[None, :] < c_start[:, None] + NA_WIN_C)
    dc = np.clip(c_idx[None, :] - c_idx[:, None], -(NA_WIN_C - 1), NA_WIN_C - 1) + (NA_WIN_C - 1)
    H = rpb.shape[0]
    e = jnp.take(rpb, jnp.asarray(dc.reshape(-1)), axis=2).reshape(H, 2 * NA_WIN_R - 1, GRID_W, GRID_W)
    e = jnp.where(col_ok, e, NEG_INF)
    tabs = [e[:, NA_WIN_R - 1 - off:2 * NA_WIN_R - 1 - off].transpose(0, 2, 1, 3).reshape(H, GRID_W, NA_WIN_R * GRID_W)
            for off in range(NA_WIN_R)]
    return jnp.stack(tabs, axis=1)


def _na_kernel(q_ref, k_ref, v_ref, kc_ref, vc_ref, bias_ref, o_ref, *, rq, rows):
    i = pl.program_id(2)
    lo = lax.broadcasted_iota(jnp.int32, (1, 128), 1) < HEAD_DIM
    win = NA_WIN_R * GRID_W
    G2 = 2 * GRID_W
    q = q_ref[0]
    zero = jnp.zeros_like(q)
    q_lo, q_hi = jnp.where(lo, q, zero), jnp.where(lo, zero, q)
    qq = jnp.concatenate([x[rr * GRID_W:(rr + 1) * GRID_W] for rr in range(rq) for x in (q_lo, q_hi)], axis=0)
    kc = kc_ref[0]
    ones = lambda n: jnp.ones((n, 128), BF16)
    s_c = _dot_nt(qq, kc)

    starts, s_nb = [], []
    for rr in range(rq):
        r = i * rq + rr
        r_start = jnp.clip(r - NA_WIN_R // 2, 0, rows - NA_WIN_R)
        off = r - r_start
        start = pl.multiple_of(r_start * GRID_W, GRID_W)
        starts.append(start)
        bias = jnp.concatenate([bias_ref[0, off], bias_ref[1, off]], axis=0)
        s_nb.append(_dot_nt(qq[rr * G2:(rr + 1) * G2], k_ref[0, pl.ds(start, win), :]) + bias)

    def lane_blocks(t):
        w = min(128, t.shape[-1])
        return [t[:, c * w:(c + 1) * w] for c in range(t.shape[-1] // w)]

    p_nb, p_c = [], []
    for rr in range(rq):
        sc = s_c[rr * G2:(rr + 1) * G2]
        blocks = lane_blocks(s_nb[rr])
        mx = blocks[0]
        for b in blocks[1:]:
            mx = jnp.maximum(mx, b)
        m = jnp.maximum(jnp.max(mx, axis=-1, keepdims=True), jnp.max(sc, axis=-1, keepdims=True))
        p_nb.append(jnp.exp2(s_nb[rr] - m).astype(BF16))
        p_c.append(jnp.exp2(sc - m).astype(BF16))

    vc_ext = jnp.concatenate([vc_ref[0], ones(kc.shape[0])], axis=1)
    o_c = jnp.dot(jnp.concatenate(p_c, axis=0), vc_ext, preferred_element_type=F32)
    for rr in range(rq):
        vw_ext = jnp.concatenate([v_ref[0, pl.ds(starts[rr], win), :], ones(win)], axis=1)
        o = jnp.dot(p_nb[rr], vw_ext, preferred_element_type=F32) + o_c[rr * G2:(rr + 1) * G2]
        o = o[:, :128] / o[:, 128:]
        o_ref[0, rr * GRID_W:(rr + 1) * GRID_W, :] = jnp.where(lo, o[:GRID_W], o[GRID_W:])


def _na_attn(q, k, v, kc, vc, bias, rq):
    B, S, _ = q.shape
    C = kc.shape[1]
    rows = S // GRID_W
    full = lambda t: pl.BlockSpec((1, t, 128), lambda b, p, m: (b, 0, p))
    return pl.pallas_call(
        functools.partial(_na_kernel, rq=rq, rows=rows),
        grid=(B, NA_HEADS // 2, rows // rq),
        in_specs=[pl.BlockSpec((1, rq * GRID_W, 128), lambda b, p, m: (b, m, p)),
                  full(S), full(S), full(C), full(C),
                  pl.BlockSpec((2, NA_WIN_R, GRID_W, NA_WIN_R * GRID_W), lambda b, p, m: (p, 0, 0, 0))],
        out_specs=pl.BlockSpec((1, rq * GRID_W, 128), lambda b, p, m: (b, m, p)),
        out_shape=jax.ShapeDtypeStruct((B, S, NA_HEADS * HEAD_DIM), F32),
        compiler_params=_cparams(("parallel", "parallel", "parallel")),
        name="na_attn",
    )(q, k, v, kc, vc, bias)


def _ctx_attn_kernel(q_ref, k_ref, v_ref, o_ref):
    lo = lax.broadcasted_iota(jnp.int32, (1, 128), 1) < HEAD_DIM
    q, k, v = q_ref[0], k_ref[0], v_ref[0]
    zero = jnp.zeros_like(q)
    outs = []
    for hl in range(2):
        qh = jnp.where(lo, q, zero) if hl == 0 else jnp.where(lo, zero, q)
        s = _dot_nt(qh, k)
        p = jnp.exp(s - jnp.max(s, axis=-1, keepdims=True))
        o = jnp.dot(p.astype(BF16), v, preferred_element_type=F32)
        outs.append(o / jnp.sum(p, axis=-1, keepdims=True))
    o_ref[0] = jnp.where(lo, outs[0], outs[1])


def _ctx_attn(q, k, v):
    B, C, _ = q.shape
    blk = pl.BlockSpec((1, C, 128), lambda b, p: (b, 0, p))
    return pl.pallas_call(
        _ctx_attn_kernel,
        grid=(B, NA_HEADS // 2),
        in_specs=[blk, blk, blk],
        out_specs=blk,
        out_shape=jax.ShapeDtypeStruct((B, C, NA_HEADS * HEAD_DIM), F32),
        compiler_params=_cparams(("parallel", "parallel")),
        name="ctx_attn",
    )(q, k, v)


PL_R, PL_V, PL_KK = 0, 1, 2
PL_LOGW, PL_KDIR, PL_B = 3, 4, 5
N_PLANES = 9
RWKV_CHUNK = 64


def _split3(x):
    hi = x.astype(BF16)
    r1 = x - hi.astype(F32)
    mid = r1.astype(BF16)
    lo = (r1 - mid.astype(F32)).astype(BF16)
    return hi, mid, lo


def _dot_exact_rhs(m, x):
    hi, mid, lo = _split3(x)
    mb = m.astype(BF16)
    d = lambda t: jnp.dot(mb, t, preferred_element_type=F32)
    return d(hi) + d(mid) + d(lo)


def _dot_exact_lhs(x, m):
    hi, mid, lo = _split3(x)
    mb = m.astype(BF16)
    d = lambda t: jnp.dot(t, mb, preferred_element_type=F32)
    return d(hi) + d(mid) + d(lo)


def _head_ones(n):
    r = lax.broadcasted_iota(jnp.int32, (n, n), 0) // HEAD_DIM
    c = lax.broadcasted_iota(jnp.int32, (n, n), 1) // HEAD_DIM
    return (r == c).astype(F32)


def _rwkv_prep_kernel(zc_ref, zp_ref, zn_ref, mu_ref, kk_ref, ka_ref, w0_ref, a0_ref, wup_ref, aup_ref, o_ref):
    tm = zc_ref.shape[1]
    m = pl.program_id(1)
    u = zc_ref[0]
    prev_row = jnp.where(m > 0, zp_ref[0, 7:8, :], 0.0)
    next_row = jnp.where(m < pl.num_programs(1) - 1, zn_ref[0, 0:1, :], 0.0)
    rows = lax.broadcasted_iota(jnp.int32, (tm, 1), 0)
    u_prev = jnp.where(rows == 0, prev_row, pltpu.roll(u, 1, 0))
    u_next = jnp.where(rows == tm - 1, next_row, pltpu.roll(u, tm - 1, 0))
    u = u + mu_ref[0:1, :] * (u_prev - u) + mu_ref[1:2, :] * (u_next - u)

    W = RWKV_WIDTH
    r, k, v = u[:, 0:W], u[:, W:2 * W], u[:, 2 * W:3 * W]
    lw = jnp.tanh(u[:, 3 * W:3 * W + 2 * LORA]).astype(BF16)
    la = u[:, 3 * W + 2 * LORA:3 * W + 4 * LORA].astype(BF16)
    ones = _head_ones(W)
    kk = k * kk_ref[...]
    ss = _dot_exact_lhs(kk * kk, ones)
    kk = kk / jnp.maximum(jnp.sqrt(ss), 1e-12)
    o_ref[0, :, PL_R * W:(PL_R + 1) * W] = r
    o_ref[0, :, PL_V * W:(PL_V + 1) * W] = v
    o_ref[0, :, PL_KK * W:(PL_KK + 1) * W] = kk
    for d in range(2):
        xw = w0_ref[d:d + 1, :] + jnp.dot(lw, wup_ref[d], preferred_element_type=F32)
        logw = -math.exp(-0.5) * _sigmoid(xw)
        a = _sigmoid(a0_ref[d:d + 1, :] + jnp.dot(la, aup_ref[d], preferred_element_type=F32))
        kdir = k * (1.0 + (a - 1.0) * ka_ref[...])
        base = 3 * d
        o_ref[0, :, (PL_LOGW + base) * W:(PL_LOGW + base + 1) * W] = logw
        o_ref[0, :, (PL_KDIR + base) * W:(PL_KDIR + base + 1) * W] = kdir
        o_ref[0, :, (PL_B + base) * W:(PL_B + base + 1) * W] = kk * a


def _rwkv_prep(z, mu_pad, k_k, k_a, w0, a0, wup2, aup2, tm):
    B, T, _ = z.shape
    wz = RWKV_PAD_WIDTH
    cb = COL_RW // wz
    nb = tm // 8
    W = RWKV_WIDTH
    const = lambda shape: pl.BlockSpec(shape, lambda b, m: (0,) * len(shape))
    return pl.pallas_call(
        _rwkv_prep_kernel,
        grid=(B, T // tm),
        in_specs=[pl.BlockSpec((1, tm, wz), lambda b, m: (b, m, cb)),
                  pl.BlockSpec((1, 8, wz), lambda b, m: (b, jnp.maximum(m * nb - 1, 0), cb)),
                  pl.BlockSpec((1, 8, wz), lambda b, m: (b, jnp.minimum((m + 1) * nb, T // 8 - 1), cb)),
                  const((2, wz)), const((1, W)), const((1, W)), const((2, W)), const((2, W)),
                  const((2, 2 * LORA, W)), const((2, 2 * LORA, W))],
        out_specs=pl.BlockSpec((1, tm, N_PLANES * W), lambda b, m: (b, m, 0)),
        out_shape=jax.ShapeDtypeStruct((B, T, N_PLANES * W), F32),
        compiler_params=_cparams(("parallel", "parallel")),
        name="rwkv_prep",
    )(z, z, z, mu_pad, k_k.reshape(1, W), k_a.reshape(1, W), w0, a0, wup2, aup2)


def _rwkv_scan_kernel(pf_ref, pb_ref, s0_ref, yf_ref, yb_ref, s_ref):
    C = RWKV_CHUNK
    W = RWKV_WIDTH
    C2 = 2 * C

    @pl.when(pl.program_id(1) == 0)
    def _():
        s_ref[...] = s0_ref[...]

    lo = lax.broadcasted_iota(jnp.int32, (1, 128), 1) < HEAD_DIM
    row = lax.broadcasted_iota(jnp.int32, (C2, C2), 0)
    col = lax.broadcasted_iota(jnp.int32, (C2, C2), 1)
    same = (row // C) == (col // C)
    eye = (row == col).astype(F32)
    tri_r = lax.broadcasted_iota(jnp.int32, (C, C), 0)
    tri_c = lax.broadcasted_iota(jnp.int32, (C, C), 1)
    levels = [((row >> k) == (col >> k)) & ((row >> (k - 1)) != (col >> (k - 1))) for k in range(1, 7)]
    p_refs, y_refs = (pf_ref, pb_ref), (yf_ref, yb_ref)
    groups = [(d, p) for d in range(2) for p in range(W // 128)]
    ng = len(groups)
    bf = lambda t: t.astype(BF16)

    def stack(x):
        return jnp.concatenate([jnp.where(lo, x, 0.0), jnp.where(lo, 0.0, x)], axis=0)

    c_dir = []
    for d in range(2):
        cum = ((tri_c <= tri_r) if d == 0 else (tri_c >= tri_r)).astype(F32)
        logw_all = p_refs[d][0, :, (PL_LOGW + 3 * d) * W:(PL_LOGW + 3 * d + 1) * W]
        c_dir.append(_dot_exact_rhs(cum, logw_all))

    ar, bk, ends, vs, decay_end = [], [], [], [], []
    for d, p in groups:
        get = lambda plane: p_refs[d][0, :, plane * W + p * 128:plane * W + (p + 1) * 128]
        r, v, kk = get(PL_R), get(PL_V), get(PL_KK)
        logw, kdir, bb = get(PL_LOGW + 3 * d), get(PL_KDIR + 3 * d), get(PL_B + 3 * d)
        c = c_dir[d][:, p * 128:(p + 1) * 128]
        last = C - 1 if d == 0 else 0
        c_last = c[last:last + 1, :]
        e_neg = jnp.exp(-c)
        e_end = jnp.exp(c_last - c)
        ar.append(bf(jnp.concatenate([stack(-kk * jnp.exp(c - logw)), stack(r * jnp.exp(c))], axis=0)))
        bk.append(bf(jnp.concatenate([stack(bb * e_neg), stack(kdir * e_neg)], axis=0)))
        ends.append(bf(jnp.concatenate([stack(bb * e_end), stack(kdir * e_end)], axis=0)))
        vs.append(bf(stack(v)))
        decay_end.append(jnp.exp(c_last))

    amat = [_dot_nt(ar[g], bk[g]) for g in range(ng)]
    a_ab, a_kr, a_rb = [], [], []
    for g, (d, p) in enumerate(groups):
        before = (col < row) if d == 0 else (col > row)
        strict = same & before
        incl = same & (before | (row == col))
        m = amat[g]
        a_ab.append(jnp.where(strict, m[:C2, :C2], 0.0))
        a_kr.append(bf(jnp.concatenate([jnp.where(strict, m[:C2, C2:], 0.0),
                                        jnp.where(incl, m[C2:, C2:], 0.0)], axis=0)))
        a_rb.append(bf(jnp.where(incl, m[C2:, :C2], 0.0)))

    t = [eye + jnp.where(levels[0], a, 0.0) for a in a_ab]
    for lvl in levels[1:]:
        ta = [jnp.dot(bf(t[g]), bf(jnp.where(lvl, a_ab[g], 0.0)), preferred_element_type=F32) for g in range(ng)]
        t = [t[g] + jnp.dot(bf(ta[g]), bf(t[g]), preferred_element_type=F32) for g in range(ng)]

    s_old = [s_ref[0, d, p] for d, p in groups]
    from_s = [_dot_nt(ar[g], bf(s_old[g])) for g in range(ng)]
    from_v = [jnp.dot(a_kr[g], vs[g], preferred_element_type=F32) for g in range(ng)]
    u = [jnp.dot(bf(t[g]), bf(from_s[g][:C2] + from_v[g][:C2]), preferred_element_type=F32) for g in range(ng)]
    for g, (d, p) in enumerate(groups):
        y_s = from_s[g][C2:] + from_v[g][C2:] + jnp.dot(a_rb[g], bf(u[g]), preferred_element_type=F32)
        y_refs[d][0, :, p * 128:(p + 1) * 128] = y_s[:C] + y_s[C:]
    for g, (d, p) in enumerate(groups):
        uv = jnp.concatenate([bf(u[g]), vs[g]], axis=0)
        s_ref[0, d, p] = s_old[g] * decay_end[g] + lax.dot_general(
            uv, ends[g], (((0,), (0,)), ((), ())), preferred_element_type=F32)


def _rwkv_scan(planes, s0):
    B, T, _ = planes.shape
    nc = T // RWKV_CHUNK
    W = RWKV_WIDTH
    st = pl.BlockSpec((1, 2, W // 128, 128, 128), lambda b, i: (b, 0, 0, 0, 0))
    y = jax.ShapeDtypeStruct((B, T, W), F32)
    return pl.pallas_call(
        _rwkv_scan_kernel,
        grid=(B, nc),
        in_specs=[pl.BlockSpec((1, RWKV_CHUNK, N_PLANES * W), lambda b, i: (b, i, 0)),
                  pl.BlockSpec((1, RWKV_CHUNK, N_PLANES * W), lambda b, i: (b, nc - 1 - i, 0)),
                  st],
        out_specs=[pl.BlockSpec((1, RWKV_CHUNK, W), lambda b, i: (b, i, 0)),
                   pl.BlockSpec((1, RWKV_CHUNK, W), lambda b, i: (b, nc - 1 - i, 0)),
                   st],
        out_shape=[y, y, jax.ShapeDtypeStruct(s0.shape, F32)],
        compiler_params=_cparams(("parallel", "arbitrary")),
        name="rwkv_scan",
    )(planes, planes, s0)


def _rwkv_readout_kernel(yf_ref, yb_ref, r_ref, v_ref, kf_ref, kb_ref, g_ref, b_ref, rk_ref, o_ref):
    ones = _head_ones(RWKV_WIDTH)
    y = yf_ref[0] + yb_ref[0]
    inv_n = 1.0 / HEAD_DIM
    mu = _dot_exact_lhs(y, ones) * inv_n
    yc = y - mu
    var = _dot_exact_lhs(yc * yc, ones) * inv_n
    yn = yc * lax.rsqrt(var + GN_EPS) * g_ref[...] + b_ref[...]
    bonus = _dot_exact_lhs(r_ref[0] * (kf_ref[0] + kb_ref[0]) * rk_ref[...], ones)
    o_ref[0] = yn + bonus * v_ref[0]


def _rwkv_readout(y_f, y_b, planes, ln_g, ln_b, r_k, tm):
    B, T, W = y_f.shape
    row = pl.BlockSpec((1, tm, W), lambda b, m: (b, m, 0))
    plane = lambda j: pl.BlockSpec((1, tm, W), lambda b, m: (b, m, j))
    vec = pl.BlockSpec((1, W), lambda b, m: (0, 0))
    return pl.pallas_call(
        _rwkv_readout_kernel,
        grid=(B, T // tm),
        in_specs=[row, row, plane(PL_R), plane(PL_V), plane(PL_KDIR), plane(PL_KDIR + 3), vec, vec, vec],
        out_specs=row,
        out_shape=jax.ShapeDtypeStruct((B, T, W), F32),
        compiler_params=_cparams(("parallel", "parallel")),
        name="rwkv_readout",
    )(y_f, y_b, planes, planes, planes, planes, ln_g.reshape(1, W), ln_b.reshape(1, W), r_k.reshape(1, W))


def _layer_weights(l, w_in, shift_mu, w_up, a_up, w_branch, w_out):
    rw, rg, na, ng, df, dg, mg = jnp.split(w_in[l], [int(i) for i in np.cumsum(IN_SIZES)[:-1]], axis=-1)
    pad = jnp.zeros((w_in.shape[1], RWKV_PAD_WIDTH - RWKV_SHIFT_WIDTH), w_in.dtype)
    w_perm = jnp.concatenate([mg, na, df, rw, pad, rg, ng, dg], axis=-1).astype(BF16)
    mu_pad = jnp.pad(shift_mu[l], ((0, 0), (0, RWKV_PAD_WIDTH - RWKV_SHIFT_WIDTH)))
    zl = jnp.zeros((LORA, RWKV_WIDTH), F32)
    wup2 = jnp.stack([jnp.concatenate([w_up[l, 0], zl]), jnp.concatenate([zl, w_up[l, 1]])]).astype(BF16)
    aup2 = jnp.stack([jnp.concatenate([a_up[l, 0], zl]), jnp.concatenate([zl, a_up[l, 1]])]).astype(BF16)
    return w_perm, mu_pad, wup2, aup2, w_branch[l].astype(BF16), w_out[l].astype(BF16)


def kernel(x, c, ctx, c_ctx, w_mod, b_mod, g_pre, g_post, w_in, shift_mu, k_k, k_a, r_k, w0, w_up, a0, a_up,
           ln_x_g, ln_x_b, rpb, lam_q, lam_k, diff_subln, w_branch, w_out):
    B, S, D = x.shape
    C = ctx.shape[1]
    depth = w_in.shape[0]
    tables = _rope_tables(S)
    rows_pad = 16
    cvec = jnp.zeros((rows_pad, D), F32).at[:B].set(c).at[B].set(c_ctx)
    mod = _modulation(cvec, w_mod, b_mod)
    hc = ctx
    tm_x = min(1024, S)
    for l in range(depth):
        last = l == depth - 1
        lambda_init = 0.8 - 0.6 * math.exp(-0.3 * l)
        w_perm, mu_pad, wup2, aup2, wb, wo = _layer_weights(l, w_in, shift_mu, w_up, a_up, w_branch, w_out)
        mod_x = mod[l, :B].reshape(B, 3, D)
        mod_c = mod[l, B:B + 1].reshape(1, 3, D)
        z_x = _inproj(x, mod_x, g_pre[l], w_perm, tm_x)
        z_c = _inproj(hc, mod_c, g_pre[l], w_perm, C)

        pl_c = _rwkv_prep(z_c, mu_pad, k_k[l], k_a[l], w0[l], a0[l], wup2, aup2, min(256, C))
        pl_x = _rwkv_prep(z_x, mu_pad, k_k[l], k_a[l], w0[l], a0[l], wup2, aup2, 256)
        s_zero = jnp.zeros((B, 2, RWKV_WIDTH // 128, 128, 128), F32)
        yf_c, yb_c, s_ctx = _rwkv_scan(pl_c, s_zero)
        yf_x, yb_x, _ = _rwkv_scan(pl_x, s_ctx)
        r_k_flat = r_k[l].reshape(RWKV_WIDTH)
        o_rw_x = _rwkv_readout(yf_x, yb_x, pl_x, ln_x_g[l], ln_x_b[l], r_k_flat, 256)

        scale = HEAD_DIM ** -0.5
        nq_x, nk_x, nv_x = _qkv_prep(z_x, COL_NA, scale * LOG2E, None, 512)
        nq_c, nk_c, nv_c = _qkv_prep(z_c, COL_NA, scale, None, C)
        o_na_x = _na_attn(nq_x, nk_x, nv_x, nk_c, nv_c, _na_bias_tables(rpb[l]) * LOG2E, 8)

        dq_x, dk_x, dv_x = _qkv_prep(z_x, COL_DF, scale * LOG2E, tables, 512)
        dq_c, dk_c, dv_c = _qkv_prep(z_c, COL_DF, scale * LOG2E, None, C)
        keys = jnp.concatenate([dk_c, dk_x], axis=1)
        vals = jnp.concatenate([dv_c, dv_x], axis=1)
        o_df_x = _diff_attn(dq_x, keys, vals, lam_q[l], lam_k[l], diff_subln[l], lambda_init, 512, C, min(512, S // 2))

        x = _merge(x, mod_x, g_post[l], o_rw_x, o_na_x, o_df_x, z_x, wb, wo, 256)
        if not last:
            o_rw_c = _rwkv_readout(yf_c, yb_c, pl_c, ln_x_g[l], ln_x_b[l], r_k_flat, min(256, C))
            o_na_c = _ctx_attn(nq_c, nk_c, nv_c)
            o_df_c = _diff_attn(dq_c, dk_c, dv_c, lam_q[l], lam_k[l], diff_subln[l], lambda_init, C, C, C)
            hc = _merge(hc, mod_c, g_post[l], o_rw_c, o_na_c, o_df_c, z_c, wb, wo, min(256, C))
    return x
```

```python
import functools
import math

import jax
import jax.numpy as jnp
import numpy as np
from jax import lax
from jax.experimental import pallas as pl
from jax.experimental.pallas import tpu as pltpu

F32 = jnp.float32
BF16 = jnp.bfloat16

DEPTH = 4
GRID_W = 64
RWKV_HEADS = 8
HEAD_DIM = 64
RWKV_WIDTH = 512
LORA = 64
GN_EPS = 64e-5
NA_HEADS = 8
NA_WIN_R = 8
NA_WIN_C = 16
DIFF_HEADS = 4
DIFF_QK_DIM = 64
DIFF_V_DIM = 128
ROPE_THETA = 10000.0
SUBLN_EPS = 1e-5
RMS_EPS = 1e-6
NEG_INF = -1e30
LOG2E = math.log2(math.e)
BRANCH_WIDTH = 512
N_BRANCH = 3

RWKV_SHIFT_WIDTH = 3 * RWKV_WIDTH + 4 * LORA
RWKV_PAD_WIDTH = 2048
TILE_W = 512
TILE_RW, TILE_NA, TILE_DF = 9, 13, 16
Z_WIDTH = 19 * TILE_W
IN_SIZES = (RWKV_SHIFT_WIDTH, 512, 1536, 512, 1536, 512, 3072)

VMEM_LIMIT = 48 * 1024 * 1024


def _cparams(sem):
    return pltpu.CompilerParams(dimension_semantics=sem, vmem_limit_bytes=VMEM_LIMIT)


def _sigmoid(x):
    return 1.0 / (1.0 + jnp.exp(-x))


def _silu(x):
    return x * _sigmoid(x)


def _bdot(a, b):
    return jnp.dot(a.astype(BF16), b.astype(BF16), preferred_element_type=F32)


def _mod_kernel(c_ref, w_ref, b_ref, o_ref):
    o_ref[0] = _bdot(_silu(c_ref[...]), w_ref[0]) + b_ref[0]


def _modulation(cvec, w_mod, b_mod):
    L, D, N = w_mod.shape
    R = cvec.shape[0]
    tn = 1024
    return pl.pallas_call(
        _mod_kernel,
        grid=(L, N // tn),
        in_specs=[pl.BlockSpec((R, D), lambda l, n: (0, 0)),
                  pl.BlockSpec((1, D, tn), lambda l, n: (l, 0, n)),
                  pl.BlockSpec((1, 1, tn), lambda l, n: (l, 0, n))],
        out_specs=pl.BlockSpec((1, R, tn), lambda l, n: (l, 0, n)),
        out_shape=jax.ShapeDtypeStruct((L, R, N), F32),
        compiler_params=_cparams(("parallel", "parallel")),
        name="modulation",
    )(cvec, w_mod, b_mod.reshape(L, 1, N))


def _inproj_kernel(*refs, rope, q_scale):
    if rope:
        x_ref, mod_ref, g_ref, w_ref, cos_ref, sa_ref, sb_ref, zg_ref, rw_ref, na_ref, df_ref, h_ref = refs
    else:
        x_ref, mod_ref, g_ref, w_ref, zg_ref, rw_ref, na_ref, df_ref, h_ref = refs
    n = pl.program_id(2)

    @pl.when(n == 0)
    def _():
        x = x_ref[0]
        y = x * lax.rsqrt(jnp.mean(x * x, axis=-1, keepdims=True) + RMS_EPS) * g_ref[...]
        h_ref[...] = (y * (1.0 + mod_ref[0, 1:2, :]) + mod_ref[0, 0:1, :]).astype(BF16)

    mm = lambda: jnp.dot(h_ref[...], w_ref[...], preferred_element_type=F32)

    def rot(t):
        if not rope:
            return t
        w = t.shape[-1]
        tile = lambda r: jnp.concatenate([r[...]] * (w // r.shape[-1]), axis=1)
        return t * tile(cos_ref) + pltpu.roll(t, w - 16, 1) * tile(sa_ref) + pltpu.roll(t, 16, 1) * tile(sb_ref)

    @pl.when(n < TILE_RW)
    def _():
        zg_ref[0] = mm()

    @pl.when((n >= TILE_RW) & (n < TILE_NA))
    def _():
        rw_ref[0] = mm()

    @pl.when(n == TILE_NA)
    def _():
        na_ref[0] = (mm() * q_scale).astype(BF16)

    @pl.when((n > TILE_NA) & (n < TILE_DF))
    def _():
        na_ref[0] = mm().astype(BF16)

    @pl.when(n == TILE_DF)
    def _():
        df_ref[0] = (rot(mm()) * q_scale).astype(BF16)

    @pl.when(n == TILE_DF + 1)
    def _():
        df_ref[0] = rot(mm()).astype(BF16)

    @pl.when(n == TILE_DF + 2)
    def _():
        df_ref[0] = mm().astype(BF16)


def _inproj(x, mod, g_pre, w_perm, tables, tm):
    B, T, D = x.shape
    tn = TILE_W
    per_batch = mod.shape[0] > 1
    in_specs = [pl.BlockSpec((1, tm, D), lambda b, m, n: (b, m, 0)),
                pl.BlockSpec((1, 3, D), (lambda b, m, n: (b, 0, 0)) if per_batch else (lambda b, m, n: (0, 0, 0))),
                pl.BlockSpec((1, D), lambda b, m, n: (0, 0)),
                pl.BlockSpec((D, tn), lambda b, m, n: (0, n))]
    args = [x, mod, g_pre.reshape(1, D), w_perm]
    if tables is not None:
        in_specs += [pl.BlockSpec((tm, 128), lambda b, m, n: (m, 0))] * 3
        args += list(tables)
    seg = lambda first, count: pl.BlockSpec((1, tm, tn), lambda b, m, n: (b, m, jnp.clip(n - first, 0, count - 1)))
    out = lambda count, dtype: jax.ShapeDtypeStruct((B, T, count * tn), dtype)
    return pl.pallas_call(
        functools.partial(_inproj_kernel, rope=tables is not None, q_scale=HEAD_DIM ** -0.5 * LOG2E),
        grid=(B, T // tm, Z_WIDTH // tn),
        in_specs=in_specs,
        out_specs=[seg(0, TILE_RW), seg(TILE_RW, TILE_NA - TILE_RW), seg(TILE_NA, TILE_DF - TILE_NA),
                   seg(TILE_DF, Z_WIDTH // tn - TILE_DF)],
        out_shape=[out(TILE_RW, F32), out(TILE_NA - TILE_RW, F32), out(TILE_DF - TILE_NA, BF16),
                   out(Z_WIDTH // tn - TILE_DF, BF16)],
        scratch_shapes=[pltpu.VMEM((tm, D), BF16)],
        compiler_params=_cparams(("parallel", "parallel", "arbitrary")),
        name="inproj",
    )(*args)


def _merge_kernel(x_ref, mod_ref, gpost_ref, orw_ref, ona_ref, odf_ref, rg_ref, ng_ref, dg_ref, mg_ref,
                  wb_ref, wo_ref, o_ref):
    D = x_ref.shape[-1]
    acc = None
    for n, (ob_ref, gate_ref) in enumerate(((orw_ref, rg_ref), (ona_ref, ng_ref), (odf_ref, dg_ref))):
        yb = _bdot(ob_ref[0] * _silu(gate_ref[0]), wb_ref[n])
        term = _sigmoid(mg_ref[0, :, n * D:(n + 1) * D]) * yb
        acc = term if acc is None else acc + term
    y = _bdot(acc, wo_ref[...])
    yn = y * lax.rsqrt(jnp.mean(y * y, axis=-1, keepdims=True) + RMS_EPS) * gpost_ref[...]
    o_ref[0] = x_ref[0] + mod_ref[0, 2:3, :] * yn


def _merge(x, mod, g_post, o_rw, o_na, o_df, zg, w_branch, w_out, tm):
    B, T, D = x.shape
    per_batch = mod.shape[0] > 1
    bw = BRANCH_WIDTH
    row = lambda b, m: (b, m, 0)
    col = lambda c: (lambda b, m: (b, m, c))
    return pl.pallas_call(
        _merge_kernel,
        grid=(B, T // tm),
        in_specs=[pl.BlockSpec((1, tm, D), row),
                  pl.BlockSpec((1, 3, D), (lambda b, m: (b, 0, 0)) if per_batch else (lambda b, m: (0, 0, 0))),
                  pl.BlockSpec((1, D), lambda b, m: (0, 0)),
                  pl.BlockSpec((1, tm, bw), row), pl.BlockSpec((1, tm, bw), row), pl.BlockSpec((1, tm, bw), row),
                  pl.BlockSpec((1, tm, bw), col(N_BRANCH * D // bw)),
                  pl.BlockSpec((1, tm, bw), col(N_BRANCH * D // bw + 1)),
                  pl.BlockSpec((1, tm, bw), col(N_BRANCH * D // bw + 2)),
                  pl.BlockSpec((1, tm, N_BRANCH * D), col(0)),
                  pl.BlockSpec((N_BRANCH, bw, D), lambda b, m: (0, 0, 0)),
                  pl.BlockSpec((D, D), lambda b, m: (0, 0))],
        out_specs=pl.BlockSpec((1, tm, D), row),
        out_shape=jax.ShapeDtypeStruct((B, T, D), F32),
        compiler_params=_cparams(("parallel", "parallel")),
        name="merge",
    )(x, mod, g_post.reshape(1, D), o_rw, o_na, o_df, zg, zg, zg, zg, w_branch, w_out)


def _rope_tables(n_tokens):
    t = np.arange(n_tokens)
    axis_dim = DIFF_QK_DIM // 2
    inv = ROPE_THETA ** (-np.arange(0, axis_dim, 2, dtype=np.float32) / axis_dim)
    ar = (t // GRID_W).astype(np.float32)[:, None] * inv
    ac = (t % GRID_W).astype(np.float32)[:, None] * inv
    ang = jnp.asarray(np.concatenate([ar, ar, ac, ac], axis=-1).astype(np.float32))
    cos, sin = jnp.cos(ang), jnp.sin(ang)
    first = (np.arange(DIFF_QK_DIM) % 32) < 16
    sin_a = jnp.where(first, -sin, 0.0)
    sin_b = jnp.where(first, 0.0, sin)
    tile = lambda a: jnp.tile(a, (1, 2))
    return tile(cos), tile(sin_a), tile(sin_b)


def _dot_nt(a, b):
    return lax.dot_general(a, b, (((1,), (1,)), ((), ())), preferred_element_type=F32)


SOFTMAX_ROWS = 128


def _diff_attn_kernel(q_ref, kc_ref, vc_ref, k_ref, v_ref, lq_ref, lk_ref, g_ref, o_ref,
                      s0, s1, p0, p1, al0, al1, m_scr, acc_scr, *, n_main, tk, lambda_init):
    tq = q_ref.shape[1]
    t_first = kc_ref.shape[1]
    q = q_ref[0]
    lo = lax.broadcasted_iota(jnp.int32, (1, 128), 1) < DIFF_QK_DIM
    zero = jnp.zeros_like(q)
    qq = jnp.concatenate([jnp.where(lo, q, zero), jnp.where(lo, zero, q)], axis=0)

    n_sub = 2 * tq // SOFTMAX_ROWS
    sub = lambda i: slice(i * SOFTMAX_ROWS, (i + 1) * SOFTMAX_ROWS)

    def scores(k_chunk, s_scr):
        s_scr[:, :k_chunk.shape[0]] = _dot_nt(qq, k_chunk)

    def softmax(size, first, s_scr, p_scr, al_scr):
        w = min(128, size)
        cols = [slice(c * w, (c + 1) * w) for c in range(size // w)]
        for i in range(n_sub):
            mx = s_scr[sub(i), cols[0]]
            for c in cols[1:]:
                mx = jnp.maximum(mx, s_scr[sub(i), c])
            mn = jnp.broadcast_to(jnp.max(mx, axis=-1, keepdims=True), (SOFTMAX_ROWS, 128))
            if not first:
                mo = m_scr[sub(i), :]
                mn = jnp.maximum(mo, mn)
                al_scr[sub(i), :] = jnp.exp2(mo - mn)
            m_scr[sub(i), :] = mn
        for i in range(n_sub):
            mn = m_scr[sub(i), :w]
            for c in cols:
                p_scr[sub(i), c] = jnp.exp2(s_scr[sub(i), c] - mn).astype(BF16)

    def accumulate(v, first, p_scr, al_scr):
        size = v.shape[0]
        v_ext = jnp.concatenate([v, jnp.ones((size, 128), BF16)], axis=1)
        pv = jnp.dot(p_scr[:, :size], v_ext, preferred_element_type=F32)
        if first:
            acc_scr[...] = pv
        else:
            al = al_scr[...]
            acc_scr[...] = acc_scr[...] * jnp.concatenate([al, al], axis=1) + pv

    scores(kc_ref[0], s0)
    softmax(t_first, True, s0, p0, al0)
    accumulate(vc_ref[0], True, p0, al0)
    if n_main:
        k_at = lambda j: k_ref[0, pl.ds(pl.multiple_of(j * tk, tk), tk), :]
        v_at = lambda j: v_ref[0, pl.ds(pl.multiple_of(j * tk, tk), tk), :]
        scores(k_at(0), s0)
        p1[...] = jnp.zeros_like(p1)
        al1[...] = jnp.ones_like(al1)

        def body(t, _):
            j = 2 * t
            scores(k_at(j + 1), s1)
            softmax(tk, False, s0, p0, al0)
            accumulate(v_at(jnp.maximum(j - 1, 0)), False, p1, al1)
            scores(k_at(jnp.minimum(j + 2, n_main - 1)), s0)
            softmax(tk, False, s1, p1, al1)
            accumulate(v_at(j), False, p0, al0)
            return 0

        lax.fori_loop(0, n_main // 2, body, 0)
        accumulate(v_at(n_main - 1), False, p1, al1)
    lqk = lq_ref[...] * lk_ref[...]
    e = jnp.exp(jnp.sum(lqk, axis=-1, keepdims=True))
    lam = e[0:1] - e[1:2] + lambda_init
    o_maps = acc_scr[:, :DIFF_V_DIM] / acc_scr[:, DIFF_V_DIM:]
    o = o_maps[:tq] - lam * o_maps[tq:]
    o = o * lax.rsqrt(jnp.mean(o * o, axis=-1, keepdims=True) + SUBLN_EPS) * g_ref[...]
    o_ref[0] = o * (1.0 - lambda_init)


def _diff_attn(qkv, qkv_c, with_latent_keys, lam_q, lam_k, subln_g, lambda_init, tq, tk):
    B, T, _ = qkv.shape
    C = qkv_c.shape[1]
    n_main = T // tk if with_latent_keys else 0
    assert n_main % 2 == 0 and (2 * tq) % SOFTMAX_ROWS == 0
    wmax = max(C, tk if n_main else 0)
    H = DIFF_HEADS
    kv = lambda t, j: pl.BlockSpec((1, t, 128), lambda b, h, m: (b, 0, j * H + h))
    small = lambda r, c: pl.BlockSpec((r, c), lambda b, h, m: (0, 0))
    stat = pltpu.VMEM((2 * tq, 128), F32)
    s_buf = pltpu.VMEM((2 * tq, wmax), F32)
    p_buf = pltpu.VMEM((2 * tq, wmax), BF16)
    return pl.pallas_call(
        functools.partial(_diff_attn_kernel, n_main=n_main, tk=tk, lambda_init=lambda_init),
        grid=(B, H, T // tq),
        in_specs=[pl.BlockSpec((1, tq, 128), lambda b, h, m: (b, m, h)), kv(C, 1), kv(C, 2), kv(T, 1), kv(T, 2),
                  small(2, DIFF_QK_DIM), small(2, DIFF_QK_DIM), small(1, DIFF_V_DIM)],
        out_specs=pl.BlockSpec((1, tq, 128), lambda b, h, m: (b, m, h)),
        out_shape=jax.ShapeDtypeStruct((B, T, H * DIFF_V_DIM), F32),
        scratch_shapes=[s_buf, s_buf, p_buf, p_buf, stat, stat, stat, pltpu.VMEM((2 * tq, 2 * DIFF_V_DIM), F32)],
        compiler_params=_cparams(("parallel", "parallel", "parallel")),
        name="diff_attn",
    )(qkv, qkv_c, qkv_c, qkv, qkv, lam_q, lam_k, subln_g.reshape(1, DIFF_V_DIM))


def _na_bias_tables(rpb):
    c_idx = np.arange(GRID_W)
    c_start = np.clip(c_idx - NA_WIN_C // 2, 0, GRID_W - NA_WIN_C)
    col_ok = (c_idx[None, :] >= c_start[:, None]) & (c_idx[None, :] < c_start[:, None] + NA_WIN_C)
    dc = np.clip(c_idx[None, :] - c_idx[:, None], -(NA_WIN_C - 1), NA_WIN_C - 1) + (NA_WIN_C - 1)
    H = rpb.shape[0]
    e = jnp.take(rpb, jnp.asarray(dc.reshape(-1)), axis=2).reshape(H, 2 * NA_WIN_R - 1, GRID_W, GRID_W)
    e = jnp.where(col_ok, e, NEG_INF)
    tabs = [e[:, NA_WIN_R - 1 - off:2 * NA_WIN_R - 1 - off].transpose(0, 2, 1, 3).reshape(H, GRID_W, NA_WIN_R * GRID_W)
            for off in range(NA_WIN_R)]
    return jnp.stack(tabs, axis=1)


def _na_kernel(q_ref, k_ref, v_ref, kc_ref, vc_ref, bias_ref, o_ref, *, rq, rows):
    i = pl.program_id(2)
    lo = lax.broadcasted_iota(jnp.int32, (1, 128), 1) < HEAD_DIM
    win = NA_WIN_R * GRID_W
    G2 = 2 * GRID_W
    q = q_ref[0]
    zero = jnp.zeros_like(q)
    q_lo, q_hi = jnp.where(lo, q, zero), jnp.where(lo, zero, q)
    qq = jnp.concatenate([x[rr * GRID_W:(rr + 1) * GRID_W] for rr in range(rq) for x in (q_lo, q_hi)], axis=0)
    kc = kc_ref[0]
    ones = lambda n: jnp.ones((n, 128), BF16)
    s_c = _dot_nt(qq, kc)

    starts, s_nb = [], []
    for rr in range(rq):
        r = i * rq + rr
        r_start = jnp.clip(r - NA_WIN_R // 2, 0, rows - NA_WIN_R)
        off = r - r_start
        start = pl.multiple_of(r_start * GRID_W, GRID_W)
        starts.append(start)
        bias = jnp.concatenate([bias_ref[0, off], bias_ref[1, off]], axis=0)
        s_nb.append(_dot_nt(qq[rr * G2:(rr + 1) * G2], k_ref[0, pl.ds(start, win), :]) + bias)

    def lane_blocks(t):
        w = min(128, t.shape[-1])
        return [t[:, c * w:(c + 1) * w] for c in range(t.shape[-1] // w)]

    p_nb, p_c = [], []
    for rr in range(rq):
        sc = s_c[rr * G2:(rr + 1) * G2]
        blocks = lane_blocks(s_nb[rr])
        mx = blocks[0]
        for b in blocks[1:]:
            mx = jnp.maximum(mx, b)
        m = jnp.maximum(jnp.max(mx, axis=-1, keepdims=True), jnp.max(sc, axis=-1, keepdims=True))
        p_nb.append(jnp.exp2(s_nb[rr] - m).astype(BF16))
        p_c.append(jnp.exp2(sc - m).astype(BF16))

    vc_ext = jnp.concatenate([vc_ref[0], ones(kc.shape[0])], axis=1)
    o_c = jnp.dot(jnp.concatenate(p_c, axis=0), vc_ext, preferred_element_type=F32)
    for rr in range(rq):
        vw_ext = jnp.concatenate([v_ref[0, pl.ds(starts[rr], win), :], ones(win)], axis=1)
        o = jnp.dot(p_nb[rr], vw_ext, preferred_element_type=F32) + o_c[rr * G2:(rr + 1) * G2]
        o = o[:, :128] / o[:, 128:]
        o_ref[0, rr * GRID_W:(rr + 1) * GRID_W, :] = jnp.where(lo, o[:GRID_W], o[GRID_W:])


def _na_attn(qkv, qkv_c, bias, rq):
    B, S, _ = qkv.shape
    C = qkv_c.shape[1]
    rows = S // GRID_W
    npair = NA_HEADS // 2
    full = lambda t, j: pl.BlockSpec((1, t, 128), lambda b, p, m: (b, 0, j * npair + p))
    return pl.pallas_call(
        functools.partial(_na_kernel, rq=rq, rows=rows),
        grid=(B, npair, rows // rq),
        in_specs=[pl.BlockSpec((1, rq * GRID_W, 128), lambda b, p, m: (b, m, p)),
                  full(S, 1), full(S, 2), full(C, 1), full(C, 2),
                  pl.BlockSpec((2, NA_WIN_R, GRID_W, NA_WIN_R * GRID_W), lambda b, p, m: (p, 0, 0, 0))],
        out_specs=pl.BlockSpec((1, rq * GRID_W, 128), lambda b, p, m: (b, m, p)),
        out_shape=jax.ShapeDtypeStruct((B, S, NA_HEADS * HEAD_DIM), F32),
        compiler_params=_cparams(("parallel", "parallel", "parallel")),
        name="na_attn",
    )(qkv, qkv, qkv, qkv_c, qkv_c, bias)


def _ctx_attn_kernel(q_ref, k_ref, v_ref, o_ref):
    lo = lax.broadcasted_iota(jnp.int32, (1, 128), 1) < HEAD_DIM
    q, k, v = q_ref[0], k_ref[0], v_ref[0]
    zero = jnp.zeros_like(q)
    outs = []
    for hl in range(2):
        qh = jnp.where(lo, q, zero) if hl == 0 else jnp.where(lo, zero, q)
        s = _dot_nt(qh, k)
        p = jnp.exp2(s - jnp.max(s, axis=-1, keepdims=True))
        o = jnp.dot(p.astype(BF16), v, preferred_element_type=F32)
        outs.append(o / jnp.sum(p, axis=-1, keepdims=True))
    o_ref[0] = jnp.where(lo, outs[0], outs[1])


def _ctx_attn(qkv):
    B, C, _ = qkv.shape
    npair = NA_HEADS // 2
    blk = lambda j: pl.BlockSpec((1, C, 128), lambda b, p: (b, 0, j * npair + p))
    return pl.pallas_call(
        _ctx_attn_kernel,
        grid=(B, npair),
        in_specs=[blk(0), blk(1), blk(2)],
        out_specs=blk(0),
        out_shape=jax.ShapeDtypeStruct((B, C, NA_HEADS * HEAD_DIM), F32),
        compiler_params=_cparams(("parallel", "parallel")),
        name="ctx_attn",
    )(qkv, qkv, qkv)


PL_R, PL_V, PL_KK = 0, 1, 2
PL_LOGW, PL_KDIR, PL_B = 3, 4, 5
N_PLANES = 9
RWKV_CHUNK = 64


def _split3(x):
    hi = x.astype(BF16)
    r1 = x - hi.astype(F32)
    mid = r1.astype(BF16)
    lo = (r1 - mid.astype(F32)).astype(BF16)
    return hi, mid, lo


def _dot_exact_rhs(m, x):
    hi, mid, lo = _split3(x)
    mb = m.astype(BF16)
    d = lambda t: jnp.dot(mb, t, preferred_element_type=F32)
    return d(hi) + d(mid) + d(lo)


def _dot_exact_lhs(x, m):
    hi, mid, lo = _split3(x)
    mb = m.astype(BF16)
    d = lambda t: jnp.dot(t, mb, preferred_element_type=F32)
    return d(hi) + d(mid) + d(lo)


def _head_ones(n):
    r = lax.broadcasted_iota(jnp.int32, (n, n), 0) // HEAD_DIM
    c = lax.broadcasted_iota(jnp.int32, (n, n), 1) // HEAD_DIM
    return (r == c).astype(F32)


def _rwkv_prep_kernel(zc_ref, zp_ref, zn_ref, mu_ref, kk_ref, ka_ref, w0_ref, a0_ref, wup_ref, aup_ref, o_ref):
    tm = zc_ref.shape[1]
    m = pl.program_id(1)
    u = zc_ref[0]
    prev_row = jnp.where(m > 0, zp_ref[0, 7:8, :], 0.0)
    next_row = jnp.where(m < pl.num_programs(1) - 1, zn_ref[0, 0:1, :], 0.0)
    rows = lax.broadcasted_iota(jnp.int32, (tm, 1), 0)
    u_prev = jnp.where(rows == 0, prev_row, pltpu.roll(u, 1, 0))
    u_next = jnp.where(rows == tm - 1, next_row, pltpu.roll(u, tm - 1, 0))
    u = u + mu_ref[0:1, :] * (u_prev - u) + mu_ref[1:2, :] * (u_next - u)

    W = RWKV_WIDTH
    r, k, v = u[:, 0:W], u[:, W:2 * W], u[:, 2 * W:3 * W]
    lw = jnp.tanh(u[:, 3 * W:3 * W + 2 * LORA]).astype(BF16)
    la = u[:, 3 * W + 2 * LORA:3 * W + 4 * LORA].astype(BF16)
    ones = _head_ones(W)
    kk = k * kk_ref[...]
    ss = _dot_exact_lhs(kk * kk, ones)
    kk = kk / jnp.maximum(jnp.sqrt(ss), 1e-12)
    o_ref[0, :, PL_R * W:(PL_R + 1) * W] = r
    o_ref[0, :, PL_V * W:(PL_V + 1) * W] = v
    o_ref[0, :, PL_KK * W:(PL_KK + 1) * W] = kk
    for d in range(2):
        xw = w0_ref[d:d + 1, :] + jnp.dot(lw, wup_ref[d], preferred_element_type=F32)
        logw = -math.exp(-0.5) * _sigmoid(xw)
        a = _sigmoid(a0_ref[d:d + 1, :] + jnp.dot(la, aup_ref[d], preferred_element_type=F32))
        kdir = k * (1.0 + (a - 1.0) * ka_ref[...])
        base = 3 * d
        o_ref[0, :, (PL_LOGW + base) * W:(PL_LOGW + base + 1) * W] = logw
        o_ref[0, :, (PL_KDIR + base) * W:(PL_KDIR + base + 1) * W] = kdir
        o_ref[0, :, (PL_B + base) * W:(PL_B + base + 1) * W] = kk * a


def _rwkv_prep(z, mu_pad, k_k, k_a, w0, a0, wup2, aup2, tm):
    B, T, wz = z.shape
    nb = tm // 8
    W = RWKV_WIDTH
    const = lambda shape: pl.BlockSpec(shape, lambda b, m: (0,) * len(shape))
    return pl.pallas_call(
        _rwkv_prep_kernel,
        grid=(B, T // tm),
        in_specs=[pl.BlockSpec((1, tm, wz), lambda b, m: (b, m, 0)),
                  pl.BlockSpec((1, 8, wz), lambda b, m: (b, jnp.maximum(m * nb - 1, 0), 0)),
                  pl.BlockSpec((1, 8, wz), lambda b, m: (b, jnp.minimum((m + 1) * nb, T // 8 - 1), 0)),
                  const((2, wz)), const((1, W)), const((1, W)), const((2, W)), const((2, W)),
                  const((2, 2 * LORA, W)), const((2, 2 * LORA, W))],
        out_specs=pl.BlockSpec((1, tm, N_PLANES * W), lambda b, m: (b, m, 0)),
        out_shape=jax.ShapeDtypeStruct((B, T, N_PLANES * W), F32),
        compiler_params=_cparams(("parallel", "parallel")),
        name="rwkv_prep",
    )(z, z, z, mu_pad, k_k.reshape(1, W), k_a.reshape(1, W), w0, a0, wup2, aup2)


def _rwkv_scan_kernel(pf_ref, pb_ref, s0_ref, yf_ref, yb_ref, s_ref):
    C = RWKV_CHUNK
    W = RWKV_WIDTH
    C2 = 2 * C

    @pl.when(pl.program_id(1) == 0)
    def _():
        s_ref[...] = s0_ref[...]

    lo = lax.broadcasted_iota(jnp.int32, (1, 128), 1) < HEAD_DIM
    row = lax.broadcasted_iota(jnp.int32, (C2, C2), 0)
    col = lax.broadcasted_iota(jnp.int32, (C2, C2), 1)
    same = (row // C) == (col // C)
    eye = (row == col).astype(F32)
    tri_r = lax.broadcasted_iota(jnp.int32, (C, C), 0)
    tri_c = lax.broadcasted_iota(jnp.int32, (C, C), 1)
    levels = [((row >> k) == (col >> k)) & ((row >> (k - 1)) != (col >> (k - 1))) for k in range(1, 7)]
    p_refs, y_refs = (pf_ref, pb_ref), (yf_ref, yb_ref)
    groups = [(d, p) for d in range(2) for p in range(W // 128)]
    ng = len(groups)
    bf = lambda t: t.astype(BF16)

    def stack(x):
        return jnp.concatenate([jnp.where(lo, x, 0.0), jnp.where(lo, 0.0, x)], axis=0)

    c_dir = []
    for d in range(2):
        cum = ((tri_c <= tri_r) if d == 0 else (tri_c >= tri_r)).astype(F32)
        logw_all = p_refs[d][0, :, (PL_LOGW + 3 * d) * W:(PL_LOGW + 3 * d + 1) * W]
        c_dir.append(_dot_exact_rhs(cum, logw_all))

    ar, bk, ends, vs, decay_end = [], [], [], [], []
    for d, p in groups:
        get = lambda plane: p_refs[d][0, :, plane * W + p * 128:plane * W + (p + 1) * 128]
        r, v, kk = get(PL_R), get(PL_V), get(PL_KK)
        logw, kdir, bb = get(PL_LOGW + 3 * d), get(PL_KDIR + 3 * d), get(PL_B + 3 * d)
        c = c_dir[d][:, p * 128:(p + 1) * 128]
        last = C - 1 if d == 0 else 0
        c_last = c[last:last + 1, :]
        e_neg = jnp.exp(-c)
        e_end = jnp.exp(c_last - c)
        ar.append(bf(jnp.concatenate([stack(-kk * jnp.exp(c - logw)), stack(r * jnp.exp(c))], axis=0)))
        bk.append(bf(jnp.concatenate([stack(bb * e_neg), stack(kdir * e_neg)], axis=0)))
        ends.append(bf(jnp.concatenate([stack(bb * e_end), stack(kdir * e_end)], axis=0)))
        vs.append(bf(stack(v)))
        decay_end.append(jnp.exp(c_last))

    amat = [_dot_nt(ar[g], bk[g]) for g in range(ng)]
    a_ab, a_kr, a_rb = [], [], []
    for g, (d, p) in enumerate(groups):
        before = (col < row) if d == 0 else (col > row)
        strict = same & before
        incl = same & (before | (row == col))
        m = amat[g]
        a_ab.append(jnp.where(strict, m[:C2, :C2], 0.0))
        a_kr.append(bf(jnp.concatenate([jnp.where(strict, m[:C2, C2:], 0.0),
                                        jnp.where(incl, m[C2:, C2:], 0.0)], axis=0)))
        a_rb.append(bf(jnp.where(incl, m[C2:, :C2], 0.0)))

    t = [eye + jnp.where(levels[0], a, 0.0) for a in a_ab]
    for lvl in levels[1:]:
        ta = [jnp.dot(bf(t[g]), bf(jnp.where(lvl, a_ab[g], 0.0)), preferred_element_type=F32) for g in range(ng)]
        t = [t[g] + jnp.dot(bf(ta[g]), bf(t[g]), preferred_element_type=F32) for g in range(ng)]

    s_old = [s_ref[0, d, p] for d, p in groups]
    from_s = [_dot_nt(ar[g], bf(s_old[g])) for g in range(ng)]
    from_v = [jnp.dot(a_kr[g], vs[g], preferred_element_type=F32) for g in range(ng)]
    u = [jnp.dot(bf(t[g]), bf(from_s[g][:C2] + from_v[g][:C2]), preferred_element_type=F32) for g in range(ng)]
    for g, (d, p) in enumerate(groups):
        y_s = from_s[g][C2:] + from_v[g][C2:] + jnp.dot(a_rb[g], bf(u[g]), preferred_element_type=F32)
        y_refs[d][0, :, p * 128:(p + 1) * 128] = y_s[:C] + y_s[C:]
    for g, (d, p) in enumerate(groups):
        uv = jnp.concatenate([bf(u[g]), vs[g]], axis=0)
        s_ref[0, d, p] = s_old[g] * decay_end[g] + lax.dot_general(
            uv, ends[g], (((0,), (0,)), ((), ())), preferred_element_type=F32)


def _rwkv_scan(planes, s0):
    B, T, _ = planes.shape
    nc = T // RWKV_CHUNK
    W = RWKV_WIDTH
    st = pl.BlockSpec((1, 2, W // 128, 128, 128), lambda b, i: (b, 0, 0, 0, 0))
    y = jax.ShapeDtypeStruct((B, T, W), F32)
    return pl.pallas_call(
        _rwkv_scan_kernel,
        grid=(B, nc),
        in_specs=[pl.BlockSpec((1, RWKV_CHUNK, N_PLANES * W), lambda b, i: (b, i, 0)),
                  pl.BlockSpec((1, RWKV_CHUNK, N_PLANES * W), lambda b, i: (b, nc - 1 - i, 0)),
                  st],
        out_specs=[pl.BlockSpec((1, RWKV_CHUNK, W), lambda b, i: (b, i, 0)),
                   pl.BlockSpec((1, RWKV_CHUNK, W), lambda b, i: (b, nc - 1 - i, 0)),
                   st],
        out_shape=[y, y, jax.ShapeDtypeStruct(s0.shape, F32)],
        compiler_params=_cparams(("parallel", "arbitrary")),
        name="rwkv_scan",
    )(planes, planes, s0)


def _rwkv_readout_kernel(yf_ref, yb_ref, r_ref, v_ref, kf_ref, kb_ref, g_ref, b_ref, rk_ref, o_ref):
    ones = _head_ones(RWKV_WIDTH)
    y = yf_ref[0] + yb_ref[0]
    inv_n = 1.0 / HEAD_DIM
    mu = _dot_exact_lhs(y, ones) * inv_n
    yc = y - mu
    var = _dot_exact_lhs(yc * yc, ones) * inv_n
    yn = yc * lax.rsqrt(var + GN_EPS) * g_ref[...] + b_ref[...]
    bonus = _dot_exact_lhs(r_ref[0] * (kf_ref[0] + kb_ref[0]) * rk_ref[...], ones)
    o_ref[0] = yn + bonus * v_ref[0]


def _rwkv_readout(y_f, y_b, planes, ln_g, ln_b, r_k, tm):
    B, T, W = y_f.shape
    row = pl.BlockSpec((1, tm, W), lambda b, m: (b, m, 0))
    plane = lambda j: pl.BlockSpec((1, tm, W), lambda b, m: (b, m, j))
    vec = pl.BlockSpec((1, W), lambda b, m: (0, 0))
    return pl.pallas_call(
        _rwkv_readout_kernel,
        grid=(B, T // tm),
        in_specs=[row, row, plane(PL_R), plane(PL_V), plane(PL_KDIR), plane(PL_KDIR + 3), vec, vec, vec],
        out_specs=row,
        out_shape=jax.ShapeDtypeStruct((B, T, W), F32),
        compiler_params=_cparams(("parallel", "parallel")),
        name="rwkv_readout",
    )(y_f, y_b, planes, planes, planes, planes, ln_g.reshape(1, W), ln_b.reshape(1, W), r_k.reshape(1, W))


def _layer_weights(l, w_in, shift_mu, w_up, a_up, w_branch, w_out):
    rw, rg, na, ng, df, dg, mg = jnp.split(w_in[l], [int(i) for i in np.cumsum(IN_SIZES)[:-1]], axis=-1)
    pad = jnp.zeros((w_in.shape[1], RWKV_PAD_WIDTH - RWKV_SHIFT_WIDTH), w_in.dtype)
    w_perm = jnp.concatenate([mg, rg, ng, dg, rw, pad, na, df], axis=-1).astype(BF16)
    mu_pad = jnp.pad(shift_mu[l], ((0, 0), (0, RWKV_PAD_WIDTH - RWKV_SHIFT_WIDTH)))
    zl = jnp.zeros((LORA, RWKV_WIDTH), F32)
    wup2 = jnp.stack([jnp.concatenate([w_up[l, 0], zl]), jnp.concatenate([zl, w_up[l, 1]])]).astype(BF16)
    aup2 = jnp.stack([jnp.concatenate([a_up[l, 0], zl]), jnp.concatenate([zl, a_up[l, 1]])]).astype(BF16)
    return w_perm, mu_pad, wup2, aup2, w_branch[l].astype(BF16), w_out[l].astype(BF16)


def kernel(x, c, ctx, c_ctx, w_mod, b_mod, g_pre, g_post, w_in, shift_mu, k_k, k_a, r_k, w0, w_up, a0, a_up,
           ln_x_g, ln_x_b, rpb, lam_q, lam_k, diff_subln, w_branch, w_out):
    B, S, D = x.shape
    C = ctx.shape[1]
    depth = w_in.shape[0]
    tables = _rope_tables(S)
    rows_pad = 16
    cvec = jnp.zeros((rows_pad, D), F32).at[:B].set(c).at[B].set(c_ctx)
    mod = _modulation(cvec, w_mod, b_mod)
    hc = ctx
    tm_x = min(1024, S)
    for l in range(depth):
        last = l == depth - 1
        lambda_init = 0.8 - 0.6 * math.exp(-0.3 * l)
        w_perm, mu_pad, wup2, aup2, wb, wo = _layer_weights(l, w_in, shift_mu, w_up, a_up, w_branch, w_out)
        mod_x = mod[l, :B].reshape(B, 3, D)
        mod_c = mod[l, B:B + 1].reshape(1, 3, D)
        zg_x, rw_x, na_x, df_x = _inproj(x, mod_x, g_pre[l], w_perm, tables, tm_x)
        zg_c, rw_c, na_c, df_c = _inproj(hc, mod_c, g_pre[l], w_perm, None, C)

        pl_c = _rwkv_prep(rw_c, mu_pad, k_k[l], k_a[l], w0[l], a0[l], wup2, aup2, min(256, C))
        pl_x = _rwkv_prep(rw_x, mu_pad, k_k[l], k_a[l], w0[l], a0[l], wup2, aup2, 256)
        s_zero = jnp.zeros((B, 2, RWKV_WIDTH // 128, 128, 128), F32)
        yf_c, yb_c, s_ctx = _rwkv_scan(pl_c, s_zero)
        yf_x, yb_x, _ = _rwkv_scan(pl_x, s_ctx)
        r_k_flat = r_k[l].reshape(RWKV_WIDTH)
        o_rw_x = _rwkv_readout(yf_x, yb_x, pl_x, ln_x_g[l], ln_x_b[l], r_k_flat, 256)

        o_na_x = _na_attn(na_x, na_c, _na_bias_tables(rpb[l]) * LOG2E, 8)
        o_df_x = _diff_attn(df_x, df_c, True, lam_q[l], lam_k[l], diff_subln[l], lambda_init, 512, min(512, S // 2))

        x = _merge(x, mod_x, g_post[l], o_rw_x, o_na_x, o_df_x, zg_x, wb, wo, 256)
        if not last:
            o_rw_c = _rwkv_readout(yf_c, yb_c, pl_c, ln_x_g[l], ln_x_b[l], r_k_flat, min(256, C))
            o_na_c = _ctx_attn(na_c)
            o_df_c = _diff_attn(df_c, df_c, False, lam_q[l], lam_k[l], diff_subln[l], lambda_init, C, C)
            hc = _merge(hc, mod_c, g_post[l], o_rw_c, o_na_c, o_df_c, zg_c, wb, wo, min(256, C))
    return x
```

```python
import functools
import math

import jax
import jax.numpy as jnp
import numpy as np
from jax import lax
from jax.experimental import pallas as pl
from jax.experimental.pallas import tpu as pltpu

F32 = jnp.float32
BF16 = jnp.bfloat16

DEPTH = 4
GRID_W = 64
RWKV_HEADS = 8
HEAD_DIM = 64
RWKV_WIDTH = 512
LORA = 64
GN_EPS = 64e-5
NA_HEADS = 8
NA_WIN_R = 8
NA_WIN_C = 16
DIFF_HEADS = 4
DIFF_QK_DIM = 64
DIFF_V_DIM = 128
ROPE_THETA = 10000.0
SUBLN_EPS = 1e-5
RMS_EPS = 1e-6
NEG_INF = -1e30
LOG2E = math.log2(math.e)
BRANCH_WIDTH = 512
N_BRANCH = 3

RWKV_SHIFT_WIDTH = 3 * RWKV_WIDTH + 4 * LORA
RWKV_PAD_WIDTH = 2048
TILE_W = 512
TILE_RW, TILE_NA, TILE_DF = 9, 13, 16
Z_WIDTH = 19 * TILE_W
IN_SIZES = (RWKV_SHIFT_WIDTH, 512, 1536, 512, 1536, 512, 3072)

VMEM_LIMIT = 48 * 1024 * 1024


def _cparams(sem):
    return pltpu.CompilerParams(dimension_semantics=sem, vmem_limit_bytes=VMEM_LIMIT)


def _sigmoid(x):
    return 1.0 / (1.0 + jnp.exp(-x))


def _silu(x):
    return x * _sigmoid(x)


def _bdot(a, b):
    return jnp.dot(a.astype(BF16), b.astype(BF16), preferred_element_type=F32)


def _mod_kernel(c_ref, w_ref, b_ref, o_ref):
    o_ref[0] = _bdot(_silu(c_ref[...]), w_ref[0]) + b_ref[0]


def _modulation(cvec, w_mod, b_mod):
    L, D, N = w_mod.shape
    R = cvec.shape[0]
    tn = 1024
    return pl.pallas_call(
        _mod_kernel,
        grid=(L, N // tn),
        in_specs=[pl.BlockSpec((R, D), lambda l, n: (0, 0)),
                  pl.BlockSpec((1, D, tn), lambda l, n: (l, 0, n)),
                  pl.BlockSpec((1, 1, tn), lambda l, n: (l, 0, n))],
        out_specs=pl.BlockSpec((1, R, tn), lambda l, n: (l, 0, n)),
        out_shape=jax.ShapeDtypeStruct((L, R, N), F32),
        compiler_params=_cparams(("parallel", "parallel")),
        name="modulation",
    )(cvec, w_mod, b_mod.reshape(L, 1, N))


def _inproj_kernel(*refs, rope, q_scale):
    if rope:
        x_ref, mod_ref, g_ref, w_ref, cos_ref, sa_ref, sb_ref, zg_ref, rw_ref, na_ref, df_ref, h_ref = refs
    else:
        x_ref, mod_ref, g_ref, w_ref, zg_ref, rw_ref, na_ref, df_ref, h_ref = refs
    n = pl.program_id(2)

    @pl.when(n == 0)
    def _():
        x = x_ref[0]
        y = x * lax.rsqrt(jnp.mean(x * x, axis=-1, keepdims=True) + RMS_EPS) * g_ref[...]
        h_ref[...] = (y * (1.0 + mod_ref[0, 1:2, :]) + mod_ref[0, 0:1, :]).astype(BF16)

    mm = lambda: jnp.dot(h_ref[...], w_ref[...], preferred_element_type=F32)

    def rot(t):
        if not rope:
            return t
        w = t.shape[-1]
        tile = lambda r: jnp.concatenate([r[...]] * (w // r.shape[-1]), axis=1)
        return t * tile(cos_ref) + pltpu.roll(t, w - 16, 1) * tile(sa_ref) + pltpu.roll(t, 16, 1) * tile(sb_ref)

    @pl.when(n < TILE_RW)
    def _():
        zg_ref[0] = mm().astype(BF16)

    @pl.when((n >= TILE_RW) & (n < TILE_NA))
    def _():
        rw_ref[0] = mm()

    @pl.when(n == TILE_NA)
    def _():
        na_ref[0] = (mm() * q_scale).astype(BF16)

    @pl.when((n > TILE_NA) & (n < TILE_DF))
    def _():
        na_ref[0] = mm().astype(BF16)

    @pl.when(n == TILE_DF)
    def _():
        df_ref[0] = (rot(mm()) * q_scale).astype(BF16)

    @pl.when(n == TILE_DF + 1)
    def _():
        df_ref[0] = rot(mm()).astype(BF16)

    @pl.when(n == TILE_DF + 2)
    def _():
        df_ref[0] = mm().astype(BF16)


def _inproj(x, mod, g_pre, w_perm, tables, tm):
    B, T, D = x.shape
    tn = TILE_W
    per_batch = mod.shape[0] > 1
    in_specs = [pl.BlockSpec((1, tm, D), lambda b, m, n: (b, m, 0)),
                pl.BlockSpec((1, 3, D), (lambda b, m, n: (b, 0, 0)) if per_batch else (lambda b, m, n: (0, 0, 0))),
                pl.BlockSpec((1, D), lambda b, m, n: (0, 0)),
                pl.BlockSpec((D, tn), lambda b, m, n: (0, n))]
    args = [x, mod, g_pre.reshape(1, D), w_perm]
    if tables is not None:
        in_specs += [pl.BlockSpec((tm, 128), lambda b, m, n: (m, 0))] * 3
        args += list(tables)
    seg = lambda first, count: pl.BlockSpec((1, tm, tn), lambda b, m, n: (b, m, jnp.clip(n - first, 0, count - 1)))
    out = lambda count, dtype: jax.ShapeDtypeStruct((B, T, count * tn), dtype)
    return pl.pallas_call(
        functools.partial(_inproj_kernel, rope=tables is not None, q_scale=HEAD_DIM ** -0.5 * LOG2E),
        grid=(B, T // tm, Z_WIDTH // tn),
        in_specs=in_specs,
        out_specs=[seg(0, TILE_RW), seg(TILE_RW, TILE_NA - TILE_RW), seg(TILE_NA, TILE_DF - TILE_NA),
                   seg(TILE_DF, Z_WIDTH // tn - TILE_DF)],
        out_shape=[out(TILE_RW, BF16), out(TILE_NA - TILE_RW, F32), out(TILE_DF - TILE_NA, BF16),
                   out(Z_WIDTH // tn - TILE_DF, BF16)],
        scratch_shapes=[pltpu.VMEM((tm, D), BF16)],
        compiler_params=_cparams(("parallel", "parallel", "arbitrary")),
        name="inproj",
    )(*args)


def _merge_kernel(x_ref, mod_ref, gpost_ref, orw_ref, ona_ref, odf_ref, rg_ref, ng_ref, dg_ref, mg_ref,
                  wb_ref, wo_ref, o_ref):
    D = x_ref.shape[-1]
    acc = None
    for n, (ob_ref, gate_ref) in enumerate(((orw_ref, rg_ref), (ona_ref, ng_ref), (odf_ref, dg_ref))):
        yb = _bdot(ob_ref[0] * _silu(gate_ref[0].astype(F32)), wb_ref[n])
        term = _sigmoid(mg_ref[0, :, n * D:(n + 1) * D].astype(F32)) * yb
        acc = term if acc is None else acc + term
    y = _bdot(acc, wo_ref[...])
    yn = y * lax.rsqrt(jnp.mean(y * y, axis=-1, keepdims=True) + RMS_EPS) * gpost_ref[...]
    o_ref[0] = x_ref[0] + mod_ref[0, 2:3, :] * yn


def _merge(x, mod, g_post, o_rw, o_na, o_df, zg, w_branch, w_out, tm):
    B, T, D = x.shape
    per_batch = mod.shape[0] > 1
    bw = BRANCH_WIDTH
    row = lambda b, m: (b, m, 0)
    col = lambda c: (lambda b, m: (b, m, c))
    return pl.pallas_call(
        _merge_kernel,
        grid=(B, T // tm),
        in_specs=[pl.BlockSpec((1, tm, D), row),
                  pl.BlockSpec((1, 3, D), (lambda b, m: (b, 0, 0)) if per_batch else (lambda b, m: (0, 0, 0))),
                  pl.BlockSpec((1, D), lambda b, m: (0, 0)),
                  pl.BlockSpec((1, tm, bw), row), pl.BlockSpec((1, tm, bw), row), pl.BlockSpec((1, tm, bw), row),
                  pl.BlockSpec((1, tm, bw), col(N_BRANCH * D // bw)),
                  pl.BlockSpec((1, tm, bw), col(N_BRANCH * D // bw + 1)),
                  pl.BlockSpec((1, tm, bw), col(N_BRANCH * D // bw + 2)),
                  pl.BlockSpec((1, tm, N_BRANCH * D), col(0)),
                  pl.BlockSpec((N_BRANCH, bw, D), lambda b, m: (0, 0, 0)),
                  pl.BlockSpec((D, D), lambda b, m: (0, 0))],
        out_specs=pl.BlockSpec((1, tm, D), row),
        out_shape=jax.ShapeDtypeStruct((B, T, D), F32),
        compiler_params=_cparams(("parallel", "parallel")),
        name="merge",
    )(x, mod, g_post.reshape(1, D), o_rw, o_na, o_df, zg, zg, zg, zg, w_branch, w_out)


def _rope_tables(n_tokens):
    t = np.arange(n_tokens)
    axis_dim = DIFF_QK_DIM // 2
    inv = ROPE_THETA ** (-np.arange(0, axis_dim, 2, dtype=np.float32) / axis_dim)
    ar = (t // GRID_W).astype(np.float32)[:, None] * inv
    ac = (t % GRID_W).astype(np.float32)[:, None] * inv
    ang = jnp.asarray(np.concatenate([ar, ar, ac, ac], axis=-1).astype(np.float32))
    cos, sin = jnp.cos(ang), jnp.sin(ang)
    first = (np.arange(DIFF_QK_DIM) % 32) < 16
    sin_a = jnp.where(first, -sin, 0.0)
    sin_b = jnp.where(first, 0.0, sin)
    tile = lambda a: jnp.tile(a, (1, 2))
    return tile(cos), tile(sin_a), tile(sin_b)


def _dot_nt(a, b):
    return lax.dot_general(a, b, (((1,), (1,)), ((), ())), preferred_element_type=F32)


SOFTMAX_ROWS = 128


def _diff_attn_kernel(q_ref, kc_ref, vc_ref, k_ref, v_ref, lq_ref, lk_ref, g_ref, o_ref,
                      sc, pc, s0, s1, p0, p1, al0, al1, m_scr, acc_scr, *, n_main, tk, lambda_init):
    tq = q_ref.shape[1]
    t_first = kc_ref.shape[1]
    q = q_ref[0]
    lo = lax.broadcasted_iota(jnp.int32, (1, 128), 1) < DIFF_QK_DIM
    zero = jnp.zeros_like(q)
    qq = jnp.concatenate([jnp.where(lo, q, zero), jnp.where(lo, zero, q)], axis=0)

    n_sub = 2 * tq // SOFTMAX_ROWS
    sub = lambda i: slice(i * SOFTMAX_ROWS, (i + 1) * SOFTMAX_ROWS)

    def scores(k_chunk, s_scr):
        s_scr[:, :k_chunk.shape[0]] = _dot_nt(qq, k_chunk)

    def softmax(size, first, s_scr, p_scr, al_scr):
        w = min(128, size)
        cols = [slice(c * w, (c + 1) * w) for c in range(size // w)]
        for i in range(n_sub):
            mx = s_scr[sub(i), cols[0]]
            for c in cols[1:]:
                mx = jnp.maximum(mx, s_scr[sub(i), c])
            mn = jnp.broadcast_to(jnp.max(mx, axis=-1, keepdims=True), (SOFTMAX_ROWS, 128))
            if not first:
                mo = m_scr[sub(i), :]
                mn = jnp.maximum(mo, mn)
                al_scr[sub(i), :] = jnp.exp2(mo - mn)
            m_scr[sub(i), :] = mn
        for i in range(n_sub):
            mn = m_scr[sub(i), :w]
            for c in cols:
                p_scr[sub(i), c] = jnp.exp2(s_scr[sub(i), c] - mn).astype(BF16)

    def accumulate(v, first, p_scr, al_scr):
        size = v.shape[0]
        v_ext = jnp.concatenate([v, jnp.ones((size, 128), BF16)], axis=1)
        pv = jnp.dot(p_scr[:, :size], v_ext, preferred_element_type=F32)
        if first:
            acc_scr[...] = pv
        else:
            al = al_scr[...]
            acc_scr[...] = acc_scr[...] * jnp.concatenate([al, al], axis=1) + pv

    scores(kc_ref[0], sc)
    if n_main == 0:
        softmax(t_first, True, sc, pc, None)
        accumulate(vc_ref[0], True, pc, None)
    else:
        def chunk_of(ref, j):
            start = j * tk if isinstance(j, int) else pl.multiple_of(j * tk, tk)
            return ref[0, pl.ds(start, tk), :]

        k_at = functools.partial(chunk_of, k_ref)
        v_at = functools.partial(chunk_of, v_ref)
        scores(k_at(0), s0)
        softmax(t_first, True, sc, pc, None)

        def pair(j, first):
            scores(k_at(j + 1), s1)
            softmax(tk, False, s0, p0, al0)
            if first:
                accumulate(vc_ref[0], True, pc, None)
            else:
                accumulate(v_at(j - 1), False, p1, al1)
            nxt = min(j + 2, n_main - 1) if isinstance(j, int) else jnp.minimum(j + 2, n_main - 1)
            scores(k_at(nxt), s0)
            softmax(tk, False, s1, p1, al1)
            accumulate(v_at(j), False, p0, al0)

        pair(0, True)

        def body(t, _):
            pair(2 * t, False)
            return 0

        lax.fori_loop(1, n_main // 2, body, 0)
        accumulate(v_at(n_main - 1), False, p1, al1)
    lqk = lq_ref[...] * lk_ref[...]
    e = jnp.exp(jnp.sum(lqk, axis=-1, keepdims=True))
    lam = e[0:1] - e[1:2] + lambda_init
    o_maps = acc_scr[:, :DIFF_V_DIM] / acc_scr[:, DIFF_V_DIM:]
    o = o_maps[:tq] - lam * o_maps[tq:]
    o = o * lax.rsqrt(jnp.mean(o * o, axis=-1, keepdims=True) + SUBLN_EPS) * g_ref[...]
    o_ref[0] = o * (1.0 - lambda_init)


def _diff_attn(qkv, qkv_c, with_latent_keys, lam_q, lam_k, subln_g, lambda_init, tq, tk):
    B, T, _ = qkv.shape
    C = qkv_c.shape[1]
    n_main = T // tk if with_latent_keys else 0
    assert n_main % 2 == 0 and (2 * tq) % SOFTMAX_ROWS == 0
    wmax = tk if n_main else 128
    H = DIFF_HEADS
    kv = lambda t, j: pl.BlockSpec((1, t, 128), lambda b, h, m: (b, 0, j * H + h))
    small = lambda r, c: pl.BlockSpec((r, c), lambda b, h, m: (0, 0))
    stat = pltpu.VMEM((2 * tq, 128), F32)
    s_buf = pltpu.VMEM((2 * tq, wmax), F32)
    p_buf = pltpu.VMEM((2 * tq, wmax), BF16)
    return pl.pallas_call(
        functools.partial(_diff_attn_kernel, n_main=n_main, tk=tk, lambda_init=lambda_init),
        grid=(B, H, T // tq),
        in_specs=[pl.BlockSpec((1, tq, 128), lambda b, h, m: (b, m, h)), kv(C, 1), kv(C, 2), kv(T, 1), kv(T, 2),
                  small(2, DIFF_QK_DIM), small(2, DIFF_QK_DIM), small(1, DIFF_V_DIM)],
        out_specs=pl.BlockSpec((1, tq, 128), lambda b, h, m: (b, m, h)),
        out_shape=jax.ShapeDtypeStruct((B, T, H * DIFF_V_DIM), F32),
        scratch_shapes=[pltpu.VMEM((2 * tq, C), F32), pltpu.VMEM((2 * tq, C), BF16), s_buf, s_buf, p_buf, p_buf,
                        stat, stat, stat, pltpu.VMEM((2 * tq, 2 * DIFF_V_DIM), F32)],
        compiler_params=_cparams(("parallel", "parallel", "parallel")),
        name="diff_attn",
    )(qkv, qkv_c, qkv_c, qkv, qkv, lam_q, lam_k, subln_g.reshape(1, DIFF_V_DIM))


def _na_bias_tables(rpb):
    c_idx = np.arange(GRID_W)
    c_start = np.clip(c_idx - NA_WIN_C // 2, 0, GRID_W - NA_WIN_C)
    col_ok = (c_idx[None, :] >= c_start[:, None]) & (c_idx[None, :] < c_start[:, None] + NA_WIN_C)
    dc = np.clip(c_idx[None, :] - c_idx[:, None], -(NA_WIN_C - 1), NA_WIN_C - 1) + (NA_WIN_C - 1)
    H = rpb.shape[0]
    e = jnp.take(rpb, jnp.asarray(dc.reshape(-1)), axis=2).reshape(H, 2 * NA_WIN_R - 1, GRID_W, GRID_W)
    e = jnp.where(col_ok, e, NEG_INF)
    tabs = [e[:, NA_WIN_R - 1 - off:2 * NA_WIN_R - 1 - off].transpose(0, 2, 1, 3).reshape(H, GRID_W, NA_WIN_R * GRID_W)
            for off in range(NA_WIN_R)]
    return jnp.stack(tabs, axis=1)


def _na_kernel(q_ref, k_ref, v_ref, kc_ref, vc_ref, bias_ref, o_ref, *, rq, rows):
    i = pl.program_id(2)
    lo = lax.broadcasted_iota(jnp.int32, (1, 128), 1) < HEAD_DIM
    win = NA_WIN_R * GRID_W
    G2 = 2 * GRID_W
    q = q_ref[0]
    zero = jnp.zeros_like(q)
    q_lo, q_hi = jnp.where(lo, q, zero), jnp.where(lo, zero, q)
    qq = jnp.concatenate([x[rr * GRID_W:(rr + 1) * GRID_W] for rr in range(rq) for x in (q_lo, q_hi)], axis=0)
    kc = kc_ref[0]
    ones = lambda n: jnp.ones((n, 128), BF16)
    s_c = _dot_nt(qq, kc)

    starts, s_nb = [], []
    for rr in range(rq):
        r = i * rq + rr
        r_start = jnp.clip(r - NA_WIN_R // 2, 0, rows - NA_WIN_R)
        off = r - r_start
        start = pl.multiple_of(r_start * GRID_W, GRID_W)
        starts.append(start)
        bias = jnp.concatenate([bias_ref[0, off], bias_ref[1, off]], axis=0)
        s_nb.append(_dot_nt(qq[rr * G2:(rr + 1) * G2], k_ref[0, pl.ds(start, win), :]) + bias)

    def lane_blocks(t):
        w = min(128, t.shape[-1])
        return [t[:, c * w:(c + 1) * w] for c in range(t.shape[-1] // w)]

    p_nb, p_c = [], []
    for rr in range(rq):
        sc = s_c[rr * G2:(rr + 1) * G2]
        blocks = lane_blocks(s_nb[rr])
        mx = blocks[0]
        for b in blocks[1:]:
            mx = jnp.maximum(mx, b)
        m = jnp.maximum(jnp.max(mx, axis=-1, keepdims=True), jnp.max(sc, axis=-1, keepdims=True))
        p_nb.append(jnp.exp2(s_nb[rr] - m).astype(BF16))
        p_c.append(jnp.exp2(sc - m).astype(BF16))

    vc_ext = jnp.concatenate([vc_ref[0], ones(kc.shape[0])], axis=1)
    o_c = jnp.dot(jnp.concatenate(p_c, axis=0), vc_ext, preferred_element_type=F32)
    for rr in range(rq):
        vw_ext = jnp.concatenate([v_ref[0, pl.ds(starts[rr], win), :], ones(win)], axis=1)
        o = jnp.dot(p_nb[rr], vw_ext, preferred_element_type=F32) + o_c[rr * G2:(rr + 1) * G2]
        o = o[:, :128] / o[:, 128:]
        o_ref[0, rr * GRID_W:(rr + 1) * GRID_W, :] = jnp.where(lo, o[:GRID_W], o[GRID_W:])


def _na_attn(qkv, qkv_c, bias, rq):
    B, S, _ = qkv.shape
    C = qkv_c.shape[1]
    rows = S // GRID_W
    npair = NA_HEADS // 2
    full = lambda t, j: pl.BlockSpec((1, t, 128), lambda b, p, m: (b, 0, j * npair + p))
    return pl.pallas_call(
        functools.partial(_na_kernel, rq=rq, rows=rows),
        grid=(B, npair, rows // rq),
        in_specs=[pl.BlockSpec((1, rq * GRID_W, 128), lambda b, p, m: (b, m, p)),
                  full(S, 1), full(S, 2), full(C, 1), full(C, 2),
                  pl.BlockSpec((2, NA_WIN_R, GRID_W, NA_WIN_R * GRID_W), lambda b, p, m: (p, 0, 0, 0))],
        out_specs=pl.BlockSpec((1, rq * GRID_W, 128), lambda b, p, m: (b, m, p)),
        out_shape=jax.ShapeDtypeStruct((B, S, NA_HEADS * HEAD_DIM), F32),
        compiler_params=_cparams(("parallel", "parallel", "parallel")),
        name="na_attn",
    )(qkv, qkv, qkv, qkv_c, qkv_c, bias)


def _ctx_attn_kernel(q_ref, k_ref, v_ref, o_ref):
    lo = lax.broadcasted_iota(jnp.int32, (1, 128), 1) < HEAD_DIM
    q, k, v = q_ref[0], k_ref[0], v_ref[0]
    zero = jnp.zeros_like(q)
    outs = []
    for hl in range(2):
        qh = jnp.where(lo, q, zero) if hl == 0 else jnp.where(lo, zero, q)
        s = _dot_nt(qh, k)
        p = jnp.exp2(s - jnp.max(s, axis=-1, keepdims=True))
        o = jnp.dot(p.astype(BF16), v, preferred_element_type=F32)
        outs.append(o / jnp.sum(p, axis=-1, keepdims=True))
    o_ref[0] = jnp.where(lo, outs[0], outs[1])


def _ctx_attn(qkv):
    B, C, _ = qkv.shape
    npair = NA_HEADS // 2
    blk = lambda j: pl.BlockSpec((1, C, 128), lambda b, p: (b, 0, j * npair + p))
    return pl.pallas_call(
        _ctx_attn_kernel,
        grid=(B, npair),
        in_specs=[blk(0), blk(1), blk(2)],
        out_specs=blk(0),
        out_shape=jax.ShapeDtypeStruct((B, C, NA_HEADS * HEAD_DIM), F32),
        compiler_params=_cparams(("parallel", "parallel")),
        name="ctx_attn",
    )(qkv, qkv, qkv)


PL_R, PL_V, PL_KK = 0, 1, 2
PL_LOGW, PL_KDIR, PL_B = 3, 4, 5
N_PLANES = 9
RWKV_CHUNK = 64


def _split3(x):
    hi = x.astype(BF16)
    r1 = x - hi.astype(F32)
    mid = r1.astype(BF16)
    lo = (r1 - mid.astype(F32)).astype(BF16)
    return hi, mid, lo


def _dot_exact_rhs(m, x):
    hi, mid, lo = _split3(x)
    mb = m.astype(BF16)
    d = lambda t: jnp.dot(mb, t, preferred_element_type=F32)
    return d(hi) + d(mid) + d(lo)


def _dot_exact_lhs(x, m):
    hi, mid, lo = _split3(x)
    mb = m.astype(BF16)
    d = lambda t: jnp.dot(t, mb, preferred_element_type=F32)
    return d(hi) + d(mid) + d(lo)


def _head_ones(n):
    r = lax.broadcasted_iota(jnp.int32, (n, n), 0) // HEAD_DIM
    c = lax.broadcasted_iota(jnp.int32, (n, n), 1) // HEAD_DIM
    return (r == c).astype(F32)


def _rwkv_prep_kernel(zc_ref, zp_ref, zn_ref, mu_ref, kk_ref, ka_ref, w0_ref, a0_ref, wup_ref, aup_ref, o_ref):
    tm = zc_ref.shape[1]
    m = pl.program_id(1)
    u = zc_ref[0]
    prev_row = jnp.where(m > 0, zp_ref[0, 7:8, :], 0.0)
    next_row = jnp.where(m < pl.num_programs(1) - 1, zn_ref[0, 0:1, :], 0.0)
    rows = lax.broadcasted_iota(jnp.int32, (tm, 1), 0)
    u_prev = jnp.where(rows == 0, prev_row, pltpu.roll(u, 1, 0))
    u_next = jnp.where(rows == tm - 1, next_row, pltpu.roll(u, tm - 1, 0))
    u = u + mu_ref[0:1, :] * (u_prev - u) + mu_ref[1:2, :] * (u_next - u)

    W = RWKV_WIDTH
    r, k, v = u[:, 0:W], u[:, W:2 * W], u[:, 2 * W:3 * W]
    lw = jnp.tanh(u[:, 3 * W:3 * W + 2 * LORA]).astype(BF16)
    la = u[:, 3 * W + 2 * LORA:3 * W + 4 * LORA].astype(BF16)
    ones = _head_ones(W)
    kk = k * kk_ref[...]
    ss = _dot_exact_lhs(kk * kk, ones)
    kk = kk / jnp.maximum(jnp.sqrt(ss), 1e-12)
    o_ref[0, :, PL_R * W:(PL_R + 1) * W] = r
    o_ref[0, :, PL_V * W:(PL_V + 1) * W] = v
    o_ref[0, :, PL_KK * W:(PL_KK + 1) * W] = kk
    for d in range(2):
        xw = w0_ref[d:d + 1, :] + jnp.dot(lw, wup_ref[d], preferred_element_type=F32)
        logw = -math.exp(-0.5) * _sigmoid(xw)
        a = _sigmoid(a0_ref[d:d + 1, :] + jnp.dot(la, aup_ref[d], preferred_element_type=F32))
        kdir = k * (1.0 + (a - 1.0) * ka_ref[...])
        base = 3 * d
        o_ref[0, :, (PL_LOGW + base) * W:(PL_LOGW + base + 1) * W] = logw
        o_ref[0, :, (PL_KDIR + base) * W:(PL_KDIR + base + 1) * W] = kdir
        o_ref[0, :, (PL_B + base) * W:(PL_B + base + 1) * W] = kk * a


def _rwkv_prep(z, mu_pad, k_k, k_a, w0, a0, wup2, aup2, tm):
    B, T, wz = z.shape
    nb = tm // 8
    W = RWKV_WIDTH
    const = lambda shape: pl.BlockSpec(shape, lambda b, m: (0,) * len(shape))
    return pl.pallas_call(
        _rwkv_prep_kernel,
        grid=(B, T // tm),
        in_specs=[pl.BlockSpec((1, tm, wz), lambda b, m: (b, m, 0)),
                  pl.BlockSpec((1, 8, wz), lambda b, m: (b, jnp.maximum(m * nb - 1, 0), 0)),
                  pl.BlockSpec((1, 8, wz), lambda b, m: (b, jnp.minimum((m + 1) * nb, T // 8 - 1), 0)),
                  const((2, wz)), const((1, W)), const((1, W)), const((2, W)), const((2, W)),
                  const((2, 2 * LORA, W)), const((2, 2 * LORA, W))],
        out_specs=pl.BlockSpec((1, tm, N_PLANES * W), lambda b, m: (b, m, 0)),
        out_shape=jax.ShapeDtypeStruct((B, T, N_PLANES * W), F32),
        compiler_params=_cparams(("parallel", "parallel")),
        name="rwkv_prep",
    )(z, z, z, mu_pad, k_k.reshape(1, W), k_a.reshape(1, W), w0, a0, wup2, aup2)


def _rwkv_scan_kernel(pf_ref, pb_ref, s0_ref, yf_ref, yb_ref, s_ref):
    C = RWKV_CHUNK
    W = RWKV_WIDTH
    C2 = 2 * C

    @pl.when(pl.program_id(1) == 0)
    def _():
        s_ref[...] = s0_ref[...]

    lo = lax.broadcasted_iota(jnp.int32, (1, 128), 1) < HEAD_DIM
    row = lax.broadcasted_iota(jnp.int32, (C2, C2), 0)
    col = lax.broadcasted_iota(jnp.int32, (C2, C2), 1)
    same = (row // C) == (col // C)
    eye = (row == col).astype(F32)
    tri_r = lax.broadcasted_iota(jnp.int32, (C, C), 0)
    tri_c = lax.broadcasted_iota(jnp.int32, (C, C), 1)
    levels = [((row >> k) == (col >> k)) & ((row >> (k - 1)) != (col >> (k - 1))) for k in range(1, 7)]
    p_refs, y_refs = (pf_ref, pb_ref), (yf_ref, yb_ref)
    groups = [(d, p) for d in range(2) for p in range(W // 128)]
    ng = len(groups)
    bf = lambda t: t.astype(BF16)

    def stack(x):
        return jnp.concatenate([jnp.where(lo, x, 0.0), jnp.where(lo, 0.0, x)], axis=0)

    c_dir = []
    for d in range(2):
        cum = ((tri_c <= tri_r) if d == 0 else (tri_c >= tri_r)).astype(F32)
        logw_all = p_refs[d][0, :, (PL_LOGW + 3 * d) * W:(PL_LOGW + 3 * d + 1) * W]
        c_dir.append(_dot_exact_rhs(cum, logw_all))

    ar, bk, ends, vs, decay_end = [], [], [], [], []
    for d, p in groups:
        get = lambda plane: p_refs[d][0, :, plane * W + p * 128:plane * W + (p + 1) * 128]
        r, v, kk = get(PL_R), get(PL_V), get(PL_KK)
        logw, kdir, bb = get(PL_LOGW + 3 * d), get(PL_KDIR + 3 * d), get(PL_B + 3 * d)
        c = c_dir[d][:, p * 128:(p + 1) * 128]
        last = C - 1 if d == 0 else 0
        c_last = c[last:last + 1, :]
        e_neg = jnp.exp(-c)
        e_end = jnp.exp(c_last - c)
        ar.append(bf(jnp.concatenate([stack(-kk * jnp.exp(c - logw)), stack(r * jnp.exp(c))], axis=0)))
        bk.append(bf(jnp.concatenate([stack(bb * e_neg), stack(kdir * e_neg)], axis=0)))
        ends.append(bf(jnp.concatenate([stack(bb * e_end), stack(kdir * e_end)], axis=0)))
        vs.append(bf(stack(v)))
        decay_end.append(jnp.exp(c_last))

    amat = [_dot_nt(ar[g], bk[g]) for g in range(ng)]
    a_ab, a_kr, a_rb = [], [], []
    for g, (d, p) in enumerate(groups):
        before = (col < row) if d == 0 else (col > row)
        strict = same & before
        incl = same & (before | (row == col))
        m = amat[g]
        a_ab.append(jnp.where(strict, m[:C2, :C2], 0.0))
        a_kr.append(bf(jnp.concatenate([jnp.where(strict, m[:C2, C2:], 0.0),
                                        jnp.where(incl, m[C2:, C2:], 0.0)], axis=0)))
        a_rb.append(bf(jnp.where(incl, m[C2:, :C2], 0.0)))

    t = [eye + jnp.where(levels[0], a, 0.0) for a in a_ab]
    for lvl in levels[1:]:
        ta = [jnp.dot(bf(t[g]), bf(jnp.where(lvl, a_ab[g], 0.0)), preferred_element_type=F32) for g in range(ng)]
        t = [t[g] + jnp.dot(bf(ta[g]), bf(t[g]), preferred_element_type=F32) for g in range(ng)]

    s_old = [s_ref[0, d, p] for d, p in groups]
    from_s = [_dot_nt(ar[g], bf(s_old[g])) for g in range(ng)]
    from_v = [jnp.dot(a_kr[g], vs[g], preferred_element_type=F32) for g in range(ng)]
    u = [jnp.dot(bf(t[g]), bf(from_s[g][:C2] + from_v[g][:C2]), preferred_element_type=F32) for g in range(ng)]
    for g, (d, p) in enumerate(groups):
        y_s = from_s[g][C2:] + from_v[g][C2:] + jnp.dot(a_rb[g], bf(u[g]), preferred_element_type=F32)
        y_refs[d][0, :, p * 128:(p + 1) * 128] = y_s[:C] + y_s[C:]
    for g, (d, p) in enumerate(groups):
        uv = jnp.concatenate([bf(u[g]), vs[g]], axis=0)
        s_ref[0, d, p] = s_old[g] * decay_end[g] + lax.dot_general(
            uv, ends[g], (((0,), (0,)), ((), ())), preferred_element_type=F32)


def _rwkv_scan(planes, s0):
    B, T, _ = planes.shape
    nc = T // RWKV_CHUNK
    W = RWKV_WIDTH
    st = pl.BlockSpec((1, 2, W // 128, 128, 128), lambda b, i: (b, 0, 0, 0, 0))
    y = jax.ShapeDtypeStruct((B, T, W), F32)
    return pl.pallas_call(
        _rwkv_scan_kernel,
        grid=(B, nc),
        in_specs=[pl.BlockSpec((1, RWKV_CHUNK, N_PLANES * W), lambda b, i: (b, i, 0)),
                  pl.BlockSpec((1, RWKV_CHUNK, N_PLANES * W), lambda b, i: (b, nc - 1 - i, 0)),
                  st],
        out_specs=[pl.BlockSpec((1, RWKV_CHUNK, W), lambda b, i: (b, i, 0)),
                   pl.BlockSpec((1, RWKV_CHUNK, W), lambda b, i: (b, nc - 1 - i, 0)),
                   st],
        out_shape=[y, y, jax.ShapeDtypeStruct(s0.shape, F32)],
        compiler_params=_cparams(("parallel", "arbitrary")),
        name="rwkv_scan",
    )(planes, planes, s0)


def _rwkv_readout_kernel(yf_ref, yb_ref, r_ref, v_ref, kf_ref, kb_ref, g_ref, b_ref, rk_ref, o_ref):
    ones = _head_ones(RWKV_WIDTH)
    y = yf_ref[0] + yb_ref[0]
    inv_n = 1.0 / HEAD_DIM
    mu = _dot_exact_lhs(y, ones) * inv_n
    yc = y - mu
    var = _dot_exact_lhs(yc * yc, ones) * inv_n
    yn = yc * lax.rsqrt(var + GN_EPS) * g_ref[...] + b_ref[...]
    bonus = _dot_exact_lhs(r_ref[0] * (kf_ref[0] + kb_ref[0]) * rk_ref[...], ones)
    o_ref[0] = yn + bonus * v_ref[0]


def _rwkv_readout(y_f, y_b, planes, ln_g, ln_b, r_k, tm):
    B, T, W = y_f.shape
    row = pl.BlockSpec((1, tm, W), lambda b, m: (b, m, 0))
    plane = lambda j: pl.BlockSpec((1, tm, W), lambda b, m: (b, m, j))
    vec = pl.BlockSpec((1, W), lambda b, m: (0, 0))
    return pl.pallas_call(
        _rwkv_readout_kernel,
        grid=(B, T // tm),
        in_specs=[row, row, plane(PL_R), plane(PL_V), plane(PL_KDIR), plane(PL_KDIR + 3), vec, vec, vec],
        out_specs=row,
        out_shape=jax.ShapeDtypeStruct((B, T, W), F32),
        compiler_params=_cparams(("parallel", "parallel")),
        name="rwkv_readout",
    )(y_f, y_b, planes, planes, planes, planes, ln_g.reshape(1, W), ln_b.reshape(1, W), r_k.reshape(1, W))


def _layer_weights(l, w_in, shift_mu, w_up, a_up, w_branch, w_out):
    rw, rg, na, ng, df, dg, mg = jnp.split(w_in[l], [int(i) for i in np.cumsum(IN_SIZES)[:-1]], axis=-1)
    pad = jnp.zeros((w_in.shape[1], RWKV_PAD_WIDTH - RWKV_SHIFT_WIDTH), w_in.dtype)
    w_perm = jnp.concatenate([mg, rg, ng, dg, rw, pad, na, df], axis=-1).astype(BF16)
    mu_pad = jnp.pad(shift_mu[l], ((0, 0), (0, RWKV_PAD_WIDTH - RWKV_SHIFT_WIDTH)))
    zl = jnp.zeros((LORA, RWKV_WIDTH), F32)
    wup2 = jnp.stack([jnp.concatenate([w_up[l, 0], zl]), jnp.concatenate([zl, w_up[l, 1]])]).astype(BF16)
    aup2 = jnp.stack([jnp.concatenate([a_up[l, 0], zl]), jnp.concatenate([zl, a_up[l, 1]])]).astype(BF16)
    return w_perm, mu_pad, wup2, aup2, w_branch[l].astype(BF16), w_out[l].astype(BF16)


def kernel(x, c, ctx, c_ctx, w_mod, b_mod, g_pre, g_post, w_in, shift_mu, k_k, k_a, r_k, w0, w_up, a0, a_up,
           ln_x_g, ln_x_b, rpb, lam_q, lam_k, diff_subln, w_branch, w_out):
    B, S, D = x.shape
    C = ctx.shape[1]
    depth = w_in.shape[0]
    tables = _rope_tables(S)
    rows_pad = 16
    cvec = jnp.zeros((rows_pad, D), F32).at[:B].set(c).at[B].set(c_ctx)
    mod = _modulation(cvec, w_mod, b_mod)
    hc = ctx
    tm_x = min(1024, S)
    for l in range(depth):
        last = l == depth - 1
        lambda_init = 0.8 - 0.6 * math.exp(-0.3 * l)
        w_perm, mu_pad, wup2, aup2, wb, wo = _layer_weights(l, w_in, shift_mu, w_up, a_up, w_branch, w_out)
        mod_x = mod[l, :B].reshape(B, 3, D)
        mod_c = mod[l, B:B + 1].reshape(1, 3, D)
        zg_x, rw_x, na_x, df_x = _inproj(x, mod_x, g_pre[l], w_perm, tables, tm_x)
        zg_c, rw_c, na_c, df_c = _inproj(hc, mod_c, g_pre[l], w_perm, None, C)

        pl_c = _rwkv_prep(rw_c, mu_pad, k_k[l], k_a[l], w0[l], a0[l], wup2, aup2, min(256, C))
        pl_x = _rwkv_prep(rw_x, mu_pad, k_k[l], k_a[l], w0[l], a0[l], wup2, aup2, 256)
        s_zero = jnp.zeros((B, 2, RWKV_WIDTH // 128, 128, 128), F32)
        yf_c, yb_c, s_ctx = _rwkv_scan(pl_c, s_zero)
        yf_x, yb_x, _ = _rwkv_scan(pl_x, s_ctx)
        r_k_flat = r_k[l].reshape(RWKV_WIDTH)
        o_rw_x = _rwkv_readout(yf_x, yb_x, pl_x, ln_x_g[l], ln_x_b[l], r_k_flat, 256)

        o_na_x = _na_attn(na_x, na_c, _na_bias_tables(rpb[l]) * LOG2E, 8)
        o_df_x = _diff_attn(df_x, df_c, True, lam_q[l], lam_k[l], diff_subln[l], lambda_init, 512, min(512, S // 2))

        x = _merge(x, mod_x, g_post[l], o_rw_x, o_na_x, o_df_x, zg_x, wb, wo, 256)
        if not last:
            o_rw_c = _rwkv_readout(yf_c, yb_c, pl_c, ln_x_g[l], ln_x_b[l], r_k_flat, min(256, C))
            o_na_c = _ctx_attn(na_c)
            o_df_c = _diff_attn(df_c, df_c, False, lam_q[l], lam_k[l], diff_subln[l], lambda_init, C, C)
            hc = _merge(hc, mod_c, g_post[l], o_rw_c, o_na_c, o_df_c, zg_c, wb, wo, min(256, C))
    return x
```

```python
import functools
import math

import jax
import jax.numpy as jnp
import numpy as np
from jax import lax
from jax.experimental import pallas as pl
from jax.experimental.pallas import tpu as pltpu

F32 = jnp.float32
BF16 = jnp.bfloat16

DEPTH = 4
GRID_W = 64
RWKV_HEADS = 8
HEAD_DIM = 64
RWKV_WIDTH = 512
LORA = 64
GN_EPS = 64e-5
NA_HEADS = 8
NA_WIN_R = 8
NA_WIN_C = 16
DIFF_HEADS = 4
DIFF_QK_DIM = 64
DIFF_V_DIM = 128
ROPE_THETA = 10000.0
SUBLN_EPS = 1e-5
RMS_EPS = 1e-6
NEG_INF = -1e30
LOG2E = math.log2(math.e)
BRANCH_WIDTH = 512
N_BRANCH = 3

RWKV_SHIFT_WIDTH = 3 * RWKV_WIDTH + 4 * LORA
RWKV_PAD_WIDTH = 2048
TILE_W = 512
TILE_RW, TILE_NA, TILE_DF = 9, 13, 16
Z_WIDTH = 19 * TILE_W
IN_SIZES = (RWKV_SHIFT_WIDTH, 512, 1536, 512, 1536, 512, 3072)

VMEM_LIMIT = 56 * 1024 * 1024


def _cparams(sem):
    return pltpu.CompilerParams(dimension_semantics=sem, vmem_limit_bytes=VMEM_LIMIT)


def _sigmoid(x):
    return 1.0 / (1.0 + jnp.exp(-x))


def _silu(x):
    return x * _sigmoid(x)


def _bdot(a, b):
    return jnp.dot(a.astype(BF16), b.astype(BF16), preferred_element_type=F32)


def _mod_kernel(c_ref, w_ref, b_ref, o_ref):
    o_ref[0] = _bdot(_silu(c_ref[...]), w_ref[0]) + b_ref[0]


def _modulation(cvec, w_mod, b_mod):
    L, D, N = w_mod.shape
    R = cvec.shape[0]
    tn = 1024
    return pl.pallas_call(
        _mod_kernel,
        grid=(L, N // tn),
        in_specs=[pl.BlockSpec((R, D), lambda l, n: (0, 0)),
                  pl.BlockSpec((1, D, tn), lambda l, n: (l, 0, n)),
                  pl.BlockSpec((1, 1, tn), lambda l, n: (l, 0, n))],
        out_specs=pl.BlockSpec((1, R, tn), lambda l, n: (l, 0, n)),
        out_shape=jax.ShapeDtypeStruct((L, R, N), F32),
        compiler_params=_cparams(("parallel", "parallel")),
        name="modulation",
    )(cvec, w_mod, b_mod.reshape(L, 1, N))


def _inproj_kernel(*refs, rope, q_scale):
    if rope:
        x_ref, mod_ref, g_ref, w_ref, cos_ref, sa_ref, sb_ref, zg_ref, rw_ref, na_ref, df_ref, h_ref = refs
    else:
        x_ref, mod_ref, g_ref, w_ref, zg_ref, rw_ref, na_ref, df_ref, h_ref = refs
    n = pl.program_id(2)

    @pl.when(n == 0)
    def _():
        x = x_ref[0]
        y = x * lax.rsqrt(jnp.mean(x * x, axis=-1, keepdims=True) + RMS_EPS) * g_ref[...]
        h_ref[...] = (y * (1.0 + mod_ref[0, 1:2, :]) + mod_ref[0, 0:1, :]).astype(BF16)

    mm = lambda: jnp.dot(h_ref[...], w_ref[...], preferred_element_type=F32)

    def rot(t):
        if not rope:
            return t
        w = t.shape[-1]
        tile = lambda r: jnp.concatenate([r[...]] * (w // r.shape[-1]), axis=1)
        return t * tile(cos_ref) + pltpu.roll(t, w - 16, 1) * tile(sa_ref) + pltpu.roll(t, 16, 1) * tile(sb_ref)

    @pl.when(n < TILE_RW)
    def _():
        zg_ref[0] = mm().astype(BF16)

    @pl.when((n >= TILE_RW) & (n < TILE_NA))
    def _():
        rw_ref[0] = mm()

    @pl.when(n == TILE_NA)
    def _():
        na_ref[0] = (mm() * q_scale).astype(BF16)

    @pl.when((n > TILE_NA) & (n < TILE_DF))
    def _():
        na_ref[0] = mm().astype(BF16)

    @pl.when(n == TILE_DF)
    def _():
        df_ref[0] = (rot(mm()) * q_scale).astype(BF16)

    @pl.when(n == TILE_DF + 1)
    def _():
        df_ref[0] = rot(mm()).astype(BF16)

    @pl.when(n == TILE_DF + 2)
    def _():
        df_ref[0] = mm().astype(BF16)


def _inproj(x, mod, g_pre, w_perm, tables, tm):
    B, T, D = x.shape
    tn = TILE_W
    per_batch = mod.shape[0] > 1
    in_specs = [pl.BlockSpec((1, tm, D), lambda b, m, n: (b, m, 0)),
                pl.BlockSpec((1, 3, D), (lambda b, m, n: (b, 0, 0)) if per_batch else (lambda b, m, n: (0, 0, 0))),
                pl.BlockSpec((1, D), lambda b, m, n: (0, 0)),
                pl.BlockSpec((D, tn), lambda b, m, n: (0, n))]
    args = [x, mod, g_pre.reshape(1, D), w_perm]
    if tables is not None:
        in_specs += [pl.BlockSpec((tm, 128), lambda b, m, n: (m, 0))] * 3
        args += list(tables)
    seg = lambda first, count: pl.BlockSpec((1, tm, tn), lambda b, m, n: (b, m, jnp.clip(n - first, 0, count - 1)))
    out = lambda count, dtype: jax.ShapeDtypeStruct((B, T, count * tn), dtype)
    return pl.pallas_call(
        functools.partial(_inproj_kernel, rope=tables is not None, q_scale=HEAD_DIM ** -0.5 * LOG2E),
        grid=(B, T // tm, Z_WIDTH // tn),
        in_specs=in_specs,
        out_specs=[seg(0, TILE_RW), seg(TILE_RW, TILE_NA - TILE_RW), seg(TILE_NA, TILE_DF - TILE_NA),
                   seg(TILE_DF, Z_WIDTH // tn - TILE_DF)],
        out_shape=[out(TILE_RW, BF16), out(TILE_NA - TILE_RW, F32), out(TILE_DF - TILE_NA, BF16),
                   out(Z_WIDTH // tn - TILE_DF, BF16)],
        scratch_shapes=[pltpu.VMEM((tm, D), BF16)],
        compiler_params=_cparams(("parallel", "parallel", "arbitrary")),
        name="inproj",
    )(*args)


def _merge_kernel(x_ref, mod_ref, gpost_ref, orw_ref, ona_ref, odf_ref, rg_ref, ng_ref, dg_ref, mg_ref,
                  wb_ref, wo_ref, o_ref):
    D = x_ref.shape[-1]
    acc = None
    for n, (ob_ref, gate_ref) in enumerate(((orw_ref, rg_ref), (ona_ref, ng_ref), (odf_ref, dg_ref))):
        yb = _bdot(ob_ref[0] * _silu(gate_ref[0].astype(F32)), wb_ref[n])
        term = _sigmoid(mg_ref[0, :, n * D:(n + 1) * D].astype(F32)) * yb
        acc = term if acc is None else acc + term
    y = _bdot(acc, wo_ref[...])
    yn = y * lax.rsqrt(jnp.mean(y * y, axis=-1, keepdims=True) + RMS_EPS) * gpost_ref[...]
    o_ref[0] = x_ref[0] + mod_ref[0, 2:3, :] * yn


def _merge(x, mod, g_post, o_rw, o_na, o_df, zg, w_branch, w_out, tm):
    B, T, D = x.shape
    per_batch = mod.shape[0] > 1
    bw = BRANCH_WIDTH
    row = lambda b, m: (b, m, 0)
    col = lambda c: (lambda b, m: (b, m, c))
    return pl.pallas_call(
        _merge_kernel,
        grid=(B, T // tm),
        in_specs=[pl.BlockSpec((1, tm, D), row),
                  pl.BlockSpec((1, 3, D), (lambda b, m: (b, 0, 0)) if per_batch else (lambda b, m: (0, 0, 0))),
                  pl.BlockSpec((1, D), lambda b, m: (0, 0)),
                  pl.BlockSpec((1, tm, bw), row), pl.BlockSpec((1, tm, bw), row), pl.BlockSpec((1, tm, bw), row),
                  pl.BlockSpec((1, tm, bw), col(N_BRANCH * D // bw)),
                  pl.BlockSpec((1, tm, bw), col(N_BRANCH * D // bw + 1)),
                  pl.BlockSpec((1, tm, bw), col(N_BRANCH * D // bw + 2)),
                  pl.BlockSpec((1, tm, N_BRANCH * D), col(0)),
                  pl.BlockSpec((N_BRANCH, bw, D), lambda b, m: (0, 0, 0)),
                  pl.BlockSpec((D, D), lambda b, m: (0, 0))],
        out_specs=pl.BlockSpec((1, tm, D), row),
        out_shape=jax.ShapeDtypeStruct((B, T, D), F32),
        compiler_params=_cparams(("parallel", "parallel")),
        name="merge",
    )(x, mod, g_post.reshape(1, D), o_rw, o_na, o_df, zg, zg, zg, zg, w_branch, w_out)


def _rope_tables(n_tokens):
    t = np.arange(n_tokens)
    axis_dim = DIFF_QK_DIM // 2
    inv = ROPE_THETA ** (-np.arange(0, axis_dim, 2, dtype=np.float32) / axis_dim)
    ar = (t // GRID_W).astype(np.float32)[:, None] * inv
    ac = (t % GRID_W).astype(np.float32)[:, None] * inv
    ang = jnp.asarray(np.concatenate([ar, ar, ac, ac], axis=-1).astype(np.float32))
    cos, sin = jnp.cos(ang), jnp.sin(ang)
    first = (np.arange(DIFF_QK_DIM) % 32) < 16
    sin_a = jnp.where(first, -sin, 0.0)
    sin_b = jnp.where(first, 0.0, sin)
    tile = lambda a: jnp.tile(a, (1, 2))
    return tile(cos), tile(sin_a), tile(sin_b)


def _dot_nt(a, b):
    return lax.dot_general(a, b, (((1,), (1,)), ((), ())), preferred_element_type=F32)


SOFTMAX_ROWS = 128


def _diff_attn_kernel(q_ref, kc_ref, vc_ref, k_ref, v_ref, lq_ref, lk_ref, g_ref, o_ref,
                      sc, pc, s0, s1, p0, p1, al0, al1, m_scr, acc_scr, *, n_main, tk, lambda_init):
    tq = q_ref.shape[1]
    t_first = kc_ref.shape[1]
    q = q_ref[0]
    lo = lax.broadcasted_iota(jnp.int32, (1, 128), 1) < DIFF_QK_DIM
    zero = jnp.zeros_like(q)
    qq = jnp.concatenate([jnp.where(lo, q, zero), jnp.where(lo, zero, q)], axis=0)

    n_sub = 2 * tq // SOFTMAX_ROWS
    sub = lambda i: slice(i * SOFTMAX_ROWS, (i + 1) * SOFTMAX_ROWS)

    def scores(k_chunk, s_scr):
        s_scr[:, :k_chunk.shape[0]] = _dot_nt(qq, k_chunk)

    def softmax(size, first, s_scr, p_scr, al_scr):
        w = min(128, size)
        cols = [slice(c * w, (c + 1) * w) for c in range(size // w)]
        for i in range(n_sub):
            mx = s_scr[sub(i), cols[0]]
            for c in cols[1:]:
                mx = jnp.maximum(mx, s_scr[sub(i), c])
            mn = jnp.broadcast_to(jnp.max(mx, axis=-1, keepdims=True), (SOFTMAX_ROWS, 128))
            if not first:
                mo = m_scr[sub(i), :]
                mn = jnp.maximum(mo, mn)
                al_scr[sub(i), :] = jnp.exp2(mo - mn)
            m_scr[sub(i), :] = mn
        for i in range(n_sub):
            mn = m_scr[sub(i), :w]
            for c in cols:
                p_scr[sub(i), c] = jnp.exp2(s_scr[sub(i), c] - mn).astype(BF16)

    def accumulate(v, first, p_scr, al_scr):
        size = v.shape[0]
        v_ext = jnp.concatenate([v, jnp.ones((size, 128), BF16)], axis=1)
        pv = jnp.dot(p_scr[:, :size], v_ext, preferred_element_type=F32)
        if first:
            acc_scr[...] = pv
        else:
            al = al_scr[...]
            acc_scr[...] = acc_scr[...] * jnp.concatenate([al, al], axis=1) + pv

    scores(kc_ref[0], sc)
    if n_main == 0:
        softmax(t_first, True, sc, pc, None)
        accumulate(vc_ref[0], True, pc, None)
    else:
        def chunk_of(ref, j):
            start = j * tk if isinstance(j, int) else pl.multiple_of(j * tk, tk)
            return ref[0, pl.ds(start, tk), :]

        k_at = functools.partial(chunk_of, k_ref)
        v_at = functools.partial(chunk_of, v_ref)
        scores(k_at(0), s0)
        softmax(t_first, True, sc, pc, None)

        def pair(j, first):
            scores(k_at(j + 1), s1)
            softmax(tk, False, s0, p0, al0)
            if first:
                accumulate(vc_ref[0], True, pc, None)
            else:
                accumulate(v_at(j - 1), False, p1, al1)
            nxt = min(j + 2, n_main - 1) if isinstance(j, int) else jnp.minimum(j + 2, n_main - 1)
            scores(k_at(nxt), s0)
            softmax(tk, False, s1, p1, al1)
            accumulate(v_at(j), False, p0, al0)

        pair(0, True)

        def body(t, _):
            pair(2 * t, False)
            return 0

        lax.fori_loop(1, n_main // 2, body, 0)
        accumulate(v_at(n_main - 1), False, p1, al1)
    lqk = lq_ref[...] * lk_ref[...]
    e = jnp.exp(jnp.sum(lqk, axis=-1, keepdims=True))
    lam = e[0:1] - e[1:2] + lambda_init
    o_maps = acc_scr[:, :DIFF_V_DIM] / acc_scr[:, DIFF_V_DIM:]
    o = o_maps[:tq] - lam * o_maps[tq:]
    o = o * lax.rsqrt(jnp.mean(o * o, axis=-1, keepdims=True) + SUBLN_EPS) * g_ref[...]
    o_ref[0] = o * (1.0 - lambda_init)


def _diff_attn(qkv, qkv_c, with_latent_keys, lam_q, lam_k, subln_g, lambda_init, tq, tk):
    B, T, _ = qkv.shape
    C = qkv_c.shape[1]
    n_main = T // tk if with_latent_keys else 0
    assert n_main % 2 == 0 and (2 * tq) % SOFTMAX_ROWS == 0
    wmax = tk if n_main else 128
    H = DIFF_HEADS
    kv = lambda t, j: pl.BlockSpec((1, t, 128), lambda b, h, m: (b, 0, j * H + h))
    small = lambda r, c: pl.BlockSpec((r, c), lambda b, h, m: (0, 0))
    stat = pltpu.VMEM((2 * tq, 128), F32)
    s_buf = pltpu.VMEM((2 * tq, wmax), F32)
    p_buf = pltpu.VMEM((2 * tq, wmax), BF16)
    return pl.pallas_call(
        functools.partial(_diff_attn_kernel, n_main=n_main, tk=tk, lambda_init=lambda_init),
        grid=(B, H, T // tq),
        in_specs=[pl.BlockSpec((1, tq, 128), lambda b, h, m: (b, m, h)), kv(C, 1), kv(C, 2), kv(T, 1), kv(T, 2),
                  small(2, DIFF_QK_DIM), small(2, DIFF_QK_DIM), small(1, DIFF_V_DIM)],
        out_specs=pl.BlockSpec((1, tq, 128), lambda b, h, m: (b, m, h)),
        out_shape=jax.ShapeDtypeStruct((B, T, H * DIFF_V_DIM), F32),
        scratch_shapes=[pltpu.VMEM((2 * tq, C), F32), pltpu.VMEM((2 * tq, C), BF16), s_buf, s_buf, p_buf, p_buf,
                        stat, stat, stat, pltpu.VMEM((2 * tq, 2 * DIFF_V_DIM), F32)],
        compiler_params=_cparams(("parallel", "parallel", "parallel")),
        name="diff_attn",
    )(qkv, qkv_c, qkv_c, qkv, qkv, lam_q, lam_k, subln_g.reshape(1, DIFF_V_DIM))


def _na_bias_tables(rpb):
    c_idx = np.arange(GRID_W)
    c_start = np.clip(c_idx - NA_WIN_C // 2, 0, GRID_W - NA_WIN_C)
    col_ok = (c_idx[None, :] >= c_start[:, None]) & (c_idx[None, :] < c_start[:, None] + NA_WIN_C)
    dc = np.clip(c_idx[None, :] - c_idx[:, None], -(NA_WIN_C - 1), NA_WIN_C - 1) + (NA_WIN_C - 1)
    H = rpb.shape[0]
    e = jnp.take(rpb, jnp.asarray(dc.reshape(-1)), axis=2).reshape(H, 2 * NA_WIN_R - 1, GRID_W, GRID_W)
    e = jnp.where(col_ok, e, NEG_INF)
    tabs = [e[:, NA_WIN_R - 1 - off:2 * NA_WIN_R - 1 - off].transpose(0, 2, 1, 3).reshape(H, GRID_W, NA_WIN_R * GRID_W)
            for off in range(NA_WIN_R)]
    return jnp.stack(tabs, axis=1)


def _na_kernel(q_ref, k_ref, v_ref, kc_ref, vc_ref, bias_ref, o_ref, *, rq, rows):
    i = pl.program_id(2)
    lo = lax.broadcasted_iota(jnp.int32, (1, 128), 1) < HEAD_DIM
    win = NA_WIN_R * GRID_W
    G2 = 2 * GRID_W
    q = q_ref[0]
    zero = jnp.zeros_like(q)
    q_lo, q_hi = jnp.where(lo, q, zero), jnp.where(lo, zero, q)
    qq = jnp.concatenate([x[rr * GRID_W:(rr + 1) * GRID_W] for rr in range(rq) for x in (q_lo, q_hi)], axis=0)
    kc = kc_ref[0]
    ones = lambda n: jnp.ones((n, 128), BF16)
    s_c = _dot_nt(qq, kc)

    starts, s_nb = [], []
    for rr in range(rq):
        r = i * rq + rr
        r_start = jnp.clip(r - NA_WIN_R // 2, 0, rows - NA_WIN_R)
        off = r - r_start
        start = pl.multiple_of(r_start * GRID_W, GRID_W)
        starts.append(start)
        bias = jnp.concatenate([bias_ref[0, off], bias_ref[1, off]], axis=0)
        s_nb.append(_dot_nt(qq[rr * G2:(rr + 1) * G2], k_ref[0, pl.ds(start, win), :]) + bias)

    def lane_blocks(t):
        w = min(128, t.shape[-1])
        return [t[:, c * w:(c + 1) * w] for c in range(t.shape[-1] // w)]

    p_nb, p_c = [], []
    for rr in range(rq):
        sc = s_c[rr * G2:(rr + 1) * G2]
        blocks = lane_blocks(s_nb[rr])
        mx = blocks[0]
        for b in blocks[1:]:
            mx = jnp.maximum(mx, b)
        m = jnp.maximum(jnp.max(mx, axis=-1, keepdims=True), jnp.max(sc, axis=-1, keepdims=True))
        p_nb.append(jnp.exp2(s_nb[rr] - m).astype(BF16))
        p_c.append(jnp.exp2(sc - m).astype(BF16))

    vc_ext = jnp.concatenate([vc_ref[0], ones(kc.shape[0])], axis=1)
    o_c = jnp.dot(jnp.concatenate(p_c, axis=0), vc_ext, preferred_element_type=F32)
    for rr in range(rq):
        vw_ext = jnp.concatenate([v_ref[0, pl.ds(starts[rr], win), :], ones(win)], axis=1)
        o = jnp.dot(p_nb[rr], vw_ext, preferred_element_type=F32) + o_c[rr * G2:(rr + 1) * G2]
        o = o[:, :128] / o[:, 128:]
        o_ref[0, rr * GRID_W:(rr + 1) * GRID_W, :] = jnp.where(lo, o[:GRID_W], o[GRID_W:])


def _na_attn(qkv, qkv_c, bias, rq):
    B, S, _ = qkv.shape
    C = qkv_c.shape[1]
    rows = S // GRID_W
    npair = NA_HEADS // 2
    full = lambda t, j: pl.BlockSpec((1, t, 128), lambda b, p, m: (b, 0, j * npair + p))
    return pl.pallas_call(
        functools.partial(_na_kernel, rq=rq, rows=rows),
        grid=(B, npair, rows // rq),
        in_specs=[pl.BlockSpec((1, rq * GRID_W, 128), lambda b, p, m: (b, m, p)),
                  full(S, 1), full(S, 2), full(C, 1), full(C, 2),
                  pl.BlockSpec((2, NA_WIN_R, GRID_W, NA_WIN_R * GRID_W), lambda b, p, m: (p, 0, 0, 0))],
        out_specs=pl.BlockSpec((1, rq * GRID_W, 128), lambda b, p, m: (b, m, p)),
        out_shape=jax.ShapeDtypeStruct((B, S, NA_HEADS * HEAD_DIM), F32),
        compiler_params=_cparams(("parallel", "parallel", "parallel")),
        name="na_attn",
    )(qkv, qkv, qkv, qkv_c, qkv_c, bias)


def _ctx_attn_kernel(q_ref, k_ref, v_ref, o_ref):
    lo = lax.broadcasted_iota(jnp.int32, (1, 128), 1) < HEAD_DIM
    q, k, v = q_ref[0], k_ref[0], v_ref[0]
    zero = jnp.zeros_like(q)
    outs = []
    for hl in range(2):
        qh = jnp.where(lo, q, zero) if hl == 0 else jnp.where(lo, zero, q)
        s = _dot_nt(qh, k)
        p = jnp.exp2(s - jnp.max(s, axis=-1, keepdims=True))
        o = jnp.dot(p.astype(BF16), v, preferred_element_type=F32)
        outs.append(o / jnp.sum(p, axis=-1, keepdims=True))
    o_ref[0] = jnp.where(lo, outs[0], outs[1])


def _ctx_attn(qkv):
    B, C, _ = qkv.shape
    npair = NA_HEADS // 2
    blk = lambda j: pl.BlockSpec((1, C, 128), lambda b, p: (b, 0, j * npair + p))
    return pl.pallas_call(
        _ctx_attn_kernel,
        grid=(B, npair),
        in_specs=[blk(0), blk(1), blk(2)],
        out_specs=blk(0),
        out_shape=jax.ShapeDtypeStruct((B, C, NA_HEADS * HEAD_DIM), F32),
        compiler_params=_cparams(("parallel", "parallel")),
        name="ctx_attn",
    )(qkv, qkv, qkv)


PL_R, PL_V, PL_KK = 0, 1, 2
PL_LOGW, PL_KDIR, PL_B = 3, 4, 5
N_PLANES = 9
RWKV_CHUNK = 64


def _split3(x):
    hi = x.astype(BF16)
    r1 = x - hi.astype(F32)
    mid = r1.astype(BF16)
    lo = (r1 - mid.astype(F32)).astype(BF16)
    return hi, mid, lo


def _dot_exact_rhs(m, x):
    hi, mid, lo = _split3(x)
    mb = m.astype(BF16)
    d = lambda t: jnp.dot(mb, t, preferred_element_type=F32)
    return d(hi) + d(mid) + d(lo)


def _dot_exact_lhs(x, m):
    hi, mid, lo = _split3(x)
    mb = m.astype(BF16)
    d = lambda t: jnp.dot(t, mb, preferred_element_type=F32)
    return d(hi) + d(mid) + d(lo)


def _head_ones(n):
    r = lax.broadcasted_iota(jnp.int32, (n, n), 0) // HEAD_DIM
    c = lax.broadcasted_iota(jnp.int32, (n, n), 1) // HEAD_DIM
    return (r == c).astype(F32)


def _rwkv_prep_kernel(zc_ref, zp_ref, zn_ref, mu_ref, kk_ref, ka_ref, w0_ref, a0_ref, wup_ref, aup_ref, o_ref):
    tm = zc_ref.shape[1]
    m = pl.program_id(1)
    u = zc_ref[0]
    prev_row = jnp.where(m > 0, zp_ref[0, 7:8, :], 0.0)
    next_row = jnp.where(m < pl.num_programs(1) - 1, zn_ref[0, 0:1, :], 0.0)
    rows = lax.broadcasted_iota(jnp.int32, (tm, 1), 0)
    u_prev = jnp.where(rows == 0, prev_row, pltpu.roll(u, 1, 0))
    u_next = jnp.where(rows == tm - 1, next_row, pltpu.roll(u, tm - 1, 0))
    u = u + mu_ref[0:1, :] * (u_prev - u) + mu_ref[1:2, :] * (u_next - u)

    W = RWKV_WIDTH
    r, k, v = u[:, 0:W], u[:, W:2 * W], u[:, 2 * W:3 * W]
    lw = jnp.tanh(u[:, 3 * W:3 * W + 2 * LORA]).astype(BF16)
    la = u[:, 3 * W + 2 * LORA:3 * W + 4 * LORA].astype(BF16)
    ones = _head_ones(W)
    kk = k * kk_ref[...]
    ss = _dot_exact_lhs(kk * kk, ones)
    kk = kk / jnp.maximum(jnp.sqrt(ss), 1e-12)
    o_ref[0, :, PL_R * W:(PL_R + 1) * W] = r
    o_ref[0, :, PL_V * W:(PL_V + 1) * W] = v
    o_ref[0, :, PL_KK * W:(PL_KK + 1) * W] = kk
    for d in range(2):
        xw = w0_ref[d:d + 1, :] + jnp.dot(lw, wup_ref[d], preferred_element_type=F32)
        logw = -math.exp(-0.5) * _sigmoid(xw)
        a = _sigmoid(a0_ref[d:d + 1, :] + jnp.dot(la, aup_ref[d], preferred_element_type=F32))
        kdir = k * (1.0 + (a - 1.0) * ka_ref[...])
        base = 3 * d
        o_ref[0, :, (PL_LOGW + base) * W:(PL_LOGW + base + 1) * W] = logw
        o_ref[0, :, (PL_KDIR + base) * W:(PL_KDIR + base + 1) * W] = kdir
        o_ref[0, :, (PL_B + base) * W:(PL_B + base + 1) * W] = kk * a


def _rwkv_prep(z, mu_pad, k_k, k_a, w0, a0, wup2, aup2, tm):
    B, T, wz = z.shape
    nb = tm // 8
    W = RWKV_WIDTH
    const = lambda shape: pl.BlockSpec(shape, lambda b, m: (0,) * len(shape))
    return pl.pallas_call(
        _rwkv_prep_kernel,
        grid=(B, T // tm),
        in_specs=[pl.BlockSpec((1, tm, wz), lambda b, m: (b, m, 0)),
                  pl.BlockSpec((1, 8, wz), lambda b, m: (b, jnp.maximum(m * nb - 1, 0), 0)),
                  pl.BlockSpec((1, 8, wz), lambda b, m: (b, jnp.minimum((m + 1) * nb, T // 8 - 1), 0)),
                  const((2, wz)), const((1, W)), const((1, W)), const((2, W)), const((2, W)),
                  const((2, 2 * LORA, W)), const((2, 2 * LORA, W))],
        out_specs=pl.BlockSpec((1, tm, N_PLANES * W), lambda b, m: (b, m, 0)),
        out_shape=jax.ShapeDtypeStruct((B, T, N_PLANES * W), F32),
        compiler_params=_cparams(("parallel", "parallel")),
        name="rwkv_prep",
    )(z, z, z, mu_pad, k_k.reshape(1, W), k_a.reshape(1, W), w0, a0, wup2, aup2)


def _rwkv_scan_kernel(pf_ref, pb_ref, s0_ref, yf_ref, yb_ref, s_ref):
    C = RWKV_CHUNK
    W = RWKV_WIDTH
    C2 = 2 * C

    @pl.when(pl.program_id(1) == 0)
    def _():
        s_ref[...] = s0_ref[...]

    lo = lax.broadcasted_iota(jnp.int32, (1, 128), 1) < HEAD_DIM
    row = lax.broadcasted_iota(jnp.int32, (C2, C2), 0)
    col = lax.broadcasted_iota(jnp.int32, (C2, C2), 1)
    same = (row // C) == (col // C)
    eye = (row == col).astype(F32)
    tri_r = lax.broadcasted_iota(jnp.int32, (C, C), 0)
    tri_c = lax.broadcasted_iota(jnp.int32, (C, C), 1)
    levels = [((row >> k) == (col >> k)) & ((row >> (k - 1)) != (col >> (k - 1))) for k in range(1, 7)]
    p_refs, y_refs = (pf_ref, pb_ref), (yf_ref, yb_ref)
    groups = [(d, p) for d in range(2) for p in range(W // 128)]
    ng = len(groups)
    bf = lambda t: t.astype(BF16)

    def stack(x):
        return jnp.concatenate([jnp.where(lo, x, 0.0), jnp.where(lo, 0.0, x)], axis=0)

    c_dir = []
    for d in range(2):
        cum = ((tri_c <= tri_r) if d == 0 else (tri_c >= tri_r)).astype(F32)
        logw_all = p_refs[d][0, :, (PL_LOGW + 3 * d) * W:(PL_LOGW + 3 * d + 1) * W]
        c_dir.append(_dot_exact_rhs(cum, logw_all))

    ar, bk, ends, vs, decay_end = [], [], [], [], []
    for d, p in groups:
        get = lambda plane: p_refs[d][0, :, plane * W + p * 128:plane * W + (p + 1) * 128]
        r, v, kk = get(PL_R), get(PL_V), get(PL_KK)
        logw, kdir, bb = get(PL_LOGW + 3 * d), get(PL_KDIR + 3 * d), get(PL_B + 3 * d)
        c = c_dir[d][:, p * 128:(p + 1) * 128]
        last = C - 1 if d == 0 else 0
        c_last = c[last:last + 1, :]
        e_neg = jnp.exp(-c)
        e_end = jnp.exp(c_last - c)
        ar.append(bf(jnp.concatenate([stack(-kk * jnp.exp(c - logw)), stack(r * jnp.exp(c))], axis=0)))
        bk.append(bf(jnp.concatenate([stack(bb * e_neg), stack(kdir * e_neg)], axis=0)))
        ends.append(bf(jnp.concatenate([stack(bb * e_end), stack(kdir * e_end)], axis=0)))
        vs.append(bf(stack(v)))
        decay_end.append(jnp.exp(c_last))

    amat = [_dot_nt(ar[g], bk[g]) for g in range(ng)]
    a_ab, a_kr, a_rb = [], [], []
    for g, (d, p) in enumerate(groups):
        before = (col < row) if d == 0 else (col > row)
        strict = same & before
        incl = same & (before | (row == col))
        m = amat[g]
        a_ab.append(jnp.where(strict, m[:C2, :C2], 0.0))
        a_kr.append(bf(jnp.concatenate([jnp.where(strict, m[:C2, C2:], 0.0),
                                        jnp.where(incl, m[C2:, C2:], 0.0)], axis=0)))
        a_rb.append(bf(jnp.where(incl, m[C2:, :C2], 0.0)))

    t = [eye + jnp.where(levels[0], a, 0.0) for a in a_ab]
    for lvl in levels[1:]:
        ta = [jnp.dot(bf(t[g]), bf(jnp.where(lvl, a_ab[g], 0.0)), preferred_element_type=F32) for g in range(ng)]
        t = [t[g] + jnp.dot(bf(ta[g]), bf(t[g]), preferred_element_type=F32) for g in range(ng)]

    s_old = [s_ref[0, d, p] for d, p in groups]
    from_s = [_dot_nt(ar[g], bf(s_old[g])) for g in range(ng)]
    from_v = [jnp.dot(a_kr[g], vs[g], preferred_element_type=F32) for g in range(ng)]
    u = [jnp.dot(bf(t[g]), bf(from_s[g][:C2] + from_v[g][:C2]), preferred_element_type=F32) for g in range(ng)]
    for g, (d, p) in enumerate(groups):
        y_s = from_s[g][C2:] + from_v[g][C2:] + jnp.dot(a_rb[g], bf(u[g]), preferred_element_type=F32)
        y_refs[d][0, :, p * 128:(p + 1) * 128] = y_s[:C] + y_s[C:]
    for g, (d, p) in enumerate(groups):
        uv = jnp.concatenate([bf(u[g]), vs[g]], axis=0)
        s_ref[0, d, p] = s_old[g] * decay_end[g] + lax.dot_general(
            uv, ends[g], (((0,), (0,)), ((), ())), preferred_element_type=F32)


def _rwkv_scan(planes, s0):
    B, T, _ = planes.shape
    nc = T // RWKV_CHUNK
    W = RWKV_WIDTH
    st = pl.BlockSpec((1, 2, W // 128, 128, 128), lambda b, i: (b, 0, 0, 0, 0))
    y = jax.ShapeDtypeStruct((B, T, W), F32)
    return pl.pallas_call(
        _rwkv_scan_kernel,
        grid=(B, nc),
        in_specs=[pl.BlockSpec((1, RWKV_CHUNK, N_PLANES * W), lambda b, i: (b, i, 0)),
                  pl.BlockSpec((1, RWKV_CHUNK, N_PLANES * W), lambda b, i: (b, nc - 1 - i, 0)),
                  st],
        out_specs=[pl.BlockSpec((1, RWKV_CHUNK, W), lambda b, i: (b, i, 0)),
                   pl.BlockSpec((1, RWKV_CHUNK, W), lambda b, i: (b, nc - 1 - i, 0)),
                   st],
        out_shape=[y, y, jax.ShapeDtypeStruct(s0.shape, F32)],
        compiler_params=_cparams(("parallel", "arbitrary")),
        name="rwkv_scan",
    )(planes, planes, s0)


def _rwkv_readout_kernel(yf_ref, yb_ref, r_ref, v_ref, kf_ref, kb_ref, g_ref, b_ref, rk_ref, o_ref):
    ones = _head_ones(RWKV_WIDTH)
    y = yf_ref[0] + yb_ref[0]
    inv_n = 1.0 / HEAD_DIM
    mu = _dot_exact_lhs(y, ones) * inv_n
    yc = y - mu
    var = _dot_exact_lhs(yc * yc, ones) * inv_n
    yn = yc * lax.rsqrt(var + GN_EPS) * g_ref[...] + b_ref[...]
    bonus = _dot_exact_lhs(r_ref[0] * (kf_ref[0] + kb_ref[0]) * rk_ref[...], ones)
    o_ref[0] = yn + bonus * v_ref[0]


def _rwkv_readout(y_f, y_b, planes, ln_g, ln_b, r_k, tm):
    B, T, W = y_f.shape
    row = pl.BlockSpec((1, tm, W), lambda b, m: (b, m, 0))
    plane = lambda j: pl.BlockSpec((1, tm, W), lambda b, m: (b, m, j))
    vec = pl.BlockSpec((1, W), lambda b, m: (0, 0))
    return pl.pallas_call(
        _rwkv_readout_kernel,
        grid=(B, T // tm),
        in_specs=[row, row, plane(PL_R), plane(PL_V), plane(PL_KDIR), plane(PL_KDIR + 3), vec, vec, vec],
        out_specs=row,
        out_shape=jax.ShapeDtypeStruct((B, T, W), F32),
        compiler_params=_cparams(("parallel", "parallel")),
        name="rwkv_readout",
    )(y_f, y_b, planes, planes, planes, planes, ln_g.reshape(1, W), ln_b.reshape(1, W), r_k.reshape(1, W))


def _layer_weights(l, w_in, shift_mu, w_up, a_up, w_branch, w_out):
    rw, rg, na, ng, df, dg, mg = jnp.split(w_in[l], [int(i) for i in np.cumsum(IN_SIZES)[:-1]], axis=-1)
    pad = jnp.zeros((w_in.shape[1], RWKV_PAD_WIDTH - RWKV_SHIFT_WIDTH), w_in.dtype)
    w_perm = jnp.concatenate([mg, rg, ng, dg, rw, pad, na, df], axis=-1).astype(BF16)
    mu_pad = jnp.pad(shift_mu[l], ((0, 0), (0, RWKV_PAD_WIDTH - RWKV_SHIFT_WIDTH)))
    zl = jnp.zeros((LORA, RWKV_WIDTH), F32)
    wup2 = jnp.stack([jnp.concatenate([w_up[l, 0], zl]), jnp.concatenate([zl, w_up[l, 1]])]).astype(BF16)
    aup2 = jnp.stack([jnp.concatenate([a_up[l, 0], zl]), jnp.concatenate([zl, a_up[l, 1]])]).astype(BF16)
    return w_perm, mu_pad, wup2, aup2, w_branch[l].astype(BF16), w_out[l].astype(BF16)


def kernel(x, c, ctx, c_ctx, w_mod, b_mod, g_pre, g_post, w_in, shift_mu, k_k, k_a, r_k, w0, w_up, a0, a_up,
           ln_x_g, ln_x_b, rpb, lam_q, lam_k, diff_subln, w_branch, w_out):
    B, S, D = x.shape
    C = ctx.shape[1]
    depth = w_in.shape[0]
    tables = _rope_tables(S)
    rows_pad = 16
    cvec = jnp.zeros((rows_pad, D), F32).at[:B].set(c).at[B].set(c_ctx)
    mod = _modulation(cvec, w_mod, b_mod)
    hc = ctx
    tm_x = min(2048, S)
    for l in range(depth):
        last = l == depth - 1
        lambda_init = 0.8 - 0.6 * math.exp(-0.3 * l)
        w_perm, mu_pad, wup2, aup2, wb, wo = _layer_weights(l, w_in, shift_mu, w_up, a_up, w_branch, w_out)
        mod_x = mod[l, :B].reshape(B, 3, D)
        mod_c = mod[l, B:B + 1].reshape(1, 3, D)
        zg_x, rw_x, na_x, df_x = _inproj(x, mod_x, g_pre[l], w_perm, tables, tm_x)
        zg_c, rw_c, na_c, df_c = _inproj(hc, mod_c, g_pre[l], w_perm, None, C)

        pl_c = _rwkv_prep(rw_c, mu_pad, k_k[l], k_a[l], w0[l], a0[l], wup2, aup2, min(256, C))
        pl_x = _rwkv_prep(rw_x, mu_pad, k_k[l], k_a[l], w0[l], a0[l], wup2, aup2, 256)
        s_zero = jnp.zeros((B, 2, RWKV_WIDTH // 128, 128, 128), F32)
        yf_c, yb_c, s_ctx = _rwkv_scan(pl_c, s_zero)
        yf_x, yb_x, _ = _rwkv_scan(pl_x, s_ctx)
        r_k_flat = r_k[l].reshape(RWKV_WIDTH)
        o_rw_x = _rwkv_readout(yf_x, yb_x, pl_x, ln_x_g[l], ln_x_b[l], r_k_flat, 512)

        o_na_x = _na_attn(na_x, na_c, _na_bias_tables(rpb[l]) * LOG2E, 8)
        o_df_x = _diff_attn(df_x, df_c, True, lam_q[l], lam_k[l], diff_subln[l], lambda_init, 512, min(512, S // 2))

        x = _merge(x, mod_x, g_post[l], o_rw_x, o_na_x, o_df_x, zg_x, wb, wo, 512)
        if not last:
            o_rw_c = _rwkv_readout(yf_c, yb_c, pl_c, ln_x_g[l], ln_x_b[l], r_k_flat, min(256, C))
            o_na_c = _ctx_attn(na_c)
            o_df_c = _diff_attn(df_c, df_c, False, lam_q[l], lam_k[l], diff_subln[l], lambda_init, C, C)
            hc = _merge(hc, mod_c, g_post[l], o_rw_c, o_na_c, o_df_c, zg_c, wb, wo, min(256, C))
    return x
```

```python
import functools
import math

import jax
import jax.numpy as jnp
import numpy as np
from jax import lax
from jax.experimental import pallas as pl
from jax.experimental.pallas import tpu as pltpu

F32 = jnp.float32
BF16 = jnp.bfloat16

DEPTH = 4
GRID_W = 64
RWKV_HEADS = 8
HEAD_DIM = 64
RWKV_WIDTH = 512
LORA = 64
GN_EPS = 64e-5
NA_HEADS = 8
NA_WIN_R = 8
NA_WIN_C = 16
DIFF_HEADS = 4
DIFF_QK_DIM = 64
DIFF_V_DIM = 128
ROPE_THETA = 10000.0
SUBLN_EPS = 1e-5
RMS_EPS = 1e-6
NEG_INF = -1e30
LOG2E = math.log2(math.e)
BRANCH_WIDTH = 512
N_BRANCH = 3

RWKV_SHIFT_WIDTH = 3 * RWKV_WIDTH + 4 * LORA
RWKV_PAD_WIDTH = 2048
TILE_W = 512
TILE_RW, TILE_NA, TILE_DF = 9, 13, 16
Z_WIDTH = 19 * TILE_W
IN_SIZES = (RWKV_SHIFT_WIDTH, 512, 1536, 512, 1536, 512, 3072)

VMEM_LIMIT = 56 * 1024 * 1024


def _cparams(sem):
    return pltpu.CompilerParams(dimension_semantics=sem, vmem_limit_bytes=VMEM_LIMIT)


def _sigmoid(x):
    return 1.0 / (1.0 + jnp.exp(-x))


def _silu(x):
    return x * _sigmoid(x)


def _bdot(a, b):
    return jnp.dot(a.astype(BF16), b.astype(BF16), preferred_element_type=F32)


def _mod_kernel(c_ref, w_ref, b_ref, o_ref):
    o_ref[0] = _bdot(_silu(c_ref[...]), w_ref[0]) + b_ref[0]


def _modulation(cvec, w_mod, b_mod):
    L, D, N = w_mod.shape
    R = cvec.shape[0]
    tn = 1024
    return pl.pallas_call(
        _mod_kernel,
        grid=(L, N // tn),
        in_specs=[pl.BlockSpec((R, D), lambda l, n: (0, 0)),
                  pl.BlockSpec((1, D, tn), lambda l, n: (l, 0, n)),
                  pl.BlockSpec((1, 1, tn), lambda l, n: (l, 0, n))],
        out_specs=pl.BlockSpec((1, R, tn), lambda l, n: (l, 0, n)),
        out_shape=jax.ShapeDtypeStruct((L, R, N), F32),
        compiler_params=_cparams(("parallel", "parallel")),
        name="modulation",
    )(cvec, w_mod, b_mod.reshape(L, 1, N))


def _inproj_kernel(*refs, rope, q_scale):
    if rope:
        x_ref, mod_ref, g_ref, w_ref, cos_ref, sa_ref, sb_ref, zg_ref, rw_ref, na_ref, df_ref, h_ref = refs
    else:
        x_ref, mod_ref, g_ref, w_ref, zg_ref, rw_ref, na_ref, df_ref, h_ref = refs
    n = pl.program_id(2)

    @pl.when(n == 0)
    def _():
        x = x_ref[0]
        y = x * lax.rsqrt(jnp.mean(x * x, axis=-1, keepdims=True) + RMS_EPS) * g_ref[...]
        h_ref[...] = (y * (1.0 + mod_ref[0, 1:2, :]) + mod_ref[0, 0:1, :]).astype(BF16)

    mm = lambda: jnp.dot(h_ref[...], w_ref[...], preferred_element_type=F32)

    def rot(t):
        if not rope:
            return t
        w = t.shape[-1]
        tile = lambda r: jnp.concatenate([r[...]] * (w // r.shape[-1]), axis=1)
        return t * tile(cos_ref) + pltpu.roll(t, w - 16, 1) * tile(sa_ref) + pltpu.roll(t, 16, 1) * tile(sb_ref)

    @pl.when(n < TILE_RW)
    def _():
        zg_ref[0] = mm().astype(BF16)

    @pl.when((n >= TILE_RW) & (n < TILE_NA))
    def _():
        rw_ref[0] = mm()

    @pl.when(n == TILE_NA)
    def _():
        na_ref[0] = (mm() * q_scale).astype(BF16)

    @pl.when((n > TILE_NA) & (n < TILE_DF))
    def _():
        na_ref[0] = mm().astype(BF16)

    @pl.when(n == TILE_DF)
    def _():
        df_ref[0] = (rot(mm()) * q_scale).astype(BF16)

    @pl.when(n == TILE_DF + 1)
    def _():
        df_ref[0] = rot(mm()).astype(BF16)

    @pl.when(n == TILE_DF + 2)
    def _():
        df_ref[0] = mm().astype(BF16)


def _inproj(x, mod, g_pre, w_perm, tables, tm):
    B, T, D = x.shape
    tn = TILE_W
    per_batch = mod.shape[0] > 1
    in_specs = [pl.BlockSpec((1, tm, D), lambda b, m, n: (b, m, 0)),
                pl.BlockSpec((1, 3, D), (lambda b, m, n: (b, 0, 0)) if per_batch else (lambda b, m, n: (0, 0, 0))),
                pl.BlockSpec((1, D), lambda b, m, n: (0, 0)),
                pl.BlockSpec((D, tn), lambda b, m, n: (0, n))]
    args = [x, mod, g_pre.reshape(1, D), w_perm]
    if tables is not None:
        in_specs += [pl.BlockSpec((tm, 128), lambda b, m, n: (m, 0))] * 3
        args += list(tables)
    seg = lambda first, count: pl.BlockSpec((1, tm, tn), lambda b, m, n: (b, m, jnp.clip(n - first, 0, count - 1)))
    out = lambda count, dtype: jax.ShapeDtypeStruct((B, T, count * tn), dtype)
    return pl.pallas_call(
        functools.partial(_inproj_kernel, rope=tables is not None, q_scale=HEAD_DIM ** -0.5 * LOG2E),
        grid=(B, T // tm, Z_WIDTH // tn),
        in_specs=in_specs,
        out_specs=[seg(0, TILE_RW), seg(TILE_RW, TILE_NA - TILE_RW), seg(TILE_NA, TILE_DF - TILE_NA),
                   seg(TILE_DF, Z_WIDTH // tn - TILE_DF)],
        out_shape=[out(TILE_RW, BF16), out(TILE_NA - TILE_RW, F32), out(TILE_DF - TILE_NA, BF16),
                   out(Z_WIDTH // tn - TILE_DF, BF16)],
        scratch_shapes=[pltpu.VMEM((tm, D), BF16)],
        compiler_params=_cparams(("parallel", "parallel", "arbitrary")),
        name="inproj",
    )(*args)


def _merge_kernel(x_ref, mod_ref, gpost_ref, orw_ref, ona_ref, odf_ref, rg_ref, ng_ref, dg_ref, mg_ref,
                  wb_ref, wo_ref, o_ref):
    D = x_ref.shape[-1]
    acc = None
    for n, (ob_ref, gate_ref) in enumerate(((orw_ref, rg_ref), (ona_ref, ng_ref), (odf_ref, dg_ref))):
        yb = _bdot(ob_ref[0] * _silu(gate_ref[0].astype(F32)), wb_ref[n])
        term = _sigmoid(mg_ref[0, :, n * D:(n + 1) * D].astype(F32)) * yb
        acc = term if acc is None else acc + term
    y = _bdot(acc, wo_ref[...])
    yn = y * lax.rsqrt(jnp.mean(y * y, axis=-1, keepdims=True) + RMS_EPS) * gpost_ref[...]
    o_ref[0] = x_ref[0] + mod_ref[0, 2:3, :] * yn


def _merge(x, mod, g_post, o_rw, o_na, o_df, zg, w_branch, w_out, tm):
    B, T, D = x.shape
    per_batch = mod.shape[0] > 1
    bw = BRANCH_WIDTH
    row = lambda b, m: (b, m, 0)
    col = lambda c: (lambda b, m: (b, m, c))
    return pl.pallas_call(
        _merge_kernel,
        grid=(B, T // tm),
        in_specs=[pl.BlockSpec((1, tm, D), row),
                  pl.BlockSpec((1, 3, D), (lambda b, m: (b, 0, 0)) if per_batch else (lambda b, m: (0, 0, 0))),
                  pl.BlockSpec((1, D), lambda b, m: (0, 0)),
                  pl.BlockSpec((1, tm, bw), row), pl.BlockSpec((1, tm, bw), row), pl.BlockSpec((1, tm, bw), row),
                  pl.BlockSpec((1, tm, bw), col(N_BRANCH * D // bw)),
                  pl.BlockSpec((1, tm, bw), col(N_BRANCH * D // bw + 1)),
                  pl.BlockSpec((1, tm, bw), col(N_BRANCH * D // bw + 2)),
                  pl.BlockSpec((1, tm, N_BRANCH * D), col(0)),
                  pl.BlockSpec((N_BRANCH, bw, D), lambda b, m: (0, 0, 0)),
                  pl.BlockSpec((D, D), lambda b, m: (0, 0))],
        out_specs=pl.BlockSpec((1, tm, D), row),
        out_shape=jax.ShapeDtypeStruct((B, T, D), F32),
        compiler_params=_cparams(("parallel", "parallel")),
        name="merge",
    )(x, mod, g_post.reshape(1, D), o_rw, o_na, o_df, zg, zg, zg, zg, w_branch, w_out)


def _rope_tables(n_tokens):
    t = np.arange(n_tokens)
    axis_dim = DIFF_QK_DIM // 2
    inv = ROPE_THETA ** (-np.arange(0, axis_dim, 2, dtype=np.float32) / axis_dim)
    ar = (t // GRID_W).astype(np.float32)[:, None] * inv
    ac = (t % GRID_W).astype(np.float32)[:, None] * inv
    ang = jnp.asarray(np.concatenate([ar, ar, ac, ac], axis=-1).astype(np.float32))
    cos, sin = jnp.cos(ang), jnp.sin(ang)
    first = (np.arange(DIFF_QK_DIM) % 32) < 16
    sin_a = jnp.where(first, -sin, 0.0)
    sin_b = jnp.where(first, 0.0, sin)
    tile = lambda a: jnp.tile(a, (1, 2))
    return tile(cos), tile(sin_a), tile(sin_b)


def _dot_nt(a, b):
    return lax.dot_general(a, b, (((1,), (1,)), ((), ())), preferred_element_type=F32)


SOFTMAX_ROWS = 128


def _diff_attn_kernel(q_ref, kc_ref, vc_ref, k_ref, v_ref, lq_ref, lk_ref, g_ref, o_ref,
                      sc, pc, s0, s1, p0, p1, al0, al1, m_scr, acc_scr, *, n_main, tk, lambda_init):
    tq = q_ref.shape[1]
    t_first = kc_ref.shape[1]
    q = q_ref[0]
    lo = lax.broadcasted_iota(jnp.int32, (1, 128), 1) < DIFF_QK_DIM
    zero = jnp.zeros_like(q)
    qq = jnp.concatenate([jnp.where(lo, q, zero), jnp.where(lo, zero, q)], axis=0)

    n_sub = 2 * tq // SOFTMAX_ROWS
    sub = lambda i: slice(i * SOFTMAX_ROWS, (i + 1) * SOFTMAX_ROWS)

    def scores(k_chunk, s_scr):
        s_scr[:, :k_chunk.shape[0]] = _dot_nt(qq, k_chunk)

    def softmax(size, first, s_scr, p_scr, al_scr):
        w = min(128, size)
        cols = [slice(c * w, (c + 1) * w) for c in range(size // w)]
        for i in range(n_sub):
            mx = s_scr[sub(i), cols[0]]
            for c in cols[1:]:
                mx = jnp.maximum(mx, s_scr[sub(i), c])
            mn = jnp.broadcast_to(jnp.max(mx, axis=-1, keepdims=True), (SOFTMAX_ROWS, 128))
            if not first:
                mo = m_scr[sub(i), :]
                mn = jnp.maximum(mo, mn)
                al_scr[sub(i), :] = jnp.exp2(mo - mn)
            m_scr[sub(i), :] = mn
        for i in range(n_sub):
            mn = m_scr[sub(i), :w]
            for c in cols:
                p_scr[sub(i), c] = jnp.exp2(s_scr[sub(i), c] - mn).astype(BF16)

    def accumulate(v, first, p_scr, al_scr):
        size = v.shape[0]
        v_ext = jnp.concatenate([v, jnp.ones((size, 128), BF16)], axis=1)
        pv = jnp.dot(p_scr[:, :size], v_ext, preferred_element_type=F32)
        if first:
            acc_scr[...] = pv
        else:
            al = al_scr[...]
            acc_scr[...] = acc_scr[...] * jnp.concatenate([al, al], axis=1) + pv

    scores(kc_ref[0], sc)
    if n_main == 0:
        softmax(t_first, True, sc, pc, None)
        accumulate(vc_ref[0], True, pc, None)
    else:
        def chunk_of(ref, j):
            start = j * tk if isinstance(j, int) else pl.multiple_of(j * tk, tk)
            return ref[0, pl.ds(start, tk), :]

        k_at = functools.partial(chunk_of, k_ref)
        v_at = functools.partial(chunk_of, v_ref)
        scores(k_at(0), s0)
        softmax(t_first, True, sc, pc, None)

        def pair(j, first):
            scores(k_at(j + 1), s1)
            softmax(tk, False, s0, p0, al0)
            if first:
                accumulate(vc_ref[0], True, pc, None)
            else:
                accumulate(v_at(j - 1), False, p1, al1)
            nxt = min(j + 2, n_main - 1) if isinstance(j, int) else jnp.minimum(j + 2, n_main - 1)
            scores(k_at(nxt), s0)
            softmax(tk, False, s1, p1, al1)
            accumulate(v_at(j), False, p0, al0)

        pair(0, True)

        def body(t, _):
            pair(2 * t, False)
            return 0

        lax.fori_loop(1, n_main // 2, body, 0)
        accumulate(v_at(n_main - 1), False, p1, al1)
    lqk = lq_ref[...] * lk_ref[...]
    e = jnp.exp(jnp.sum(lqk, axis=-1, keepdims=True))
    lam = e[0:1] - e[1:2] + lambda_init
    o_maps = acc_scr[:, :DIFF_V_DIM] / acc_scr[:, DIFF_V_DIM:]
    o = o_maps[:tq] - lam * o_maps[tq:]
    o = o * lax.rsqrt(jnp.mean(o * o, axis=-1, keepdims=True) + SUBLN_EPS) * g_ref[...]
    o_ref[0] = o * (1.0 - lambda_init)


def _diff_attn(qkv, qkv_c, with_latent_keys, lam_q, lam_k, subln_g, lambda_init, tq, tk):
    B, T, _ = qkv.shape
    C = qkv_c.shape[1]
    n_main = T // tk if with_latent_keys else 0
    assert n_main % 2 == 0 and (2 * tq) % SOFTMAX_ROWS == 0
    wmax = tk if n_main else 128
    H = DIFF_HEADS
    kv = lambda t, j: pl.BlockSpec((1, t, 128), lambda b, h, m: (b, 0, j * H + h))
    small = lambda r, c: pl.BlockSpec((r, c), lambda b, h, m: (0, 0))
    stat = pltpu.VMEM((2 * tq, 128), F32)
    s_buf = pltpu.VMEM((2 * tq, wmax), F32)
    p_buf = pltpu.VMEM((2 * tq, wmax), BF16)
    return pl.pallas_call(
        functools.partial(_diff_attn_kernel, n_main=n_main, tk=tk, lambda_init=lambda_init),
        grid=(B, H, T // tq),
        in_specs=[pl.BlockSpec((1, tq, 128), lambda b, h, m: (b, m, h)), kv(C, 1), kv(C, 2), kv(T, 1), kv(T, 2),
                  small(2, DIFF_QK_DIM), small(2, DIFF_QK_DIM), small(1, DIFF_V_DIM)],
        out_specs=pl.BlockSpec((1, tq, 128), lambda b, h, m: (b, m, h)),
        out_shape=jax.ShapeDtypeStruct((B, T, H * DIFF_V_DIM), F32),
        scratch_shapes=[pltpu.VMEM((2 * tq, C), F32), pltpu.VMEM((2 * tq, C), BF16), s_buf, s_buf, p_buf, p_buf,
                        stat, stat, stat, pltpu.VMEM((2 * tq, 2 * DIFF_V_DIM), F32)],
        compiler_params=_cparams(("parallel", "parallel", "parallel")),
        name="diff_attn",
    )(qkv, qkv_c, qkv_c, qkv, qkv, lam_q, lam_k, subln_g.reshape(1, DIFF_V_DIM))


def _na_bias_tables(rpb):
    c_idx = np.arange(GRID_W)
    c_start = np.clip(c_idx - NA_WIN_C // 2, 0, GRID_W - NA_WIN_C)
    col_ok = (c_idx[None, :] >= c_start[:, None]) & (c_idx[None, :] < c_start[:, None] + NA_WIN_C)
    dc = np.clip(c_idx[None, :] - c_idx[:, None], -(NA_WIN_C - 1), NA_WIN_C - 1) + (NA_WIN_C - 1)
    H = rpb.shape[0]
    e = jnp.take(rpb, jnp.asarray(dc.reshape(-1)), axis=2).reshape(H, 2 * NA_WIN_R - 1, GRID_W, GRID_W)
    e = jnp.where(col_ok, e, NEG_INF)
    tabs = [e[:, NA_WIN_R - 1 - off:2 * NA_WIN_R - 1 - off].transpose(0, 2, 1, 3).reshape(H, GRID_W, NA_WIN_R * GRID_W)
            for off in range(NA_WIN_R)]
    return jnp.stack(tabs, axis=1)


def _na_kernel(q_ref, k_ref, v_ref, kc_ref, vc_ref, bias_ref, o_ref, *, rq, rows):
    i = pl.program_id(2)
    lo = lax.broadcasted_iota(jnp.int32, (1, 128), 1) < HEAD_DIM
    win = NA_WIN_R * GRID_W
    G2 = 2 * GRID_W
    q = q_ref[0]
    zero = jnp.zeros_like(q)
    q_lo, q_hi = jnp.where(lo, q, zero), jnp.where(lo, zero, q)
    qq = jnp.concatenate([x[rr * GRID_W:(rr + 1) * GRID_W] for rr in range(rq) for x in (q_lo, q_hi)], axis=0)
    kc = kc_ref[0]
    ones = lambda n: jnp.ones((n, 128), BF16)
    s_c = _dot_nt(qq, kc)

    starts, s_nb = [], []
    for rr in range(rq):
        r = i * rq + rr
        r_start = jnp.clip(r - NA_WIN_R // 2, 0, rows - NA_WIN_R)
        off = r - r_start
        start = pl.multiple_of(r_start * GRID_W, GRID_W)
        starts.append(start)
        bias = jnp.concatenate([bias_ref[0, off], bias_ref[1, off]], axis=0)
        s_nb.append(_dot_nt(qq[rr * G2:(rr + 1) * G2], k_ref[0, pl.ds(start, win), :]) + bias)

    def lane_blocks(t):
        w = min(128, t.shape[-1])
        return [t[:, c * w:(c + 1) * w] for c in range(t.shape[-1] // w)]

    p_nb, p_c = [], []
    for rr in range(rq):
        sc = s_c[rr * G2:(rr + 1) * G2]
        blocks = lane_blocks(s_nb[rr])
        mx = blocks[0]
        for b in blocks[1:]:
            mx = jnp.maximum(mx, b)
        m = jnp.maximum(jnp.max(mx, axis=-1, keepdims=True), jnp.max(sc, axis=-1, keepdims=True))
        p_nb.append(jnp.exp2(s_nb[rr] - m).astype(BF16))
        p_c.append(jnp.exp2(sc - m).astype(BF16))

    vc_ext = jnp.concatenate([vc_ref[0], ones(kc.shape[0])], axis=1)
    o_c = jnp.dot(jnp.concatenate(p_c, axis=0), vc_ext, preferred_element_type=F32)
    for rr in range(rq):
        vw_ext = jnp.concatenate([v_ref[0, pl.ds(starts[rr], win), :], ones(win)], axis=1)
        o = jnp.dot(p_nb[rr], vw_ext, preferred_element_type=F32) + o_c[rr * G2:(rr + 1) * G2]
        o = o[:, :128] / o[:, 128:]
        o_ref[0, rr * GRID_W:(rr + 1) * GRID_W, :] = jnp.where(lo, o[:GRID_W], o[GRID_W:])


def _na_attn(qkv, qkv_c, bias, rq):
    B, S, _ = qkv.shape
    C = qkv_c.shape[1]
    rows = S // GRID_W
    npair = NA_HEADS // 2
    full = lambda t, j: pl.BlockSpec((1, t, 128), lambda b, p, m: (b, 0, j * npair + p))
    return pl.pallas_call(
        functools.partial(_na_kernel, rq=rq, rows=rows),
        grid=(B, npair, rows // rq),
        in_specs=[pl.BlockSpec((1, rq * GRID_W, 128), lambda b, p, m: (b, m, p)),
                  full(S, 1), full(S, 2), full(C, 1), full(C, 2),
                  pl.BlockSpec((2, NA_WIN_R, GRID_W, NA_WIN_R * GRID_W), lambda b, p, m: (p, 0, 0, 0))],
        out_specs=pl.BlockSpec((1, rq * GRID_W, 128), lambda b, p, m: (b, m, p)),
        out_shape=jax.ShapeDtypeStruct((B, S, NA_HEADS * HEAD_DIM), F32),
        compiler_params=_cparams(("parallel", "parallel", "parallel")),
        name="na_attn",
    )(qkv, qkv, qkv, qkv_c, qkv_c, bias)


def _ctx_attn_kernel(q_ref, k_ref, v_ref, o_ref):
    lo = lax.broadcasted_iota(jnp.int32, (1, 128), 1) < HEAD_DIM
    q, k, v = q_ref[0], k_ref[0], v_ref[0]
    zero = jnp.zeros_like(q)
    outs = []
    for hl in range(2):
        qh = jnp.where(lo, q, zero) if hl == 0 else jnp.where(lo, zero, q)
        s = _dot_nt(qh, k)
        p = jnp.exp2(s - jnp.max(s, axis=-1, keepdims=True))
        o = jnp.dot(p.astype(BF16), v, preferred_element_type=F32)
        outs.append(o / jnp.sum(p, axis=-1, keepdims=True))
    o_ref[0] = jnp.where(lo, outs[0], outs[1])


def _ctx_attn(qkv):
    B, C, _ = qkv.shape
    npair = NA_HEADS // 2
    blk = lambda j: pl.BlockSpec((1, C, 128), lambda b, p: (b, 0, j * npair + p))
    return pl.pallas_call(
        _ctx_attn_kernel,
        grid=(B, npair),
        in_specs=[blk(0), blk(1), blk(2)],
        out_specs=blk(0),
        out_shape=jax.ShapeDtypeStruct((B, C, NA_HEADS * HEAD_DIM), F32),
        compiler_params=_cparams(("parallel", "parallel")),
        name="ctx_attn",
    )(qkv, qkv, qkv)


PL_R, PL_V, PL_KK = 0, 1, 2
PL_LOGW, PL_KDIR, PL_B = 3, 4, 5
N_PLANES = 9
RWKV_CHUNK = 64


def _split3(x):
    hi = x.astype(BF16)
    r1 = x - hi.astype(F32)
    mid = r1.astype(BF16)
    lo = (r1 - mid.astype(F32)).astype(BF16)
    return hi, mid, lo


def _dot_exact_rhs(m, x):
    hi, mid, lo = _split3(x)
    mb = m.astype(BF16)
    d = lambda t: jnp.dot(mb, t, preferred_element_type=F32)
    return d(hi) + d(mid) + d(lo)


def _dot_exact_lhs(x, m):
    hi, mid, lo = _split3(x)
    mb = m.astype(BF16)
    d = lambda t: jnp.dot(t, mb, preferred_element_type=F32)
    return d(hi) + d(mid) + d(lo)


def _head_ones(n):
    r = lax.broadcasted_iota(jnp.int32, (n, n), 0) // HEAD_DIM
    c = lax.broadcasted_iota(jnp.int32, (n, n), 1) // HEAD_DIM
    return (r == c).astype(F32)


def _rwkv_prep_kernel(zc_ref, zp_ref, zn_ref, mu_ref, kk_ref, ka_ref, w0_ref, a0_ref, wup_ref, aup_ref, o_ref):
    tm = zc_ref.shape[1]
    m = pl.program_id(1)
    u = zc_ref[0]
    prev_row = jnp.where(m > 0, zp_ref[0, 7:8, :], 0.0)
    next_row = jnp.where(m < pl.num_programs(1) - 1, zn_ref[0, 0:1, :], 0.0)
    rows = lax.broadcasted_iota(jnp.int32, (tm, 1), 0)
    u_prev = jnp.where(rows == 0, prev_row, pltpu.roll(u, 1, 0))
    u_next = jnp.where(rows == tm - 1, next_row, pltpu.roll(u, tm - 1, 0))
    u = u + mu_ref[0:1, :] * (u_prev - u) + mu_ref[1:2, :] * (u_next - u)

    W = RWKV_WIDTH
    r, k, v = u[:, 0:W], u[:, W:2 * W], u[:, 2 * W:3 * W]
    lw = jnp.tanh(u[:, 3 * W:3 * W + 2 * LORA]).astype(BF16)
    la = u[:, 3 * W + 2 * LORA:3 * W + 4 * LORA].astype(BF16)
    ones = _head_ones(W)
    kk = k * kk_ref[...]
    ss = _dot_exact_lhs(kk * kk, ones)
    kk = kk / jnp.maximum(jnp.sqrt(ss), 1e-12)
    o_ref[0, :, PL_R * W:(PL_R + 1) * W] = r
    o_ref[0, :, PL_V * W:(PL_V + 1) * W] = v
    o_ref[0, :, PL_KK * W:(PL_KK + 1) * W] = kk
    for d in range(2):
        xw = w0_ref[d:d + 1, :] + jnp.dot(lw, wup_ref[d], preferred_element_type=F32)
        logw = -math.exp(-0.5) * _sigmoid(xw)
        a = _sigmoid(a0_ref[d:d + 1, :] + jnp.dot(la, aup_ref[d], preferred_element_type=F32))
        kdir = k * (1.0 + (a - 1.0) * ka_ref[...])
        base = 3 * d
        o_ref[0, :, (PL_LOGW + base) * W:(PL_LOGW + base + 1) * W] = logw
        o_ref[0, :, (PL_KDIR + base) * W:(PL_KDIR + base + 1) * W] = kdir
        o_ref[0, :, (PL_B + base) * W:(PL_B + base + 1) * W] = kk * a


def _rwkv_prep(z, mu_pad, k_k, k_a, w0, a0, wup2, aup2, tm):
    B, T, wz = z.shape
    nb = tm // 8
    W = RWKV_WIDTH
    const = lambda shape: pl.BlockSpec(shape, lambda b, m: (0,) * len(shape))
    return pl.pallas_call(
        _rwkv_prep_kernel,
        grid=(B, T // tm),
        in_specs=[pl.BlockSpec((1, tm, wz), lambda b, m: (b, m, 0)),
                  pl.BlockSpec((1, 8, wz), lambda b, m: (b, jnp.maximum(m * nb - 1, 0), 0)),
                  pl.BlockSpec((1, 8, wz), lambda b, m: (b, jnp.minimum((m + 1) * nb, T // 8 - 1), 0)),
                  const((2, wz)), const((1, W)), const((1, W)), const((2, W)), const((2, W)),
                  const((2, 2 * LORA, W)), const((2, 2 * LORA, W))],
        out_specs=pl.BlockSpec((1, tm, N_PLANES * W), lambda b, m: (b, m, 0)),
        out_shape=jax.ShapeDtypeStruct((B, T, N_PLANES * W), F32),
        compiler_params=_cparams(("parallel", "parallel")),
        name="rwkv_prep",
    )(z, z, z, mu_pad, k_k.reshape(1, W), k_a.reshape(1, W), w0, a0, wup2, aup2)


def _rwkv_scan_kernel(pf_ref, pb_ref, s0_ref, yf_ref, yb_ref, s_ref):
    C = RWKV_CHUNK
    W = RWKV_WIDTH
    C2 = 2 * C

    @pl.when(pl.program_id(1) == 0)
    def _():
        s_ref[...] = s0_ref[...]

    lo = lax.broadcasted_iota(jnp.int32, (1, 128), 1) < HEAD_DIM
    row = lax.broadcasted_iota(jnp.int32, (C2, C2), 0)
    col = lax.broadcasted_iota(jnp.int32, (C2, C2), 1)
    same = (row // C) == (col // C)
    eye = (row == col).astype(F32)
    tri_r = lax.broadcasted_iota(jnp.int32, (C, C), 0)
    tri_c = lax.broadcasted_iota(jnp.int32, (C, C), 1)
    levels = [((row >> k) == (col >> k)) & ((row >> (k - 1)) != (col >> (k - 1))) for k in range(1, 7)]
    p_refs, y_refs = (pf_ref, pb_ref), (yf_ref, yb_ref)
    nb = pf_ref.shape[0]
    groups = [(n, d, p) for n in range(nb) for d in range(2) for p in range(W // 128)]
    ng = len(groups)
    bf = lambda t: t.astype(BF16)

    def stack(x):
        return jnp.concatenate([jnp.where(lo, x, 0.0), jnp.where(lo, 0.0, x)], axis=0)

    c_dir = {}
    for d in range(2):
        cum = ((tri_c <= tri_r) if d == 0 else (tri_c >= tri_r)).astype(F32)
        for n in range(nb):
            logw_all = p_refs[d][n, :, (PL_LOGW + 3 * d) * W:(PL_LOGW + 3 * d + 1) * W]
            c_dir[n, d] = _dot_exact_rhs(cum, logw_all)

    ar, bk, ends, vs, decay_end = [], [], [], [], []
    for n, d, p in groups:
        get = lambda plane: p_refs[d][n, :, plane * W + p * 128:plane * W + (p + 1) * 128]
        r, v, kk = get(PL_R), get(PL_V), get(PL_KK)
        logw, kdir, bb = get(PL_LOGW + 3 * d), get(PL_KDIR + 3 * d), get(PL_B + 3 * d)
        c = c_dir[n, d][:, p * 128:(p + 1) * 128]
        last = C - 1 if d == 0 else 0
        c_last = c[last:last + 1, :]
        e_neg = jnp.exp(-c)
        e_end = jnp.exp(c_last - c)
        ar.append(bf(jnp.concatenate([stack(-kk * jnp.exp(c - logw)), stack(r * jnp.exp(c))], axis=0)))
        bk.append(bf(jnp.concatenate([stack(bb * e_neg), stack(kdir * e_neg)], axis=0)))
        ends.append(bf(jnp.concatenate([stack(bb * e_end), stack(kdir * e_end)], axis=0)))
        vs.append(bf(stack(v)))
        decay_end.append(jnp.exp(c_last))

    amat = [_dot_nt(ar[g], bk[g]) for g in range(ng)]
    a_ab, a_kr, a_rb = [], [], []
    for g, (n, d, p) in enumerate(groups):
        before = (col < row) if d == 0 else (col > row)
        strict = same & before
        incl = same & (before | (row == col))
        m = amat[g]
        a_ab.append(jnp.where(strict, m[:C2, :C2], 0.0))
        a_kr.append(bf(jnp.concatenate([jnp.where(strict, m[:C2, C2:], 0.0),
                                        jnp.where(incl, m[C2:, C2:], 0.0)], axis=0)))
        a_rb.append(bf(jnp.where(incl, m[C2:, :C2], 0.0)))

    t = [eye + jnp.where(levels[0], a, 0.0) for a in a_ab]
    for lvl in levels[1:]:
        ta = [jnp.dot(bf(t[g]), bf(jnp.where(lvl, a_ab[g], 0.0)), preferred_element_type=F32) for g in range(ng)]
        t = [t[g] + jnp.dot(bf(ta[g]), bf(t[g]), preferred_element_type=F32) for g in range(ng)]

    s_old = [s_ref[n, d, p] for n, d, p in groups]
    from_s = [_dot_nt(ar[g], bf(s_old[g])) for g in range(ng)]
    from_v = [jnp.dot(a_kr[g], vs[g], preferred_element_type=F32) for g in range(ng)]
    u = [jnp.dot(bf(t[g]), bf(from_s[g][:C2] + from_v[g][:C2]), preferred_element_type=F32) for g in range(ng)]
    for g, (n, d, p) in enumerate(groups):
        y_s = from_s[g][C2:] + from_v[g][C2:] + jnp.dot(a_rb[g], bf(u[g]), preferred_element_type=F32)
        y_refs[d][n, :, p * 128:(p + 1) * 128] = y_s[:C] + y_s[C:]
    for g, (n, d, p) in enumerate(groups):
        uv = jnp.concatenate([bf(u[g]), vs[g]], axis=0)
        s_ref[n, d, p] = s_old[g] * decay_end[g] + lax.dot_general(
            uv, ends[g], (((0,), (0,)), ((), ())), preferred_element_type=F32)


def _rwkv_scan(planes, s0):
    B, T, _ = planes.shape
    nc = T // RWKV_CHUNK
    W = RWKV_WIDTH
    nb = 2 if B % 2 == 0 else 1
    st = pl.BlockSpec((nb, 2, W // 128, 128, 128), lambda b, i: (b, 0, 0, 0, 0))
    y = jax.ShapeDtypeStruct((B, T, W), F32)
    return pl.pallas_call(
        _rwkv_scan_kernel,
        grid=(B // nb, nc),
        in_specs=[pl.BlockSpec((nb, RWKV_CHUNK, N_PLANES * W), lambda b, i: (b, i, 0)),
                  pl.BlockSpec((nb, RWKV_CHUNK, N_PLANES * W), lambda b, i: (b, nc - 1 - i, 0)),
                  st],
        out_specs=[pl.BlockSpec((nb, RWKV_CHUNK, W), lambda b, i: (b, i, 0)),
                   pl.BlockSpec((nb, RWKV_CHUNK, W), lambda b, i: (b, nc - 1 - i, 0)),
                   st],
        out_shape=[y, y, jax.ShapeDtypeStruct(s0.shape, F32)],
        compiler_params=_cparams(("parallel", "arbitrary")),
        name="rwkv_scan",
    )(planes, planes, s0)


def _rwkv_readout_kernel(yf_ref, yb_ref, r_ref, v_ref, kf_ref, kb_ref, g_ref, b_ref, rk_ref, o_ref):
    ones = _head_ones(RWKV_WIDTH)
    y = yf_ref[0] + yb_ref[0]
    inv_n = 1.0 / HEAD_DIM
    mu = _dot_exact_lhs(y, ones) * inv_n
    yc = y - mu
    var = _dot_exact_lhs(yc * yc, ones) * inv_n
    yn = yc * lax.rsqrt(var + GN_EPS) * g_ref[...] + b_ref[...]
    bonus = _dot_exact_lhs(r_ref[0] * (kf_ref[0] + kb_ref[0]) * rk_ref[...], ones)
    o_ref[0] = yn + bonus * v_ref[0]


def _rwkv_readout(y_f, y_b, planes, ln_g, ln_b, r_k, tm):
    B, T, W = y_f.shape
    row = pl.BlockSpec((1, tm, W), lambda b, m: (b, m, 0))
    plane = lambda j: pl.BlockSpec((1, tm, W), lambda b, m: (b, m, j))
    vec = pl.BlockSpec((1, W), lambda b, m: (0, 0))
    return pl.pallas_call(
        _rwkv_readout_kernel,
        grid=(B, T // tm),
        in_specs=[row, row, plane(PL_R), plane(PL_V), plane(PL_KDIR), plane(PL_KDIR + 3), vec, vec, vec],
        out_specs=row,
        out_shape=jax.ShapeDtypeStruct((B, T, W), F32),
        compiler_params=_cparams(("parallel", "parallel")),
        name="rwkv_readout",
    )(y_f, y_b, planes, planes, planes, planes, ln_g.reshape(1, W), ln_b.reshape(1, W), r_k.reshape(1, W))


def _layer_weights(l, w_in, shift_mu, w_up, a_up, w_branch, w_out):
    rw, rg, na, ng, df, dg, mg = jnp.split(w_in[l], [int(i) for i in np.cumsum(IN_SIZES)[:-1]], axis=-1)
    pad = jnp.zeros((w_in.shape[1], RWKV_PAD_WIDTH - RWKV_SHIFT_WIDTH), w_in.dtype)
    w_perm = jnp.concatenate([mg, rg, ng, dg, rw, pad, na, df], axis=-1).astype(BF16)
    mu_pad = jnp.pad(shift_mu[l], ((0, 0), (0, RWKV_PAD_WIDTH - RWKV_SHIFT_WIDTH)))
    zl = jnp.zeros((LORA, RWKV_WIDTH), F32)
    wup2 = jnp.stack([jnp.concatenate([w_up[l, 0], zl]), jnp.concatenate([zl, w_up[l, 1]])]).astype(BF16)
    aup2 = jnp.stack([jnp.concatenate([a_up[l, 0], zl]), jnp.concatenate([zl, a_up[l, 1]])]).astype(BF16)
    return w_perm, mu_pad, wup2, aup2, w_branch[l].astype(BF16), w_out[l].astype(BF16)


def kernel(x, c, ctx, c_ctx, w_mod, b_mod, g_pre, g_post, w_in, shift_mu, k_k, k_a, r_k, w0, w_up, a0, a_up,
           ln_x_g, ln_x_b, rpb, lam_q, lam_k, diff_subln, w_branch, w_out):
    B, S, D = x.shape
    C = ctx.shape[1]
    depth = w_in.shape[0]
    tables = _rope_tables(S)
    rows_pad = 16
    cvec = jnp.zeros((rows_pad, D), F32).at[:B].set(c).at[B].set(c_ctx)
    mod = _modulation(cvec, w_mod, b_mod)
    hc = ctx
    tm_x = min(2048, S)
    for l in range(depth):
        last = l == depth - 1
        lambda_init = 0.8 - 0.6 * math.exp(-0.3 * l)
        w_perm, mu_pad, wup2, aup2, wb, wo = _layer_weights(l, w_in, shift_mu, w_up, a_up, w_branch, w_out)
        mod_x = mod[l, :B].reshape(B, 3, D)
        mod_c = mod[l, B:B + 1].reshape(1, 3, D)
        zg_x, rw_x, na_x, df_x = _inproj(x, mod_x, g_pre[l], w_perm, tables, tm_x)
        ctx_out = _inproj(hc.reshape(1, B * C, D), mod_c, g_pre[l], w_perm, None, B * C)
        zg_c, rw_c, na_c, df_c = (t.reshape(B, C, t.shape[-1]) for t in ctx_out)

        pl_c = _rwkv_prep(rw_c, mu_pad, k_k[l], k_a[l], w0[l], a0[l], wup2, aup2, min(256, C))
        pl_x = _rwkv_prep(rw_x, mu_pad, k_k[l], k_a[l], w0[l], a0[l], wup2, aup2, 512)
        s_zero = jnp.zeros((B, 2, RWKV_WIDTH // 128, 128, 128), F32)
        yf_c, yb_c, s_ctx = _rwkv_scan(pl_c, s_zero)
        yf_x, yb_x, _ = _rwkv_scan(pl_x, s_ctx)
        r_k_flat = r_k[l].reshape(RWKV_WIDTH)
        o_rw_x = _rwkv_readout(yf_x, yb_x, pl_x, ln_x_g[l], ln_x_b[l], r_k_flat, 512)

        o_na_x = _na_attn(na_x, na_c, _na_bias_tables(rpb[l]) * LOG2E, min(16, S // GRID_W))
        o_df_x = _diff_attn(df_x, df_c, True, lam_q[l], lam_k[l], diff_subln[l], lambda_init, 512, min(512, S // 2))

        x = _merge(x, mod_x, g_post[l], o_rw_x, o_na_x, o_df_x, zg_x, wb, wo, 512)
        if not last:
            o_rw_c = _rwkv_readout(yf_c, yb_c, pl_c, ln_x_g[l], ln_x_b[l], r_k_flat, min(256, C))
            o_na_c = _ctx_attn(na_c)
            o_df_c = _diff_attn(df_c, df_c, False, lam_q[l], lam_k[l], diff_subln[l], lambda_init, C, C)
            hc = _merge(hc, mod_c, g_post[l], o_rw_c, o_na_c, o_df_c, zg_c, wb, wo, min(256, C))
    return x
```

```python
import functools
import math

import jax
import jax.numpy as jnp
import numpy as np
from jax import lax
from jax.experimental import pallas as pl
from jax.experimental.pallas import tpu as pltpu

F32 = jnp.float32
BF16 = jnp.bfloat16

DEPTH = 4
GRID_W = 64
RWKV_HEADS = 8
HEAD_DIM = 64
RWKV_WIDTH = 512
LORA = 64
GN_EPS = 64e-5
NA_HEADS = 8
NA_WIN_R = 8
NA_WIN_C = 16
DIFF_HEADS = 4
DIFF_QK_DIM = 64
DIFF_V_DIM = 128
ROPE_THETA = 10000.0
SUBLN_EPS = 1e-5
RMS_EPS = 1e-6
NEG_INF = -1e30
LOG2E = math.log2(math.e)
BRANCH_WIDTH = 512
N_BRANCH = 3

RWKV_SHIFT_WIDTH = 3 * RWKV_WIDTH + 4 * LORA
RWKV_PAD_WIDTH = 2048
TILE_W = 512
TILE_RW, TILE_NA, TILE_DF = 9, 13, 16
Z_WIDTH = 19 * TILE_W
IN_SIZES = (RWKV_SHIFT_WIDTH, 512, 1536, 512, 1536, 512, 3072)

VMEM_LIMIT = 56 * 1024 * 1024


def _cparams(sem):
    return pltpu.CompilerParams(dimension_semantics=sem, vmem_limit_bytes=VMEM_LIMIT)


def _sigmoid(x):
    return 1.0 / (1.0 + jnp.exp(-x))


def _silu(x):
    return x * _sigmoid(x)


def _bdot(a, b):
    return jnp.dot(a.astype(BF16), b.astype(BF16), preferred_element_type=F32)


def _mod_kernel(c_ref, w_ref, b_ref, o_ref):
    o_ref[0] = _bdot(_silu(c_ref[...]), w_ref[0]) + b_ref[0]


def _modulation(cvec, w_mod, b_mod):
    L, D, N = w_mod.shape
    R = cvec.shape[0]
    tn = 1024
    return pl.pallas_call(
        _mod_kernel,
        grid=(L, N // tn),
        in_specs=[pl.BlockSpec((R, D), lambda l, n: (0, 0)),
                  pl.BlockSpec((1, D, tn), lambda l, n: (l, 0, n)),
                  pl.BlockSpec((1, 1, tn), lambda l, n: (l, 0, n))],
        out_specs=pl.BlockSpec((1, R, tn), lambda l, n: (l, 0, n)),
        out_shape=jax.ShapeDtypeStruct((L, R, N), F32),
        compiler_params=_cparams(("parallel", "parallel")),
        name="modulation",
    )(cvec, w_mod, b_mod.reshape(L, 1, N))


def _inproj_kernel(*refs, rope, q_scale):
    if rope:
        x_ref, mod_ref, g_ref, w_ref, cos_ref, sa_ref, sb_ref, zg_ref, rw_ref, na_ref, df_ref, h_ref = refs
    else:
        x_ref, mod_ref, g_ref, w_ref, zg_ref, rw_ref, na_ref, df_ref, h_ref = refs
    n = pl.program_id(2)

    @pl.when(n == 0)
    def _():
        x = x_ref[0]
        y = x * lax.rsqrt(jnp.mean(x * x, axis=-1, keepdims=True) + RMS_EPS) * g_ref[...]
        h_ref[...] = (y * (1.0 + mod_ref[0, 1:2, :]) + mod_ref[0, 0:1, :]).astype(BF16)

    mm = lambda: jnp.dot(h_ref[...], w_ref[0], preferred_element_type=F32)

    def rot(t):
        if not rope:
            return t
        w = t.shape[-1]
        tile = lambda r: jnp.concatenate([r[...]] * (w // r.shape[-1]), axis=1)
        return t * tile(cos_ref) + pltpu.roll(t, w - 16, 1) * tile(sa_ref) + pltpu.roll(t, 16, 1) * tile(sb_ref)

    @pl.when(n < TILE_RW)
    def _():
        zg_ref[0] = mm().astype(BF16)

    @pl.when((n >= TILE_RW) & (n < TILE_NA))
    def _():
        rw_ref[0] = mm()

    @pl.when(n == TILE_NA)
    def _():
        na_ref[0] = (mm() * q_scale).astype(BF16)

    @pl.when((n > TILE_NA) & (n < TILE_DF))
    def _():
        na_ref[0] = mm().astype(BF16)

    @pl.when(n == TILE_DF)
    def _():
        df_ref[0] = (rot(mm()) * q_scale).astype(BF16)

    @pl.when(n == TILE_DF + 1)
    def _():
        df_ref[0] = rot(mm()).astype(BF16)

    @pl.when(n == TILE_DF + 2)
    def _():
        df_ref[0] = mm().astype(BF16)


def _inproj(x, mod, g_pre, w_perm, tables, tm):
    B, T, D = x.shape
    tn = TILE_W
    per_batch = mod.shape[0] > 1
    in_specs = [pl.BlockSpec((1, tm, D), lambda b, m, n: (b, m, 0)),
                pl.BlockSpec((1, 3, D), (lambda b, m, n: (b, 0, 0)) if per_batch else (lambda b, m, n: (0, 0, 0))),
                pl.BlockSpec((1, D), lambda b, m, n: (0, 0)),
                pl.BlockSpec((1, D, tn), lambda b, m, n: (n, 0, 0))]
    args = [x, mod, g_pre.reshape(1, D), w_perm]
    if tables is not None:
        in_specs += [pl.BlockSpec((tm, 128), lambda b, m, n: (m, 0))] * 3
        args += list(tables)
    seg = lambda first, count: pl.BlockSpec((1, tm, tn), lambda b, m, n: (b, m, jnp.clip(n - first, 0, count - 1)))
    out = lambda count, dtype: jax.ShapeDtypeStruct((B, T, count * tn), dtype)
    return pl.pallas_call(
        functools.partial(_inproj_kernel, rope=tables is not None, q_scale=HEAD_DIM ** -0.5 * LOG2E),
        grid=(B, T // tm, Z_WIDTH // tn),
        in_specs=in_specs,
        out_specs=[seg(0, TILE_RW), seg(TILE_RW, TILE_NA - TILE_RW), seg(TILE_NA, TILE_DF - TILE_NA),
                   seg(TILE_DF, Z_WIDTH // tn - TILE_DF)],
        out_shape=[out(TILE_RW, BF16), out(TILE_NA - TILE_RW, F32), out(TILE_DF - TILE_NA, BF16),
                   out(Z_WIDTH // tn - TILE_DF, BF16)],
        scratch_shapes=[pltpu.VMEM((tm, D), BF16)],
        compiler_params=_cparams(("parallel", "parallel", "arbitrary")),
        name="inproj",
    )(*args)


def _merge_kernel(x_ref, mod_ref, gpost_ref, orw_ref, ona_ref, odf_ref, rg_ref, ng_ref, dg_ref, mg_ref,
                  wb_ref, wo_ref, o_ref):
    D = x_ref.shape[-1]
    acc = None
    for n, (ob_ref, gate_ref) in enumerate(((orw_ref, rg_ref), (ona_ref, ng_ref), (odf_ref, dg_ref))):
        yb = _bdot(ob_ref[0] * _silu(gate_ref[0].astype(F32)), wb_ref[n])
        term = _sigmoid(mg_ref[0, :, n * D:(n + 1) * D].astype(F32)) * yb
        acc = term if acc is None else acc + term
    y = _bdot(acc, wo_ref[...])
    yn = y * lax.rsqrt(jnp.mean(y * y, axis=-1, keepdims=True) + RMS_EPS) * gpost_ref[...]
    o_ref[0] = x_ref[0] + mod_ref[0, 2:3, :] * yn


def _merge(x, mod, g_post, o_rw, o_na, o_df, zg, w_branch, w_out, tm):
    B, T, D = x.shape
    per_batch = mod.shape[0] > 1
    bw = BRANCH_WIDTH
    row = lambda b, m: (b, m, 0)
    col = lambda c: (lambda b, m: (b, m, c))
    return pl.pallas_call(
        _merge_kernel,
        grid=(B, T // tm),
        in_specs=[pl.BlockSpec((1, tm, D), row),
                  pl.BlockSpec((1, 3, D), (lambda b, m: (b, 0, 0)) if per_batch else (lambda b, m: (0, 0, 0))),
                  pl.BlockSpec((1, D), lambda b, m: (0, 0)),
                  pl.BlockSpec((1, tm, bw), row), pl.BlockSpec((1, tm, bw), row), pl.BlockSpec((1, tm, bw), row),
                  pl.BlockSpec((1, tm, bw), col(N_BRANCH * D // bw)),
                  pl.BlockSpec((1, tm, bw), col(N_BRANCH * D // bw + 1)),
                  pl.BlockSpec((1, tm, bw), col(N_BRANCH * D // bw + 2)),
                  pl.BlockSpec((1, tm, N_BRANCH * D), col(0)),
                  pl.BlockSpec((N_BRANCH, bw, D), lambda b, m: (0, 0, 0)),
                  pl.BlockSpec((D, D), lambda b, m: (0, 0))],
        out_specs=pl.BlockSpec((1, tm, D), row),
        out_shape=jax.ShapeDtypeStruct((B, T, D), F32),
        compiler_params=_cparams(("parallel", "parallel")),
        name="merge",
    )(x, mod, g_post.reshape(1, D), o_rw, o_na, o_df, zg, zg, zg, zg, w_branch, w_out)


def _rope_tables(n_tokens):
    t = np.arange(n_tokens)
    axis_dim = DIFF_QK_DIM // 2
    inv = ROPE_THETA ** (-np.arange(0, axis_dim, 2, dtype=np.float32) / axis_dim)
    ar = (t // GRID_W).astype(np.float32)[:, None] * inv
    ac = (t % GRID_W).astype(np.float32)[:, None] * inv
    ang = jnp.asarray(np.concatenate([ar, ar, ac, ac], axis=-1).astype(np.float32))
    cos, sin = jnp.cos(ang), jnp.sin(ang)
    first = (np.arange(DIFF_QK_DIM) % 32) < 16
    sin_a = jnp.where(first, -sin, 0.0)
    sin_b = jnp.where(first, 0.0, sin)
    tile = lambda a: jnp.tile(a, (1, 2))
    return tile(cos), tile(sin_a), tile(sin_b)


def _dot_nt(a, b):
    return lax.dot_general(a, b, (((1,), (1,)), ((), ())), preferred_element_type=F32)


SOFTMAX_ROWS = 128


def _diff_attn_kernel(q_ref, kc_ref, vc_ref, k_ref, v_ref, lq_ref, lk_ref, g_ref, o_ref,
                      sc, pc, s0, s1, p0, p1, al0, al1, m_scr, acc_scr, *, n_main, tk, lambda_init):
    tq = q_ref.shape[1]
    t_first = kc_ref.shape[1]
    q = q_ref[0]
    lo = lax.broadcasted_iota(jnp.int32, (1, 128), 1) < DIFF_QK_DIM
    zero = jnp.zeros_like(q)
    qq = jnp.concatenate([jnp.where(lo, q, zero), jnp.where(lo, zero, q)], axis=0)

    n_sub = 2 * tq // SOFTMAX_ROWS
    sub = lambda i: slice(i * SOFTMAX_ROWS, (i + 1) * SOFTMAX_ROWS)

    def scores(k_chunk, s_scr):
        s_scr[:, :k_chunk.shape[0]] = _dot_nt(qq, k_chunk)

    def softmax(size, first, s_scr, p_scr, al_scr):
        w = min(128, size)
        cols = [slice(c * w, (c + 1) * w) for c in range(size // w)]
        for i in range(n_sub):
            mx = s_scr[sub(i), cols[0]]
            for c in cols[1:]:
                mx = jnp.maximum(mx, s_scr[sub(i), c])
            mn = jnp.broadcast_to(jnp.max(mx, axis=-1, keepdims=True), (SOFTMAX_ROWS, 128))
            if not first:
                mo = m_scr[sub(i), :]
                mn = jnp.maximum(mo, mn)
                al_scr[sub(i), :] = jnp.exp2(mo - mn)
            m_scr[sub(i), :] = mn
        for i in range(n_sub):
            mn = m_scr[sub(i), :w]
            for c in cols:
                p_scr[sub(i), c] = jnp.exp2(s_scr[sub(i), c] - mn).astype(BF16)

    def accumulate(v, first, p_scr, al_scr):
        size = v.shape[0]
        v_ext = jnp.concatenate([v, jnp.ones((size, 128), BF16)], axis=1)
        pv = jnp.dot(p_scr[:, :size], v_ext, preferred_element_type=F32)
        if first:
            acc_scr[...] = pv
        else:
            al = al_scr[...]
            acc_scr[...] = acc_scr[...] * jnp.concatenate([al, al], axis=1) + pv

    scores(kc_ref[0], sc)
    if n_main == 0:
        softmax(t_first, True, sc, pc, None)
        accumulate(vc_ref[0], True, pc, None)
    else:
        def chunk_of(ref, j):
            start = j * tk if isinstance(j, int) else pl.multiple_of(j * tk, tk)
            return ref[0, pl.ds(start, tk), :]

        k_at = functools.partial(chunk_of, k_ref)
        v_at = functools.partial(chunk_of, v_ref)
        scores(k_at(0), s0)
        softmax(t_first, True, sc, pc, None)

        def pair(j, first):
            scores(k_at(j + 1), s1)
            softmax(tk, False, s0, p0, al0)
            if first:
                accumulate(vc_ref[0], True, pc, None)
            else:
                accumulate(v_at(j - 1), False, p1, al1)
            nxt = min(j + 2, n_main - 1) if isinstance(j, int) else jnp.minimum(j + 2, n_main - 1)
            scores(k_at(nxt), s0)
            softmax(tk, False, s1, p1, al1)
            accumulate(v_at(j), False, p0, al0)

        pair(0, True)

        def body(t, _):
            pair(2 * t, False)
            return 0

        lax.fori_loop(1, n_main // 2, body, 0)
        accumulate(v_at(n_main - 1), False, p1, al1)
    lqk = lq_ref[...] * lk_ref[...]
    e = jnp.exp(jnp.sum(lqk, axis=-1, keepdims=True))
    lam = e[0:1] - e[1:2] + lambda_init
    o_maps = acc_scr[:, :DIFF_V_DIM] / acc_scr[:, DIFF_V_DIM:]
    o = o_maps[:tq] - lam * o_maps[tq:]
    o = o * lax.rsqrt(jnp.mean(o * o, axis=-1, keepdims=True) + SUBLN_EPS) * g_ref[...]
    o_ref[0] = o * (1.0 - lambda_init)


def _diff_attn(qkv, qkv_c, with_latent_keys, lam_q, lam_k, subln_g, lambda_init, tq, tk):
    B, T, _ = qkv.shape
    C = qkv_c.shape[1]
    n_main = T // tk if with_latent_keys else 0
    assert n_main % 2 == 0 and (2 * tq) % SOFTMAX_ROWS == 0
    wmax = tk if n_main else 128
    H = DIFF_HEADS
    kv = lambda t, j: pl.BlockSpec((1, t, 128), lambda b, h, m: (b, 0, j * H + h))
    small = lambda r, c: pl.BlockSpec((r, c), lambda b, h, m: (0, 0))
    stat = pltpu.VMEM((2 * tq, 128), F32)
    s_buf = pltpu.VMEM((2 * tq, wmax), F32)
    p_buf = pltpu.VMEM((2 * tq, wmax), BF16)
    return pl.pallas_call(
        functools.partial(_diff_attn_kernel, n_main=n_main, tk=tk, lambda_init=lambda_init),
        grid=(B, H, T // tq),
        in_specs=[pl.BlockSpec((1, tq, 128), lambda b, h, m: (b, m, h)), kv(C, 1), kv(C, 2), kv(T, 1), kv(T, 2),
                  small(2, DIFF_QK_DIM), small(2, DIFF_QK_DIM), small(1, DIFF_V_DIM)],
        out_specs=pl.BlockSpec((1, tq, 128), lambda b, h, m: (b, m, h)),
        out_shape=jax.ShapeDtypeStruct((B, T, H * DIFF_V_DIM), F32),
        scratch_shapes=[pltpu.VMEM((2 * tq, C), F32), pltpu.VMEM((2 * tq, C), BF16), s_buf, s_buf, p_buf, p_buf,
                        stat, stat, stat, pltpu.VMEM((2 * tq, 2 * DIFF_V_DIM), F32)],
        compiler_params=_cparams(("parallel", "parallel", "parallel")),
        name="diff_attn",
    )(qkv, qkv_c, qkv_c, qkv, qkv, lam_q, lam_k, subln_g.reshape(1, DIFF_V_DIM))


def _na_bias_tables(rpb):
    c_idx = np.arange(GRID_W)
    c_start = np.clip(c_idx - NA_WIN_C // 2, 0, GRID_W - NA_WIN_C)
    col_ok = (c_idx[None, :] >= c_start[:, None]) & (c_idx[None, :] < c_start[:, None] + NA_WIN_C)
    dc = np.clip(c_idx[None, :] - c_idx[:, None], -(NA_WIN_C - 1), NA_WIN_C - 1) + (NA_WIN_C - 1)
    H = rpb.shape[0]
    e = jnp.take(rpb, jnp.asarray(dc.reshape(-1)), axis=2).reshape(H, 2 * NA_WIN_R - 1, GRID_W, GRID_W)
    e = jnp.where(col_ok, e, NEG_INF)
    tabs = [e[:, NA_WIN_R - 1 - off:2 * NA_WIN_R - 1 - off].transpose(0, 2, 1, 3).reshape(H, GRID_W, NA_WIN_R * GRID_W)
            for off in range(NA_WIN_R)]
    return jnp.stack(tabs, axis=1)


def _na_kernel(q_ref, k_ref, v_ref, kc_ref, vc_ref, bias_ref, o_ref, *, rq, rows):
    i = pl.program_id(2)
    lo = lax.broadcasted_iota(jnp.int32, (1, 128), 1) < HEAD_DIM
    win = NA_WIN_R * GRID_W
    G2 = 2 * GRID_W
    q = q_ref[0]
    zero = jnp.zeros_like(q)
    q_lo, q_hi = jnp.where(lo, q, zero), jnp.where(lo, zero, q)
    qq = jnp.concatenate([x[rr * GRID_W:(rr + 1) * GRID_W] for rr in range(rq) for x in (q_lo, q_hi)], axis=0)
    kc = kc_ref[0]
    ones = lambda n: jnp.ones((n, 128), BF16)
    s_c = _dot_nt(qq, kc)

    starts, s_nb = [], []
    for rr in range(rq):
        r = i * rq + rr
        r_start = jnp.clip(r - NA_WIN_R // 2, 0, rows - NA_WIN_R)
        off = r - r_start
        start = pl.multiple_of(r_start * GRID_W, GRID_W)
        starts.append(start)
        bias = jnp.concatenate([bias_ref[0, off], bias_ref[1, off]], axis=0)
        s_nb.append(_dot_nt(qq[rr * G2:(rr + 1) * G2], k_ref[0, pl.ds(start, win), :]) + bias)

    def lane_blocks(t):
        w = min(128, t.shape[-1])
        return [t[:, c * w:(c + 1) * w] for c in range(t.shape[-1] // w)]

    p_nb, p_c = [], []
    for rr in range(rq):
        sc = s_c[rr * G2:(rr + 1) * G2]
        blocks = lane_blocks(s_nb[rr])
        mx = blocks[0]
        for b in blocks[1:]:
            mx = jnp.maximum(mx, b)
        m = jnp.maximum(jnp.max(mx, axis=-1, keepdims=True), jnp.max(sc, axis=-1, keepdims=True))
        p_nb.append(jnp.exp2(s_nb[rr] - m).astype(BF16))
        p_c.append(jnp.exp2(sc - m).astype(BF16))

    vc_ext = jnp.concatenate([vc_ref[0], ones(kc.shape[0])], axis=1)
    o_c = jnp.dot(jnp.concatenate(p_c, axis=0), vc_ext, preferred_element_type=F32)
    for rr in range(rq):
        vw_ext = jnp.concatenate([v_ref[0, pl.ds(starts[rr], win), :], ones(win)], axis=1)
        o = jnp.dot(p_nb[rr], vw_ext, preferred_element_type=F32) + o_c[rr * G2:(rr + 1) * G2]
        o = o[:, :128] / o[:, 128:]
        o_ref[0, rr * GRID_W:(rr + 1) * GRID_W, :] = jnp.where(lo, o[:GRID_W], o[GRID_W:])


def _na_attn(qkv, qkv_c, bias, rq):
    B, S, _ = qkv.shape
    C = qkv_c.shape[1]
    rows = S // GRID_W
    npair = NA_HEADS // 2
    full = lambda t, j: pl.BlockSpec((1, t, 128), lambda b, p, m: (b, 0, j * npair + p))
    return pl.pallas_call(
        functools.partial(_na_kernel, rq=rq, rows=rows),
        grid=(B, npair, rows // rq),
        in_specs=[pl.BlockSpec((1, rq * GRID_W, 128), lambda b, p, m: (b, m, p)),
                  full(S, 1), full(S, 2), full(C, 1), full(C, 2),
                  pl.BlockSpec((2, NA_WIN_R, GRID_W, NA_WIN_R * GRID_W), lambda b, p, m: (p, 0, 0, 0))],
        out_specs=pl.BlockSpec((1, rq * GRID_W, 128), lambda b, p, m: (b, m, p)),
        out_shape=jax.ShapeDtypeStruct((B, S, NA_HEADS * HEAD_DIM), F32),
        compiler_params=_cparams(("parallel", "parallel", "parallel")),
        name="na_attn",
    )(qkv, qkv, qkv, qkv_c, qkv_c, bias)


def _ctx_attn_kernel(q_ref, k_ref, v_ref, o_ref):
    lo = lax.broadcasted_iota(jnp.int32, (1, 128), 1) < HEAD_DIM
    q, k, v = q_ref[0], k_ref[0], v_ref[0]
    zero = jnp.zeros_like(q)
    outs = []
    for hl in range(2):
        qh = jnp.where(lo, q, zero) if hl == 0 else jnp.where(lo, zero, q)
        s = _dot_nt(qh, k)
        p = jnp.exp2(s - jnp.max(s, axis=-1, keepdims=True))
        o = jnp.dot(p.astype(BF16), v, preferred_element_type=F32)
        outs.append(o / jnp.sum(p, axis=-1, keepdims=True))
    o_ref[0] = jnp.where(lo, outs[0], outs[1])


def _ctx_attn(qkv):
    B, C, _ = qkv.shape
    npair = NA_HEADS // 2
    blk = lambda j: pl.BlockSpec((1, C, 128), lambda b, p: (b, 0, j * npair + p))
    return pl.pallas_call(
        _ctx_attn_kernel,
        grid=(B, npair),
        in_specs=[blk(0), blk(1), blk(2)],
        out_specs=blk(0),
        out_shape=jax.ShapeDtypeStruct((B, C, NA_HEADS * HEAD_DIM), F32),
        compiler_params=_cparams(("parallel", "parallel")),
        name="ctx_attn",
    )(qkv, qkv, qkv)


PL_R, PL_V, PL_KK = 0, 1, 2
PL_LOGW, PL_KDIR, PL_B = 3, 4, 5
N_PLANES = 9
RWKV_CHUNK = 64


def _split3(x):
    hi = x.astype(BF16)
    r1 = x - hi.astype(F32)
    mid = r1.astype(BF16)
    lo = (r1 - mid.astype(F32)).astype(BF16)
    return hi, mid, lo


def _dot_exact_rhs(m, x):
    hi, mid, lo = _split3(x)
    mb = m.astype(BF16)
    d = lambda t: jnp.dot(mb, t, preferred_element_type=F32)
    return d(hi) + d(mid) + d(lo)


def _dot_exact_lhs(x, m):
    hi, mid, lo = _split3(x)
    mb = m.astype(BF16)
    d = lambda t: jnp.dot(t, mb, preferred_element_type=F32)
    return d(hi) + d(mid) + d(lo)


def _head_ones(n):
    r = lax.broadcasted_iota(jnp.int32, (n, n), 0) // HEAD_DIM
    c = lax.broadcasted_iota(jnp.int32, (n, n), 1) // HEAD_DIM
    return (r == c).astype(F32)


def _rwkv_prep_kernel(zc_ref, zp_ref, zn_ref, mu_ref, kk_ref, ka_ref, w0_ref, a0_ref, wup_ref, aup_ref, o_ref):
    tm = zc_ref.shape[1]
    m = pl.program_id(1)
    u = zc_ref[0]
    prev_row = jnp.where(m > 0, zp_ref[0, 7:8, :], 0.0)
    next_row = jnp.where(m < pl.num_programs(1) - 1, zn_ref[0, 0:1, :], 0.0)
    rows = lax.broadcasted_iota(jnp.int32, (tm, 1), 0)
    u_prev = jnp.where(rows == 0, prev_row, pltpu.roll(u, 1, 0))
    u_next = jnp.where(rows == tm - 1, next_row, pltpu.roll(u, tm - 1, 0))
    u = u + mu_ref[0:1, :] * (u_prev - u) + mu_ref[1:2, :] * (u_next - u)

    W = RWKV_WIDTH
    r, k, v = u[:, 0:W], u[:, W:2 * W], u[:, 2 * W:3 * W]
    lw = jnp.tanh(u[:, 3 * W:3 * W + 2 * LORA]).astype(BF16)
    la = u[:, 3 * W + 2 * LORA:3 * W + 4 * LORA].astype(BF16)
    ones = _head_ones(W)
    kk = k * kk_ref[...]
    ss = _dot_exact_lhs(kk * kk, ones)
    kk = kk / jnp.maximum(jnp.sqrt(ss), 1e-12)
    o_ref[0, :, PL_R * W:(PL_R + 1) * W] = r
    o_ref[0, :, PL_V * W:(PL_V + 1) * W] = v
    o_ref[0, :, PL_KK * W:(PL_KK + 1) * W] = kk
    for d in range(2):
        xw = w0_ref[d:d + 1, :] + jnp.dot(lw, wup_ref[d], preferred_element_type=F32)
        logw = -math.exp(-0.5) * _sigmoid(xw)
        a = _sigmoid(a0_ref[d:d + 1, :] + jnp.dot(la, aup_ref[d], preferred_element_type=F32))
        kdir = k * (1.0 + (a - 1.0) * ka_ref[...])
        base = 3 * d
        o_ref[0, :, (PL_LOGW + base) * W:(PL_LOGW + base + 1) * W] = logw
        o_ref[0, :, (PL_KDIR + base) * W:(PL_KDIR + base + 1) * W] = kdir
        o_ref[0, :, (PL_B + base) * W:(PL_B + base + 1) * W] = kk * a


def _rwkv_prep(z, mu_pad, k_k, k_a, w0, a0, wup2, aup2, tm):
    B, T, wz = z.shape
    nb = tm // 8
    W = RWKV_WIDTH
    const = lambda shape: pl.BlockSpec(shape, lambda b, m: (0,) * len(shape))
    return pl.pallas_call(
        _rwkv_prep_kernel,
        grid=(B, T // tm),
        in_specs=[pl.BlockSpec((1, tm, wz), lambda b, m: (b, m, 0)),
                  pl.BlockSpec((1, 8, wz), lambda b, m: (b, jnp.maximum(m * nb - 1, 0), 0)),
                  pl.BlockSpec((1, 8, wz), lambda b, m: (b, jnp.minimum((m + 1) * nb, T // 8 - 1), 0)),
                  const((2, wz)), const((1, W)), const((1, W)), const((2, W)), const((2, W)),
                  const((2, 2 * LORA, W)), const((2, 2 * LORA, W))],
        out_specs=pl.BlockSpec((1, tm, N_PLANES * W), lambda b, m: (b, m, 0)),
        out_shape=jax.ShapeDtypeStruct((B, T, N_PLANES * W), F32),
        compiler_params=_cparams(("parallel", "parallel")),
        name="rwkv_prep",
    )(z, z, z, mu_pad, k_k.reshape(1, W), k_a.reshape(1, W), w0, a0, wup2, aup2)


def _rwkv_scan_kernel(pf_ref, pb_ref, s0_ref, yf_ref, yb_ref, s_ref):
    C = RWKV_CHUNK
    W = RWKV_WIDTH
    C2 = 2 * C

    @pl.when(pl.program_id(1) == 0)
    def _():
        s_ref[...] = s0_ref[...]

    lo = lax.broadcasted_iota(jnp.int32, (1, 128), 1) < HEAD_DIM
    row = lax.broadcasted_iota(jnp.int32, (C2, C2), 0)
    col = lax.broadcasted_iota(jnp.int32, (C2, C2), 1)
    same = (row // C) == (col // C)
    eye = (row == col).astype(F32)
    tri_r = lax.broadcasted_iota(jnp.int32, (C, C), 0)
    tri_c = lax.broadcasted_iota(jnp.int32, (C, C), 1)
    levels = [((row >> k) == (col >> k)) & ((row >> (k - 1)) != (col >> (k - 1))) for k in range(1, 7)]
    p_refs, y_refs = (pf_ref, pb_ref), (yf_ref, yb_ref)
    nb = pf_ref.shape[0]
    groups = [(n, d, p) for n in range(nb) for d in range(2) for p in range(W // 128)]
    ng = len(groups)
    bf = lambda t: t.astype(BF16)

    def stack(x):
        return jnp.concatenate([jnp.where(lo, x, 0.0), jnp.where(lo, 0.0, x)], axis=0)

    c_dir = {}
    for d in range(2):
        cum = ((tri_c <= tri_r) if d == 0 else (tri_c >= tri_r)).astype(F32)
        for n in range(nb):
            logw_all = p_refs[d][n, :, (PL_LOGW + 3 * d) * W:(PL_LOGW + 3 * d + 1) * W]
            c_dir[n, d] = _dot_exact_rhs(cum, logw_all)

    ar, bk, ends, vs, decay_end = [], [], [], [], []
    for n, d, p in groups:
        get = lambda plane: p_refs[d][n, :, plane * W + p * 128:plane * W + (p + 1) * 128]
        r, v, kk = get(PL_R), get(PL_V), get(PL_KK)
        logw, kdir, bb = get(PL_LOGW + 3 * d), get(PL_KDIR + 3 * d), get(PL_B + 3 * d)
        c = c_dir[n, d][:, p * 128:(p + 1) * 128]
        last = C - 1 if d == 0 else 0
        c_last = c[last:last + 1, :]
        e_neg = jnp.exp(-c)
        e_end = jnp.exp(c_last - c)
        ar.append(bf(jnp.concatenate([stack(-kk * jnp.exp(c - logw)), stack(r * jnp.exp(c))], axis=0)))
        bk.append(bf(jnp.concatenate([stack(bb * e_neg), stack(kdir * e_neg)], axis=0)))
        ends.append(bf(jnp.concatenate([stack(bb * e_end), stack(kdir * e_end)], axis=0)))
        vs.append(bf(stack(v)))
        decay_end.append(jnp.exp(c_last))

    amat = [_dot_nt(ar[g], bk[g]) for g in range(ng)]
    a_ab, a_kr, a_rb = [], [], []
    for g, (n, d, p) in enumerate(groups):
        before = (col < row) if d == 0 else (col > row)
        strict = same & before
        incl = same & (before | (row == col))
        m = amat[g]
        a_ab.append(jnp.where(strict, m[:C2, :C2], 0.0))
        a_kr.append(bf(jnp.concatenate([jnp.where(strict, m[:C2, C2:], 0.0),
                                        jnp.where(incl, m[C2:, C2:], 0.0)], axis=0)))
        a_rb.append(bf(jnp.where(incl, m[C2:, :C2], 0.0)))

    t = [eye + jnp.where(levels[0], a, 0.0) for a in a_ab]
    for lvl in levels[1:]:
        ta = [jnp.dot(bf(t[g]), bf(jnp.where(lvl, a_ab[g], 0.0)), preferred_element_type=F32) for g in range(ng)]
        t = [t[g] + jnp.dot(bf(ta[g]), bf(t[g]), preferred_element_type=F32) for g in range(ng)]

    s_old = [s_ref[n, d, p] for n, d, p in groups]
    from_s = [_dot_nt(ar[g], bf(s_old[g])) for g in range(ng)]
    from_v = [jnp.dot(a_kr[g], vs[g], preferred_element_type=F32) for g in range(ng)]
    u = [jnp.dot(bf(t[g]), bf(from_s[g][:C2] + from_v[g][:C2]), preferred_element_type=F32) for g in range(ng)]
    for g, (n, d, p) in enumerate(groups):
        y_s = from_s[g][C2:] + from_v[g][C2:] + jnp.dot(a_rb[g], bf(u[g]), preferred_element_type=F32)
        y_refs[d][n, :, p * 128:(p + 1) * 128] = y_s[:C] + y_s[C:]
    for g, (n, d, p) in enumerate(groups):
        uv = jnp.concatenate([bf(u[g]), vs[g]], axis=0)
        s_ref[n, d, p] = s_old[g] * decay_end[g] + lax.dot_general(
            uv, ends[g], (((0,), (0,)), ((), ())), preferred_element_type=F32)


def _rwkv_scan(planes, s0):
    B, T, _ = planes.shape
    nc = T // RWKV_CHUNK
    W = RWKV_WIDTH
    nb = 2 if B % 2 == 0 else 1
    st = pl.BlockSpec((nb, 2, W // 128, 128, 128), lambda b, i: (b, 0, 0, 0, 0))
    y = jax.ShapeDtypeStruct((B, T, W), F32)
    return pl.pallas_call(
        _rwkv_scan_kernel,
        grid=(B // nb, nc),
        in_specs=[pl.BlockSpec((nb, RWKV_CHUNK, N_PLANES * W), lambda b, i: (b, i, 0)),
                  pl.BlockSpec((nb, RWKV_CHUNK, N_PLANES * W), lambda b, i: (b, nc - 1 - i, 0)),
                  st],
        out_specs=[pl.BlockSpec((nb, RWKV_CHUNK, W), lambda b, i: (b, i, 0)),
                   pl.BlockSpec((nb, RWKV_CHUNK, W), lambda b, i: (b, nc - 1 - i, 0)),
                   st],
        out_shape=[y, y, jax.ShapeDtypeStruct(s0.shape, F32)],
        compiler_params=_cparams(("parallel", "arbitrary")),
        name="rwkv_scan",
    )(planes, planes, s0)


def _rwkv_readout_kernel(yf_ref, yb_ref, r_ref, v_ref, kf_ref, kb_ref, g_ref, b_ref, rk_ref, o_ref):
    ones = _head_ones(RWKV_WIDTH)
    y = yf_ref[0] + yb_ref[0]
    inv_n = 1.0 / HEAD_DIM
    mu = _dot_exact_lhs(y, ones) * inv_n
    yc = y - mu
    var = _dot_exact_lhs(yc * yc, ones) * inv_n
    yn = yc * lax.rsqrt(var + GN_EPS) * g_ref[...] + b_ref[...]
    bonus = _dot_exact_lhs(r_ref[0] * (kf_ref[0] + kb_ref[0]) * rk_ref[...], ones)
    o_ref[0] = yn + bonus * v_ref[0]


def _rwkv_readout(y_f, y_b, planes, ln_g, ln_b, r_k, tm):
    B, T, W = y_f.shape
    row = pl.BlockSpec((1, tm, W), lambda b, m: (b, m, 0))
    plane = lambda j: pl.BlockSpec((1, tm, W), lambda b, m: (b, m, j))
    vec = pl.BlockSpec((1, W), lambda b, m: (0, 0))
    return pl.pallas_call(
        _rwkv_readout_kernel,
        grid=(B, T // tm),
        in_specs=[row, row, plane(PL_R), plane(PL_V), plane(PL_KDIR), plane(PL_KDIR + 3), vec, vec, vec],
        out_specs=row,
        out_shape=jax.ShapeDtypeStruct((B, T, W), F32),
        compiler_params=_cparams(("parallel", "parallel")),
        name="rwkv_readout",
    )(y_f, y_b, planes, planes, planes, planes, ln_g.reshape(1, W), ln_b.reshape(1, W), r_k.reshape(1, W))


def _layer_weights(l, w_in, shift_mu, w_up, a_up, w_branch, w_out):
    rw, rg, na, ng, df, dg, mg = jnp.split(w_in[l], [int(i) for i in np.cumsum(IN_SIZES)[:-1]], axis=-1)
    pad = jnp.zeros((w_in.shape[1], RWKV_PAD_WIDTH - RWKV_SHIFT_WIDTH), w_in.dtype)
    w_perm = jnp.concatenate([mg, rg, ng, dg, rw, pad, na, df], axis=-1).astype(BF16)
    w_perm = w_perm.reshape(w_in.shape[1], Z_WIDTH // TILE_W, TILE_W).transpose(1, 0, 2)
    mu_pad = jnp.pad(shift_mu[l], ((0, 0), (0, RWKV_PAD_WIDTH - RWKV_SHIFT_WIDTH)))
    zl = jnp.zeros((LORA, RWKV_WIDTH), F32)
    wup2 = jnp.stack([jnp.concatenate([w_up[l, 0], zl]), jnp.concatenate([zl, w_up[l, 1]])]).astype(BF16)
    aup2 = jnp.stack([jnp.concatenate([a_up[l, 0], zl]), jnp.concatenate([zl, a_up[l, 1]])]).astype(BF16)
    return w_perm, mu_pad, wup2, aup2, w_branch[l].astype(BF16), w_out[l].astype(BF16)


def kernel(x, c, ctx, c_ctx, w_mod, b_mod, g_pre, g_post, w_in, shift_mu, k_k, k_a, r_k, w0, w_up, a0, a_up,
           ln_x_g, ln_x_b, rpb, lam_q, lam_k, diff_subln, w_branch, w_out):
    B, S, D = x.shape
    C = ctx.shape[1]
    depth = w_in.shape[0]
    tables = _rope_tables(S)
    rows_pad = 16
    cvec = jnp.zeros((rows_pad, D), F32).at[:B].set(c).at[B].set(c_ctx)
    mod = _modulation(cvec, w_mod, b_mod)
    hc = ctx
    tm_x = min(2048, S)
    for l in range(depth):
        last = l == depth - 1
        lambda_init = 0.8 - 0.6 * math.exp(-0.3 * l)
        w_perm, mu_pad, wup2, aup2, wb, wo = _layer_weights(l, w_in, shift_mu, w_up, a_up, w_branch, w_out)
        mod_x = mod[l, :B].reshape(B, 3, D)
        mod_c = mod[l, B:B + 1].reshape(1, 3, D)
        zg_x, rw_x, na_x, df_x = _inproj(x, mod_x, g_pre[l], w_perm, tables, tm_x)
        ctx_out = _inproj(hc.reshape(1, B * C, D), mod_c, g_pre[l], w_perm, None, B * C)
        zg_c, rw_c, na_c, df_c = (t.reshape(B, C, t.shape[-1]) for t in ctx_out)

        pl_c = _rwkv_prep(rw_c, mu_pad, k_k[l], k_a[l], w0[l], a0[l], wup2, aup2, min(256, C))
        pl_x = _rwkv_prep(rw_x, mu_pad, k_k[l], k_a[l], w0[l], a0[l], wup2, aup2, 512)
        s_zero = jnp.zeros((B, 2, RWKV_WIDTH // 128, 128, 128), F32)
        yf_c, yb_c, s_ctx = _rwkv_scan(pl_c, s_zero)
        yf_x, yb_x, _ = _rwkv_scan(pl_x, s_ctx)
        r_k_flat = r_k[l].reshape(RWKV_WIDTH)
        o_rw_x = _rwkv_readout(yf_x, yb_x, pl_x, ln_x_g[l], ln_x_b[l], r_k_flat, 512)

        o_na_x = _na_attn(na_x, na_c, _na_bias_tables(rpb[l]) * LOG2E, min(16, S // GRID_W))
        o_df_x = _diff_attn(df_x, df_c, True, lam_q[l], lam_k[l], diff_subln[l], lambda_init, 512, min(512, S // 2))

        x = _merge(x, mod_x, g_post[l], o_rw_x, o_na_x, o_df_x, zg_x, wb, wo, 512)
        if not last:
            o_rw_c = _rwkv_readout(yf_c, yb_c, pl_c, ln_x_g[l], ln_x_b[l], r_k_flat, min(256, C))
            o_na_c = _ctx_attn(na_c)
            o_df_c = _diff_attn(df_c, df_c, False, lam_q[l], lam_k[l], diff_subln[l], lambda_init, C, C)
            hc = _merge(hc, mod_c, g_post[l], o_rw_c, o_na_c, o_df_c, zg_c, wb, wo, min(256, C))
    return x
```

```python
import functools
import math

import jax
import jax.numpy as jnp
import numpy as np
from jax import lax
from jax.experimental import pallas as pl
from jax.experimental.pallas import tpu as pltpu

F32 = jnp.float32
BF16 = jnp.bfloat16

DEPTH = 4
GRID_W = 64
RWKV_HEADS = 8
HEAD_DIM = 64
RWKV_WIDTH = 512
LORA = 64
GN_EPS = 64e-5
NA_HEADS = 8
NA_WIN_R = 8
NA_WIN_C = 16
DIFF_HEADS = 4
DIFF_QK_DIM = 64
DIFF_V_DIM = 128
ROPE_THETA = 10000.0
SUBLN_EPS = 1e-5
RMS_EPS = 1e-6
NEG_INF = -1e30
LOG2E = math.log2(math.e)
BRANCH_WIDTH = 512
N_BRANCH = 3

RWKV_SHIFT_WIDTH = 3 * RWKV_WIDTH + 4 * LORA
RWKV_PAD_WIDTH = 2048
TILE_W = 512
TILE_RW, TILE_NA, TILE_DF = 9, 13, 16
Z_WIDTH = 19 * TILE_W
IN_SIZES = (RWKV_SHIFT_WIDTH, 512, 1536, 512, 1536, 512, 3072)

VMEM_LIMIT = 56 * 1024 * 1024


def _cparams(sem):
    return pltpu.CompilerParams(dimension_semantics=sem, vmem_limit_bytes=VMEM_LIMIT)


def _sigmoid(x):
    return 1.0 / (1.0 + jnp.exp(-x))


def _silu(x):
    return x * _sigmoid(x)


def _bdot(a, b):
    return jnp.dot(a.astype(BF16), b.astype(BF16), preferred_element_type=F32)


def _mod_kernel(c_ref, w_ref, b_ref, o_ref):
    o_ref[0] = _bdot(_silu(c_ref[...]), w_ref[0]) + b_ref[0]


def _modulation(cvec, w_mod, b_mod):
    L, D, N = w_mod.shape
    R = cvec.shape[0]
    tn = 1024
    return pl.pallas_call(
        _mod_kernel,
        grid=(L, N // tn),
        in_specs=[pl.BlockSpec((R, D), lambda l, n: (0, 0)),
                  pl.BlockSpec((1, D, tn), lambda l, n: (l, 0, n)),
                  pl.BlockSpec((1, 1, tn), lambda l, n: (l, 0, n))],
        out_specs=pl.BlockSpec((1, R, tn), lambda l, n: (l, 0, n)),
        out_shape=jax.ShapeDtypeStruct((L, R, N), F32),
        compiler_params=_cparams(("parallel", "parallel")),
        name="modulation",
    )(cvec, w_mod, b_mod.reshape(L, 1, N))


def _inproj_kernel(*refs, rope, q_scale):
    if rope:
        x_ref, mod_ref, g_ref, w_ref, cos_ref, sa_ref, sb_ref, zg_ref, rw_ref, na_ref, df_ref, h_ref = refs
    else:
        x_ref, mod_ref, g_ref, w_ref, zg_ref, rw_ref, na_ref, df_ref, h_ref = refs
    n = pl.program_id(2)

    @pl.when(n == 0)
    def _():
        x = x_ref[0]
        y = x * lax.rsqrt(jnp.mean(x * x, axis=-1, keepdims=True) + RMS_EPS) * g_ref[...]
        h_ref[...] = (y * (1.0 + mod_ref[0, 1:2, :]) + mod_ref[0, 0:1, :]).astype(BF16)

    mm = lambda: jnp.dot(h_ref[...], w_ref[...], preferred_element_type=F32)

    def rot(t):
        if not rope:
            return t
        w = t.shape[-1]
        tile = lambda r: jnp.concatenate([r[...]] * (w // r.shape[-1]), axis=1)
        return t * tile(cos_ref) + pltpu.roll(t, w - 16, 1) * tile(sa_ref) + pltpu.roll(t, 16, 1) * tile(sb_ref)

    @pl.when(n < TILE_RW)
    def _():
        zg_ref[0] = mm().astype(BF16)

    @pl.when((n >= TILE_RW) & (n < TILE_NA))
    def _():
        rw_ref[0] = mm()

    @pl.when(n == TILE_NA)
    def _():
        na_ref[0] = (mm() * q_scale).astype(BF16)

    @pl.when((n > TILE_NA) & (n < TILE_DF))
    def _():
        na_ref[0] = mm().astype(BF16)

    @pl.when(n == TILE_DF)
    def _():
        df_ref[0] = (rot(mm()) * q_scale).astype(BF16)

    @pl.when(n == TILE_DF + 1)
    def _():
        df_ref[0] = rot(mm()).astype(BF16)

    @pl.when(n == TILE_DF + 2)
    def _():
        df_ref[0] = mm().astype(BF16)


def _inproj(x, mod, g_pre, w_perm, tables, tm):
    B, T, D = x.shape
    tn = TILE_W
    per_batch = mod.shape[0] > 1
    in_specs = [pl.BlockSpec((1, tm, D), lambda b, m, n: (b, m, 0)),
                pl.BlockSpec((1, 3, D), (lambda b, m, n: (b, 0, 0)) if per_batch else (lambda b, m, n: (0, 0, 0))),
                pl.BlockSpec((1, D), lambda b, m, n: (0, 0)),
                pl.BlockSpec((D, tn), lambda b, m, n: (0, n))]
    args = [x, mod, g_pre.reshape(1, D), w_perm]
    if tables is not None:
        in_specs += [pl.BlockSpec((tm, 128), lambda b, m, n: (m, 0))] * 3
        args += list(tables)
    seg = lambda first, count: pl.BlockSpec((1, tm, tn), lambda b, m, n: (b, m, jnp.clip(n - first, 0, count - 1)))
    out = lambda count, dtype: jax.ShapeDtypeStruct((B, T, count * tn), dtype)
    return pl.pallas_call(
        functools.partial(_inproj_kernel, rope=tables is not None, q_scale=HEAD_DIM ** -0.5 * LOG2E),
        grid=(B, T // tm, Z_WIDTH // tn),
        in_specs=in_specs,
        out_specs=[seg(0, TILE_RW), seg(TILE_RW, TILE_NA - TILE_RW), seg(TILE_NA, TILE_DF - TILE_NA),
                   seg(TILE_DF, Z_WIDTH // tn - TILE_DF)],
        out_shape=[out(TILE_RW, BF16), out(TILE_NA - TILE_RW, F32), out(TILE_DF - TILE_NA, BF16),
                   out(Z_WIDTH // tn - TILE_DF, BF16)],
        scratch_shapes=[pltpu.VMEM((tm, D), BF16)],
        compiler_params=_cparams(("parallel", "parallel", "arbitrary")),
        name="inproj",
    )(*args)


def _merge_kernel(x_ref, mod_ref, gpost_ref, yf_ref, yb_ref, bv_ref, lng_ref, lnb_ref, ona_ref, odf_ref,
                  rg_ref, ng_ref, dg_ref, mg_ref, wb_ref, wo_ref, o_ref):
    D = x_ref.shape[-1]
    ones = _head_ones(RWKV_WIDTH)
    y = yf_ref[0] + yb_ref[0]
    mu = _dot_hilo(y, ones) * (1.0 / HEAD_DIM)
    yc = y - mu
    var = _dot_hilo(yc * yc, ones) * (1.0 / HEAD_DIM)
    o_rw = yc * lax.rsqrt(var + GN_EPS) * lng_ref[...] + lnb_ref[...] + bv_ref[0]
    acc = None
    for n, (ob, gate_ref) in enumerate(((o_rw, rg_ref), (ona_ref[0], ng_ref), (odf_ref[0], dg_ref))):
        yb = _bdot(ob * _silu(gate_ref[0].astype(F32)), wb_ref[n])
        term = _sigmoid(mg_ref[0, :, n * D:(n + 1) * D].astype(F32)) * yb
        acc = term if acc is None else acc + term
    y = _bdot(acc, wo_ref[...])
    yn = y * lax.rsqrt(jnp.mean(y * y, axis=-1, keepdims=True) + RMS_EPS) * gpost_ref[...]
    o_ref[0] = x_ref[0] + mod_ref[0, 2:3, :] * yn


def _merge(x, mod, g_post, y_f, y_b, bonus, ln_g, ln_b, o_na, o_df, zg, w_branch, w_out, tm):
    B, T, D = x.shape
    per_batch = mod.shape[0] > 1
    bw = BRANCH_WIDTH
    row = lambda b, m: (b, m, 0)
    col = lambda c: (lambda b, m: (b, m, c))
    return pl.pallas_call(
        _merge_kernel,
        grid=(B, T // tm),
        in_specs=[pl.BlockSpec((1, tm, D), row),
                  pl.BlockSpec((1, 3, D), (lambda b, m: (b, 0, 0)) if per_batch else (lambda b, m: (0, 0, 0))),
                  pl.BlockSpec((1, D), lambda b, m: (0, 0)),
                  pl.BlockSpec((1, tm, bw), row), pl.BlockSpec((1, tm, bw), row), pl.BlockSpec((1, tm, bw), row),
                  pl.BlockSpec((1, bw), lambda b, m: (0, 0)), pl.BlockSpec((1, bw), lambda b, m: (0, 0)),
                  pl.BlockSpec((1, tm, bw), row), pl.BlockSpec((1, tm, bw), row),
                  pl.BlockSpec((1, tm, bw), col(N_BRANCH * D // bw)),
                  pl.BlockSpec((1, tm, bw), col(N_BRANCH * D // bw + 1)),
                  pl.BlockSpec((1, tm, bw), col(N_BRANCH * D // bw + 2)),
                  pl.BlockSpec((1, tm, N_BRANCH * D), col(0)),
                  pl.BlockSpec((N_BRANCH, bw, D), lambda b, m: (0, 0, 0)),
                  pl.BlockSpec((D, D), lambda b, m: (0, 0))],
        out_specs=pl.BlockSpec((1, tm, D), row),
        out_shape=jax.ShapeDtypeStruct((B, T, D), F32),
        compiler_params=_cparams(("parallel", "parallel")),
        name="merge",
    )(x, mod, g_post.reshape(1, D), y_f, y_b, bonus, ln_g.reshape(1, bw), ln_b.reshape(1, bw), o_na, o_df,
      zg, zg, zg, zg, w_branch, w_out)


def _rope_tables(n_tokens):
    t = np.arange(n_tokens)
    axis_dim = DIFF_QK_DIM // 2
    inv = ROPE_THETA ** (-np.arange(0, axis_dim, 2, dtype=np.float32) / axis_dim)
    ar = (t // GRID_W).astype(np.float32)[:, None] * inv
    ac = (t % GRID_W).astype(np.float32)[:, None] * inv
    ang = jnp.asarray(np.concatenate([ar, ar, ac, ac], axis=-1).astype(np.float32))
    cos, sin = jnp.cos(ang), jnp.sin(ang)
    first = (np.arange(DIFF_QK_DIM) % 32) < 16
    sin_a = jnp.where(first, -sin, 0.0)
    sin_b = jnp.where(first, 0.0, sin)
    tile = lambda a: jnp.tile(a, (1, 2))
    return tile(cos), tile(sin_a), tile(sin_b)


def _dot_nt(a, b):
    return lax.dot_general(a, b, (((1,), (1,)), ((), ())), preferred_element_type=F32)


SOFTMAX_ROWS = 128


def _diff_attn_kernel(q_ref, kc_ref, vc_ref, k_ref, v_ref, lq_ref, lk_ref, g_ref, o_ref,
                      sc, pc, s0, s1, p0, p1, al0, al1, m_scr, acc_scr, *, n_main, tk, lambda_init):
    tq = q_ref.shape[1]
    t_first = kc_ref.shape[1]
    q = q_ref[0]
    lo = lax.broadcasted_iota(jnp.int32, (1, 128), 1) < DIFF_QK_DIM
    zero = jnp.zeros_like(q)
    qq = jnp.concatenate([jnp.where(lo, q, zero), jnp.where(lo, zero, q)], axis=0)

    n_sub = 2 * tq // SOFTMAX_ROWS
    sub = lambda i: slice(i * SOFTMAX_ROWS, (i + 1) * SOFTMAX_ROWS)

    def scores(k_chunk, s_scr):
        s_scr[:, :k_chunk.shape[0]] = _dot_nt(qq, k_chunk)

    def softmax(size, first, s_scr, p_scr, al_scr):
        w = min(128, size)
        cols = [slice(c * w, (c + 1) * w) for c in range(size // w)]
        for i in range(n_sub):
            mx = s_scr[sub(i), cols[0]]
            for c in cols[1:]:
                mx = jnp.maximum(mx, s_scr[sub(i), c])
            mn = jnp.broadcast_to(jnp.max(mx, axis=-1, keepdims=True), (SOFTMAX_ROWS, 128))
            if not first:
                mo = m_scr[sub(i), :]
                mn = jnp.maximum(mo, mn)
                al_scr[sub(i), :] = jnp.exp2(mo - mn)
            m_scr[sub(i), :] = mn
        for i in range(n_sub):
            mn = m_scr[sub(i), :w]
            for c in cols:
                p_scr[sub(i), c] = jnp.exp2(s_scr[sub(i), c] - mn).astype(BF16)

    def accumulate(v, first, p_scr, al_scr):
        size = v.shape[0]
        v_ext = jnp.concatenate([v, jnp.ones((size, 128), BF16)], axis=1)
        pv = jnp.dot(p_scr[:, :size], v_ext, preferred_element_type=F32)
        if first:
            acc_scr[...] = pv
        else:
            al = al_scr[...]
            acc_scr[...] = acc_scr[...] * jnp.concatenate([al, al], axis=1) + pv

    scores(kc_ref[0], sc)
    if n_main == 0:
        softmax(t_first, True, sc, pc, None)
        accumulate(vc_ref[0], True, pc, None)
    else:
        def chunk_of(ref, j):
            start = j * tk if isinstance(j, int) else pl.multiple_of(j * tk, tk)
            return ref[0, pl.ds(start, tk), :]

        k_at = functools.partial(chunk_of, k_ref)
        v_at = functools.partial(chunk_of, v_ref)
        scores(k_at(0), s0)
        softmax(t_first, True, sc, pc, None)

        def pair(j, first):
            scores(k_at(j + 1), s1)
            softmax(tk, False, s0, p0, al0)
            if first:
                accumulate(vc_ref[0], True, pc, None)
            else:
                accumulate(v_at(j - 1), False, p1, al1)
            nxt = min(j + 2, n_main - 1) if isinstance(j, int) else jnp.minimum(j + 2, n_main - 1)
            scores(k_at(nxt), s0)
            softmax(tk, False, s1, p1, al1)
            accumulate(v_at(j), False, p0, al0)

        pair(0, True)

        def body(t, _):
            pair(2 * t, False)
            return 0

        lax.fori_loop(1, n_main // 2, body, 0)
        accumulate(v_at(n_main - 1), False, p1, al1)
    lqk = lq_ref[...] * lk_ref[...]
    e = jnp.exp(jnp.sum(lqk, axis=-1, keepdims=True))
    lam = e[0:1] - e[1:2] + lambda_init
    o_maps = acc_scr[:, :DIFF_V_DIM] / acc_scr[:, DIFF_V_DIM:]
    o = o_maps[:tq] - lam * o_maps[tq:]
    o = o * lax.rsqrt(jnp.mean(o * o, axis=-1, keepdims=True) + SUBLN_EPS) * g_ref[...]
    o_ref[0] = o * (1.0 - lambda_init)


def _diff_attn(qkv, qkv_c, with_latent_keys, lam_q, lam_k, subln_g, lambda_init, tq, tk):
    B, T, _ = qkv.shape
    C = qkv_c.shape[1]
    n_main = T // tk if with_latent_keys else 0
    assert n_main % 2 == 0 and (2 * tq) % SOFTMAX_ROWS == 0
    wmax = tk if n_main else 128
    H = DIFF_HEADS
    kv = lambda t, j: pl.BlockSpec((1, t, 128), lambda b, h, m: (b, 0, j * H + h))
    small = lambda r, c: pl.BlockSpec((r, c), lambda b, h, m: (0, 0))
    stat = pltpu.VMEM((2 * tq, 128), F32)
    s_buf = pltpu.VMEM((2 * tq, wmax), F32)
    p_buf = pltpu.VMEM((2 * tq, wmax), BF16)
    return pl.pallas_call(
        functools.partial(_diff_attn_kernel, n_main=n_main, tk=tk, lambda_init=lambda_init),
        grid=(B, H, T // tq),
        in_specs=[pl.BlockSpec((1, tq, 128), lambda b, h, m: (b, m, h)), kv(C, 1), kv(C, 2), kv(T, 1), kv(T, 2),
                  small(2, DIFF_QK_DIM), small(2, DIFF_QK_DIM), small(1, DIFF_V_DIM)],
        out_specs=pl.BlockSpec((1, tq, 128), lambda b, h, m: (b, m, h)),
        out_shape=jax.ShapeDtypeStruct((B, T, H * DIFF_V_DIM), F32),
        scratch_shapes=[pltpu.VMEM((2 * tq, C), F32), pltpu.VMEM((2 * tq, C), BF16), s_buf, s_buf, p_buf, p_buf,
                        stat, stat, stat, pltpu.VMEM((2 * tq, 2 * DIFF_V_DIM), F32)],
        compiler_params=_cparams(("parallel", "parallel", "parallel")),
        name="diff_attn",
    )(qkv, qkv_c, qkv_c, qkv, qkv, lam_q, lam_k, subln_g.reshape(1, DIFF_V_DIM))


def _na_bias_tables(rpb):
    c_idx = np.arange(GRID_W)
    c_start = np.clip(c_idx - NA_WIN_C // 2, 0, GRID_W - NA_WIN_C)
    col_ok = (c_idx[None, :] >= c_start[:, None]) & (c_idx[None, :] < c_start[:, None] + NA_WIN_C)
    dc = np.clip(c_idx[None, :] - c_idx[:, None], -(NA_WIN_C - 1), NA_WIN_C - 1) + (NA_WIN_C - 1)
    H = rpb.shape[0]
    e = jnp.take(rpb, jnp.asarray(dc.reshape(-1)), axis=2).reshape(H, 2 * NA_WIN_R - 1, GRID_W, GRID_W)
    e = jnp.where(col_ok, e, NEG_INF)
    tabs = [e[:, NA_WIN_R - 1 - off:2 * NA_WIN_R - 1 - off].transpose(0, 2, 1, 3).reshape(H, GRID_W, NA_WIN_R * GRID_W)
            for off in range(NA_WIN_R)]
    return jnp.stack(tabs, axis=1)


def _na_kernel(q_ref, k_ref, v_ref, kc_ref, vc_ref, bias_ref, o_ref, *, rq, rows):
    i = pl.program_id(2)
    lo = lax.broadcasted_iota(jnp.int32, (1, 128), 1) < HEAD_DIM
    win = NA_WIN_R * GRID_W
    G2 = 2 * GRID_W
    q = q_ref[0]
    zero = jnp.zeros_like(q)
    q_lo, q_hi = jnp.where(lo, q, zero), jnp.where(lo, zero, q)
    qq = jnp.concatenate([x[rr * GRID_W:(rr + 1) * GRID_W] for rr in range(rq) for x in (q_lo, q_hi)], axis=0)
    kc = kc_ref[0]
    ones = lambda n: jnp.ones((n, 128), BF16)
    s_c = _dot_nt(qq, kc)

    starts, s_nb = [], []
    for rr in range(rq):
        r = i * rq + rr
        r_start = jnp.clip(r - NA_WIN_R // 2, 0, rows - NA_WIN_R)
        off = r - r_start
        start = pl.multiple_of(r_start * GRID_W, GRID_W)
        starts.append(start)
        bias = jnp.concatenate([bias_ref[0, off], bias_ref[1, off]], axis=0)
        s_nb.append(_dot_nt(qq[rr * G2:(rr + 1) * G2], k_ref[0, pl.ds(start, win), :]) + bias)

    def lane_blocks(t):
        w = min(128, t.shape[-1])
        return [t[:, c * w:(c + 1) * w] for c in range(t.shape[-1] // w)]

    p_nb, p_c = [], []
    for rr in range(rq):
        sc = s_c[rr * G2:(rr + 1) * G2]
        blocks = lane_blocks(s_nb[rr])
        mx = blocks[0]
        for b in blocks[1:]:
            mx = jnp.maximum(mx, b)
        m = jnp.maximum(jnp.max(mx, axis=-1, keepdims=True), jnp.max(sc, axis=-1, keepdims=True))
        p_nb.append(jnp.exp2(s_nb[rr] - m).astype(BF16))
        p_c.append(jnp.exp2(sc - m).astype(BF16))

    vc_ext = jnp.concatenate([vc_ref[0], ones(kc.shape[0])], axis=1)
    o_c = jnp.dot(jnp.concatenate(p_c, axis=0), vc_ext, preferred_element_type=F32)
    for rr in range(rq):
        vw_ext = jnp.concatenate([v_ref[0, pl.ds(starts[rr], win), :], ones(win)], axis=1)
        o = jnp.dot(p_nb[rr], vw_ext, preferred_element_type=F32) + o_c[rr * G2:(rr + 1) * G2]
        o = o[:, :128] / o[:, 128:]
        o_ref[0, rr * GRID_W:(rr + 1) * GRID_W, :] = jnp.where(lo, o[:GRID_W], o[GRID_W:])


def _na_attn(qkv, qkv_c, bias, rq):
    B, S, _ = qkv.shape
    C = qkv_c.shape[1]
    rows = S // GRID_W
    npair = NA_HEADS // 2
    full = lambda t, j: pl.BlockSpec((1, t, 128), lambda b, p, m: (b, 0, j * npair + p))
    return pl.pallas_call(
        functools.partial(_na_kernel, rq=rq, rows=rows),
        grid=(B, npair, rows // rq),
        in_specs=[pl.BlockSpec((1, rq * GRID_W, 128), lambda b, p, m: (b, m, p)),
                  full(S, 1), full(S, 2), full(C, 1), full(C, 2),
                  pl.BlockSpec((2, NA_WIN_R, GRID_W, NA_WIN_R * GRID_W), lambda b, p, m: (p, 0, 0, 0))],
        out_specs=pl.BlockSpec((1, rq * GRID_W, 128), lambda b, p, m: (b, m, p)),
        out_shape=jax.ShapeDtypeStruct((B, S, NA_HEADS * HEAD_DIM), F32),
        compiler_params=_cparams(("parallel", "parallel", "parallel")),
        name="na_attn",
    )(qkv, qkv, qkv, qkv_c, qkv_c, bias)


def _ctx_attn_kernel(q_ref, k_ref, v_ref, o_ref):
    lo = lax.broadcasted_iota(jnp.int32, (1, 128), 1) < HEAD_DIM
    q, k, v = q_ref[0], k_ref[0], v_ref[0]
    zero = jnp.zeros_like(q)
    outs = []
    for hl in range(2):
        qh = jnp.where(lo, q, zero) if hl == 0 else jnp.where(lo, zero, q)
        s = _dot_nt(qh, k)
        p = jnp.exp2(s - jnp.max(s, axis=-1, keepdims=True))
        o = jnp.dot(p.astype(BF16), v, preferred_element_type=F32)
        outs.append(o / jnp.sum(p, axis=-1, keepdims=True))
    o_ref[0] = jnp.where(lo, outs[0], outs[1])


def _ctx_attn(qkv):
    B, C, _ = qkv.shape
    npair = NA_HEADS // 2
    blk = lambda j: pl.BlockSpec((1, C, 128), lambda b, p: (b, 0, j * npair + p))
    return pl.pallas_call(
        _ctx_attn_kernel,
        grid=(B, npair),
        in_specs=[blk(0), blk(1), blk(2)],
        out_specs=blk(0),
        out_shape=jax.ShapeDtypeStruct((B, C, NA_HEADS * HEAD_DIM), F32),
        compiler_params=_cparams(("parallel", "parallel")),
        name="ctx_attn",
    )(qkv, qkv, qkv)


PL_R, PL_V, PL_KK = 0, 1, 2
PL_LOGW, PL_KDIR, PL_B = 3, 4, 5
N_PLANES = 9
RWKV_CHUNK = 64


def _split3(x):
    hi = x.astype(BF16)
    r1 = x - hi.astype(F32)
    mid = r1.astype(BF16)
    lo = (r1 - mid.astype(F32)).astype(BF16)
    return hi, mid, lo


def _dot_exact_rhs(m, x):
    hi, mid, lo = _split3(x)
    mb = m.astype(BF16)
    d = lambda t: jnp.dot(mb, t, preferred_element_type=F32)
    return d(hi) + d(mid) + d(lo)


def _dot_hilo(x, m):
    hi = x.astype(BF16)
    lo = (x - hi.astype(F32)).astype(BF16)
    mb = m.astype(BF16)
    d = lambda t: jnp.dot(t, mb, preferred_element_type=F32)
    return d(hi) + d(lo)


def _head_ones(n):
    r = lax.broadcasted_iota(jnp.int32, (n, n), 0) // HEAD_DIM
    c = lax.broadcasted_iota(jnp.int32, (n, n), 1) // HEAD_DIM
    return (r == c).astype(F32)


def _rwkv_prep_kernel(zc_ref, zp_ref, zn_ref, mu_ref, kk_ref, ka_ref, rk_ref, w0_ref, a0_ref, wup_ref, aup_ref,
                      o_ref, bv_ref):
    tm = zc_ref.shape[1]
    m = pl.program_id(1)
    u = zc_ref[0]
    prev_row = jnp.where(m > 0, zp_ref[0, 7:8, :], 0.0)
    next_row = jnp.where(m < pl.num_programs(1) - 1, zn_ref[0, 0:1, :], 0.0)
    rows = lax.broadcasted_iota(jnp.int32, (tm, 1), 0)
    u_prev = jnp.where(rows == 0, prev_row, pltpu.roll(u, 1, 0))
    u_next = jnp.where(rows == tm - 1, next_row, pltpu.roll(u, tm - 1, 0))
    u = u + mu_ref[0:1, :] * (u_prev - u) + mu_ref[1:2, :] * (u_next - u)

    W = RWKV_WIDTH
    r, k, v = u[:, 0:W], u[:, W:2 * W], u[:, 2 * W:3 * W]
    lw = jnp.tanh(u[:, 3 * W:3 * W + 2 * LORA]).astype(BF16)
    la = u[:, 3 * W + 2 * LORA:3 * W + 4 * LORA].astype(BF16)
    ones = _head_ones(W)
    kk = k * kk_ref[...]
    ss = _dot_hilo(kk * kk, ones)
    kk = kk / jnp.maximum(jnp.sqrt(ss), 1e-12)
    o_ref[0, :, PL_R * W:(PL_R + 1) * W] = r
    o_ref[0, :, PL_V * W:(PL_V + 1) * W] = v
    o_ref[0, :, PL_KK * W:(PL_KK + 1) * W] = kk
    k_both = None
    for d in range(2):
        xw = w0_ref[d:d + 1, :] + jnp.dot(lw, wup_ref[d], preferred_element_type=F32)
        logw = -math.exp(-0.5) * _sigmoid(xw)
        a = _sigmoid(a0_ref[d:d + 1, :] + jnp.dot(la, aup_ref[d], preferred_element_type=F32))
        kdir = k * (1.0 + (a - 1.0) * ka_ref[...])
        base = 3 * d
        o_ref[0, :, (PL_LOGW + base) * W:(PL_LOGW + base + 1) * W] = logw
        o_ref[0, :, (PL_KDIR + base) * W:(PL_KDIR + base + 1) * W] = kdir
        o_ref[0, :, (PL_B + base) * W:(PL_B + base + 1) * W] = kk * a
        k_both = kdir if k_both is None else k_both + kdir
    bv_ref[0] = _dot_hilo(r * k_both * rk_ref[...], ones) * v


def _rwkv_prep(z, mu_pad, k_k, k_a, r_k, w0, a0, wup2, aup2, tm):
    B, T, wz = z.shape
    nb = tm // 8
    W = RWKV_WIDTH
    const = lambda shape: pl.BlockSpec(shape, lambda b, m: (0,) * len(shape))
    return pl.pallas_call(
        _rwkv_prep_kernel,
        grid=(B, T // tm),
        in_specs=[pl.BlockSpec((1, tm, wz), lambda b, m: (b, m, 0)),
                  pl.BlockSpec((1, 8, wz), lambda b, m: (b, jnp.maximum(m * nb - 1, 0), 0)),
                  pl.BlockSpec((1, 8, wz), lambda b, m: (b, jnp.minimum((m + 1) * nb, T // 8 - 1), 0)),
                  const((2, wz)), const((1, W)), const((1, W)), const((1, W)), const((2, W)), const((2, W)),
                  const((2, 2 * LORA, W)), const((2, 2 * LORA, W))],
        out_specs=[pl.BlockSpec((1, tm, N_PLANES * W), lambda b, m: (b, m, 0)),
                   pl.BlockSpec((1, tm, W), lambda b, m: (b, m, 0))],
        out_shape=[jax.ShapeDtypeStruct((B, T, N_PLANES * W), F32), jax.ShapeDtypeStruct((B, T, W), F32)],
        compiler_params=_cparams(("parallel", "parallel")),
        name="rwkv_prep",
    )(z, z, z, mu_pad, k_k.reshape(1, W), k_a.reshape(1, W), r_k.reshape(1, W), w0, a0, wup2, aup2)


def _rwkv_scan_kernel(pf_ref, pb_ref, s0_ref, yf_ref, yb_ref, s_ref):
    C = RWKV_CHUNK
    W = RWKV_WIDTH
    C2 = 2 * C

    @pl.when(pl.program_id(1) == 0)
    def _():
        s_ref[...] = s0_ref[...]

    lo = lax.broadcasted_iota(jnp.int32, (1, 128), 1) < HEAD_DIM
    row = lax.broadcasted_iota(jnp.int32, (C2, C2), 0)
    col = lax.broadcasted_iota(jnp.int32, (C2, C2), 1)
    same = (row // C) == (col // C)
    eye = (row == col).astype(F32)
    tri_r = lax.broadcasted_iota(jnp.int32, (C, C), 0)
    tri_c = lax.broadcasted_iota(jnp.int32, (C, C), 1)
    levels = [((row >> k) == (col >> k)) & ((row >> (k - 1)) != (col >> (k - 1))) for k in range(1, 7)]
    p_refs, y_refs = (pf_ref, pb_ref), (yf_ref, yb_ref)
    nb = pf_ref.shape[0]
    groups = [(n, d, p) for n in range(nb) for d in range(2) for p in range(W // 128)]
    ng = len(groups)
    bf = lambda t: t.astype(BF16)

    def stack(x):
        return jnp.concatenate([jnp.where(lo, x, 0.0), jnp.where(lo, 0.0, x)], axis=0)

    c_dir = {}
    for d in range(2):
        cum = ((tri_c <= tri_r) if d == 0 else (tri_c >= tri_r)).astype(F32)
        for n in range(nb):
            logw_all = p_refs[d][n, :, (PL_LOGW + 3 * d) * W:(PL_LOGW + 3 * d + 1) * W]
            c_dir[n, d] = _dot_exact_rhs(cum, logw_all)

    ar, bk, ends, vs, decay_end = [], [], [], [], []
    for n, d, p in groups:
        get = lambda plane: p_refs[d][n, :, plane * W + p * 128:plane * W + (p + 1) * 128]
        r, v, kk = get(PL_R), get(PL_V), get(PL_KK)
        logw, kdir, bb = get(PL_LOGW + 3 * d), get(PL_KDIR + 3 * d), get(PL_B + 3 * d)
        c = c_dir[n, d][:, p * 128:(p + 1) * 128]
        last = C - 1 if d == 0 else 0
        c_last = c[last:last + 1, :]
        e_neg = jnp.exp(-c)
        e_end = jnp.exp(c_last - c)
        ar.append(bf(jnp.concatenate([stack(-kk * jnp.exp(c - logw)), stack(r * jnp.exp(c))], axis=0)))
        bk.append(bf(jnp.concatenate([stack(bb * e_neg), stack(kdir * e_neg)], axis=0)))
        ends.append(bf(jnp.concatenate([stack(bb * e_end), stack(kdir * e_end)], axis=0)))
        vs.append(bf(stack(v)))
        decay_end.append(jnp.exp(c_last))

    amat = [_dot_nt(ar[g], bk[g]) for g in range(ng)]
    a_ab, a_kr, a_rb = [], [], []
    for g, (n, d, p) in enumerate(groups):
        before = (col < row) if d == 0 else (col > row)
        strict = same & before
        incl = same & (before | (row == col))
        m = amat[g]
        a_ab.append(jnp.where(strict, m[:C2, :C2], 0.0))
        a_kr.append(bf(jnp.concatenate([jnp.where(strict, m[:C2, C2:], 0.0),
                                        jnp.where(incl, m[C2:, C2:], 0.0)], axis=0)))
        a_rb.append(bf(jnp.where(incl, m[C2:, :C2], 0.0)))

    t = [eye + jnp.where(levels[0], a, 0.0) for a in a_ab]
    for lvl in levels[1:]:
        ta = [jnp.dot(bf(t[g]), bf(jnp.where(lvl, a_ab[g], 0.0)), preferred_element_type=F32) for g in range(ng)]
        t = [t[g] + jnp.dot(bf(ta[g]), bf(t[g]), preferred_element_type=F32) for g in range(ng)]

    s_old = [s_ref[n, d, p] for n, d, p in groups]
    from_s = [_dot_nt(ar[g], bf(s_old[g])) for g in range(ng)]
    from_v = [jnp.dot(a_kr[g], vs[g], preferred_element_type=F32) for g in range(ng)]
    u = [jnp.dot(bf(t[g]), bf(from_s[g][:C2] + from_v[g][:C2]), preferred_element_type=F32) for g in range(ng)]
    for g, (n, d, p) in enumerate(groups):
        y_s = from_s[g][C2:] + from_v[g][C2:] + jnp.dot(a_rb[g], bf(u[g]), preferred_element_type=F32)
        y_refs[d][n, :, p * 128:(p + 1) * 128] = y_s[:C] + y_s[C:]
    for g, (n, d, p) in enumerate(groups):
        uv = jnp.concatenate([bf(u[g]), vs[g]], axis=0)
        s_ref[n, d, p] = s_old[g] * decay_end[g] + lax.dot_general(
            uv, ends[g], (((0,), (0,)), ((), ())), preferred_element_type=F32)


def _rwkv_scan(planes, s0):
    B, T, _ = planes.shape
    nc = T // RWKV_CHUNK
    W = RWKV_WIDTH
    nb = 2 if B % 2 == 0 else 1
    st = pl.BlockSpec((nb, 2, W // 128, 128, 128), lambda b, i: (b, 0, 0, 0, 0))
    y = jax.ShapeDtypeStruct((B, T, W), F32)
    return pl.pallas_call(
        _rwkv_scan_kernel,
        grid=(B // nb, nc),
        in_specs=[pl.BlockSpec((nb, RWKV_CHUNK, N_PLANES * W), lambda b, i: (b, i, 0)),
                  pl.BlockSpec((nb, RWKV_CHUNK, N_PLANES * W), lambda b, i: (b, nc - 1 - i, 0)),
                  st],
        out_specs=[pl.BlockSpec((nb, RWKV_CHUNK, W), lambda b, i: (b, i, 0)),
                   pl.BlockSpec((nb, RWKV_CHUNK, W), lambda b, i: (b, nc - 1 - i, 0)),
                   st],
        out_shape=[y, y, jax.ShapeDtypeStruct(s0.shape, F32)],
        compiler_params=_cparams(("parallel", "arbitrary")),
        name="rwkv_scan",
    )(planes, planes, s0)


def _layer_weights(l, w_in, shift_mu, w_up, a_up, w_branch, w_out):
    rw, rg, na, ng, df, dg, mg = jnp.split(w_in[l], [int(i) for i in np.cumsum(IN_SIZES)[:-1]], axis=-1)
    pad = jnp.zeros((w_in.shape[1], RWKV_PAD_WIDTH - RWKV_SHIFT_WIDTH), w_in.dtype)
    w_perm = jnp.concatenate([mg, rg, ng, dg, rw, pad, na, df], axis=-1).astype(BF16)
    mu_pad = jnp.pad(shift_mu[l], ((0, 0), (0, RWKV_PAD_WIDTH - RWKV_SHIFT_WIDTH)))
    zl = jnp.zeros((LORA, RWKV_WIDTH), F32)
    wup2 = jnp.stack([jnp.concatenate([w_up[l, 0], zl]), jnp.concatenate([zl, w_up[l, 1]])]).astype(BF16)
    aup2 = jnp.stack([jnp.concatenate([a_up[l, 0], zl]), jnp.concatenate([zl, a_up[l, 1]])]).astype(BF16)
    return w_perm, mu_pad, wup2, aup2, w_branch[l].astype(BF16), w_out[l].astype(BF16)


def kernel(x, c, ctx, c_ctx, w_mod, b_mod, g_pre, g_post, w_in, shift_mu, k_k, k_a, r_k, w0, w_up, a0, a_up,
           ln_x_g, ln_x_b, rpb, lam_q, lam_k, diff_subln, w_branch, w_out):
    B, S, D = x.shape
    C = ctx.shape[1]
    depth = w_in.shape[0]
    tables = _rope_tables(S)
    rows_pad = 16
    cvec = jnp.zeros((rows_pad, D), F32).at[:B].set(c).at[B].set(c_ctx)
    mod = _modulation(cvec, w_mod, b_mod)
    hc = ctx
    tm_x = min(2048, S)
    for l in range(depth):
        last = l == depth - 1
        lambda_init = 0.8 - 0.6 * math.exp(-0.3 * l)
        w_perm, mu_pad, wup2, aup2, wb, wo = _layer_weights(l, w_in, shift_mu, w_up, a_up, w_branch, w_out)
        mod_x = mod[l, :B].reshape(B, 3, D)
        mod_c = mod[l, B:B + 1].reshape(1, 3, D)
        zg_x, rw_x, na_x, df_x = _inproj(x, mod_x, g_pre[l], w_perm, tables, tm_x)
        ctx_out = _inproj(hc.reshape(1, B * C, D), mod_c, g_pre[l], w_perm, None, B * C)
        zg_c, rw_c, na_c, df_c = (t.reshape(B, C, t.shape[-1]) for t in ctx_out)

        r_k_flat = r_k[l].reshape(RWKV_WIDTH)
        pl_c, bv_c = _rwkv_prep(rw_c, mu_pad, k_k[l], k_a[l], r_k_flat, w0[l], a0[l], wup2, aup2, min(256, C))
        pl_x, bv_x = _rwkv_prep(rw_x, mu_pad, k_k[l], k_a[l], r_k_flat, w0[l], a0[l], wup2, aup2, 512)
        s_zero = jnp.zeros((B, 2, RWKV_WIDTH // 128, 128, 128), F32)
        yf_c, yb_c, s_ctx = _rwkv_scan(pl_c, s_zero)
        yf_x, yb_x, _ = _rwkv_scan(pl_x, s_ctx)

        o_na_x = _na_attn(na_x, na_c, _na_bias_tables(rpb[l]) * LOG2E, min(16, S // GRID_W))
        o_df_x = _diff_attn(df_x, df_c, True, lam_q[l], lam_k[l], diff_subln[l], lambda_init, 512, min(512, S // 2))

        x = _merge(x, mod_x, g_post[l], yf_x, yb_x, bv_x, ln_x_g[l], ln_x_b[l], o_na_x, o_df_x, zg_x, wb, wo, 512)
        if not last:
            o_na_c = _ctx_attn(na_c)
            o_df_c = _diff_attn(df_c, df_c, False, lam_q[l], lam_k[l], diff_subln[l], lambda_init, C, C)
            hc = _merge(hc, mod_c, g_post[l], yf_c, yb_c, bv_c, ln_x_g[l], ln_x_b[l], o_na_c, o_df_c, zg_c, wb, wo,
                        min(256, C))
    return x
```

```python
import functools
import math

import jax
import jax.numpy as jnp
import numpy as np
from jax import lax
from jax.experimental import pallas as pl
from jax.experimental.pallas import tpu as pltpu

F32 = jnp.float32
BF16 = jnp.bfloat16

DEPTH = 4
GRID_W = 64
RWKV_HEADS = 8
HEAD_DIM = 64
RWKV_WIDTH = 512
LORA = 64
GN_EPS = 64e-5
NA_HEADS = 8
NA_WIN_R = 8
NA_WIN_C = 16
DIFF_HEADS = 4
DIFF_QK_DIM = 64
DIFF_V_DIM = 128
ROPE_THETA = 10000.0
SUBLN_EPS = 1e-5
RMS_EPS = 1e-6
NEG_INF = -1e30
LOG2E = math.log2(math.e)
BRANCH_WIDTH = 512
N_BRANCH = 3

RWKV_SHIFT_WIDTH = 3 * RWKV_WIDTH + 4 * LORA
RWKV_PAD_WIDTH = 2048
TILE_W = 512
TILE_RW, TILE_NA, TILE_DF = 9, 13, 16
Z_WIDTH = 19 * TILE_W
IN_SIZES = (RWKV_SHIFT_WIDTH, 512, 1536, 512, 1536, 512, 3072)

VMEM_LIMIT = 56 * 1024 * 1024


def _cparams(sem):
    return pltpu.CompilerParams(dimension_semantics=sem, vmem_limit_bytes=VMEM_LIMIT)


def _sigmoid(x):
    return 1.0 / (1.0 + jnp.exp(-x))


def _silu(x):
    return x * _sigmoid(x)


def _bdot(a, b):
    return jnp.dot(a.astype(BF16), b.astype(BF16), preferred_element_type=F32)


def _mod_kernel(c_ref, w_ref, b_ref, o_ref):
    o_ref[0] = _bdot(_silu(c_ref[...]), w_ref[0]) + b_ref[0]


def _modulation(cvec, w_mod, b_mod):
    L, D, N = w_mod.shape
    R = cvec.shape[0]
    tn = 1024
    return pl.pallas_call(
        _mod_kernel,
        grid=(L, N // tn),
        in_specs=[pl.BlockSpec((R, D), lambda l, n: (0, 0)),
                  pl.BlockSpec((1, D, tn), lambda l, n: (l, 0, n)),
                  pl.BlockSpec((1, 1, tn), lambda l, n: (l, 0, n))],
        out_specs=pl.BlockSpec((1, R, tn), lambda l, n: (l, 0, n)),
        out_shape=jax.ShapeDtypeStruct((L, R, N), F32),
        compiler_params=_cparams(("parallel", "parallel")),
        name="modulation",
    )(cvec, w_mod, b_mod.reshape(L, 1, N))


def _inproj_kernel(*refs, rope, q_scale):
    if rope:
        x_ref, mod_ref, g_ref, w_ref, cos_ref, sa_ref, sb_ref, zg_ref, rw_ref, na_ref, df_ref, h_ref = refs
    else:
        x_ref, mod_ref, g_ref, w_ref, zg_ref, rw_ref, na_ref, df_ref, h_ref = refs
    n = pl.program_id(2)

    @pl.when(n == 0)
    def _():
        x = x_ref[0]
        y = x * lax.rsqrt(jnp.mean(x * x, axis=-1, keepdims=True) + RMS_EPS) * g_ref[...]
        h_ref[...] = (y * (1.0 + mod_ref[0, 1:2, :]) + mod_ref[0, 0:1, :]).astype(BF16)

    mm = lambda: jnp.dot(h_ref[...], w_ref[...], preferred_element_type=F32)

    def rot(t):
        if not rope:
            return t
        w = t.shape[-1]
        tile = lambda r: jnp.concatenate([r[...]] * (w // r.shape[-1]), axis=1)
        return t * tile(cos_ref) + pltpu.roll(t, w - 16, 1) * tile(sa_ref) + pltpu.roll(t, 16, 1) * tile(sb_ref)

    @pl.when(n < TILE_RW)
    def _():
        zg_ref[0] = mm().astype(BF16)

    @pl.when((n >= TILE_RW) & (n < TILE_NA))
    def _():
        rw_ref[0] = mm()

    @pl.when(n == TILE_NA)
    def _():
        na_ref[0] = (mm() * q_scale).astype(BF16)

    @pl.when((n > TILE_NA) & (n < TILE_DF))
    def _():
        na_ref[0] = mm().astype(BF16)

    @pl.when(n == TILE_DF)
    def _():
        df_ref[0] = (rot(mm()) * q_scale).astype(BF16)

    @pl.when(n == TILE_DF + 1)
    def _():
        df_ref[0] = rot(mm()).astype(BF16)

    @pl.when(n == TILE_DF + 2)
    def _():
        df_ref[0] = mm().astype(BF16)


def _inproj(x, mod, g_pre, w_perm, tables, tm):
    B, T, D = x.shape
    tn = TILE_W
    per_batch = mod.shape[0] > 1
    in_specs = [pl.BlockSpec((1, tm, D), lambda b, m, n: (b, m, 0)),
                pl.BlockSpec((1, 3, D), (lambda b, m, n: (b, 0, 0)) if per_batch else (lambda b, m, n: (0, 0, 0))),
                pl.BlockSpec((1, D), lambda b, m, n: (0, 0)),
                pl.BlockSpec((D, tn), lambda b, m, n: (0, n))]
    args = [x, mod, g_pre.reshape(1, D), w_perm]
    if tables is not None:
        in_specs += [pl.BlockSpec((tm, 128), lambda b, m, n: (m, 0))] * 3
        args += list(tables)
    seg = lambda first, count: pl.BlockSpec((1, tm, tn), lambda b, m, n: (b, m, jnp.clip(n - first, 0, count - 1)))
    out = lambda count, dtype: jax.ShapeDtypeStruct((B, T, count * tn), dtype)
    return pl.pallas_call(
        functools.partial(_inproj_kernel, rope=tables is not None, q_scale=HEAD_DIM ** -0.5 * LOG2E),
        grid=(B, T // tm, Z_WIDTH // tn),
        in_specs=in_specs,
        out_specs=[seg(0, TILE_RW), seg(TILE_RW, TILE_NA - TILE_RW), seg(TILE_NA, TILE_DF - TILE_NA),
                   seg(TILE_DF, Z_WIDTH // tn - TILE_DF)],
        out_shape=[out(TILE_RW, BF16), out(TILE_NA - TILE_RW, F32), out(TILE_DF - TILE_NA, BF16),
                   out(Z_WIDTH // tn - TILE_DF, BF16)],
        scratch_shapes=[pltpu.VMEM((tm, D), BF16)],
        compiler_params=_cparams(("parallel", "parallel", "arbitrary")),
        name="inproj",
    )(*args)


def _merge_kernel(x_ref, mod_ref, gpost_ref, yf_ref, yb_ref, bv_ref, lng_ref, lnb_ref, ona_ref, odf_ref,
                  rg_ref, ng_ref, dg_ref, mg_ref, wb_ref, wo_ref, o_ref):
    D = x_ref.shape[-1]
    ones = _head_ones(RWKV_WIDTH)
    y = yf_ref[0] + yb_ref[0]
    mu = _dot_hilo(y, ones) * (1.0 / HEAD_DIM)
    yc = y - mu
    var = _dot_hilo(yc * yc, ones) * (1.0 / HEAD_DIM)
    o_rw = yc * lax.rsqrt(var + GN_EPS) * lng_ref[...] + lnb_ref[...] + bv_ref[0]
    acc = None
    for n, (ob, gate_ref) in enumerate(((o_rw, rg_ref), (ona_ref[0], ng_ref), (odf_ref[0], dg_ref))):
        yb = _bdot(ob * _silu(gate_ref[0].astype(F32)), wb_ref[n])
        term = _sigmoid(mg_ref[0, :, n * D:(n + 1) * D].astype(F32)) * yb
        acc = term if acc is None else acc + term
    y = _bdot(acc, wo_ref[...])
    yn = y * lax.rsqrt(jnp.mean(y * y, axis=-1, keepdims=True) + RMS_EPS) * gpost_ref[...]
    o_ref[0] = x_ref[0] + mod_ref[0, 2:3, :] * yn


def _merge(x, mod, g_post, y_f, y_b, bonus, ln_g, ln_b, o_na, o_df, zg, w_branch, w_out, tm):
    B, T, D = x.shape
    per_batch = mod.shape[0] > 1
    bw = BRANCH_WIDTH
    row = lambda b, m: (b, m, 0)
    col = lambda c: (lambda b, m: (b, m, c))
    return pl.pallas_call(
        _merge_kernel,
        grid=(B, T // tm),
        in_specs=[pl.BlockSpec((1, tm, D), row),
                  pl.BlockSpec((1, 3, D), (lambda b, m: (b, 0, 0)) if per_batch else (lambda b, m: (0, 0, 0))),
                  pl.BlockSpec((1, D), lambda b, m: (0, 0)),
                  pl.BlockSpec((1, tm, bw), row), pl.BlockSpec((1, tm, bw), row), pl.BlockSpec((1, tm, bw), row),
                  pl.BlockSpec((1, bw), lambda b, m: (0, 0)), pl.BlockSpec((1, bw), lambda b, m: (0, 0)),
                  pl.BlockSpec((1, tm, bw), row), pl.BlockSpec((1, tm, bw), row),
                  pl.BlockSpec((1, tm, bw), col(N_BRANCH * D // bw)),
                  pl.BlockSpec((1, tm, bw), col(N_BRANCH * D // bw + 1)),
                  pl.BlockSpec((1, tm, bw), col(N_BRANCH * D // bw + 2)),
                  pl.BlockSpec((1, tm, N_BRANCH * D), col(0)),
                  pl.BlockSpec((N_BRANCH, bw, D), lambda b, m: (0, 0, 0)),
                  pl.BlockSpec((D, D), lambda b, m: (0, 0))],
        out_specs=pl.BlockSpec((1, tm, D), row),
        out_shape=jax.ShapeDtypeStruct((B, T, D), F32),
        compiler_params=_cparams(("parallel", "parallel")),
        name="merge",
    )(x, mod, g_post.reshape(1, D), y_f, y_b, bonus, ln_g.reshape(1, bw), ln_b.reshape(1, bw), o_na, o_df,
      zg, zg, zg, zg, w_branch, w_out)


def _rope_tables(n_tokens):
    t = np.arange(n_tokens)
    axis_dim = DIFF_QK_DIM // 2
    inv = ROPE_THETA ** (-np.arange(0, axis_dim, 2, dtype=np.float32) / axis_dim)
    ar = (t // GRID_W).astype(np.float32)[:, None] * inv
    ac = (t % GRID_W).astype(np.float32)[:, None] * inv
    ang = jnp.asarray(np.concatenate([ar, ar, ac, ac], axis=-1).astype(np.float32))
    cos, sin = jnp.cos(ang), jnp.sin(ang)
    first = (np.arange(DIFF_QK_DIM) % 32) < 16
    sin_a = jnp.where(first, -sin, 0.0)
    sin_b = jnp.where(first, 0.0, sin)
    tile = lambda a: jnp.tile(a, (1, 2))
    return tile(cos), tile(sin_a), tile(sin_b)


def _dot_nt(a, b):
    return lax.dot_general(a, b, (((1,), (1,)), ((), ())), preferred_element_type=F32)


SOFTMAX_ROWS = 128


def _diff_attn_kernel(q_ref, kc_ref, vc_ref, k_ref, v_ref, lq_ref, lk_ref, g_ref, o_ref,
                      sc, pc, s0, s1, p0, p1, al0, al1, m_scr, acc_scr, *, n_main, tk, lambda_init):
    tq = q_ref.shape[1]
    t_first = kc_ref.shape[1]
    q = q_ref[0]
    lo = lax.broadcasted_iota(jnp.int32, (1, 128), 1) < DIFF_QK_DIM
    zero = jnp.zeros_like(q)
    qq = jnp.concatenate([jnp.where(lo, q, zero), jnp.where(lo, zero, q)], axis=0)

    n_sub = 2 * tq // SOFTMAX_ROWS
    sub = lambda i: slice(i * SOFTMAX_ROWS, (i + 1) * SOFTMAX_ROWS)

    def scores(k_chunk, s_scr):
        s_scr[:, :k_chunk.shape[0]] = _dot_nt(qq, k_chunk)

    def softmax(size, first, s_scr, p_scr, al_scr):
        w = min(128, size)
        cols = [slice(c * w, (c + 1) * w) for c in range(size // w)]
        for i in range(n_sub):
            mx = s_scr[sub(i), cols[0]]
            for c in cols[1:]:
                mx = jnp.maximum(mx, s_scr[sub(i), c])
            mn = jnp.broadcast_to(jnp.max(mx, axis=-1, keepdims=True), (SOFTMAX_ROWS, 128))
            if not first:
                mo = m_scr[sub(i), :]
                mn = jnp.maximum(mo, mn)
                al_scr[sub(i), :] = jnp.exp2(mo - mn)
            m_scr[sub(i), :] = mn
        for i in range(n_sub):
            mn = m_scr[sub(i), :w]
            for c in cols:
                p_scr[sub(i), c] = jnp.exp2(s_scr[sub(i), c] - mn).astype(BF16)

    def accumulate(v, first, p_scr, al_scr):
        size = v.shape[0]
        v_ext = jnp.concatenate([v, jnp.ones((size, 128), BF16)], axis=1)
        pv = jnp.dot(p_scr[:, :size], v_ext, preferred_element_type=F32)
        if first:
            acc_scr[...] = pv
        else:
            al = al_scr[...]
            acc_scr[...] = acc_scr[...] * jnp.concatenate([al, al], axis=1) + pv

    scores(kc_ref[0], sc)
    if n_main == 0:
        softmax(t_first, True, sc, pc, None)
        accumulate(vc_ref[0], True, pc, None)
    else:
        def chunk_of(ref, j):
            start = j * tk if isinstance(j, int) else pl.multiple_of(j * tk, tk)
            return ref[0, pl.ds(start, tk), :]

        k_at = functools.partial(chunk_of, k_ref)
        v_at = functools.partial(chunk_of, v_ref)
        scores(k_at(0), s0)
        softmax(t_first, True, sc, pc, None)

        def pair(j, first):
            scores(k_at(j + 1), s1)
            softmax(tk, False, s0, p0, al0)
            if first:
                accumulate(vc_ref[0], True, pc, None)
            else:
                accumulate(v_at(j - 1), False, p1, al1)
            nxt = min(j + 2, n_main - 1) if isinstance(j, int) else jnp.minimum(j + 2, n_main - 1)
            scores(k_at(nxt), s0)
            softmax(tk, False, s1, p1, al1)
            accumulate(v_at(j), False, p0, al0)

        pair(0, True)

        def body(t, _):
            pair(2 * t, False)
            return 0

        lax.fori_loop(1, n_main // 2, body, 0)
        accumulate(v_at(n_main - 1), False, p1, al1)
    lqk = lq_ref[...] * lk_ref[...]
    e = jnp.exp(jnp.sum(lqk, axis=-1, keepdims=True))
    lam = e[0:1] - e[1:2] + lambda_init
    o_maps = acc_scr[:, :DIFF_V_DIM] / acc_scr[:, DIFF_V_DIM:]
    o = o_maps[:tq] - lam * o_maps[tq:]
    o = o * lax.rsqrt(jnp.mean(o * o, axis=-1, keepdims=True) + SUBLN_EPS) * g_ref[...]
    o_ref[0] = o * (1.0 - lambda_init)


def _diff_attn(qkv, qkv_c, with_latent_keys, lam_q, lam_k, subln_g, lambda_init, tq, tk):
    B, T, _ = qkv.shape
    C = qkv_c.shape[1]
    n_main = T // tk if with_latent_keys else 0
    assert n_main % 2 == 0 and (2 * tq) % SOFTMAX_ROWS == 0
    wmax = tk if n_main else 128
    H = DIFF_HEADS
    kv = lambda t, j: pl.BlockSpec((1, t, 128), lambda b, h, m: (b, 0, j * H + h))
    small = lambda r, c: pl.BlockSpec((r, c), lambda b, h, m: (0, 0))
    stat = pltpu.VMEM((2 * tq, 128), F32)
    s_buf = pltpu.VMEM((2 * tq, wmax), F32)
    p_buf = pltpu.VMEM((2 * tq, wmax), BF16)
    return pl.pallas_call(
        functools.partial(_diff_attn_kernel, n_main=n_main, tk=tk, lambda_init=lambda_init),
        grid=(B, H, T // tq),
        in_specs=[pl.BlockSpec((1, tq, 128), lambda b, h, m: (b, m, h)), kv(C, 1), kv(C, 2), kv(T, 1), kv(T, 2),
                  small(2, DIFF_QK_DIM), small(2, DIFF_QK_DIM), small(1, DIFF_V_DIM)],
        out_specs=pl.BlockSpec((1, tq, 128), lambda b, h, m: (b, m, h)),
        out_shape=jax.ShapeDtypeStruct((B, T, H * DIFF_V_DIM), F32),
        scratch_shapes=[pltpu.VMEM((2 * tq, C), F32), pltpu.VMEM((2 * tq, C), BF16), s_buf, s_buf, p_buf, p_buf,
                        stat, stat, stat, pltpu.VMEM((2 * tq, 2 * DIFF_V_DIM), F32)],
        compiler_params=_cparams(("parallel", "parallel", "parallel")),
        name="diff_attn",
    )(qkv, qkv_c, qkv_c, qkv, qkv, lam_q, lam_k, subln_g.reshape(1, DIFF_V_DIM))


def _na_bias_tables(rpb):
    c_idx = np.arange(GRID_W)
    c_start = np.clip(c_idx - NA_WIN_C // 2, 0, GRID_W - NA_WIN_C)
    col_ok = (c_idx[None, :] >= c_start[:, None]) & (c_idx[None, :] < c_start[:, None] + NA_WIN_C)
    dc = np.clip(c_idx[None, :] - c_idx[:, None], -(NA_WIN_C - 1), NA_WIN_C - 1) + (NA_WIN_C - 1)
    H = rpb.shape[0]
    e = jnp.take(rpb, jnp.asarray(dc.reshape(-1)), axis=2).reshape(H, 2 * NA_WIN_R - 1, GRID_W, GRID_W)
    e = jnp.where(col_ok, e, NEG_INF)
    tabs = [e[:, NA_WIN_R - 1 - off:2 * NA_WIN_R - 1 - off].transpose(0, 2, 1, 3).reshape(H, GRID_W, NA_WIN_R * GRID_W)
            for off in range(NA_WIN_R)]
    return jnp.stack(tabs, axis=1)


def _na_kernel(q_ref, k_ref, v_ref, kc_ref, vc_ref, bias_ref, o_ref, *, rq, rows):
    i = pl.program_id(2)
    lo = lax.broadcasted_iota(jnp.int32, (1, 128), 1) < HEAD_DIM
    win = NA_WIN_R * GRID_W
    G2 = 2 * GRID_W
    q = q_ref[0]
    zero = jnp.zeros_like(q)
    q_lo, q_hi = jnp.where(lo, q, zero), jnp.where(lo, zero, q)
    qq = jnp.concatenate([x[rr * GRID_W:(rr + 1) * GRID_W] for rr in range(rq) for x in (q_lo, q_hi)], axis=0)
    kc = kc_ref[0]
    ones = lambda n: jnp.ones((n, 128), BF16)
    s_c = _dot_nt(qq, kc)

    starts, s_nb = [], []
    for rr in range(rq):
        r = i * rq + rr
        r_start = jnp.clip(r - NA_WIN_R // 2, 0, rows - NA_WIN_R)
        off = r - r_start
        start = pl.multiple_of(r_start * GRID_W, GRID_W)
        starts.append(start)
        bias = jnp.concatenate([bias_ref[0, off], bias_ref[1, off]], axis=0)
        s_nb.append(_dot_nt(qq[rr * G2:(rr + 1) * G2], k_ref[0, pl.ds(start, win), :]) + bias)

    def lane_blocks(t):
        w = min(128, t.shape[-1])
        return [t[:, c * w:(c + 1) * w] for c in range(t.shape[-1] // w)]

    p_nb, p_c = [], []
    for rr in range(rq):
        sc = s_c[rr * G2:(rr + 1) * G2]
        blocks = lane_blocks(s_nb[rr])
        mx = blocks[0]
        for b in blocks[1:]:
            mx = jnp.maximum(mx, b)
        m = jnp.maximum(jnp.max(mx, axis=-1, keepdims=True), jnp.max(sc, axis=-1, keepdims=True))
        p_nb.append(jnp.exp2(s_nb[rr] - m).astype(BF16))
        p_c.append(jnp.exp2(sc - m).astype(BF16))

    vc_ext = jnp.concatenate([vc_ref[0], ones(kc.shape[0])], axis=1)
    o_c = jnp.dot(jnp.concatenate(p_c, axis=0), vc_ext, preferred_element_type=F32)
    for rr in range(rq):
        vw_ext = jnp.concatenate([v_ref[0, pl.ds(starts[rr], win), :], ones(win)], axis=1)
        o = jnp.dot(p_nb[rr], vw_ext, preferred_element_type=F32) + o_c[rr * G2:(rr + 1) * G2]
        o = o[:, :128] / o[:, 128:]
        o_ref[0, rr * GRID_W:(rr + 1) * GRID_W, :] = jnp.where(lo, o[:GRID_W], o[GRID_W:])


def _na_attn(qkv, qkv_c, bias, rq):
    B, S, _ = qkv.shape
    C = qkv_c.shape[1]
    rows = S // GRID_W
    npair = NA_HEADS // 2
    full = lambda t, j: pl.BlockSpec((1, t, 128), lambda b, p, m: (b, 0, j * npair + p))
    return pl.pallas_call(
        functools.partial(_na_kernel, rq=rq, rows=rows),
        grid=(B, npair, rows // rq),
        in_specs=[pl.BlockSpec((1, rq * GRID_W, 128), lambda b, p, m: (b, m, p)),
                  full(S, 1), full(S, 2), full(C, 1), full(C, 2),
                  pl.BlockSpec((2, NA_WIN_R, GRID_W, NA_WIN_R * GRID_W), lambda b, p, m: (p, 0, 0, 0))],
        out_specs=pl.BlockSpec((1, rq * GRID_W, 128), lambda b, p, m: (b, m, p)),
        out_shape=jax.ShapeDtypeStruct((B, S, NA_HEADS * HEAD_DIM), F32),
        compiler_params=_cparams(("parallel", "parallel", "parallel")),
        name="na_attn",
    )(qkv, qkv, qkv, qkv_c, qkv_c, bias)


def _ctx_attn_kernel(q_ref, k_ref, v_ref, o_ref):
    lo = lax.broadcasted_iota(jnp.int32, (1, 128), 1) < HEAD_DIM
    q, k, v = q_ref[0], k_ref[0], v_ref[0]
    zero = jnp.zeros_like(q)
    outs = []
    for hl in range(2):
        qh = jnp.where(lo, q, zero) if hl == 0 else jnp.where(lo, zero, q)
        s = _dot_nt(qh, k)
        p = jnp.exp2(s - jnp.max(s, axis=-1, keepdims=True))
        o = jnp.dot(p.astype(BF16), v, preferred_element_type=F32)
        outs.append(o / jnp.sum(p, axis=-1, keepdims=True))
    o_ref[0] = jnp.where(lo, outs[0], outs[1])


def _ctx_attn(qkv):
    B, C, _ = qkv.shape
    npair = NA_HEADS // 2
    blk = lambda j: pl.BlockSpec((1, C, 128), lambda b, p: (b, 0, j * npair + p))
    return pl.pallas_call(
        _ctx_attn_kernel,
        grid=(B, npair),
        in_specs=[blk(0), blk(1), blk(2)],
        out_specs=blk(0),
        out_shape=jax.ShapeDtypeStruct((B, C, NA_HEADS * HEAD_DIM), F32),
        compiler_params=_cparams(("parallel", "parallel")),
        name="ctx_attn",
    )(qkv, qkv, qkv)


PL_R, PL_V, PL_KK = 0, 1, 2
PL_LOGW, PL_KDIR, PL_B = 3, 4, 5
N_PLANES = 9
RWKV_CHUNK = 64


def _split3(x):
    hi = x.astype(BF16)
    r1 = x - hi.astype(F32)
    mid = r1.astype(BF16)
    lo = (r1 - mid.astype(F32)).astype(BF16)
    return hi, mid, lo


def _dot_exact_rhs(m, x):
    hi, mid, lo = _split3(x)
    mb = m.astype(BF16)
    d = lambda t: jnp.dot(mb, t, preferred_element_type=F32)
    return d(hi) + d(mid) + d(lo)


def _dot_hilo(x, m):
    hi = x.astype(BF16)
    lo = (x - hi.astype(F32)).astype(BF16)
    mb = m.astype(BF16)
    d = lambda t: jnp.dot(t, mb, preferred_element_type=F32)
    return d(hi) + d(lo)


def _head_ones(n):
    r = lax.broadcasted_iota(jnp.int32, (n, n), 0) // HEAD_DIM
    c = lax.broadcasted_iota(jnp.int32, (n, n), 1) // HEAD_DIM
    return (r == c).astype(F32)


def _rwkv_prep_kernel(zc_ref, zp_ref, zn_ref, mu_ref, kk_ref, ka_ref, rk_ref, w0_ref, a0_ref, wup_ref, aup_ref,
                      o_ref, bv_ref):
    tm = zc_ref.shape[1]
    m = pl.program_id(1)
    u = zc_ref[0]
    prev_row = jnp.where(m > 0, zp_ref[0, 7:8, :], 0.0)
    next_row = jnp.where(m < pl.num_programs(1) - 1, zn_ref[0, 0:1, :], 0.0)
    rows = lax.broadcasted_iota(jnp.int32, (tm, 1), 0)
    u_prev = jnp.where(rows == 0, prev_row, pltpu.roll(u, 1, 0))
    u_next = jnp.where(rows == tm - 1, next_row, pltpu.roll(u, tm - 1, 0))
    u = u + mu_ref[0:1, :] * (u_prev - u) + mu_ref[1:2, :] * (u_next - u)

    W = RWKV_WIDTH
    r, k, v = u[:, 0:W], u[:, W:2 * W], u[:, 2 * W:3 * W]
    lw = jnp.tanh(u[:, 3 * W:3 * W + 2 * LORA]).astype(BF16)
    la = u[:, 3 * W + 2 * LORA:3 * W + 4 * LORA].astype(BF16)
    ones = _head_ones(W)
    kk = k * kk_ref[...]
    ss = _dot_hilo(kk * kk, ones)
    kk = kk / jnp.maximum(jnp.sqrt(ss), 1e-12)
    o_ref[0, :, PL_R * W:(PL_R + 1) * W] = r
    o_ref[0, :, PL_V * W:(PL_V + 1) * W] = v
    o_ref[0, :, PL_KK * W:(PL_KK + 1) * W] = kk
    k_both = None
    for d in range(2):
        xw = w0_ref[d:d + 1, :] + jnp.dot(lw, wup_ref[d], preferred_element_type=F32)
        logw = -math.exp(-0.5) * _sigmoid(xw)
        a = _sigmoid(a0_ref[d:d + 1, :] + jnp.dot(la, aup_ref[d], preferred_element_type=F32))
        kdir = k * (1.0 + (a - 1.0) * ka_ref[...])
        base = 3 * d
        o_ref[0, :, (PL_LOGW + base) * W:(PL_LOGW + base + 1) * W] = logw
        o_ref[0, :, (PL_KDIR + base) * W:(PL_KDIR + base + 1) * W] = kdir
        o_ref[0, :, (PL_B + base) * W:(PL_B + base + 1) * W] = kk * a
        k_both = kdir if k_both is None else k_both + kdir
    bv_ref[0] = _dot_hilo(r * k_both * rk_ref[...], ones) * v


def _rwkv_prep(z, mu_pad, k_k, k_a, r_k, w0, a0, wup2, aup2, tm):
    B, T, wz = z.shape
    nb = tm // 8
    W = RWKV_WIDTH
    const = lambda shape: pl.BlockSpec(shape, lambda b, m: (0,) * len(shape))
    return pl.pallas_call(
        _rwkv_prep_kernel,
        grid=(B, T // tm),
        in_specs=[pl.BlockSpec((1, tm, wz), lambda b, m: (b, m, 0)),
                  pl.BlockSpec((1, 8, wz), lambda b, m: (b, jnp.maximum(m * nb - 1, 0), 0)),
                  pl.BlockSpec((1, 8, wz), lambda b, m: (b, jnp.minimum((m + 1) * nb, T // 8 - 1), 0)),
                  const((2, wz)), const((1, W)), const((1, W)), const((1, W)), const((2, W)), const((2, W)),
                  const((2, 2 * LORA, W)), const((2, 2 * LORA, W))],
        out_specs=[pl.BlockSpec((1, tm, N_PLANES * W), lambda b, m: (b, m, 0)),
                   pl.BlockSpec((1, tm, W), lambda b, m: (b, m, 0))],
        out_shape=[jax.ShapeDtypeStruct((B, T, N_PLANES * W), F32), jax.ShapeDtypeStruct((B, T, W), F32)],
        compiler_params=_cparams(("parallel", "parallel")),
        name="rwkv_prep",
    )(z, z, z, mu_pad, k_k.reshape(1, W), k_a.reshape(1, W), r_k.reshape(1, W), w0, a0, wup2, aup2)


def _rwkv_scan_kernel(pf_ref, pb_ref, s0_ref, yf_ref, yb_ref, s_ref):
    C = RWKV_CHUNK
    W = RWKV_WIDTH
    C2 = 2 * C

    @pl.when(pl.program_id(1) == 0)
    def _():
        s_ref[...] = s0_ref[...]

    lo = lax.broadcasted_iota(jnp.int32, (1, 128), 1) < HEAD_DIM
    row = lax.broadcasted_iota(jnp.int32, (C2, C2), 0)
    col = lax.broadcasted_iota(jnp.int32, (C2, C2), 1)
    same = (row // C) == (col // C)
    eye = (row == col).astype(F32)
    tri_r = lax.broadcasted_iota(jnp.int32, (C, C), 0)
    tri_c = lax.broadcasted_iota(jnp.int32, (C, C), 1)
    levels = [((row >> k) == (col >> k)) & ((row >> (k - 1)) != (col >> (k - 1))) for k in range(1, 7)]
    p_refs, y_refs = (pf_ref, pb_ref), (yf_ref, yb_ref)
    nb = pf_ref.shape[0]
    groups = [(n, d, p) for n in range(nb) for d in range(2) for p in range(W // 128)]
    ng = len(groups)
    bf = lambda t: t.astype(BF16)

    def stack(x):
        return jnp.concatenate([jnp.where(lo, x, 0.0), jnp.where(lo, 0.0, x)], axis=0)

    c_dir = {}
    for d in range(2):
        cum = ((tri_c <= tri_r) if d == 0 else (tri_c >= tri_r)).astype(F32)
        for n in range(nb):
            logw_all = p_refs[d][n, :, (PL_LOGW + 3 * d) * W:(PL_LOGW + 3 * d + 1) * W]
            c_dir[n, d] = _dot_exact_rhs(cum, logw_all)

    ar, bk, ends, vs, decay_end = [], [], [], [], []
    for n, d, p in groups:
        get = lambda plane: p_refs[d][n, :, plane * W + p * 128:plane * W + (p + 1) * 128]
        r, v, kk = get(PL_R), get(PL_V), get(PL_KK)
        logw, kdir, bb = get(PL_LOGW + 3 * d), get(PL_KDIR + 3 * d), get(PL_B + 3 * d)
        c = c_dir[n, d][:, p * 128:(p + 1) * 128]
        last = C - 1 if d == 0 else 0
        c_last = c[last:last + 1, :]
        e_neg = jnp.exp(-c)
        e_end = jnp.exp(c_last - c)
        ar.append(bf(jnp.concatenate([stack(-kk * jnp.exp(c - logw)), stack(r * jnp.exp(c))], axis=0)))
        bk.append(bf(jnp.concatenate([stack(bb * e_neg), stack(kdir * e_neg)], axis=0)))
        ends.append(bf(jnp.concatenate([stack(bb * e_end), stack(kdir * e_end)], axis=0)))
        vs.append(bf(stack(v)))
        decay_end.append(jnp.exp(c_last))

    amat = [_dot_nt(ar[g], bk[g]) for g in range(ng)]
    a_ab, a_kr, a_rb = [], [], []
    for g, (n, d, p) in enumerate(groups):
        before = (col < row) if d == 0 else (col > row)
        strict = same & before
        incl = same & (before | (row == col))
        m = amat[g]
        a_ab.append(jnp.where(strict, m[:C2, :C2], 0.0))
        a_kr.append(bf(jnp.concatenate([jnp.where(strict, m[:C2, C2:], 0.0),
                                        jnp.where(incl, m[C2:, C2:], 0.0)], axis=0)))
        a_rb.append(bf(jnp.where(incl, m[C2:, :C2], 0.0)))

    t = [eye + jnp.where(levels[0], a, 0.0) for a in a_ab]
    for lvl in levels[1:]:
        ta = [jnp.dot(bf(t[g]), bf(jnp.where(lvl, a_ab[g], 0.0)), preferred_element_type=F32) for g in range(ng)]
        t = [t[g] + jnp.dot(bf(ta[g]), bf(t[g]), preferred_element_type=F32) for g in range(ng)]

    s_old = [s_ref[n, d, p] for n, d, p in groups]
    from_s = [_dot_nt(ar[g], bf(s_old[g])) for g in range(ng)]
    from_v = [jnp.dot(a_kr[g], vs[g], preferred_element_type=F32) for g in range(ng)]
    u = [jnp.dot(bf(t[g]), bf(from_s[g][:C2] + from_v[g][:C2]), preferred_element_type=F32) for g in range(ng)]
    for g, (n, d, p) in enumerate(groups):
        y_s = from_s[g][C2:] + from_v[g][C2:] + jnp.dot(a_rb[g], bf(u[g]), preferred_element_type=F32)
        y_refs[d][n, :, p * 128:(p + 1) * 128] = y_s[:C] + y_s[C:]
    for g, (n, d, p) in enumerate(groups):
        uv = jnp.concatenate([bf(u[g]), vs[g]], axis=0)
        s_ref[n, d, p] = s_old[g] * decay_end[g] + lax.dot_general(
            uv, ends[g], (((0,), (0,)), ((), ())), preferred_element_type=F32)


def _rwkv_scan(planes, s0):
    B, T, _ = planes.shape
    nc = T // RWKV_CHUNK
    W = RWKV_WIDTH
    nb = 2 if B % 2 == 0 else 1
    st = pl.BlockSpec((nb, 2, W // 128, 128, 128), lambda b, i: (b, 0, 0, 0, 0))
    y = jax.ShapeDtypeStruct((B, T, W), F32)
    return pl.pallas_call(
        _rwkv_scan_kernel,
        grid=(B // nb, nc),
        in_specs=[pl.BlockSpec((nb, RWKV_CHUNK, N_PLANES * W), lambda b, i: (b, i, 0)),
                  pl.BlockSpec((nb, RWKV_CHUNK, N_PLANES * W), lambda b, i: (b, nc - 1 - i, 0)),
                  st],
        out_specs=[pl.BlockSpec((nb, RWKV_CHUNK, W), lambda b, i: (b, i, 0)),
                   pl.BlockSpec((nb, RWKV_CHUNK, W), lambda b, i: (b, nc - 1 - i, 0)),
                   st],
        out_shape=[y, y, jax.ShapeDtypeStruct(s0.shape, F32)],
        compiler_params=_cparams(("parallel", "arbitrary")),
        name="rwkv_scan",
    )(planes, planes, s0)


def _layer_weights(l, w_in, shift_mu, w_up, a_up, w_branch, w_out):
    rw, rg, na, ng, df, dg, mg = jnp.split(w_in[l], [int(i) for i in np.cumsum(IN_SIZES)[:-1]], axis=-1)
    pad = jnp.zeros((w_in.shape[1], RWKV_PAD_WIDTH - RWKV_SHIFT_WIDTH), w_in.dtype)
    w_perm = jnp.concatenate([mg, rg, ng, dg, rw, pad, na, df], axis=-1).astype(BF16)
    mu_pad = jnp.pad(shift_mu[l], ((0, 0), (0, RWKV_PAD_WIDTH - RWKV_SHIFT_WIDTH)))
    zl = jnp.zeros((LORA, RWKV_WIDTH), F32)
    wup2 = jnp.stack([jnp.concatenate([w_up[l, 0], zl]), jnp.concatenate([zl, w_up[l, 1]])]).astype(BF16)
    aup2 = jnp.stack([jnp.concatenate([a_up[l, 0], zl]), jnp.concatenate([zl, a_up[l, 1]])]).astype(BF16)
    return w_perm, mu_pad, wup2, aup2, w_branch[l].astype(BF16), w_out[l].astype(BF16)


def kernel(x, c, ctx, c_ctx, w_mod, b_mod, g_pre, g_post, w_in, shift_mu, k_k, k_a, r_k, w0, w_up, a0, a_up,
           ln_x_g, ln_x_b, rpb, lam_q, lam_k, diff_subln, w_branch, w_out):
    B, S, D = x.shape
    C = ctx.shape[1]
    depth = w_in.shape[0]
    tables = _rope_tables(S)
    rows_pad = 16
    cvec = jnp.zeros((rows_pad, D), F32).at[:B].set(c).at[B].set(c_ctx)
    mod = _modulation(cvec, w_mod, b_mod)
    hc = ctx
    tm_x = min(2048, S)
    for l in range(depth):
        last = l == depth - 1
        lambda_init = 0.8 - 0.6 * math.exp(-0.3 * l)
        w_perm, mu_pad, wup2, aup2, wb, wo = _layer_weights(l, w_in, shift_mu, w_up, a_up, w_branch, w_out)
        mod_x = mod[l, :B].reshape(B, 3, D)
        mod_c = mod[l, B:B + 1].reshape(1, 3, D)
        zg_x, rw_x, na_x, df_x = _inproj(x, mod_x, g_pre[l], w_perm, tables, tm_x)
        ctx_out = _inproj(hc.reshape(1, B * C, D), mod_c, g_pre[l], w_perm, None, B * C)
        zg_c, rw_c, na_c, df_c = (t.reshape(B, C, t.shape[-1]) for t in ctx_out)

        r_k_flat = r_k[l].reshape(RWKV_WIDTH)
        pl_c, bv_c = _rwkv_prep(rw_c, mu_pad, k_k[l], k_a[l], r_k_flat, w0[l], a0[l], wup2, aup2, min(256, C))
        pl_x, bv_x = _rwkv_prep(rw_x, mu_pad, k_k[l], k_a[l], r_k_flat, w0[l], a0[l], wup2, aup2, 512)
        s_zero = jnp.zeros((B, 2, RWKV_WIDTH // 128, 128, 128), F32)
        yf_c, yb_c, s_ctx = _rwkv_scan(pl_c, s_zero)
        yf_x, yb_x, _ = _rwkv_scan(pl_x, s_ctx)

        o_na_x = _na_attn(na_x, na_c, _na_bias_tables(rpb[l]) * LOG2E, min(16, S // GRID_W))
        o_df_x = _diff_attn(df_x, df_c, True, lam_q[l], lam_k[l], diff_subln[l], lambda_init, min(1024, S), min(512, S // 2))

        x = _merge(x, mod_x, g_post[l], yf_x, yb_x, bv_x, ln_x_g[l], ln_x_b[l], o_na_x, o_df_x, zg_x, wb, wo, 512)
        if not last:
            o_na_c = _ctx_attn(na_c)
            o_df_c = _diff_attn(df_c, df_c, False, lam_q[l], lam_k[l], diff_subln[l], lambda_init, C, C)
            hc = _merge(hc, mod_c, g_post[l], yf_c, yb_c, bv_c, ln_x_g[l], ln_x_b[l], o_na_c, o_df_c, zg_c, wb, wo,
                        min(256, C))
    return x
```

```python
import functools
import math

import jax
import jax.numpy as jnp
import numpy as np
from jax import lax
from jax.experimental import pallas as pl
from jax.experimental.pallas import tpu as pltpu

F32 = jnp.float32
BF16 = jnp.bfloat16

DEPTH = 4
GRID_W = 64
RWKV_HEADS = 8
HEAD_DIM = 64
RWKV_WIDTH = 512
LORA = 64
GN_EPS = 64e-5
NA_HEADS = 8
NA_WIN_R = 8
NA_WIN_C = 16
DIFF_HEADS = 4
DIFF_QK_DIM = 64
DIFF_V_DIM = 128
ROPE_THETA = 10000.0
SUBLN_EPS = 1e-5
RMS_EPS = 1e-6
NEG_INF = -1e30
LOG2E = math.log2(math.e)
BRANCH_WIDTH = 512
N_BRANCH = 3

RWKV_SHIFT_WIDTH = 3 * RWKV_WIDTH + 4 * LORA
RWKV_PAD_WIDTH = 2048
TILE_W = 512
TILE_RW, TILE_NA, TILE_DF = 9, 13, 16
Z_WIDTH = 19 * TILE_W
IN_SIZES = (RWKV_SHIFT_WIDTH, 512, 1536, 512, 1536, 512, 3072)

VMEM_LIMIT = 56 * 1024 * 1024


def _cparams(sem):
    return pltpu.CompilerParams(dimension_semantics=sem, vmem_limit_bytes=VMEM_LIMIT)


def _sigmoid(x):
    return 1.0 / (1.0 + jnp.exp(-x))


def _silu(x):
    return x * _sigmoid(x)


def _bdot(a, b):
    return jnp.dot(a.astype(BF16), b.astype(BF16), preferred_element_type=F32)


def _mod_kernel(c_ref, w_ref, b_ref, o_ref):
    o_ref[0] = _bdot(_silu(c_ref[...]), w_ref[0]) + b_ref[0]


def _modulation(cvec, w_mod, b_mod):
    L, D, N = w_mod.shape
    R = cvec.shape[0]
    tn = 1024
    return pl.pallas_call(
        _mod_kernel,
        grid=(L, N // tn),
        in_specs=[pl.BlockSpec((R, D), lambda l, n: (0, 0)),
                  pl.BlockSpec((1, D, tn), lambda l, n: (l, 0, n)),
                  pl.BlockSpec((1, 1, tn), lambda l, n: (l, 0, n))],
        out_specs=pl.BlockSpec((1, R, tn), lambda l, n: (l, 0, n)),
        out_shape=jax.ShapeDtypeStruct((L, R, N), F32),
        compiler_params=_cparams(("parallel", "parallel")),
        name="modulation",
    )(cvec, w_mod, b_mod.reshape(L, 1, N))


def _inproj_kernel(*refs, rope, q_scale):
    if rope:
        x_ref, mod_ref, g_ref, w_ref, cos_ref, sa_ref, sb_ref, zg_ref, rw_ref, na_ref, df_ref, h_ref = refs
    else:
        x_ref, mod_ref, g_ref, w_ref, zg_ref, rw_ref, na_ref, df_ref, h_ref = refs
    n = pl.program_id(2)

    @pl.when(n == 0)
    def _():
        x = x_ref[0]
        y = x * lax.rsqrt(jnp.mean(x * x, axis=-1, keepdims=True) + RMS_EPS) * g_ref[...]
        h_ref[...] = (y * (1.0 + mod_ref[0, 1:2, :]) + mod_ref[0, 0:1, :]).astype(BF16)

    mm = lambda: jnp.dot(h_ref[...], w_ref[...], preferred_element_type=F32)

    def rot(t):
        if not rope:
            return t
        w = t.shape[-1]
        tile = lambda r: jnp.concatenate([r[...]] * (w // r.shape[-1]), axis=1)
        return t * tile(cos_ref) + pltpu.roll(t, w - 16, 1) * tile(sa_ref) + pltpu.roll(t, 16, 1) * tile(sb_ref)

    @pl.when(n < TILE_RW)
    def _():
        zg_ref[0] = mm().astype(BF16)

    @pl.when((n >= TILE_RW) & (n < TILE_NA))
    def _():
        rw_ref[0] = mm()

    @pl.when(n == TILE_NA)
    def _():
        na_ref[0] = (mm() * q_scale).astype(BF16)

    @pl.when((n > TILE_NA) & (n < TILE_DF))
    def _():
        na_ref[0] = mm().astype(BF16)

    @pl.when(n == TILE_DF)
    def _():
        df_ref[0] = (rot(mm()) * q_scale).astype(BF16)

    @pl.when(n == TILE_DF + 1)
    def _():
        df_ref[0] = rot(mm()).astype(BF16)

    @pl.when(n == TILE_DF + 2)
    def _():
        df_ref[0] = mm().astype(BF16)


def _inproj(x, mod, g_pre, w_perm, tables, tm):
    B, T, D = x.shape
    tn = TILE_W
    per_batch = mod.shape[0] > 1
    in_specs = [pl.BlockSpec((1, tm, D), lambda b, m, n: (b, m, 0)),
                pl.BlockSpec((1, 3, D), (lambda b, m, n: (b, 0, 0)) if per_batch else (lambda b, m, n: (0, 0, 0))),
                pl.BlockSpec((1, D), lambda b, m, n: (0, 0)),
                pl.BlockSpec((D, tn), lambda b, m, n: (0, n))]
    args = [x, mod, g_pre.reshape(1, D), w_perm]
    if tables is not None:
        in_specs += [pl.BlockSpec((tm, 128), lambda b, m, n: (m, 0))] * 3
        args += list(tables)
    seg = lambda first, count: pl.BlockSpec((1, tm, tn), lambda b, m, n: (b, m, jnp.clip(n - first, 0, count - 1)))
    out = lambda count, dtype: jax.ShapeDtypeStruct((B, T, count * tn), dtype)
    return pl.pallas_call(
        functools.partial(_inproj_kernel, rope=tables is not None, q_scale=HEAD_DIM ** -0.5 * LOG2E),
        grid=(B, T // tm, Z_WIDTH // tn),
        in_specs=in_specs,
        out_specs=[seg(0, TILE_RW), seg(TILE_RW, TILE_NA - TILE_RW), seg(TILE_NA, TILE_DF - TILE_NA),
                   seg(TILE_DF, Z_WIDTH // tn - TILE_DF)],
        out_shape=[out(TILE_RW, BF16), out(TILE_NA - TILE_RW, F32), out(TILE_DF - TILE_NA, BF16),
                   out(Z_WIDTH // tn - TILE_DF, BF16)],
        scratch_shapes=[pltpu.VMEM((tm, D), BF16)],
        compiler_params=_cparams(("parallel", "parallel", "arbitrary")),
        name="inproj",
    )(*args)


def _merge_kernel(x_ref, mod_ref, gpost_ref, yf_ref, yb_ref, bv_ref, lng_ref, lnb_ref, ona_ref, odf_ref,
                  rg_ref, ng_ref, dg_ref, mg_ref, wb_ref, wo_ref, o_ref):
    D = x_ref.shape[-1]
    ones = _head_ones(RWKV_WIDTH)
    y = yf_ref[0] + yb_ref[0]
    mu = _dot_hilo(y, ones) * (1.0 / HEAD_DIM)
    yc = y - mu
    var = _dot_hilo(yc * yc, ones) * (1.0 / HEAD_DIM)
    o_rw = yc * lax.rsqrt(var + GN_EPS) * lng_ref[...] + lnb_ref[...] + bv_ref[0]
    acc = None
    for n, (ob, gate_ref) in enumerate(((o_rw, rg_ref), (ona_ref[0], ng_ref), (odf_ref[0], dg_ref))):
        yb = _bdot(ob * _silu(gate_ref[0].astype(F32)), wb_ref[n])
        term = _sigmoid(mg_ref[0, :, n * D:(n + 1) * D].astype(F32)) * yb
        acc = term if acc is None else acc + term
    y = _bdot(acc, wo_ref[...])
    yn = y * lax.rsqrt(jnp.mean(y * y, axis=-1, keepdims=True) + RMS_EPS) * gpost_ref[...]
    o_ref[0] = x_ref[0] + mod_ref[0, 2:3, :] * yn


def _merge(x, mod, g_post, y_f, y_b, bonus, ln_g, ln_b, o_na, o_df, zg, w_branch, w_out, tm):
    B, T, D = x.shape
    per_batch = mod.shape[0] > 1
    bw = BRANCH_WIDTH
    row = lambda b, m: (b, m, 0)
    col = lambda c: (lambda b, m: (b, m, c))
    return pl.pallas_call(
        _merge_kernel,
        grid=(B, T // tm),
        in_specs=[pl.BlockSpec((1, tm, D), row),
                  pl.BlockSpec((1, 3, D), (lambda b, m: (b, 0, 0)) if per_batch else (lambda b, m: (0, 0, 0))),
                  pl.BlockSpec((1, D), lambda b, m: (0, 0)),
                  pl.BlockSpec((1, tm, bw), row), pl.BlockSpec((1, tm, bw), row), pl.BlockSpec((1, tm, bw), row),
                  pl.BlockSpec((1, bw), lambda b, m: (0, 0)), pl.BlockSpec((1, bw), lambda b, m: (0, 0)),
                  pl.BlockSpec((1, tm, bw), row), pl.BlockSpec((1, tm, bw), row),
                  pl.BlockSpec((1, tm, bw), col(N_BRANCH * D // bw)),
                  pl.BlockSpec((1, tm, bw), col(N_BRANCH * D // bw + 1)),
                  pl.BlockSpec((1, tm, bw), col(N_BRANCH * D // bw + 2)),
                  pl.BlockSpec((1, tm, N_BRANCH * D), col(0)),
                  pl.BlockSpec((N_BRANCH, bw, D), lambda b, m: (0, 0, 0)),
                  pl.BlockSpec((D, D), lambda b, m: (0, 0))],
        out_specs=pl.BlockSpec((1, tm, D), row),
        out_shape=jax.ShapeDtypeStruct((B, T, D), F32),
        compiler_params=_cparams(("parallel", "parallel")),
        name="merge",
    )(x, mod, g_post.reshape(1, D), y_f, y_b, bonus, ln_g.reshape(1, bw), ln_b.reshape(1, bw), o_na, o_df,
      zg, zg, zg, zg, w_branch, w_out)


def _rope_tables(n_tokens):
    t = np.arange(n_tokens)
    axis_dim = DIFF_QK_DIM // 2
    inv = ROPE_THETA ** (-np.arange(0, axis_dim, 2, dtype=np.float32) / axis_dim)
    ar = (t // GRID_W).astype(np.float32)[:, None] * inv
    ac = (t % GRID_W).astype(np.float32)[:, None] * inv
    ang = jnp.asarray(np.concatenate([ar, ar, ac, ac], axis=-1).astype(np.float32))
    cos, sin = jnp.cos(ang), jnp.sin(ang)
    first = (np.arange(DIFF_QK_DIM) % 32) < 16
    sin_a = jnp.where(first, -sin, 0.0)
    sin_b = jnp.where(first, 0.0, sin)
    tile = lambda a: jnp.tile(a, (1, 2))
    return tile(cos), tile(sin_a), tile(sin_b)


def _dot_nt(a, b):
    return lax.dot_general(a, b, (((1,), (1,)), ((), ())), preferred_element_type=F32)


SOFTMAX_ROWS = 128


def _diff_attn_kernel(q_ref, kc_ref, vc_ref, k_ref, v_ref, lq_ref, lk_ref, g_ref, o_ref,
                      sc, pc, s0, s1, p0, p1, al0, al1, m_scr, acc_scr, *, n_main, tk, lambda_init):
    tq = q_ref.shape[1]
    t_first = kc_ref.shape[1]
    q = q_ref[0]
    lo = lax.broadcasted_iota(jnp.int32, (1, 128), 1) < DIFF_QK_DIM
    zero = jnp.zeros_like(q)
    qq = jnp.concatenate([jnp.where(lo, q, zero), jnp.where(lo, zero, q)], axis=0)

    n_sub = 2 * tq // SOFTMAX_ROWS
    sub = lambda i: slice(i * SOFTMAX_ROWS, (i + 1) * SOFTMAX_ROWS)

    def scores(k_chunk, s_scr):
        s_scr[:, :k_chunk.shape[0]] = _dot_nt(qq, k_chunk)

    def softmax(size, first, s_scr, p_scr, al_scr):
        w = min(128, size)
        cols = [slice(c * w, (c + 1) * w) for c in range(size // w)]
        for i in range(n_sub):
            mx = s_scr[sub(i), cols[0]]
            for c in cols[1:]:
                mx = jnp.maximum(mx, s_scr[sub(i), c])
            mn = jnp.broadcast_to(jnp.max(mx, axis=-1, keepdims=True), (SOFTMAX_ROWS, 128))
            if not first:
                mo = m_scr[sub(i), :]
                mn = jnp.maximum(mo, mn)
                al_scr[sub(i), :] = jnp.exp2(mo - mn)
            m_scr[sub(i), :] = mn
        for i in range(n_sub):
            mn = m_scr[sub(i), :w]
            for c in cols:
                p_scr[sub(i), c] = jnp.exp2((s_scr[sub(i), c] - mn).astype(BF16))

    def accumulate(v, first, p_scr, al_scr):
        size = v.shape[0]
        v_ext = jnp.concatenate([v, jnp.ones((size, 128), BF16)], axis=1)
        pv = jnp.dot(p_scr[:, :size], v_ext, preferred_element_type=F32)
        if first:
            acc_scr[...] = pv
        else:
            al = al_scr[...]
            acc_scr[...] = acc_scr[...] * jnp.concatenate([al, al], axis=1) + pv

    scores(kc_ref[0], sc)
    if n_main == 0:
        softmax(t_first, True, sc, pc, None)
        accumulate(vc_ref[0], True, pc, None)
    else:
        def chunk_of(ref, j):
            start = j * tk if isinstance(j, int) else pl.multiple_of(j * tk, tk)
            return ref[0, pl.ds(start, tk), :]

        k_at = functools.partial(chunk_of, k_ref)
        v_at = functools.partial(chunk_of, v_ref)
        scores(k_at(0), s0)
        softmax(t_first, True, sc, pc, None)

        def pair(j, first):
            scores(k_at(j + 1), s1)
            softmax(tk, False, s0, p0, al0)
            if first:
                accumulate(vc_ref[0], True, pc, None)
            else:
                accumulate(v_at(j - 1), False, p1, al1)
            nxt = min(j + 2, n_main - 1) if isinstance(j, int) else jnp.minimum(j + 2, n_main - 1)
            scores(k_at(nxt), s0)
            softmax(tk, False, s1, p1, al1)
            accumulate(v_at(j), False, p0, al0)

        pair(0, True)

        def body(t, _):
            pair(2 * t, False)
            return 0

        lax.fori_loop(1, n_main // 2, body, 0)
        accumulate(v_at(n_main - 1), False, p1, al1)
    lqk = lq_ref[...] * lk_ref[...]
    e = jnp.exp(jnp.sum(lqk, axis=-1, keepdims=True))
    lam = e[0:1] - e[1:2] + lambda_init
    o_maps = acc_scr[:, :DIFF_V_DIM] / acc_scr[:, DIFF_V_DIM:]
    o = o_maps[:tq] - lam * o_maps[tq:]
    o = o * lax.rsqrt(jnp.mean(o * o, axis=-1, keepdims=True) + SUBLN_EPS) * g_ref[...]
    o_ref[0] = o * (1.0 - lambda_init)


def _diff_attn(qkv, qkv_c, with_latent_keys, lam_q, lam_k, subln_g, lambda_init, tq, tk):
    B, T, _ = qkv.shape
    C = qkv_c.shape[1]
    n_main = T // tk if with_latent_keys else 0
    assert n_main % 2 == 0 and (2 * tq) % SOFTMAX_ROWS == 0
    wmax = tk if n_main else 128
    H = DIFF_HEADS
    kv = lambda t, j: pl.BlockSpec((1, t, 128), lambda b, h, m: (b, 0, j * H + h))
    small = lambda r, c: pl.BlockSpec((r, c), lambda b, h, m: (0, 0))
    stat = pltpu.VMEM((2 * tq, 128), F32)
    s_buf = pltpu.VMEM((2 * tq, wmax), F32)
    p_buf = pltpu.VMEM((2 * tq, wmax), BF16)
    return pl.pallas_call(
        functools.partial(_diff_attn_kernel, n_main=n_main, tk=tk, lambda_init=lambda_init),
        grid=(B, H, T // tq),
        in_specs=[pl.BlockSpec((1, tq, 128), lambda b, h, m: (b, m, h)), kv(C, 1), kv(C, 2), kv(T, 1), kv(T, 2),
                  small(2, DIFF_QK_DIM), small(2, DIFF_QK_DIM), small(1, DIFF_V_DIM)],
        out_specs=pl.BlockSpec((1, tq, 128), lambda b, h, m: (b, m, h)),
        out_shape=jax.ShapeDtypeStruct((B, T, H * DIFF_V_DIM), F32),
        scratch_shapes=[pltpu.VMEM((2 * tq, C), F32), pltpu.VMEM((2 * tq, C), BF16), s_buf, s_buf, p_buf, p_buf,
                        stat, stat, stat, pltpu.VMEM((2 * tq, 2 * DIFF_V_DIM), F32)],
        compiler_params=_cparams(("parallel", "parallel", "parallel")),
        name="diff_attn",
    )(qkv, qkv_c, qkv_c, qkv, qkv, lam_q, lam_k, subln_g.reshape(1, DIFF_V_DIM))


def _na_bias_tables(rpb):
    c_idx = np.arange(GRID_W)
    c_start = np.clip(c_idx - NA_WIN_C // 2, 0, GRID_W - NA_WIN_C)
    col_ok = (c_idx[None, :] >= c_start[:, None]) & (c_idx[None, :] < c_start[:, None] + NA_WIN_C)
    dc = np.clip(c_idx[None, :] - c_idx[:, None], -(NA_WIN_C - 1), NA_WIN_C - 1) + (NA_WIN_C - 1)
    H = rpb.shape[0]
    e = jnp.take(rpb, jnp.asarray(dc.reshape(-1)), axis=2).reshape(H, 2 * NA_WIN_R - 1, GRID_W, GRID_W)
    e = jnp.where(col_ok, e, NEG_INF)
    tabs = [e[:, NA_WIN_R - 1 - off:2 * NA_WIN_R - 1 - off].transpose(0, 2, 1, 3).reshape(H, GRID_W, NA_WIN_R * GRID_W)
            for off in range(NA_WIN_R)]
    return jnp.stack(tabs, axis=1)


def _na_kernel(q_ref, k_ref, v_ref, kc_ref, vc_ref, bias_ref, o_ref, *, rq, rows):
    i = pl.program_id(2)
    lo = lax.broadcasted_iota(jnp.int32, (1, 128), 1) < HEAD_DIM
    win = NA_WIN_R * GRID_W
    G2 = 2 * GRID_W
    q = q_ref[0]
    zero = jnp.zeros_like(q)
    q_lo, q_hi = jnp.where(lo, q, zero), jnp.where(lo, zero, q)
    qq = jnp.concatenate([x[rr * GRID_W:(rr + 1) * GRID_W] for rr in range(rq) for x in (q_lo, q_hi)], axis=0)
    kc = kc_ref[0]
    ones = lambda n: jnp.ones((n, 128), BF16)
    s_c = _dot_nt(qq, kc)

    starts, s_nb = [], []
    for rr in range(rq):
        r = i * rq + rr
        r_start = jnp.clip(r - NA_WIN_R // 2, 0, rows - NA_WIN_R)
        off = r - r_start
        start = pl.multiple_of(r_start * GRID_W, GRID_W)
        starts.append(start)
        bias = jnp.concatenate([bias_ref[0, off], bias_ref[1, off]], axis=0)
        s_nb.append(_dot_nt(qq[rr * G2:(rr + 1) * G2], k_ref[0, pl.ds(start, win), :]) + bias)

    def lane_blocks(t):
        w = min(128, t.shape[-1])
        return [t[:, c * w:(c + 1) * w] for c in range(t.shape[-1] // w)]

    p_nb, p_c = [], []
    for rr in range(rq):
        sc = s_c[rr * G2:(rr + 1) * G2]
        blocks = lane_blocks(s_nb[rr])
        mx = blocks[0]
        for b in blocks[1:]:
            mx = jnp.maximum(mx, b)
        m = jnp.maximum(jnp.max(mx, axis=-1, keepdims=True), jnp.max(sc, axis=-1, keepdims=True))
        p_nb.append(jnp.exp2(s_nb[rr] - m).astype(BF16))
        p_c.append(jnp.exp2(sc - m).astype(BF16))

    vc_ext = jnp.concatenate([vc_ref[0], ones(kc.shape[0])], axis=1)
    o_c = jnp.dot(jnp.concatenate(p_c, axis=0), vc_ext, preferred_element_type=F32)
    for rr in range(rq):
        vw_ext = jnp.concatenate([v_ref[0, pl.ds(starts[rr], win), :], ones(win)], axis=1)
        o = jnp.dot(p_nb[rr], vw_ext, preferred_element_type=F32) + o_c[rr * G2:(rr + 1) * G2]
        o = o[:, :128] / o[:, 128:]
        o_ref[0, rr * GRID_W:(rr + 1) * GRID_W, :] = jnp.where(lo, o[:GRID_W], o[GRID_W:])


def _na_attn(qkv, qkv_c, bias, rq):
    B, S, _ = qkv.shape
    C = qkv_c.shape[1]
    rows = S // GRID_W
    npair = NA_HEADS // 2
    full = lambda t, j: pl.BlockSpec((1, t, 128), lambda b, p, m: (b, 0, j * npair + p))
    return pl.pallas_call(
        functools.partial(_na_kernel, rq=rq, rows=rows),
        grid=(B, npair, rows // rq),
        in_specs=[pl.BlockSpec((1, rq * GRID_W, 128), lambda b, p, m: (b, m, p)),
                  full(S, 1), full(S, 2), full(C, 1), full(C, 2),
                  pl.BlockSpec((2, NA_WIN_R, GRID_W, NA_WIN_R * GRID_W), lambda b, p, m: (p, 0, 0, 0))],
        out_specs=pl.BlockSpec((1, rq * GRID_W, 128), lambda b, p, m: (b, m, p)),
        out_shape=jax.ShapeDtypeStruct((B, S, NA_HEADS * HEAD_DIM), F32),
        compiler_params=_cparams(("parallel", "parallel", "parallel")),
        name="na_attn",
    )(qkv, qkv, qkv, qkv_c, qkv_c, bias)


def _ctx_attn_kernel(q_ref, k_ref, v_ref, o_ref):
    lo = lax.broadcasted_iota(jnp.int32, (1, 128), 1) < HEAD_DIM
    q, k, v = q_ref[0], k_ref[0], v_ref[0]
    zero = jnp.zeros_like(q)
    outs = []
    for hl in range(2):
        qh = jnp.where(lo, q, zero) if hl == 0 else jnp.where(lo, zero, q)
        s = _dot_nt(qh, k)
        p = jnp.exp2(s - jnp.max(s, axis=-1, keepdims=True))
        o = jnp.dot(p.astype(BF16), v, preferred_element_type=F32)
        outs.append(o / jnp.sum(p, axis=-1, keepdims=True))
    o_ref[0] = jnp.where(lo, outs[0], outs[1])


def _ctx_attn(qkv):
    B, C, _ = qkv.shape
    npair = NA_HEADS // 2
    blk = lambda j: pl.BlockSpec((1, C, 128), lambda b, p: (b, 0, j * npair + p))
    return pl.pallas_call(
        _ctx_attn_kernel,
        grid=(B, npair),
        in_specs=[blk(0), blk(1), blk(2)],
        out_specs=blk(0),
        out_shape=jax.ShapeDtypeStruct((B, C, NA_HEADS * HEAD_DIM), F32),
        compiler_params=_cparams(("parallel", "parallel")),
        name="ctx_attn",
    )(qkv, qkv, qkv)


PL_R, PL_V, PL_KK = 0, 1, 2
PL_LOGW, PL_KDIR, PL_B = 3, 4, 5
N_PLANES = 9
RWKV_CHUNK = 64


def _split3(x):
    hi = x.astype(BF16)
    r1 = x - hi.astype(F32)
    mid = r1.astype(BF16)
    lo = (r1 - mid.astype(F32)).astype(BF16)
    return hi, mid, lo


def _dot_exact_rhs(m, x):
    hi, mid, lo = _split3(x)
    mb = m.astype(BF16)
    d = lambda t: jnp.dot(mb, t, preferred_element_type=F32)
    return d(hi) + d(mid) + d(lo)


def _dot_hilo(x, m):
    hi = x.astype(BF16)
    lo = (x - hi.astype(F32)).astype(BF16)
    mb = m.astype(BF16)
    d = lambda t: jnp.dot(t, mb, preferred_element_type=F32)
    return d(hi) + d(lo)


def _head_ones(n):
    r = lax.broadcasted_iota(jnp.int32, (n, n), 0) // HEAD_DIM
    c = lax.broadcasted_iota(jnp.int32, (n, n), 1) // HEAD_DIM
    return (r == c).astype(F32)


def _rwkv_prep_kernel(zc_ref, zp_ref, zn_ref, mu_ref, kk_ref, ka_ref, rk_ref, w0_ref, a0_ref, wup_ref, aup_ref,
                      o_ref, bv_ref):
    tm = zc_ref.shape[1]
    m = pl.program_id(1)
    u = zc_ref[0]
    prev_row = jnp.where(m > 0, zp_ref[0, 7:8, :], 0.0)
    next_row = jnp.where(m < pl.num_programs(1) - 1, zn_ref[0, 0:1, :], 0.0)
    rows = lax.broadcasted_iota(jnp.int32, (tm, 1), 0)
    u_prev = jnp.where(rows == 0, prev_row, pltpu.roll(u, 1, 0))
    u_next = jnp.where(rows == tm - 1, next_row, pltpu.roll(u, tm - 1, 0))
    u = u + mu_ref[0:1, :] * (u_prev - u) + mu_ref[1:2, :] * (u_next - u)

    W = RWKV_WIDTH
    r, k, v = u[:, 0:W], u[:, W:2 * W], u[:, 2 * W:3 * W]
    lw = jnp.tanh(u[:, 3 * W:3 * W + 2 * LORA]).astype(BF16)
    la = u[:, 3 * W + 2 * LORA:3 * W + 4 * LORA].astype(BF16)
    ones = _head_ones(W)
    kk = k * kk_ref[...]
    ss = _dot_hilo(kk * kk, ones)
    kk = kk / jnp.maximum(jnp.sqrt(ss), 1e-12)
    o_ref[0, :, PL_R * W:(PL_R + 1) * W] = r
    o_ref[0, :, PL_V * W:(PL_V + 1) * W] = v
    o_ref[0, :, PL_KK * W:(PL_KK + 1) * W] = kk
    k_both = None
    for d in range(2):
        xw = w0_ref[d:d + 1, :] + jnp.dot(lw, wup_ref[d], preferred_element_type=F32)
        logw = -math.exp(-0.5) * _sigmoid(xw)
        a = _sigmoid(a0_ref[d:d + 1, :] + jnp.dot(la, aup_ref[d], preferred_element_type=F32))
        kdir = k * (1.0 + (a - 1.0) * ka_ref[...])
        base = 3 * d
        o_ref[0, :, (PL_LOGW + base) * W:(PL_LOGW + base + 1) * W] = logw
        o_ref[0, :, (PL_KDIR + base) * W:(PL_KDIR + base + 1) * W] = kdir
        o_ref[0, :, (PL_B + base) * W:(PL_B + base + 1) * W] = kk * a
        k_both = kdir if k_both is None else k_both + kdir
    bv_ref[0] = _dot_hilo(r * k_both * rk_ref[...], ones) * v


def _rwkv_prep(z, mu_pad, k_k, k_a, r_k, w0, a0, wup2, aup2, tm):
    B, T, wz = z.shape
    nb = tm // 8
    W = RWKV_WIDTH
    const = lambda shape: pl.BlockSpec(shape, lambda b, m: (0,) * len(shape))
    return pl.pallas_call(
        _rwkv_prep_kernel,
        grid=(B, T // tm),
        in_specs=[pl.BlockSpec((1, tm, wz), lambda b, m: (b, m, 0)),
                  pl.BlockSpec((1, 8, wz), lambda b, m: (b, jnp.maximum(m * nb - 1, 0), 0)),
                  pl.BlockSpec((1, 8, wz), lambda b, m: (b, jnp.minimum((m + 1) * nb, T // 8 - 1), 0)),
                  const((2, wz)), const((1, W)), const((1, W)), const((1, W)), const((2, W)), const((2, W)),
                  const((2, 2 * LORA, W)), const((2, 2 * LORA, W))],
        out_specs=[pl.BlockSpec((1, tm, N_PLANES * W), lambda b, m: (b, m, 0)),
                   pl.BlockSpec((1, tm, W), lambda b, m: (b, m, 0))],
        out_shape=[jax.ShapeDtypeStruct((B, T, N_PLANES * W), F32), jax.ShapeDtypeStruct((B, T, W), F32)],
        compiler_params=_cparams(("parallel", "parallel")),
        name="rwkv_prep",
    )(z, z, z, mu_pad, k_k.reshape(1, W), k_a.reshape(1, W), r_k.reshape(1, W), w0, a0, wup2, aup2)


def _rwkv_scan_kernel(pf_ref, pb_ref, s0_ref, yf_ref, yb_ref, s_ref):
    C = RWKV_CHUNK
    W = RWKV_WIDTH
    C2 = 2 * C

    @pl.when(pl.program_id(1) == 0)
    def _():
        s_ref[...] = s0_ref[...]

    lo = lax.broadcasted_iota(jnp.int32, (1, 128), 1) < HEAD_DIM
    row = lax.broadcasted_iota(jnp.int32, (C2, C2), 0)
    col = lax.broadcasted_iota(jnp.int32, (C2, C2), 1)
    same = (row // C) == (col // C)
    eye = (row == col).astype(F32)
    tri_r = lax.broadcasted_iota(jnp.int32, (C, C), 0)
    tri_c = lax.broadcasted_iota(jnp.int32, (C, C), 1)
    levels = [((row >> k) == (col >> k)) & ((row >> (k - 1)) != (col >> (k - 1))) for k in range(1, 7)]
    p_refs, y_refs = (pf_ref, pb_ref), (yf_ref, yb_ref)
    nb = pf_ref.shape[0]
    groups = [(n, d, p) for n in range(nb) for d in range(2) for p in range(W // 128)]
    ng = len(groups)
    bf = lambda t: t.astype(BF16)

    def stack(x):
        return jnp.concatenate([jnp.where(lo, x, 0.0), jnp.where(lo, 0.0, x)], axis=0)

    c_dir = {}
    for d in range(2):
        cum = ((tri_c <= tri_r) if d == 0 else (tri_c >= tri_r)).astype(F32)
        for n in range(nb):
            logw_all = p_refs[d][n, :, (PL_LOGW + 3 * d) * W:(PL_LOGW + 3 * d + 1) * W]
            c_dir[n, d] = _dot_exact_rhs(cum, logw_all)

    ar, bk, ends, vs, decay_end = [], [], [], [], []
    for n, d, p in groups:
        get = lambda plane: p_refs[d][n, :, plane * W + p * 128:plane * W + (p + 1) * 128]
        r, v, kk = get(PL_R), get(PL_V), get(PL_KK)
        logw, kdir, bb = get(PL_LOGW + 3 * d), get(PL_KDIR + 3 * d), get(PL_B + 3 * d)
        c = c_dir[n, d][:, p * 128:(p + 1) * 128]
        last = C - 1 if d == 0 else 0
        c_last = c[last:last + 1, :]
        e_neg = jnp.exp(-c)
        e_end = jnp.exp(c_last - c)
        ar.append(bf(jnp.concatenate([stack(-kk * jnp.exp(c - logw)), stack(r * jnp.exp(c))], axis=0)))
        bk.append(bf(jnp.concatenate([stack(bb * e_neg), stack(kdir * e_neg)], axis=0)))
        ends.append(bf(jnp.concatenate([stack(bb * e_end), stack(kdir * e_end)], axis=0)))
        vs.append(bf(stack(v)))
        decay_end.append(jnp.exp(c_last))

    amat = [_dot_nt(ar[g], bk[g]) for g in range(ng)]
    a_ab, a_kr, a_rb = [], [], []
    for g, (n, d, p) in enumerate(groups):
        before = (col < row) if d == 0 else (col > row)
        strict = same & before
        incl = same & (before | (row == col))
        m = amat[g]
        a_ab.append(jnp.where(strict, m[:C2, :C2], 0.0))
        a_kr.append(bf(jnp.concatenate([jnp.where(strict, m[:C2, C2:], 0.0),
                                        jnp.where(incl, m[C2:, C2:], 0.0)], axis=0)))
        a_rb.append(bf(jnp.where(incl, m[C2:, :C2], 0.0)))

    t = [eye + jnp.where(levels[0], a, 0.0) for a in a_ab]
    for lvl in levels[1:]:
        ta = [jnp.dot(bf(t[g]), bf(jnp.where(lvl, a_ab[g], 0.0)), preferred_element_type=F32) for g in range(ng)]
        t = [t[g] + jnp.dot(bf(ta[g]), bf(t[g]), preferred_element_type=F32) for g in range(ng)]

    s_old = [s_ref[n, d, p] for n, d, p in groups]
    from_s = [_dot_nt(ar[g], bf(s_old[g])) for g in range(ng)]
    from_v = [jnp.dot(a_kr[g], vs[g], preferred_element_type=F32) for g in range(ng)]
    u = [jnp.dot(bf(t[g]), bf(from_s[g][:C2] + from_v[g][:C2]), preferred_element_type=F32) for g in range(ng)]
    for g, (n, d, p) in enumerate(groups):
        y_s = from_s[g][C2:] + from_v[g][C2:] + jnp.dot(a_rb[g], bf(u[g]), preferred_element_type=F32)
        y_refs[d][n, :, p * 128:(p + 1) * 128] = y_s[:C] + y_s[C:]
    for g, (n, d, p) in enumerate(groups):
        uv = jnp.concatenate([bf(u[g]), vs[g]], axis=0)
        s_ref[n, d, p] = s_old[g] * decay_end[g] + lax.dot_general(
            uv, ends[g], (((0,), (0,)), ((), ())), preferred_element_type=F32)


def _rwkv_scan(planes, s0):
    B, T, _ = planes.shape
    nc = T // RWKV_CHUNK
    W = RWKV_WIDTH
    nb = 2 if B % 2 == 0 else 1
    st = pl.BlockSpec((nb, 2, W // 128, 128, 128), lambda b, i: (b, 0, 0, 0, 0))
    y = jax.ShapeDtypeStruct((B, T, W), F32)
    return pl.pallas_call(
        _rwkv_scan_kernel,
        grid=(B // nb, nc),
        in_specs=[pl.BlockSpec((nb, RWKV_CHUNK, N_PLANES * W), lambda b, i: (b, i, 0)),
                  pl.BlockSpec((nb, RWKV_CHUNK, N_PLANES * W), lambda b, i: (b, nc - 1 - i, 0)),
                  st],
        out_specs=[pl.BlockSpec((nb, RWKV_CHUNK, W), lambda b, i: (b, i, 0)),
                   pl.BlockSpec((nb, RWKV_CHUNK, W), lambda b, i: (b, nc - 1 - i, 0)),
                   st],
        out_shape=[y, y, jax.ShapeDtypeStruct(s0.shape, F32)],
        compiler_params=_cparams(("parallel", "arbitrary")),
        name="rwkv_scan",
    )(planes, planes, s0)


def _layer_weights(l, w_in, shift_mu, w_up, a_up, w_branch, w_out):
    rw, rg, na, ng, df, dg, mg = jnp.split(w_in[l], [int(i) for i in np.cumsum(IN_SIZES)[:-1]], axis=-1)
    pad = jnp.zeros((w_in.shape[1], RWKV_PAD_WIDTH - RWKV_SHIFT_WIDTH), w_in.dtype)
    w_perm = jnp.concatenate([mg, rg, ng, dg, rw, pad, na, df], axis=-1).astype(BF16)
    mu_pad = jnp.pad(shift_mu[l], ((0, 0), (0, RWKV_PAD_WIDTH - RWKV_SHIFT_WIDTH)))
    zl = jnp.zeros((LORA, RWKV_WIDTH), F32)
    wup2 = jnp.stack([jnp.concatenate([w_up[l, 0], zl]), jnp.concatenate([zl, w_up[l, 1]])]).astype(BF16)
    aup2 = jnp.stack([jnp.concatenate([a_up[l, 0], zl]), jnp.concatenate([zl, a_up[l, 1]])]).astype(BF16)
    return w_perm, mu_pad, wup2, aup2, w_branch[l].astype(BF16), w_out[l].astype(BF16)


def kernel(x, c, ctx, c_ctx, w_mod, b_mod, g_pre, g_post, w_in, shift_mu, k_k, k_a, r_k, w0, w_up, a0, a_up,
           ln_x_g, ln_x_b, rpb, lam_q, lam_k, diff_subln, w_branch, w_out):
    B, S, D = x.shape
    C = ctx.shape[1]
    depth = w_in.shape[0]
    tables = _rope_tables(S)
    rows_pad = 16
    cvec = jnp.zeros((rows_pad, D), F32).at[:B].set(c).at[B].set(c_ctx)
    mod = _modulation(cvec, w_mod, b_mod)
    hc = ctx
    tm_x = min(2048, S)
    for l in range(depth):
        last = l == depth - 1
        lambda_init = 0.8 - 0.6 * math.exp(-0.3 * l)
        w_perm, mu_pad, wup2, aup2, wb, wo = _layer_weights(l, w_in, shift_mu, w_up, a_up, w_branch, w_out)
        mod_x = mod[l, :B].reshape(B, 3, D)
        mod_c = mod[l, B:B + 1].reshape(1, 3, D)
        zg_x, rw_x, na_x, df_x = _inproj(x, mod_x, g_pre[l], w_perm, tables, tm_x)
        ctx_out = _inproj(hc.reshape(1, B * C, D), mod_c, g_pre[l], w_perm, None, B * C)
        zg_c, rw_c, na_c, df_c = (t.reshape(B, C, t.shape[-1]) for t in ctx_out)

        r_k_flat = r_k[l].reshape(RWKV_WIDTH)
        pl_c, bv_c = _rwkv_prep(rw_c, mu_pad, k_k[l], k_a[l], r_k_flat, w0[l], a0[l], wup2, aup2, min(256, C))
        pl_x, bv_x = _rwkv_prep(rw_x, mu_pad, k_k[l], k_a[l], r_k_flat, w0[l], a0[l], wup2, aup2, 512)
        s_zero = jnp.zeros((B, 2, RWKV_WIDTH // 128, 128, 128), F32)
        yf_c, yb_c, s_ctx = _rwkv_scan(pl_c, s_zero)
        yf_x, yb_x, _ = _rwkv_scan(pl_x, s_ctx)

        o_na_x = _na_attn(na_x, na_c, _na_bias_tables(rpb[l]) * LOG2E, min(16, S // GRID_W))
        o_df_x = _diff_attn(df_x, df_c, True, lam_q[l], lam_k[l], diff_subln[l], lambda_init, min(1024, S), min(512, S // 2))

        x = _merge(x, mod_x, g_post[l], yf_x, yb_x, bv_x, ln_x_g[l], ln_x_b[l], o_na_x, o_df_x, zg_x, wb, wo, 512)
        if not last:
            o_na_c = _ctx_attn(na_c)
            o_df_c = _diff_attn(df_c, df_c, False, lam_q[l], lam_k[l], diff_subln[l], lambda_init, C, C)
            hc = _merge(hc, mod_c, g_post[l], yf_c, yb_c, bv_c, ln_x_g[l], ln_x_b[l], o_na_c, o_df_c, zg_c, wb, wo,
                        min(256, C))
    return x
```

```python
import functools
import math

import jax
import jax.numpy as jnp
import numpy as np
from jax import lax
from jax.experimental import pallas as pl
from jax.experimental.pallas import tpu as pltpu

F32 = jnp.float32
BF16 = jnp.bfloat16

DEPTH = 4
GRID_W = 64
RWKV_HEADS = 8
HEAD_DIM = 64
RWKV_WIDTH = 512
LORA = 64
GN_EPS = 64e-5
NA_HEADS = 8
NA_WIN_R = 8
NA_WIN_C = 16
DIFF_HEADS = 4
DIFF_QK_DIM = 64
DIFF_V_DIM = 128
ROPE_THETA = 10000.0
SUBLN_EPS = 1e-5
RMS_EPS = 1e-6
NEG_INF = -1e30
LOG2E = math.log2(math.e)
BRANCH_WIDTH = 512
N_BRANCH = 3

RWKV_SHIFT_WIDTH = 3 * RWKV_WIDTH + 4 * LORA
RWKV_PAD_WIDTH = 2048
TILE_W = 512
TILE_RW, TILE_NA, TILE_DF = 9, 13, 16
Z_WIDTH = 19 * TILE_W
IN_SIZES = (RWKV_SHIFT_WIDTH, 512, 1536, 512, 1536, 512, 3072)

VMEM_LIMIT = 56 * 1024 * 1024


def _cparams(sem):
    return pltpu.CompilerParams(dimension_semantics=sem, vmem_limit_bytes=VMEM_LIMIT)


def _sigmoid(x):
    return 1.0 / (1.0 + jnp.exp(-x))


def _silu(x):
    return x * _sigmoid(x)


def _bdot(a, b):
    return jnp.dot(a.astype(BF16), b.astype(BF16), preferred_element_type=F32)


def _mod_kernel(c_ref, w_ref, b_ref, o_ref):
    o_ref[0] = _bdot(_silu(c_ref[...]), w_ref[0]) + b_ref[0]


def _modulation(cvec, w_mod, b_mod):
    L, D, N = w_mod.shape
    R = cvec.shape[0]
    tn = 1024
    return pl.pallas_call(
        _mod_kernel,
        grid=(L, N // tn),
        in_specs=[pl.BlockSpec((R, D), lambda l, n: (0, 0)),
                  pl.BlockSpec((1, D, tn), lambda l, n: (l, 0, n)),
                  pl.BlockSpec((1, 1, tn), lambda l, n: (l, 0, n))],
        out_specs=pl.BlockSpec((1, R, tn), lambda l, n: (l, 0, n)),
        out_shape=jax.ShapeDtypeStruct((L, R, N), F32),
        compiler_params=_cparams(("parallel", "parallel")),
        name="modulation",
    )(cvec, w_mod, b_mod.reshape(L, 1, N))


def _inproj_kernel(*refs, rope, q_scale):
    if rope:
        x_ref, mod_ref, g_ref, w_ref, cos_ref, sa_ref, sb_ref, zg_ref, rw_ref, na_ref, df_ref, h_ref = refs
    else:
        x_ref, mod_ref, g_ref, w_ref, zg_ref, rw_ref, na_ref, df_ref, h_ref = refs
    n = pl.program_id(2)

    @pl.when(n == 0)
    def _():
        x = x_ref[0]
        y = x * lax.rsqrt(jnp.mean(x * x, axis=-1, keepdims=True) + RMS_EPS) * g_ref[...]
        h_ref[...] = (y * (1.0 + mod_ref[0, 1:2, :]) + mod_ref[0, 0:1, :]).astype(BF16)

    mm = lambda: jnp.dot(h_ref[...], w_ref[...], preferred_element_type=F32)

    def rot(t):
        if not rope:
            return t
        w = t.shape[-1]
        tile = lambda r: jnp.concatenate([r[...]] * (w // r.shape[-1]), axis=1)
        return t * tile(cos_ref) + pltpu.roll(t, w - 16, 1) * tile(sa_ref) + pltpu.roll(t, 16, 1) * tile(sb_ref)

    @pl.when(n < TILE_RW)
    def _():
        zg_ref[0] = mm().astype(BF16)

    @pl.when((n >= TILE_RW) & (n < TILE_NA))
    def _():
        rw_ref[0] = mm()

    @pl.when(n == TILE_NA)
    def _():
        na_ref[0] = (mm() * q_scale).astype(BF16)

    @pl.when((n > TILE_NA) & (n < TILE_DF))
    def _():
        na_ref[0] = mm().astype(BF16)

    @pl.when(n == TILE_DF)
    def _():
        df_ref[0] = (rot(mm()) * q_scale).astype(BF16)

    @pl.when(n == TILE_DF + 1)
    def _():
        df_ref[0] = rot(mm()).astype(BF16)

    @pl.when(n == TILE_DF + 2)
    def _():
        df_ref[0] = mm().astype(BF16)


def _inproj(x, mod, g_pre, w_perm, tables, tm):
    B, T, D = x.shape
    tn = TILE_W
    per_batch = mod.shape[0] > 1
    in_specs = [pl.BlockSpec((1, tm, D), lambda b, m, n: (b, m, 0)),
                pl.BlockSpec((1, 3, D), (lambda b, m, n: (b, 0, 0)) if per_batch else (lambda b, m, n: (0, 0, 0))),
                pl.BlockSpec((1, D), lambda b, m, n: (0, 0)),
                pl.BlockSpec((D, tn), lambda b, m, n: (0, n))]
    args = [x, mod, g_pre.reshape(1, D), w_perm]
    if tables is not None:
        in_specs += [pl.BlockSpec((tm, 128), lambda b, m, n: (m, 0))] * 3
        args += list(tables)
    seg = lambda first, count: pl.BlockSpec((1, tm, tn), lambda b, m, n: (b, m, jnp.clip(n - first, 0, count - 1)))
    out = lambda count, dtype: jax.ShapeDtypeStruct((B, T, count * tn), dtype)
    return pl.pallas_call(
        functools.partial(_inproj_kernel, rope=tables is not None, q_scale=HEAD_DIM ** -0.5 * LOG2E),
        grid=(B, T // tm, Z_WIDTH // tn),
        in_specs=in_specs,
        out_specs=[seg(0, TILE_RW), seg(TILE_RW, TILE_NA - TILE_RW), seg(TILE_NA, TILE_DF - TILE_NA),
                   seg(TILE_DF, Z_WIDTH // tn - TILE_DF)],
        out_shape=[out(TILE_RW, BF16), out(TILE_NA - TILE_RW, F32), out(TILE_DF - TILE_NA, BF16),
                   out(Z_WIDTH // tn - TILE_DF, BF16)],
        scratch_shapes=[pltpu.VMEM((tm, D), BF16)],
        compiler_params=_cparams(("parallel", "parallel", "arbitrary")),
        name="inproj",
    )(*args)


def _merge_kernel(x_ref, mod_ref, gpost_ref, yf_ref, yb_ref, bv_ref, lng_ref, lnb_ref, ona_ref, odf_ref,
                  rg_ref, ng_ref, dg_ref, mg_ref, wb_ref, wo_ref, o_ref):
    D = x_ref.shape[-1]
    ones = _head_ones(RWKV_WIDTH)
    y = yf_ref[0] + yb_ref[0]
    mu = _dot_hilo(y, ones) * (1.0 / HEAD_DIM)
    yc = y - mu
    var = _dot_hilo(yc * yc, ones) * (1.0 / HEAD_DIM)
    o_rw = yc * lax.rsqrt(var + GN_EPS) * lng_ref[...] + lnb_ref[...] + bv_ref[0]
    acc = None
    for n, (ob, gate_ref) in enumerate(((o_rw, rg_ref), (ona_ref[0], ng_ref), (odf_ref[0], dg_ref))):
        yb = _bdot(ob * _silu(gate_ref[0].astype(F32)), wb_ref[n])
        term = _sigmoid(mg_ref[0, :, n * D:(n + 1) * D].astype(F32)) * yb
        acc = term if acc is None else acc + term
    y = _bdot(acc, wo_ref[...])
    yn = y * lax.rsqrt(jnp.mean(y * y, axis=-1, keepdims=True) + RMS_EPS) * gpost_ref[...]
    o_ref[0] = x_ref[0] + mod_ref[0, 2:3, :] * yn


def _merge(x, mod, g_post, y_f, y_b, bonus, ln_g, ln_b, o_na, o_df, zg, w_branch, w_out, tm):
    B, T, D = x.shape
    per_batch = mod.shape[0] > 1
    bw = BRANCH_WIDTH
    row = lambda b, m: (b, m, 0)
    col = lambda c: (lambda b, m: (b, m, c))
    return pl.pallas_call(
        _merge_kernel,
        grid=(B, T // tm),
        in_specs=[pl.BlockSpec((1, tm, D), row),
                  pl.BlockSpec((1, 3, D), (lambda b, m: (b, 0, 0)) if per_batch else (lambda b, m: (0, 0, 0))),
                  pl.BlockSpec((1, D), lambda b, m: (0, 0)),
                  pl.BlockSpec((1, tm, bw), row), pl.BlockSpec((1, tm, bw), row), pl.BlockSpec((1, tm, bw), row),
                  pl.BlockSpec((1, bw), lambda b, m: (0, 0)), pl.BlockSpec((1, bw), lambda b, m: (0, 0)),
                  pl.BlockSpec((1, tm, bw), row), pl.BlockSpec((1, tm, bw), row),
                  pl.BlockSpec((1, tm, bw), col(N_BRANCH * D // bw)),
                  pl.BlockSpec((1, tm, bw), col(N_BRANCH * D // bw + 1)),
                  pl.BlockSpec((1, tm, bw), col(N_BRANCH * D // bw + 2)),
                  pl.BlockSpec((1, tm, N_BRANCH * D), col(0)),
                  pl.BlockSpec((N_BRANCH, bw, D), lambda b, m: (0, 0, 0)),
                  pl.BlockSpec((D, D), lambda b, m: (0, 0))],
        out_specs=pl.BlockSpec((1, tm, D), row),
        out_shape=jax.ShapeDtypeStruct((B, T, D), F32),
        compiler_params=_cparams(("parallel", "parallel")),
        name="merge",
    )(x, mod, g_post.reshape(1, D), y_f, y_b, bonus, ln_g.reshape(1, bw), ln_b.reshape(1, bw), o_na, o_df,
      zg, zg, zg, zg, w_branch, w_out)


def _rope_tables(n_tokens):
    t = np.arange(n_tokens)
    axis_dim = DIFF_QK_DIM // 2
    inv = ROPE_THETA ** (-np.arange(0, axis_dim, 2, dtype=np.float32) / axis_dim)
    ar = (t // GRID_W).astype(np.float32)[:, None] * inv
    ac = (t % GRID_W).astype(np.float32)[:, None] * inv
    ang = jnp.asarray(np.concatenate([ar, ar, ac, ac], axis=-1).astype(np.float32))
    cos, sin = jnp.cos(ang), jnp.sin(ang)
    first = (np.arange(DIFF_QK_DIM) % 32) < 16
    sin_a = jnp.where(first, -sin, 0.0)
    sin_b = jnp.where(first, 0.0, sin)
    tile = lambda a: jnp.tile(a, (1, 2))
    return tile(cos), tile(sin_a), tile(sin_b)


def _dot_nt(a, b):
    return lax.dot_general(a, b, (((1,), (1,)), ((), ())), preferred_element_type=F32)


SOFTMAX_ROWS = 128


def _diff_attn_kernel(q_ref, kc_ref, vc_ref, k_ref, v_ref, lq_ref, lk_ref, g_ref, o_ref,
                      sc, pc, s0, s1, p0, p1, al0, al1, m_scr, acc_scr, *, n_main, tk, lambda_init):
    tq = q_ref.shape[1]
    t_first = kc_ref.shape[1]
    q = q_ref[0]
    lo = lax.broadcasted_iota(jnp.int32, (1, 128), 1) < DIFF_QK_DIM
    zero = jnp.zeros_like(q)
    qq = jnp.concatenate([jnp.where(lo, q, zero), jnp.where(lo, zero, q)], axis=0)

    n_sub = 2 * tq // SOFTMAX_ROWS
    sub = lambda i: slice(i * SOFTMAX_ROWS, (i + 1) * SOFTMAX_ROWS)

    def scores(k_chunk, s_scr):
        s_scr[:, :k_chunk.shape[0]] = _dot_nt(qq, k_chunk)

    def softmax(size, first, s_scr, p_scr, al_scr):
        w = min(128, size)
        cols = [slice(c * w, (c + 1) * w) for c in range(size // w)]
        for i in range(n_sub):
            mx = s_scr[sub(i), cols[0]]
            for c in cols[1:]:
                mx = jnp.maximum(mx, s_scr[sub(i), c])
            mn = jnp.broadcast_to(jnp.max(mx, axis=-1, keepdims=True), (SOFTMAX_ROWS, 128))
            if not first:
                mo = m_scr[sub(i), :]
                mn = jnp.maximum(mo, mn)
                al_scr[sub(i), :] = jnp.exp2(mo - mn)
            m_scr[sub(i), :] = mn
        for i in range(n_sub):
            mn = m_scr[sub(i), :w]
            for c in cols:
                p_scr[sub(i), c] = jnp.exp2(s_scr[sub(i), c] - mn).astype(BF16)

    def accumulate(v, first, p_scr, al_scr):
        size = v.shape[0]
        v_ext = jnp.concatenate([v, jnp.ones((size, 128), BF16)], axis=1)
        pv = jnp.dot(p_scr[:, :size], v_ext, preferred_element_type=F32)
        if first:
            acc_scr[...] = pv
        else:
            al = al_scr[...]
            acc_scr[...] = acc_scr[...] * jnp.concatenate([al, al], axis=1) + pv

    scores(kc_ref[0], sc)
    if n_main == 0:
        softmax(t_first, True, sc, pc, None)
        accumulate(vc_ref[0], True, pc, None)
    else:
        def chunk_of(ref, j):
            start = j * tk if isinstance(j, int) else pl.multiple_of(j * tk, tk)
            return ref[0, pl.ds(start, tk), :]

        k_at = functools.partial(chunk_of, k_ref)
        v_at = functools.partial(chunk_of, v_ref)
        scores(k_at(0), s0)
        softmax(t_first, True, sc, pc, None)

        def pair(j, first):
            scores(k_at(j + 1), s1)
            softmax(tk, False, s0, p0, al0)
            if first:
                accumulate(vc_ref[0], True, pc, None)
            else:
                accumulate(v_at(j - 1), False, p1, al1)
            nxt = min(j + 2, n_main - 1) if isinstance(j, int) else jnp.minimum(j + 2, n_main - 1)
            scores(k_at(nxt), s0)
            softmax(tk, False, s1, p1, al1)
            accumulate(v_at(j), False, p0, al0)

        pair(0, True)

        def body(t, _):
            pair(2 * t, False)
            return 0

        lax.fori_loop(1, n_main // 2, body, 0)
        accumulate(v_at(n_main - 1), False, p1, al1)
    lqk = lq_ref[...] * lk_ref[...]
    e = jnp.exp(jnp.sum(lqk, axis=-1, keepdims=True))
    lam = e[0:1] - e[1:2] + lambda_init
    o_maps = acc_scr[:, :DIFF_V_DIM] / acc_scr[:, DIFF_V_DIM:]
    o = o_maps[:tq] - lam * o_maps[tq:]
    o = o * lax.rsqrt(jnp.mean(o * o, axis=-1, keepdims=True) + SUBLN_EPS) * g_ref[...]
    o_ref[0] = o * (1.0 - lambda_init)


def _diff_attn(qkv, qkv_c, with_latent_keys, lam_q, lam_k, subln_g, lambda_init, tq, tk):
    B, T, _ = qkv.shape
    C = qkv_c.shape[1]
    n_main = T // tk if with_latent_keys else 0
    assert n_main % 2 == 0 and (2 * tq) % SOFTMAX_ROWS == 0
    wmax = tk if n_main else 128
    H = DIFF_HEADS
    kv = lambda t, j: pl.BlockSpec((1, t, 128), lambda b, h, m: (b, 0, j * H + h))
    small = lambda r, c: pl.BlockSpec((r, c), lambda b, h, m: (0, 0))
    stat = pltpu.VMEM((2 * tq, 128), F32)
    s_buf = pltpu.VMEM((2 * tq, wmax), F32)
    p_buf = pltpu.VMEM((2 * tq, wmax), BF16)
    return pl.pallas_call(
        functools.partial(_diff_attn_kernel, n_main=n_main, tk=tk, lambda_init=lambda_init),
        grid=(B, H, T // tq),
        in_specs=[pl.BlockSpec((1, tq, 128), lambda b, h, m: (b, m, h)), kv(C, 1), kv(C, 2), kv(T, 1), kv(T, 2),
                  small(2, DIFF_QK_DIM), small(2, DIFF_QK_DIM), small(1, DIFF_V_DIM)],
        out_specs=pl.BlockSpec((1, tq, 128), lambda b, h, m: (b, m, h)),
        out_shape=jax.ShapeDtypeStruct((B, T, H * DIFF_V_DIM), F32),
        scratch_shapes=[pltpu.VMEM((2 * tq, C), F32), pltpu.VMEM((2 * tq, C), BF16), s_buf, s_buf, p_buf, p_buf,
                        stat, stat, stat, pltpu.VMEM((2 * tq, 2 * DIFF_V_DIM), F32)],
        compiler_params=_cparams(("parallel", "parallel", "parallel")),
        name="diff_attn",
    )(qkv, qkv_c, qkv_c, qkv, qkv, lam_q, lam_k, subln_g.reshape(1, DIFF_V_DIM))


def _na_bias_tables(rpb):
    c_idx = np.arange(GRID_W)
    c_start = np.clip(c_idx - NA_WIN_C // 2, 0, GRID_W - NA_WIN_C)
    col_ok = (c_idx[None, :] >= c_start[:, None]) & (c_idx[None, :] < c_start[:, None] + NA_WIN_C)
    dc = np.clip(c_idx[None, :] - c_idx[:, None], -(NA_WIN_C - 1), NA_WIN_C - 1) + (NA_WIN_C - 1)
    H = rpb.shape[0]
    e = jnp.take(rpb, jnp.asarray(dc.reshape(-1)), axis=2).reshape(H, 2 * NA_WIN_R - 1, GRID_W, GRID_W)
    e = jnp.where(col_ok, e, NEG_INF)
    tabs = [e[:, NA_WIN_R - 1 - off:2 * NA_WIN_R - 1 - off].transpose(0, 2, 1, 3).reshape(H, GRID_W, NA_WIN_R * GRID_W)
            for off in range(NA_WIN_R)]
    return jnp.stack(tabs, axis=1)


def _na_kernel(q_ref, k_ref, v_ref, kc_ref, vc_ref, bias_ref, o_ref, *, rq, rows):
    i = pl.program_id(2)
    lo = lax.broadcasted_iota(jnp.int32, (1, 128), 1) < HEAD_DIM
    win = NA_WIN_R * GRID_W
    G2 = 2 * GRID_W
    q = q_ref[0]
    zero = jnp.zeros_like(q)
    q_lo, q_hi = jnp.where(lo, q, zero), jnp.where(lo, zero, q)
    qq = jnp.concatenate([x[rr * GRID_W:(rr + 1) * GRID_W] for rr in range(rq) for x in (q_lo, q_hi)], axis=0)
    kc = kc_ref[0]
    ones = lambda n: jnp.ones((n, 128), BF16)
    s_c = _dot_nt(qq, kc)

    starts, s_nb = [], []
    for rr in range(rq):
        r = i * rq + rr
        r_start = jnp.clip(r - NA_WIN_R // 2, 0, rows - NA_WIN_R)
        off = r - r_start
        start = pl.multiple_of(r_start * GRID_W, GRID_W)
        starts.append(start)
        bias = jnp.concatenate([bias_ref[0, off], bias_ref[1, off]], axis=0)
        s_nb.append(_dot_nt(qq[rr * G2:(rr + 1) * G2], k_ref[0, pl.ds(start, win), :]) + bias)

    def lane_blocks(t):
        w = min(128, t.shape[-1])
        return [t[:, c * w:(c + 1) * w] for c in range(t.shape[-1] // w)]

    p_nb, p_c = [], []
    for rr in range(rq):
        sc = s_c[rr * G2:(rr + 1) * G2]
        blocks = lane_blocks(s_nb[rr])
        mx = blocks[0]
        for b in blocks[1:]:
            mx = jnp.maximum(mx, b)
        m = jnp.maximum(jnp.max(mx, axis=-1, keepdims=True), jnp.max(sc, axis=-1, keepdims=True))
        p_nb.append(jnp.exp2(s_nb[rr] - m).astype(BF16))
        p_c.append(jnp.exp2(sc - m).astype(BF16))

    vc_ext = jnp.concatenate([vc_ref[0], ones(kc.shape[0])], axis=1)
    o_c = jnp.dot(jnp.concatenate(p_c, axis=0), vc_ext, preferred_element_type=F32)
    for rr in range(rq):
        vw_ext = jnp.concatenate([v_ref[0, pl.ds(starts[rr], win), :], ones(win)], axis=1)
        o = jnp.dot(p_nb[rr], vw_ext, preferred_element_type=F32) + o_c[rr * G2:(rr + 1) * G2]
        o = o[:, :128] / o[:, 128:]
        o_ref[0, rr * GRID_W:(rr + 1) * GRID_W, :] = jnp.where(lo, o[:GRID_W], o[GRID_W:])


def _na_attn(qkv, qkv_c, bias, rq):
    B, S, _ = qkv.shape
    C = qkv_c.shape[1]
    rows = S // GRID_W
    npair = NA_HEADS // 2
    full = lambda t, j: pl.BlockSpec((1, t, 128), lambda b, p, m: (b, 0, j * npair + p))
    return pl.pallas_call(
        functools.partial(_na_kernel, rq=rq, rows=rows),
        grid=(B, npair, rows // rq),
        in_specs=[pl.BlockSpec((1, rq * GRID_W, 128), lambda b, p, m: (b, m, p)),
                  full(S, 1), full(S, 2), full(C, 1), full(C, 2),
                  pl.BlockSpec((2, NA_WIN_R, GRID_W, NA_WIN_R * GRID_W), lambda b, p, m: (p, 0, 0, 0))],
        out_specs=pl.BlockSpec((1, rq * GRID_W, 128), lambda b, p, m: (b, m, p)),
        out_shape=jax.ShapeDtypeStruct((B, S, NA_HEADS * HEAD_DIM), F32),
        compiler_params=_cparams(("parallel", "parallel", "parallel")),
        name="na_attn",
    )(qkv, qkv, qkv, qkv_c, qkv_c, bias)


def _ctx_attn_kernel(q_ref, k_ref, v_ref, o_ref):
    lo = lax.broadcasted_iota(jnp.int32, (1, 128), 1) < HEAD_DIM
    q, k, v = q_ref[0], k_ref[0], v_ref[0]
    zero = jnp.zeros_like(q)
    outs = []
    for hl in range(2):
        qh = jnp.where(lo, q, zero) if hl == 0 else jnp.where(lo, zero, q)
        s = _dot_nt(qh, k)
        p = jnp.exp2(s - jnp.max(s, axis=-1, keepdims=True))
        o = jnp.dot(p.astype(BF16), v, preferred_element_type=F32)
        outs.append(o / jnp.sum(p, axis=-1, keepdims=True))
    o_ref[0] = jnp.where(lo, outs[0], outs[1])


def _ctx_attn(qkv):
    B, C, _ = qkv.shape
    npair = NA_HEADS // 2
    blk = lambda j: pl.BlockSpec((1, C, 128), lambda b, p: (b, 0, j * npair + p))
    return pl.pallas_call(
        _ctx_attn_kernel,
        grid=(B, npair),
        in_specs=[blk(0), blk(1), blk(2)],
        out_specs=blk(0),
        out_shape=jax.ShapeDtypeStruct((B, C, NA_HEADS * HEAD_DIM), F32),
        compiler_params=_cparams(("parallel", "parallel")),
        name="ctx_attn",
    )(qkv, qkv, qkv)


PL_R, PL_V, PL_KK = 0, 1, 2
PL_LOGW, PL_KDIR, PL_B = 3, 4, 5
N_PLANES = 9
RWKV_CHUNK = 64


def _split3(x):
    hi = x.astype(BF16)
    r1 = x - hi.astype(F32)
    mid = r1.astype(BF16)
    lo = (r1 - mid.astype(F32)).astype(BF16)
    return hi, mid, lo


def _dot_exact_rhs(m, x):
    hi, mid, lo = _split3(x)
    mb = m.astype(BF16)
    d = lambda t: jnp.dot(mb, t, preferred_element_type=F32)
    return d(hi) + d(mid) + d(lo)


def _dot_hilo(x, m):
    hi = x.astype(BF16)
    lo = (x - hi.astype(F32)).astype(BF16)
    mb = m.astype(BF16)
    d = lambda t: jnp.dot(t, mb, preferred_element_type=F32)
    return d(hi) + d(lo)


def _head_ones(n):
    r = lax.broadcasted_iota(jnp.int32, (n, n), 0) // HEAD_DIM
    c = lax.broadcasted_iota(jnp.int32, (n, n), 1) // HEAD_DIM
    return (r == c).astype(F32)


def _rwkv_prep_kernel(zc_ref, zp_ref, zn_ref, mu_ref, kk_ref, ka_ref, rk_ref, w0_ref, a0_ref, wup_ref, aup_ref,
                      o_ref, bv_ref):
    tm = zc_ref.shape[1]
    m = pl.program_id(1)
    u = zc_ref[0]
    prev_row = jnp.where(m > 0, zp_ref[0, 7:8, :], 0.0)
    next_row = jnp.where(m < pl.num_programs(1) - 1, zn_ref[0, 0:1, :], 0.0)
    rows = lax.broadcasted_iota(jnp.int32, (tm, 1), 0)
    u_prev = jnp.where(rows == 0, prev_row, pltpu.roll(u, 1, 0))
    u_next = jnp.where(rows == tm - 1, next_row, pltpu.roll(u, tm - 1, 0))
    u = u + mu_ref[0:1, :] * (u_prev - u) + mu_ref[1:2, :] * (u_next - u)

    W = RWKV_WIDTH
    r, k, v = u[:, 0:W], u[:, W:2 * W], u[:, 2 * W:3 * W]
    lw = jnp.tanh(u[:, 3 * W:3 * W + 2 * LORA]).astype(BF16)
    la = u[:, 3 * W + 2 * LORA:3 * W + 4 * LORA].astype(BF16)
    ones = _head_ones(W)
    kk = k * kk_ref[...]
    ss = _dot_hilo(kk * kk, ones)
    kk = kk / jnp.maximum(jnp.sqrt(ss), 1e-12)
    o_ref[0, :, PL_R * W:(PL_R + 1) * W] = r
    o_ref[0, :, PL_V * W:(PL_V + 1) * W] = v
    o_ref[0, :, PL_KK * W:(PL_KK + 1) * W] = kk
    k_both = None
    for d in range(2):
        xw = w0_ref[d:d + 1, :] + jnp.dot(lw, wup_ref[d], preferred_element_type=F32)
        logw = -math.exp(-0.5) * _sigmoid(xw)
        a = _sigmoid(a0_ref[d:d + 1, :] + jnp.dot(la, aup_ref[d], preferred_element_type=F32))
        kdir = k * (1.0 + (a - 1.0) * ka_ref[...])
        base = 3 * d
        o_ref[0, :, (PL_LOGW + base) * W:(PL_LOGW + base + 1) * W] = logw
        o_ref[0, :, (PL_KDIR + base) * W:(PL_KDIR + base + 1) * W] = kdir
        o_ref[0, :, (PL_B + base) * W:(PL_B + base + 1) * W] = kk * a
        k_both = kdir if k_both is None else k_both + kdir
    bv_ref[0] = _dot_hilo(r * k_both * rk_ref[...], ones) * v


def _rwkv_prep(z, mu_pad, k_k, k_a, r_k, w0, a0, wup2, aup2, tm):
    B, T, wz = z.shape
    nb = tm // 8
    W = RWKV_WIDTH
    const = lambda shape: pl.BlockSpec(shape, lambda b, m: (0,) * len(shape))
    return pl.pallas_call(
        _rwkv_prep_kernel,
        grid=(B, T // tm),
        in_specs=[pl.BlockSpec((1, tm, wz), lambda b, m: (b, m, 0)),
                  pl.BlockSpec((1, 8, wz), lambda b, m: (b, jnp.maximum(m * nb - 1, 0), 0)),
                  pl.BlockSpec((1, 8, wz), lambda b, m: (b, jnp.minimum((m + 1) * nb, T // 8 - 1), 0)),
                  const((2, wz)), const((1, W)), const((1, W)), const((1, W)), const((2, W)), const((2, W)),
                  const((2, 2 * LORA, W)), const((2, 2 * LORA, W))],
        out_specs=[pl.BlockSpec((1, tm, N_PLANES * W), lambda b, m: (b, m, 0)),
                   pl.BlockSpec((1, tm, W), lambda b, m: (b, m, 0))],
        out_shape=[jax.ShapeDtypeStruct((B, T, N_PLANES * W), F32), jax.ShapeDtypeStruct((B, T, W), F32)],
        compiler_params=_cparams(("parallel", "parallel")),
        name="rwkv_prep",
    )(z, z, z, mu_pad, k_k.reshape(1, W), k_a.reshape(1, W), r_k.reshape(1, W), w0, a0, wup2, aup2)


def _rwkv_scan_kernel(pf_ref, pb_ref, s0_ref, yf_ref, yb_ref, s_ref):
    C = RWKV_CHUNK
    W = RWKV_WIDTH
    C2 = 2 * C

    @pl.when(pl.program_id(1) == 0)
    def _():
        s_ref[...] = s0_ref[...]

    lo = lax.broadcasted_iota(jnp.int32, (1, 128), 1) < HEAD_DIM
    row = lax.broadcasted_iota(jnp.int32, (C2, C2), 0)
    col = lax.broadcasted_iota(jnp.int32, (C2, C2), 1)
    same = (row // C) == (col // C)
    eye = (row == col).astype(F32)
    tri_r = lax.broadcasted_iota(jnp.int32, (C, C), 0)
    tri_c = lax.broadcasted_iota(jnp.int32, (C, C), 1)
    levels = [((row >> k) == (col >> k)) & ((row >> (k - 1)) != (col >> (k - 1))) for k in range(1, 7)]
    p_refs, y_refs = (pf_ref, pb_ref), (yf_ref, yb_ref)
    nb = pf_ref.shape[0]
    groups = [(n, d, p) for n in range(nb) for d in range(2) for p in range(W // 128)]
    ng = len(groups)
    bf = lambda t: t.astype(BF16)

    def stack(x):
        return jnp.concatenate([jnp.where(lo, x, 0.0), jnp.where(lo, 0.0, x)], axis=0)

    c_dir = {}
    for d in range(2):
        cum = ((tri_c <= tri_r) if d == 0 else (tri_c >= tri_r)).astype(F32)
        for n in range(nb):
            logw_all = p_refs[d][n, :, (PL_LOGW + 3 * d) * W:(PL_LOGW + 3 * d + 1) * W]
            c_dir[n, d] = _dot_exact_rhs(cum, logw_all)

    ar, bk, ends, vs, decay_end = [], [], [], [], []
    for n, d, p in groups:
        get = lambda plane: p_refs[d][n, :, plane * W + p * 128:plane * W + (p + 1) * 128]
        r, v, kk = get(PL_R), get(PL_V), get(PL_KK)
        logw, kdir, bb = get(PL_LOGW + 3 * d), get(PL_KDIR + 3 * d), get(PL_B + 3 * d)
        c = c_dir[n, d][:, p * 128:(p + 1) * 128]
        last = C - 1 if d == 0 else 0
        c_last = c[last:last + 1, :]
        e_neg = jnp.exp(-c)
        e_end = jnp.exp(c_last - c)
        ar.append(bf(jnp.concatenate([stack(-kk * jnp.exp(c - logw)), stack(r * jnp.exp(c))], axis=0)))
        bk.append(bf(jnp.concatenate([stack(bb * e_neg), stack(kdir * e_neg)], axis=0)))
        ends.append(bf(jnp.concatenate([stack(bb * e_end), stack(kdir * e_end)], axis=0)))
        vs.append(bf(stack(v)))
        decay_end.append(jnp.exp(c_last))

    amat = [_dot_nt(ar[g], bk[g]) for g in range(ng)]
    a_ab, a_kr, a_rb = [], [], []
    for g, (n, d, p) in enumerate(groups):
        before = (col < row) if d == 0 else (col > row)
        strict = same & before
        incl = same & (before | (row == col))
        m = amat[g]
        a_ab.append(jnp.where(strict, m[:C2, :C2], 0.0))
        a_kr.append(bf(jnp.concatenate([jnp.where(strict, m[:C2, C2:], 0.0),
                                        jnp.where(incl, m[C2:, C2:], 0.0)], axis=0)))
        a_rb.append(bf(jnp.where(incl, m[C2:, :C2], 0.0)))

    t = [eye + jnp.where(levels[0], a, 0.0) for a in a_ab]
    for lvl in levels[1:]:
        ta = [jnp.dot(bf(t[g]), bf(jnp.where(lvl, a_ab[g], 0.0)), preferred_element_type=F32) for g in range(ng)]
        t = [t[g] + jnp.dot(bf(ta[g]), bf(t[g]), preferred_element_type=F32) for g in range(ng)]

    s_old = [s_ref[n, d, p] for n, d, p in groups]
    from_s = [_dot_nt(ar[g], bf(s_old[g])) for g in range(ng)]
    from_v = [jnp.dot(a_kr[g], vs[g], preferred_element_type=F32) for g in range(ng)]
    u = [jnp.dot(bf(t[g]), bf(from_s[g][:C2] + from_v[g][:C2]), preferred_element_type=F32) for g in range(ng)]
    for g, (n, d, p) in enumerate(groups):
        y_s = from_s[g][C2:] + from_v[g][C2:] + jnp.dot(a_rb[g], bf(u[g]), preferred_element_type=F32)
        y_refs[d][n, :, p * 128:(p + 1) * 128] = y_s[:C] + y_s[C:]
    for g, (n, d, p) in enumerate(groups):
        uv = jnp.concatenate([bf(u[g]), vs[g]], axis=0)
        s_ref[n, d, p] = s_old[g] * decay_end[g] + lax.dot_general(
            uv, ends[g], (((0,), (0,)), ((), ())), preferred_element_type=F32)


def _rwkv_scan(planes, s0):
    B, T, _ = planes.shape
    nc = T // RWKV_CHUNK
    W = RWKV_WIDTH
    nb = 4 if B % 4 == 0 else 1
    st = pl.BlockSpec((nb, 2, W // 128, 128, 128), lambda b, i: (b, 0, 0, 0, 0))
    y = jax.ShapeDtypeStruct((B, T, W), F32)
    return pl.pallas_call(
        _rwkv_scan_kernel,
        grid=(B // nb, nc),
        in_specs=[pl.BlockSpec((nb, RWKV_CHUNK, N_PLANES * W), lambda b, i: (b, i, 0)),
                  pl.BlockSpec((nb, RWKV_CHUNK, N_PLANES * W), lambda b, i: (b, nc - 1 - i, 0)),
                  st],
        out_specs=[pl.BlockSpec((nb, RWKV_CHUNK, W), lambda b, i: (b, i, 0)),
                   pl.BlockSpec((nb, RWKV_CHUNK, W), lambda b, i: (b, nc - 1 - i, 0)),
                   st],
        out_shape=[y, y, jax.ShapeDtypeStruct(s0.shape, F32)],
        compiler_params=_cparams(("parallel", "arbitrary")),
        name="rwkv_scan",
    )(planes, planes, s0)


def _layer_weights(l, w_in, shift_mu, w_up, a_up, w_branch, w_out):
    rw, rg, na, ng, df, dg, mg = jnp.split(w_in[l], [int(i) for i in np.cumsum(IN_SIZES)[:-1]], axis=-1)
    pad = jnp.zeros((w_in.shape[1], RWKV_PAD_WIDTH - RWKV_SHIFT_WIDTH), w_in.dtype)
    w_perm = jnp.concatenate([mg, rg, ng, dg, rw, pad, na, df], axis=-1).astype(BF16)
    mu_pad = jnp.pad(shift_mu[l], ((0, 0), (0, RWKV_PAD_WIDTH - RWKV_SHIFT_WIDTH)))
    zl = jnp.zeros((LORA, RWKV_WIDTH), F32)
    wup2 = jnp.stack([jnp.concatenate([w_up[l, 0], zl]), jnp.concatenate([zl, w_up[l, 1]])]).astype(BF16)
    aup2 = jnp.stack([jnp.concatenate([a_up[l, 0], zl]), jnp.concatenate([zl, a_up[l, 1]])]).astype(BF16)
    return w_perm, mu_pad, wup2, aup2, w_branch[l].astype(BF16), w_out[l].astype(BF16)


def kernel(x, c, ctx, c_ctx, w_mod, b_mod, g_pre, g_post, w_in, shift_mu, k_k, k_a, r_k, w0, w_up, a0, a_up,
           ln_x_g, ln_x_b, rpb, lam_q, lam_k, diff_subln, w_branch, w_out):
    B, S, D = x.shape
    C = ctx.shape[1]
    depth = w_in.shape[0]
    tables = _rope_tables(S)
    rows_pad = 16
    cvec = jnp.zeros((rows_pad, D), F32).at[:B].set(c).at[B].set(c_ctx)
    mod = _modulation(cvec, w_mod, b_mod)
    hc = ctx
    tm_x = min(2048, S)
    for l in range(depth):
        last = l == depth - 1
        lambda_init = 0.8 - 0.6 * math.exp(-0.3 * l)
        w_perm, mu_pad, wup2, aup2, wb, wo = _layer_weights(l, w_in, shift_mu, w_up, a_up, w_branch, w_out)
        mod_x = mod[l, :B].reshape(B, 3, D)
        mod_c = mod[l, B:B + 1].reshape(1, 3, D)
        zg_x, rw_x, na_x, df_x = _inproj(x, mod_x, g_pre[l], w_perm, tables, tm_x)
        ctx_out = _inproj(hc.reshape(1, B * C, D), mod_c, g_pre[l], w_perm, None, B * C)
        zg_c, rw_c, na_c, df_c = (t.reshape(B, C, t.shape[-1]) for t in ctx_out)

        r_k_flat = r_k[l].reshape(RWKV_WIDTH)
        pl_c, bv_c = _rwkv_prep(rw_c, mu_pad, k_k[l], k_a[l], r_k_flat, w0[l], a0[l], wup2, aup2, min(256, C))
        pl_x, bv_x = _rwkv_prep(rw_x, mu_pad, k_k[l], k_a[l], r_k_flat, w0[l], a0[l], wup2, aup2, 512)
        s_zero = jnp.zeros((B, 2, RWKV_WIDTH // 128, 128, 128), F32)
        yf_c, yb_c, s_ctx = _rwkv_scan(pl_c, s_zero)
        yf_x, yb_x, _ = _rwkv_scan(pl_x, s_ctx)

        o_na_x = _na_attn(na_x, na_c, _na_bias_tables(rpb[l]) * LOG2E, min(32, S // GRID_W))
        o_df_x = _diff_attn(df_x, df_c, True, lam_q[l], lam_k[l], diff_subln[l], lambda_init, min(1024, S), min(512, S // 2))

        x = _merge(x, mod_x, g_post[l], yf_x, yb_x, bv_x, ln_x_g[l], ln_x_b[l], o_na_x, o_df_x, zg_x, wb, wo, 512)
        if not last:
            o_na_c = _ctx_attn(na_c)
            o_df_c = _diff_attn(df_c, df_c, False, lam_q[l], lam_k[l], diff_subln[l], lambda_init, C, C)
            hc = _merge(hc, mod_c, g_post[l], yf_c, yb_c, bv_c, ln_x_g[l], ln_x_b[l], o_na_c, o_df_c, zg_c, wb, wo,
                        min(256, C))
    return x
```

```python
import functools
import math

import jax
import jax.numpy as jnp
import numpy as np
from jax import lax
from jax.experimental import pallas as pl
from jax.experimental.pallas import tpu as pltpu

F32 = jnp.float32
BF16 = jnp.bfloat16

DEPTH = 4
GRID_W = 64
RWKV_HEADS = 8
HEAD_DIM = 64
RWKV_WIDTH = 512
LORA = 64
GN_EPS = 64e-5
NA_HEADS = 8
NA_WIN_R = 8
NA_WIN_C = 16
DIFF_HEADS = 4
DIFF_QK_DIM = 64
DIFF_V_DIM = 128
ROPE_THETA = 10000.0
SUBLN_EPS = 1e-5
RMS_EPS = 1e-6
NEG_INF = -1e30
LOG2E = math.log2(math.e)
BRANCH_WIDTH = 512
N_BRANCH = 3

RWKV_SHIFT_WIDTH = 3 * RWKV_WIDTH + 4 * LORA
RWKV_PAD_WIDTH = 2048
TILE_W = 512
TILE_RW, TILE_NA, TILE_DF = 9, 13, 16
Z_WIDTH = 19 * TILE_W
IN_SIZES = (RWKV_SHIFT_WIDTH, 512, 1536, 512, 1536, 512, 3072)

VMEM_LIMIT = 56 * 1024 * 1024


def _cparams(sem):
    return pltpu.CompilerParams(dimension_semantics=sem, vmem_limit_bytes=VMEM_LIMIT)


def _sigmoid(x):
    return 1.0 / (1.0 + jnp.exp(-x))


def _silu(x):
    return x * _sigmoid(x)


def _bdot(a, b):
    return jnp.dot(a.astype(BF16), b.astype(BF16), preferred_element_type=F32)


def _mod_kernel(c_ref, w_ref, b_ref, o_ref):
    o_ref[0] = _bdot(_silu(c_ref[...]), w_ref[0]) + b_ref[0]


def _modulation(cvec, w_mod, b_mod):
    L, D, N = w_mod.shape
    R = cvec.shape[0]
    tn = 1024
    return pl.pallas_call(
        _mod_kernel,
        grid=(L, N // tn),
        in_specs=[pl.BlockSpec((R, D), lambda l, n: (0, 0)),
                  pl.BlockSpec((1, D, tn), lambda l, n: (l, 0, n)),
                  pl.BlockSpec((1, 1, tn), lambda l, n: (l, 0, n))],
        out_specs=pl.BlockSpec((1, R, tn), lambda l, n: (l, 0, n)),
        out_shape=jax.ShapeDtypeStruct((L, R, N), F32),
        compiler_params=_cparams(("parallel", "parallel")),
        name="modulation",
    )(cvec, w_mod, b_mod.reshape(L, 1, N))


def _inproj_kernel(*refs, rope, q_scale):
    if rope:
        x_ref, mod_ref, g_ref, w_ref, cos_ref, sa_ref, sb_ref, zg_ref, rw_ref, na_ref, df_ref, h_ref = refs
    else:
        x_ref, mod_ref, g_ref, w_ref, zg_ref, rw_ref, na_ref, df_ref, h_ref = refs
    n = pl.program_id(2)

    @pl.when(n == 0)
    def _():
        x = x_ref[0]
        y = x * lax.rsqrt(jnp.mean(x * x, axis=-1, keepdims=True) + RMS_EPS) * g_ref[...]
        h_ref[...] = (y * (1.0 + mod_ref[0, 1:2, :]) + mod_ref[0, 0:1, :]).astype(BF16)

    mm = lambda: jnp.dot(h_ref[...], w_ref[...], preferred_element_type=F32)

    def rot(t):
        if not rope:
            return t
        w = t.shape[-1]
        tile = lambda r: jnp.concatenate([r[...]] * (w // r.shape[-1]), axis=1)
        return t * tile(cos_ref) + pltpu.roll(t, w - 16, 1) * tile(sa_ref) + pltpu.roll(t, 16, 1) * tile(sb_ref)

    @pl.when(n < TILE_RW)
    def _():
        zg_ref[0] = mm().astype(BF16)

    @pl.when((n >= TILE_RW) & (n < TILE_NA))
    def _():
        rw_ref[0] = mm()

    @pl.when(n == TILE_NA)
    def _():
        na_ref[0] = (mm() * q_scale).astype(BF16)

    @pl.when((n > TILE_NA) & (n < TILE_DF))
    def _():
        na_ref[0] = mm().astype(BF16)

    @pl.when(n == TILE_DF)
    def _():
        df_ref[0] = (rot(mm()) * q_scale).astype(BF16)

    @pl.when(n == TILE_DF + 1)
    def _():
        df_ref[0] = rot(mm()).astype(BF16)

    @pl.when(n == TILE_DF + 2)
    def _():
        df_ref[0] = mm().astype(BF16)


def _inproj(x, mod, g_pre, w_perm, tables, tm):
    B, T, D = x.shape
    tn = TILE_W
    per_batch = mod.shape[0] > 1
    in_specs = [pl.BlockSpec((1, tm, D), lambda b, m, n: (b, m, 0)),
                pl.BlockSpec((1, 3, D), (lambda b, m, n: (b, 0, 0)) if per_batch else (lambda b, m, n: (0, 0, 0))),
                pl.BlockSpec((1, D), lambda b, m, n: (0, 0)),
                pl.BlockSpec((D, tn), lambda b, m, n: (0, n))]
    args = [x, mod, g_pre.reshape(1, D), w_perm]
    if tables is not None:
        in_specs += [pl.BlockSpec((tm, 128), lambda b, m, n: (m, 0))] * 3
        args += list(tables)
    seg = lambda first, count: pl.BlockSpec((1, tm, tn), lambda b, m, n: (b, m, jnp.clip(n - first, 0, count - 1)))
    out = lambda count, dtype: jax.ShapeDtypeStruct((B, T, count * tn), dtype)
    return pl.pallas_call(
        functools.partial(_inproj_kernel, rope=tables is not None, q_scale=HEAD_DIM ** -0.5 * LOG2E),
        grid=(B, T // tm, Z_WIDTH // tn),
        in_specs=in_specs,
        out_specs=[seg(0, TILE_RW), seg(TILE_RW, TILE_NA - TILE_RW), seg(TILE_NA, TILE_DF - TILE_NA),
                   seg(TILE_DF, Z_WIDTH // tn - TILE_DF)],
        out_shape=[out(TILE_RW, BF16), out(TILE_NA - TILE_RW, F32), out(TILE_DF - TILE_NA, BF16),
                   out(Z_WIDTH // tn - TILE_DF, BF16)],
        scratch_shapes=[pltpu.VMEM((tm, D), BF16)],
        compiler_params=_cparams(("parallel", "parallel", "arbitrary")),
        name="inproj",
    )(*args)


def _merge_kernel(x_ref, mod_ref, gpost_ref, yf_ref, yb_ref, bv_ref, lng_ref, lnb_ref, ona_ref, odf_ref,
                  rg_ref, ng_ref, dg_ref, mg_ref, wb_ref, wo_ref, o_ref):
    D = x_ref.shape[-1]
    ones = _head_ones(RWKV_WIDTH)
    y = yf_ref[0] + yb_ref[0]
    mu = _dot_hilo(y, ones) * (1.0 / HEAD_DIM)
    yc = y - mu
    var = _dot_hilo(yc * yc, ones) * (1.0 / HEAD_DIM)
    o_rw = yc * lax.rsqrt(var + GN_EPS) * lng_ref[...] + lnb_ref[...] + bv_ref[0]
    acc = None
    for n, (ob, gate_ref) in enumerate(((o_rw, rg_ref), (ona_ref[0], ng_ref), (odf_ref[0], dg_ref))):
        yb = _bdot(ob * _silu(gate_ref[0].astype(F32)), wb_ref[n])
        term = _sigmoid(mg_ref[0, :, n * D:(n + 1) * D].astype(F32)) * yb
        acc = term if acc is None else acc + term
    y = _bdot(acc, wo_ref[...])
    yn = y * lax.rsqrt(jnp.mean(y * y, axis=-1, keepdims=True) + RMS_EPS) * gpost_ref[...]
    o_ref[0] = x_ref[0] + mod_ref[0, 2:3, :] * yn


def _merge(x, mod, g_post, y_f, y_b, bonus, ln_g, ln_b, o_na, o_df, zg, w_branch, w_out, tm):
    B, T, D = x.shape
    per_batch = mod.shape[0] > 1
    bw = BRANCH_WIDTH
    row = lambda b, m: (b, m, 0)
    col = lambda c: (lambda b, m: (b, m, c))
    return pl.pallas_call(
        _merge_kernel,
        grid=(B, T // tm),
        in_specs=[pl.BlockSpec((1, tm, D), row),
                  pl.BlockSpec((1, 3, D), (lambda b, m: (b, 0, 0)) if per_batch else (lambda b, m: (0, 0, 0))),
                  pl.BlockSpec((1, D), lambda b, m: (0, 0)),
                  pl.BlockSpec((1, tm, bw), row), pl.BlockSpec((1, tm, bw), row), pl.BlockSpec((1, tm, bw), row),
                  pl.BlockSpec((1, bw), lambda b, m: (0, 0)), pl.BlockSpec((1, bw), lambda b, m: (0, 0)),
                  pl.BlockSpec((1, tm, bw), row), pl.BlockSpec((1, tm, bw), row),
                  pl.BlockSpec((1, tm, bw), col(N_BRANCH * D // bw)),
                  pl.BlockSpec((1, tm, bw), col(N_BRANCH * D // bw + 1)),
                  pl.BlockSpec((1, tm, bw), col(N_BRANCH * D // bw + 2)),
                  pl.BlockSpec((1, tm, N_BRANCH * D), col(0)),
                  pl.BlockSpec((N_BRANCH, bw, D), lambda b, m: (0, 0, 0)),
                  pl.BlockSpec((D, D), lambda b, m: (0, 0))],
        out_specs=pl.BlockSpec((1, tm, D), row),
        out_shape=jax.ShapeDtypeStruct((B, T, D), F32),
        compiler_params=_cparams(("parallel", "parallel")),
        name="merge",
    )(x, mod, g_post.reshape(1, D), y_f, y_b, bonus, ln_g.reshape(1, bw), ln_b.reshape(1, bw), o_na, o_df,
      zg, zg, zg, zg, w_branch, w_out)


def _rope_tables(n_tokens):
    t = np.arange(n_tokens)
    axis_dim = DIFF_QK_DIM // 2
    inv = ROPE_THETA ** (-np.arange(0, axis_dim, 2, dtype=np.float32) / axis_dim)
    ar = (t // GRID_W).astype(np.float32)[:, None] * inv
    ac = (t % GRID_W).astype(np.float32)[:, None] * inv
    ang = jnp.asarray(np.concatenate([ar, ar, ac, ac], axis=-1).astype(np.float32))
    cos, sin = jnp.cos(ang), jnp.sin(ang)
    first = (np.arange(DIFF_QK_DIM) % 32) < 16
    sin_a = jnp.where(first, -sin, 0.0)
    sin_b = jnp.where(first, 0.0, sin)
    tile = lambda a: jnp.tile(a, (1, 2))
    return tile(cos), tile(sin_a), tile(sin_b)


def _dot_nt(a, b):
    return lax.dot_general(a, b, (((1,), (1,)), ((), ())), preferred_element_type=F32)


SOFTMAX_ROWS = 128


def _diff_attn_kernel(q_ref, kc_ref, vc_ref, k_ref, v_ref, lq_ref, lk_ref, g_ref, o_ref,
                      sc, pc, s0, s1, p0, p1, al0, al1, m_scr, acc_scr, *, n_main, tk, lambda_init):
    tq = q_ref.shape[1]
    t_first = kc_ref.shape[1]
    q = q_ref[0]
    lo = lax.broadcasted_iota(jnp.int32, (1, 128), 1) < DIFF_QK_DIM
    zero = jnp.zeros_like(q)
    qq = jnp.concatenate([jnp.where(lo, q, zero), jnp.where(lo, zero, q)], axis=0)

    n_sub = 2 * tq // SOFTMAX_ROWS
    sub = lambda i: slice(i * SOFTMAX_ROWS, (i + 1) * SOFTMAX_ROWS)

    def scores(k_chunk, s_scr):
        s_scr[:, :k_chunk.shape[0]] = _dot_nt(qq, k_chunk)

    def softmax(size, first, s_scr, p_scr, al_scr):
        w = min(128, size)
        cols = [slice(c * w, (c + 1) * w) for c in range(size // w)]
        for i in range(n_sub):
            mx = s_scr[sub(i), cols[0]]
            for c in cols[1:]:
                mx = jnp.maximum(mx, s_scr[sub(i), c])
            mn = jnp.broadcast_to(jnp.max(mx, axis=-1, keepdims=True), (SOFTMAX_ROWS, 128))
            if not first:
                mo = m_scr[sub(i), :]
                mn = jnp.maximum(mo, mn)
                al_scr[sub(i), :] = jnp.exp2(mo - mn)
            m_scr[sub(i), :] = mn
        for i in range(n_sub):
            mn = m_scr[sub(i), :w]
            for c in cols:
                p_scr[sub(i), c] = jnp.exp2(s_scr[sub(i), c] - mn).astype(BF16)

    def accumulate(v, first, p_scr, al_scr):
        size = v.shape[0]
        v_ext = jnp.concatenate([v, jnp.ones((size, 128), BF16)], axis=1)
        pv = jnp.dot(p_scr[:, :size], v_ext, preferred_element_type=F32)
        if first:
            acc_scr[...] = pv
        else:
            al = al_scr[...]
            acc_scr[...] = acc_scr[...] * jnp.concatenate([al, al], axis=1) + pv

    scores(kc_ref[0], sc)
    if n_main == 0:
        softmax(t_first, True, sc, pc, None)
        accumulate(vc_ref[0], True, pc, None)
    else:
        def chunk_of(ref, j):
            start = j * tk if isinstance(j, int) else pl.multiple_of(j * tk, tk)
            return ref[0, pl.ds(start, tk), :]

        k_at = functools.partial(chunk_of, k_ref)
        v_at = functools.partial(chunk_of, v_ref)
        scores(k_at(0), s0)
        softmax(t_first, True, sc, pc, None)

        def pair(j, first):
            scores(k_at(j + 1), s1)
            softmax(tk, False, s0, p0, al0)
            if first:
                accumulate(vc_ref[0], True, pc, None)
            else:
                accumulate(v_at(j - 1), False, p1, al1)
            nxt = min(j + 2, n_main - 1) if isinstance(j, int) else jnp.minimum(j + 2, n_main - 1)
            scores(k_at(nxt), s0)
            softmax(tk, False, s1, p1, al1)
            accumulate(v_at(j), False, p0, al0)

        pair(0, True)

        def body(t, _):
            pair(2 * t, False)
            return 0

        lax.fori_loop(1, n_main // 2, body, 0)
        accumulate(v_at(n_main - 1), False, p1, al1)
    lqk = lq_ref[...] * lk_ref[...]
    e = jnp.exp(jnp.sum(lqk, axis=-1, keepdims=True))
    lam = e[0:1] - e[1:2] + lambda_init
    o_maps = acc_scr[:, :DIFF_V_DIM] / acc_scr[:, DIFF_V_DIM:]
    o = o_maps[:tq] - lam * o_maps[tq:]
    o = o * lax.rsqrt(jnp.mean(o * o, axis=-1, keepdims=True) + SUBLN_EPS) * g_ref[...]
    o_ref[0] = o * (1.0 - lambda_init)


def _diff_attn(qkv, qkv_c, with_latent_keys, lam_q, lam_k, subln_g, lambda_init, tq, tk):
    B, T, _ = qkv.shape
    C = qkv_c.shape[1]
    n_main = T // tk if with_latent_keys else 0
    assert n_main % 2 == 0 and (2 * tq) % SOFTMAX_ROWS == 0
    wmax = tk if n_main else 128
    H = DIFF_HEADS
    kv = lambda t, j: pl.BlockSpec((1, t, 128), lambda b, h, m: (b, 0, j * H + h))
    small = lambda r, c: pl.BlockSpec((r, c), lambda b, h, m: (0, 0))
    stat = pltpu.VMEM((2 * tq, 128), F32)
    s_buf = pltpu.VMEM((2 * tq, wmax), F32)
    p_buf = pltpu.VMEM((2 * tq, wmax), BF16)
    return pl.pallas_call(
        functools.partial(_diff_attn_kernel, n_main=n_main, tk=tk, lambda_init=lambda_init),
        grid=(B, H, T // tq),
        in_specs=[pl.BlockSpec((1, tq, 128), lambda b, h, m: (b, m, h)), kv(C, 1), kv(C, 2), kv(T, 1), kv(T, 2),
                  small(2, DIFF_QK_DIM), small(2, DIFF_QK_DIM), small(1, DIFF_V_DIM)],
        out_specs=pl.BlockSpec((1, tq, 128), lambda b, h, m: (b, m, h)),
        out_shape=jax.ShapeDtypeStruct((B, T, H * DIFF_V_DIM), F32),
        scratch_shapes=[pltpu.VMEM((2 * tq, C), F32), pltpu.VMEM((2 * tq, C), BF16), s_buf, s_buf, p_buf, p_buf,
                        stat, stat, stat, pltpu.VMEM((2 * tq, 2 * DIFF_V_DIM), F32)],
        compiler_params=_cparams(("parallel", "parallel", "parallel")),
        name="diff_attn",
    )(qkv, qkv_c, qkv_c, qkv, qkv, lam_q, lam_k, subln_g.reshape(1, DIFF_V_DIM))


def _na_bias_tables(rpb):
    c_idx = np.arange(GRID_W)
    c_start = np.clip(c_idx - NA_WIN_C // 2, 0, GRID_W - NA_WIN_C)
    col_ok = (c_idx[None, :] >= c_start[:, None]) & (c_idx[None, :] < c_start[:, None] + NA_WIN_C)
    dc = np.clip(c_idx[None, :] - c_idx[:, None], -(NA_WIN_C - 1), NA_WIN_C - 1) + (NA_WIN_C - 1)
    H = rpb.shape[0]
    e = jnp.take(rpb, jnp.asarray(dc.reshape(-1)), axis=2).reshape(H, 2 * NA_WIN_R - 1, GRID_W, GRID_W)
    e = jnp.where(col_ok, e, NEG_INF)
    tabs = [e[:, NA_WIN_R - 1 - off:2 * NA_WIN_R - 1 - off].transpose(0, 2, 1, 3).reshape(H, GRID_W, NA_WIN_R * GRID_W)
            for off in range(NA_WIN_R)]
    return jnp.stack(tabs, axis=1)


def _na_kernel(q_ref, k_ref, v_ref, kc_ref, vc_ref, bias_ref, o_ref, *, rq, rows):
    i = pl.program_id(2)
    lo = lax.broadcasted_iota(jnp.int32, (1, 128), 1) < HEAD_DIM
    win = NA_WIN_R * GRID_W
    G2 = 2 * GRID_W
    q = q_ref[0]
    zero = jnp.zeros_like(q)
    q_lo, q_hi = jnp.where(lo, q, zero), jnp.where(lo, zero, q)
    qq = jnp.concatenate([x[rr * GRID_W:(rr + 1) * GRID_W] for rr in range(rq) for x in (q_lo, q_hi)], axis=0)
    kc = kc_ref[0]
    ones = lambda n: jnp.ones((n, 128), BF16)
    s_c = _dot_nt(qq, kc)

    starts, s_nb = [], []
    for rr in range(rq):
        r = i * rq + rr
        r_start = jnp.clip(r - NA_WIN_R // 2, 0, rows - NA_WIN_R)
        off = r - r_start
        start = pl.multiple_of(r_start * GRID_W, GRID_W)
        starts.append(start)
        bias = jnp.concatenate([bias_ref[0, off], bias_ref[1, off]], axis=0)
        s_nb.append(_dot_nt(qq[rr * G2:(rr + 1) * G2], k_ref[0, pl.ds(start, win), :]) + bias)

    def lane_blocks(t):
        w = min(128, t.shape[-1])
        return [t[:, c * w:(c + 1) * w] for c in range(t.shape[-1] // w)]

    p_nb, p_c = [], []
    for rr in range(rq):
        sc = s_c[rr * G2:(rr + 1) * G2]
        blocks = lane_blocks(s_nb[rr])
        mx = blocks[0]
        for b in blocks[1:]:
            mx = jnp.maximum(mx, b)
        m = jnp.maximum(jnp.max(mx, axis=-1, keepdims=True), jnp.max(sc, axis=-1, keepdims=True))
        p_nb.append(jnp.exp2(s_nb[rr] - m).astype(BF16))
        p_c.append(jnp.exp2(sc - m).astype(BF16))

    vc_ext = jnp.concatenate([vc_ref[0], ones(kc.shape[0])], axis=1)
    o_c = jnp.dot(jnp.concatenate(p_c, axis=0), vc_ext, preferred_element_type=F32)
    for rr in range(rq):
        vw_ext = jnp.concatenate([v_ref[0, pl.ds(starts[rr], win), :], ones(win)], axis=1)
        o = jnp.dot(p_nb[rr], vw_ext, preferred_element_type=F32) + o_c[rr * G2:(rr + 1) * G2]
        o = o[:, :128] / o[:, 128:]
        o_ref[0, rr * GRID_W:(rr + 1) * GRID_W, :] = jnp.where(lo, o[:GRID_W], o[GRID_W:])


def _na_attn(qkv, qkv_c, bias, rq):
    B, S, _ = qkv.shape
    C = qkv_c.shape[1]
    rows = S // GRID_W
    npair = NA_HEADS // 2
    full = lambda t, j: pl.BlockSpec((1, t, 128), lambda b, p, m: (b, 0, j * npair + p))
    return pl.pallas_call(
        functools.partial(_na_kernel, rq=rq, rows=rows),
        grid=(B, npair, rows // rq),
        in_specs=[pl.BlockSpec((1, rq * GRID_W, 128), lambda b, p, m: (b, m, p)),
                  full(S, 1), full(S, 2), full(C, 1), full(C, 2),
                  pl.BlockSpec((2, NA_WIN_R, GRID_W, NA_WIN_R * GRID_W), lambda b, p, m: (p, 0, 0, 0))],
        out_specs=pl.BlockSpec((1, rq * GRID_W, 128), lambda b, p, m: (b, m, p)),
        out_shape=jax.ShapeDtypeStruct((B, S, NA_HEADS * HEAD_DIM), F32),
        compiler_params=_cparams(("parallel", "parallel", "parallel")),
        name="na_attn",
    )(qkv, qkv, qkv, qkv_c, qkv_c, bias)


def _ctx_attn_kernel(q_ref, k_ref, v_ref, o_ref):
    lo = lax.broadcasted_iota(jnp.int32, (1, 128), 1) < HEAD_DIM
    q, k, v = q_ref[0], k_ref[0], v_ref[0]
    zero = jnp.zeros_like(q)
    outs = []
    for hl in range(2):
        qh = jnp.where(lo, q, zero) if hl == 0 else jnp.where(lo, zero, q)
        s = _dot_nt(qh, k)
        p = jnp.exp2(s - jnp.max(s, axis=-1, keepdims=True))
        o = jnp.dot(p.astype(BF16), v, preferred_element_type=F32)
        outs.append(o / jnp.sum(p, axis=-1, keepdims=True))
    o_ref[0] = jnp.where(lo, outs[0], outs[1])


def _ctx_attn(qkv):
    B, C, _ = qkv.shape
    npair = NA_HEADS // 2
    blk = lambda j: pl.BlockSpec((1, C, 128), lambda b, p: (b, 0, j * npair + p))
    return pl.pallas_call(
        _ctx_attn_kernel,
        grid=(B, npair),
        in_specs=[blk(0), blk(1), blk(2)],
        out_specs=blk(0),
        out_shape=jax.ShapeDtypeStruct((B, C, NA_HEADS * HEAD_DIM), F32),
        compiler_params=_cparams(("parallel", "parallel")),
        name="ctx_attn",
    )(qkv, qkv, qkv)


PL_R, PL_V, PL_KK = 0, 1, 2
PL_LOGW, PL_KDIR, PL_B = 3, 4, 5
N_PLANES = 9
RWKV_CHUNK = 64


def _split3(x):
    hi = x.astype(BF16)
    r1 = x - hi.astype(F32)
    mid = r1.astype(BF16)
    lo = (r1 - mid.astype(F32)).astype(BF16)
    return hi, mid, lo


def _dot_exact_rhs(m, x):
    hi, mid, lo = _split3(x)
    mb = m.astype(BF16)
    d = lambda t: jnp.dot(mb, t, preferred_element_type=F32)
    return d(hi) + d(mid) + d(lo)


def _dot_hilo(x, m):
    hi = x.astype(BF16)
    lo = (x - hi.astype(F32)).astype(BF16)
    mb = m.astype(BF16)
    d = lambda t: jnp.dot(t, mb, preferred_element_type=F32)
    return d(hi) + d(lo)


def _head_ones(n):
    r = lax.broadcasted_iota(jnp.int32, (n, n), 0) // HEAD_DIM
    c = lax.broadcasted_iota(jnp.int32, (n, n), 1) // HEAD_DIM
    return (r == c).astype(F32)


def _rwkv_prep_kernel(zc_ref, zp_ref, zn_ref, mu_ref, kk_ref, ka_ref, rk_ref, w0_ref, a0_ref, wup_ref, aup_ref,
                      o_ref, bv_ref):
    tm = zc_ref.shape[1]
    m = pl.program_id(1)
    u = zc_ref[0]
    prev_row = jnp.where(m > 0, zp_ref[0, 7:8, :], 0.0)
    next_row = jnp.where(m < pl.num_programs(1) - 1, zn_ref[0, 0:1, :], 0.0)
    rows = lax.broadcasted_iota(jnp.int32, (tm, 1), 0)
    u_prev = jnp.where(rows == 0, prev_row, pltpu.roll(u, 1, 0))
    u_next = jnp.where(rows == tm - 1, next_row, pltpu.roll(u, tm - 1, 0))
    u = u + mu_ref[0:1, :] * (u_prev - u) + mu_ref[1:2, :] * (u_next - u)

    W = RWKV_WIDTH
    r, k, v = u[:, 0:W], u[:, W:2 * W], u[:, 2 * W:3 * W]
    lw = jnp.tanh(u[:, 3 * W:3 * W + 2 * LORA]).astype(BF16)
    la = u[:, 3 * W + 2 * LORA:3 * W + 4 * LORA].astype(BF16)
    ones = _head_ones(W)
    kk = k * kk_ref[...]
    ss = _dot_hilo(kk * kk, ones)
    kk = kk / jnp.maximum(jnp.sqrt(ss), 1e-12)
    o_ref[0, :, PL_R * W:(PL_R + 1) * W] = r
    o_ref[0, :, PL_V * W:(PL_V + 1) * W] = v
    o_ref[0, :, PL_KK * W:(PL_KK + 1) * W] = kk
    k_both = None
    for d in range(2):
        xw = w0_ref[d:d + 1, :] + jnp.dot(lw, wup_ref[d], preferred_element_type=F32)
        logw = -math.exp(-0.5) * _sigmoid(xw)
        a = _sigmoid(a0_ref[d:d + 1, :] + jnp.dot(la, aup_ref[d], preferred_element_type=F32))
        kdir = k * (1.0 + (a - 1.0) * ka_ref[...])
        base = 3 * d
        o_ref[0, :, (PL_LOGW + base) * W:(PL_LOGW + base + 1) * W] = logw
        o_ref[0, :, (PL_KDIR + base) * W:(PL_KDIR + base + 1) * W] = kdir
        o_ref[0, :, (PL_B + base) * W:(PL_B + base + 1) * W] = kk * a
        k_both = kdir if k_both is None else k_both + kdir
    bv_ref[0] = _dot_hilo(r * k_both * rk_ref[...], ones) * v


def _rwkv_prep(z, mu_pad, k_k, k_a, r_k, w0, a0, wup2, aup2, tm):
    B, T, wz = z.shape
    nb = tm // 8
    W = RWKV_WIDTH
    const = lambda shape: pl.BlockSpec(shape, lambda b, m: (0,) * len(shape))
    return pl.pallas_call(
        _rwkv_prep_kernel,
        grid=(B, T // tm),
        in_specs=[pl.BlockSpec((1, tm, wz), lambda b, m: (b, m, 0)),
                  pl.BlockSpec((1, 8, wz), lambda b, m: (b, jnp.maximum(m * nb - 1, 0), 0)),
                  pl.BlockSpec((1, 8, wz), lambda b, m: (b, jnp.minimum((m + 1) * nb, T // 8 - 1), 0)),
                  const((2, wz)), const((1, W)), const((1, W)), const((1, W)), const((2, W)), const((2, W)),
                  const((2, 2 * LORA, W)), const((2, 2 * LORA, W))],
        out_specs=[pl.BlockSpec((1, tm, N_PLANES * W), lambda b, m: (b, m, 0)),
                   pl.BlockSpec((1, tm, W), lambda b, m: (b, m, 0))],
        out_shape=[jax.ShapeDtypeStruct((B, T, N_PLANES * W), F32), jax.ShapeDtypeStruct((B, T, W), F32)],
        compiler_params=_cparams(("parallel", "parallel")),
        name="rwkv_prep",
    )(z, z, z, mu_pad, k_k.reshape(1, W), k_a.reshape(1, W), r_k.reshape(1, W), w0, a0, wup2, aup2)


def _rwkv_scan_kernel(pf_ref, pb_ref, s0_ref, yf_ref, yb_ref, s_ref):
    C = RWKV_CHUNK
    W = RWKV_WIDTH
    C2 = 2 * C

    @pl.when(pl.program_id(1) == 0)
    def _():
        s_ref[...] = s0_ref[...]

    lo = lax.broadcasted_iota(jnp.int32, (1, 128), 1) < HEAD_DIM
    row = lax.broadcasted_iota(jnp.int32, (C2, C2), 0)
    col = lax.broadcasted_iota(jnp.int32, (C2, C2), 1)
    same = (row // C) == (col // C)
    eye = (row == col).astype(F32)
    tri_r = lax.broadcasted_iota(jnp.int32, (C, C), 0)
    tri_c = lax.broadcasted_iota(jnp.int32, (C, C), 1)
    levels = [((row >> k) == (col >> k)) & ((row >> (k - 1)) != (col >> (k - 1))) for k in range(1, 7)]
    p_refs, y_refs = (pf_ref, pb_ref), (yf_ref, yb_ref)
    nb = pf_ref.shape[0]
    groups = [(n, d, p) for n in range(nb) for d in range(2) for p in range(W // 128)]
    ng = len(groups)
    bf = lambda t: t.astype(BF16)

    def stack(x):
        return jnp.concatenate([jnp.where(lo, x, 0.0), jnp.where(lo, 0.0, x)], axis=0)

    c_dir = {}
    for d in range(2):
        cum = ((tri_c <= tri_r) if d == 0 else (tri_c >= tri_r)).astype(F32)
        for n in range(nb):
            logw_all = p_refs[d][n, :, (PL_LOGW + 3 * d) * W:(PL_LOGW + 3 * d + 1) * W]
            c_dir[n, d] = _dot_exact_rhs(cum, logw_all)

    ar, bk, ends, vs, decay_end = [], [], [], [], []
    for n, d, p in groups:
        get = lambda plane: p_refs[d][n, :, plane * W + p * 128:plane * W + (p + 1) * 128]
        r, v, kk = get(PL_R), get(PL_V), get(PL_KK)
        logw, kdir, bb = get(PL_LOGW + 3 * d), get(PL_KDIR + 3 * d), get(PL_B + 3 * d)
        c = c_dir[n, d][:, p * 128:(p + 1) * 128]
        last = C - 1 if d == 0 else 0
        c_last = c[last:last + 1, :]
        e_neg = jnp.exp(-c)
        e_end = jnp.exp(c_last - c)
        ar.append(bf(jnp.concatenate([stack(-kk * jnp.exp(c - logw)), stack(r * jnp.exp(c))], axis=0)))
        bk.append(bf(jnp.concatenate([stack(bb * e_neg), stack(kdir * e_neg)], axis=0)))
        ends.append(bf(jnp.concatenate([stack(bb * e_end), stack(kdir * e_end)], axis=0)))
        vs.append(bf(stack(v)))
        decay_end.append(jnp.exp(c_last))

    amat = [_dot_nt(ar[g], bk[g]) for g in range(ng)]
    a_ab, a_kr, a_rb = [], [], []
    for g, (n, d, p) in enumerate(groups):
        before = (col < row) if d == 0 else (col > row)
        strict = same & before
        incl = same & (before | (row == col))
        m = amat[g]
        a_ab.append(jnp.where(strict, m[:C2, :C2], 0.0))
        a_kr.append(bf(jnp.concatenate([jnp.where(strict, m[:C2, C2:], 0.0),
                                        jnp.where(incl, m[C2:, C2:], 0.0)], axis=0)))
        a_rb.append(bf(jnp.where(incl, m[C2:, :C2], 0.0)))

    t = [eye + jnp.where(levels[0], a, 0.0) for a in a_ab]
    for lvl in levels[1:]:
        ta = [jnp.dot(bf(t[g]), bf(jnp.where(lvl, a_ab[g], 0.0)), preferred_element_type=F32) for g in range(ng)]
        t = [t[g] + jnp.dot(bf(ta[g]), bf(t[g]), preferred_element_type=F32) for g in range(ng)]

    s_old = [s_ref[n, d, p] for n, d, p in groups]
    from_s = [_dot_nt(ar[g], bf(s_old[g])) for g in range(ng)]
    from_v = [jnp.dot(a_kr[g], vs[g], preferred_element_type=F32) for g in range(ng)]
    u = [jnp.dot(bf(t[g]), bf(from_s[g][:C2] + from_v[g][:C2]), preferred_element_type=F32) for g in range(ng)]
    for g, (n, d, p) in enumerate(groups):
        y_s = from_s[g][C2:] + from_v[g][C2:] + jnp.dot(a_rb[g], bf(u[g]), preferred_element_type=F32)
        y_refs[d][n, :, p * 128:(p + 1) * 128] = y_s[:C] + y_s[C:]
    for g, (n, d, p) in enumerate(groups):
        uv = jnp.concatenate([bf(u[g]), vs[g]], axis=0)
        s_ref[n, d, p] = s_old[g] * decay_end[g] + lax.dot_general(
            uv, ends[g], (((0,), (0,)), ((), ())), preferred_element_type=F32)


def _rwkv_scan(planes, s0):
    B, T, _ = planes.shape
    nc = T // RWKV_CHUNK
    W = RWKV_WIDTH
    nb = 4 if B % 4 == 0 else 1
    st = pl.BlockSpec((nb, 2, W // 128, 128, 128), lambda b, i: (b, 0, 0, 0, 0))
    y = jax.ShapeDtypeStruct((B, T, W), F32)
    return pl.pallas_call(
        _rwkv_scan_kernel,
        grid=(B // nb, nc),
        in_specs=[pl.BlockSpec((nb, RWKV_CHUNK, N_PLANES * W), lambda b, i: (b, i, 0)),
                  pl.BlockSpec((nb, RWKV_CHUNK, N_PLANES * W), lambda b, i: (b, nc - 1 - i, 0)),
                  st],
        out_specs=[pl.BlockSpec((nb, RWKV_CHUNK, W), lambda b, i: (b, i, 0)),
                   pl.BlockSpec((nb, RWKV_CHUNK, W), lambda b, i: (b, nc - 1 - i, 0)),
                   st],
        out_shape=[y, y, jax.ShapeDtypeStruct(s0.shape, F32)],
        compiler_params=_cparams(("parallel", "arbitrary")),
        name="rwkv_scan",
    )(planes, planes, s0)


def _layer_weights(l, w_in, shift_mu, w_up, a_up, w_branch, w_out):
    rw, rg, na, ng, df, dg, mg = jnp.split(w_in[l], [int(i) for i in np.cumsum(IN_SIZES)[:-1]], axis=-1)
    pad = jnp.zeros((w_in.shape[1], RWKV_PAD_WIDTH - RWKV_SHIFT_WIDTH), w_in.dtype)
    w_perm = jnp.concatenate([mg, rg, ng, dg, rw, pad, na, df], axis=-1).astype(BF16)
    mu_pad = jnp.pad(shift_mu[l], ((0, 0), (0, RWKV_PAD_WIDTH - RWKV_SHIFT_WIDTH)))
    zl = jnp.zeros((LORA, RWKV_WIDTH), F32)
    wup2 = jnp.stack([jnp.concatenate([w_up[l, 0], zl]), jnp.concatenate([zl, w_up[l, 1]])]).astype(BF16)
    aup2 = jnp.stack([jnp.concatenate([a_up[l, 0], zl]), jnp.concatenate([zl, a_up[l, 1]])]).astype(BF16)
    return w_perm, mu_pad, wup2, aup2, w_branch[l].astype(BF16), w_out[l].astype(BF16)


def kernel(x, c, ctx, c_ctx, w_mod, b_mod, g_pre, g_post, w_in, shift_mu, k_k, k_a, r_k, w0, w_up, a0, a_up,
           ln_x_g, ln_x_b, rpb, lam_q, lam_k, diff_subln, w_branch, w_out):
    B, S, D = x.shape
    C = ctx.shape[1]
    depth = w_in.shape[0]
    tables = _rope_tables(S)
    rows_pad = 16
    cvec = jnp.zeros((rows_pad, D), F32).at[:B].set(c).at[B].set(c_ctx)
    mod = _modulation(cvec, w_mod, b_mod)
    hc = ctx
    tm_x = min(2048, S)
    for l in range(depth):
        last = l == depth - 1
        lambda_init = 0.8 - 0.6 * math.exp(-0.3 * l)
        w_perm, mu_pad, wup2, aup2, wb, wo = _layer_weights(l, w_in, shift_mu, w_up, a_up, w_branch, w_out)
        mod_x = mod[l, :B].reshape(B, 3, D)
        mod_c = mod[l, B:B + 1].reshape(1, 3, D)
        zg_x, rw_x, na_x, df_x = _inproj(x, mod_x, g_pre[l], w_perm, tables, tm_x)
        ctx_out = _inproj(hc.reshape(1, B * C, D), mod_c, g_pre[l], w_perm, None, B * C)
        zg_c, rw_c, na_c, df_c = (t.reshape(B, C, t.shape[-1]) for t in ctx_out)

        r_k_flat = r_k[l].reshape(RWKV_WIDTH)
        pl_c, bv_c = _rwkv_prep(rw_c, mu_pad, k_k[l], k_a[l], r_k_flat, w0[l], a0[l], wup2, aup2, min(256, C))
        pl_x, bv_x = _rwkv_prep(rw_x, mu_pad, k_k[l], k_a[l], r_k_flat, w0[l], a0[l], wup2, aup2, 512)
        s_zero = jnp.zeros((B, 2, RWKV_WIDTH // 128, 128, 128), F32)
        yf_c, yb_c, s_ctx = _rwkv_scan(pl_c, s_zero)
        yf_x, yb_x, _ = _rwkv_scan(pl_x, s_ctx)

        o_na_x = _na_attn(na_x, na_c, _na_bias_tables(rpb[l]) * LOG2E, min(32, S // GRID_W))
        o_df_x = _diff_attn(df_x, df_c, True, lam_q[l], lam_k[l], diff_subln[l], lambda_init, min(1024, S), min(1024, S // 2))

        x = _merge(x, mod_x, g_post[l], yf_x, yb_x, bv_x, ln_x_g[l], ln_x_b[l], o_na_x, o_df_x, zg_x, wb, wo, 512)
        if not last:
            o_na_c = _ctx_attn(na_c)
            o_df_c = _diff_attn(df_c, df_c, False, lam_q[l], lam_k[l], diff_subln[l], lambda_init, C, C)
            hc = _merge(hc, mod_c, g_post[l], yf_c, yb_c, bv_c, ln_x_g[l], ln_x_b[l], o_na_c, o_df_c, zg_c, wb, wo,
                        min(256, C))
    return x
```

```python
import functools
import math

import jax
import jax.numpy as jnp
import numpy as np
from jax import lax
from jax.experimental import pallas as pl
from jax.experimental.pallas import tpu as pltpu

F32 = jnp.float32
BF16 = jnp.bfloat16

DEPTH = 4
GRID_W = 64
RWKV_HEADS = 8
HEAD_DIM = 64
RWKV_WIDTH = 512
LORA = 64
GN_EPS = 64e-5
NA_HEADS = 8
NA_WIN_R = 8
NA_WIN_C = 16
DIFF_HEADS = 4
DIFF_QK_DIM = 64
DIFF_V_DIM = 128
ROPE_THETA = 10000.0
SUBLN_EPS = 1e-5
RMS_EPS = 1e-6
NEG_INF = -1e30
LOG2E = math.log2(math.e)
BRANCH_WIDTH = 512
N_BRANCH = 3

RWKV_SHIFT_WIDTH = 3 * RWKV_WIDTH + 4 * LORA
RWKV_PAD_WIDTH = 2048
TILE_W = 512
TILE_RW, TILE_NA, TILE_DF = 9, 13, 16
Z_WIDTH = 19 * TILE_W
IN_SIZES = (RWKV_SHIFT_WIDTH, 512, 1536, 512, 1536, 512, 3072)

VMEM_LIMIT = 56 * 1024 * 1024
INPROJ_ROWS = 2048
PREP_ROWS = 512
MERGE_ROWS = 512
CTX_ROWS = 256
DIFF_Q_ROWS = 1024
DIFF_KEY_CHUNK = 1024
NA_ROWS_PER_STEP = 32
SCAN_BATCH_PER_STEP = 4


def _cparams(sem):
    return pltpu.CompilerParams(dimension_semantics=sem, vmem_limit_bytes=VMEM_LIMIT)


def _sigmoid(x):
    return 1.0 / (1.0 + jnp.exp(-x))


def _silu(x):
    return x * _sigmoid(x)


def _bdot(a, b):
    return jnp.dot(a.astype(BF16), b.astype(BF16), preferred_element_type=F32)


def _mod_kernel(c_ref, w_ref, b_ref, o_ref):
    o_ref[0] = _bdot(_silu(c_ref[...]), w_ref[0]) + b_ref[0]


def _modulation(cvec, w_mod, b_mod):
    L, D, N = w_mod.shape
    R = cvec.shape[0]
    tn = 1024
    return pl.pallas_call(
        _mod_kernel,
        grid=(L, N // tn),
        in_specs=[pl.BlockSpec((R, D), lambda l, n: (0, 0)),
                  pl.BlockSpec((1, D, tn), lambda l, n: (l, 0, n)),
                  pl.BlockSpec((1, 1, tn), lambda l, n: (l, 0, n))],
        out_specs=pl.BlockSpec((1, R, tn), lambda l, n: (l, 0, n)),
        out_shape=jax.ShapeDtypeStruct((L, R, N), F32),
        compiler_params=_cparams(("parallel", "parallel")),
        name="modulation",
    )(cvec, w_mod, b_mod.reshape(L, 1, N))


def _inproj_kernel(*refs, rope, q_scale):
    if rope:
        x_ref, mod_ref, g_ref, w_ref, cos_ref, sa_ref, sb_ref, zg_ref, rw_ref, na_ref, df_ref, h_ref = refs
    else:
        x_ref, mod_ref, g_ref, w_ref, zg_ref, rw_ref, na_ref, df_ref, h_ref = refs
    n = pl.program_id(2)

    @pl.when(n == 0)
    def _():
        x = x_ref[0]
        y = x * lax.rsqrt(jnp.mean(x * x, axis=-1, keepdims=True) + RMS_EPS) * g_ref[...]
        h_ref[...] = (y * (1.0 + mod_ref[0, 1:2, :]) + mod_ref[0, 0:1, :]).astype(BF16)

    mm = lambda: jnp.dot(h_ref[...], w_ref[...], preferred_element_type=F32)

    def rot(t):
        if not rope:
            return t
        w = t.shape[-1]
        tile = lambda r: jnp.concatenate([r[...]] * (w // r.shape[-1]), axis=1)
        return t * tile(cos_ref) + pltpu.roll(t, w - 16, 1) * tile(sa_ref) + pltpu.roll(t, 16, 1) * tile(sb_ref)

    @pl.when(n < TILE_RW)
    def _():
        zg_ref[0] = mm().astype(BF16)

    @pl.when((n >= TILE_RW) & (n < TILE_NA))
    def _():
        rw_ref[0] = mm()

    @pl.when(n == TILE_NA)
    def _():
        na_ref[0] = (mm() * q_scale).astype(BF16)

    @pl.when((n > TILE_NA) & (n < TILE_DF))
    def _():
        na_ref[0] = mm().astype(BF16)

    @pl.when(n == TILE_DF)
    def _():
        df_ref[0] = (rot(mm()) * q_scale).astype(BF16)

    @pl.when(n == TILE_DF + 1)
    def _():
        df_ref[0] = rot(mm()).astype(BF16)

    @pl.when(n == TILE_DF + 2)
    def _():
        df_ref[0] = mm().astype(BF16)


def _inproj(x, mod, g_pre, w_perm, tables, tm):
    B, T, D = x.shape
    tn = TILE_W
    per_batch = mod.shape[0] > 1
    in_specs = [pl.BlockSpec((1, tm, D), lambda b, m, n: (b, m, 0)),
                pl.BlockSpec((1, 3, D), (lambda b, m, n: (b, 0, 0)) if per_batch else (lambda b, m, n: (0, 0, 0))),
                pl.BlockSpec((1, D), lambda b, m, n: (0, 0)),
                pl.BlockSpec((D, tn), lambda b, m, n: (0, n))]
    args = [x, mod, g_pre.reshape(1, D), w_perm]
    if tables is not None:
        in_specs += [pl.BlockSpec((tm, 128), lambda b, m, n: (m, 0))] * 3
        args += list(tables)
    seg = lambda first, count: pl.BlockSpec((1, tm, tn), lambda b, m, n: (b, m, jnp.clip(n - first, 0, count - 1)))
    out = lambda count, dtype: jax.ShapeDtypeStruct((B, T, count * tn), dtype)
    return pl.pallas_call(
        functools.partial(_inproj_kernel, rope=tables is not None, q_scale=HEAD_DIM ** -0.5 * LOG2E),
        grid=(B, T // tm, Z_WIDTH // tn),
        in_specs=in_specs,
        out_specs=[seg(0, TILE_RW), seg(TILE_RW, TILE_NA - TILE_RW), seg(TILE_NA, TILE_DF - TILE_NA),
                   seg(TILE_DF, Z_WIDTH // tn - TILE_DF)],
        out_shape=[out(TILE_RW, BF16), out(TILE_NA - TILE_RW, F32), out(TILE_DF - TILE_NA, BF16),
                   out(Z_WIDTH // tn - TILE_DF, BF16)],
        scratch_shapes=[pltpu.VMEM((tm, D), BF16)],
        compiler_params=_cparams(("parallel", "parallel", "arbitrary")),
        name="inproj",
    )(*args)


def _merge_kernel(x_ref, mod_ref, gpost_ref, yf_ref, yb_ref, bv_ref, lng_ref, lnb_ref, ona_ref, odf_ref,
                  rg_ref, ng_ref, dg_ref, mg_ref, wb_ref, wo_ref, o_ref):
    D = x_ref.shape[-1]
    ones = _head_ones(RWKV_WIDTH)
    y = yf_ref[0] + yb_ref[0]
    mu = _dot_hilo(y, ones) * (1.0 / HEAD_DIM)
    yc = y - mu
    var = _dot_hilo(yc * yc, ones) * (1.0 / HEAD_DIM)
    o_rw = yc * lax.rsqrt(var + GN_EPS) * lng_ref[...] + lnb_ref[...] + bv_ref[0]
    acc = None
    for n, (ob, gate_ref) in enumerate(((o_rw, rg_ref), (ona_ref[0], ng_ref), (odf_ref[0], dg_ref))):
        yb = _bdot(ob * _silu(gate_ref[0].astype(F32)), wb_ref[n])
        term = _sigmoid(mg_ref[0, :, n * D:(n + 1) * D].astype(F32)) * yb
        acc = term if acc is None else acc + term
    y = _bdot(acc, wo_ref[...])
    yn = y * lax.rsqrt(jnp.mean(y * y, axis=-1, keepdims=True) + RMS_EPS) * gpost_ref[...]
    o_ref[0] = x_ref[0] + mod_ref[0, 2:3, :] * yn


def _merge(x, mod, g_post, y_f, y_b, bonus, ln_g, ln_b, o_na, o_df, zg, w_branch, w_out, tm):
    B, T, D = x.shape
    per_batch = mod.shape[0] > 1
    bw = BRANCH_WIDTH
    row = lambda b, m: (b, m, 0)
    col = lambda c: (lambda b, m: (b, m, c))
    return pl.pallas_call(
        _merge_kernel,
        grid=(B, T // tm),
        in_specs=[pl.BlockSpec((1, tm, D), row),
                  pl.BlockSpec((1, 3, D), (lambda b, m: (b, 0, 0)) if per_batch else (lambda b, m: (0, 0, 0))),
                  pl.BlockSpec((1, D), lambda b, m: (0, 0)),
                  pl.BlockSpec((1, tm, bw), row), pl.BlockSpec((1, tm, bw), row), pl.BlockSpec((1, tm, bw), row),
                  pl.BlockSpec((1, bw), lambda b, m: (0, 0)), pl.BlockSpec((1, bw), lambda b, m: (0, 0)),
                  pl.BlockSpec((1, tm, bw), row), pl.BlockSpec((1, tm, bw), row),
                  pl.BlockSpec((1, tm, bw), col(N_BRANCH * D // bw)),
                  pl.BlockSpec((1, tm, bw), col(N_BRANCH * D // bw + 1)),
                  pl.BlockSpec((1, tm, bw), col(N_BRANCH * D // bw + 2)),
                  pl.BlockSpec((1, tm, N_BRANCH * D), col(0)),
                  pl.BlockSpec((N_BRANCH, bw, D), lambda b, m: (0, 0, 0)),
                  pl.BlockSpec((D, D), lambda b, m: (0, 0))],
        out_specs=pl.BlockSpec((1, tm, D), row),
        out_shape=jax.ShapeDtypeStruct((B, T, D), F32),
        compiler_params=_cparams(("parallel", "parallel")),
        name="merge",
    )(x, mod, g_post.reshape(1, D), y_f, y_b, bonus, ln_g.reshape(1, bw), ln_b.reshape(1, bw), o_na, o_df,
      zg, zg, zg, zg, w_branch, w_out)


def _rope_tables(n_tokens):
    t = np.arange(n_tokens)
    axis_dim = DIFF_QK_DIM // 2
    inv = ROPE_THETA ** (-np.arange(0, axis_dim, 2, dtype=np.float32) / axis_dim)
    ar = (t // GRID_W).astype(np.float32)[:, None] * inv
    ac = (t % GRID_W).astype(np.float32)[:, None] * inv
    ang = jnp.asarray(np.concatenate([ar, ar, ac, ac], axis=-1).astype(np.float32))
    cos, sin = jnp.cos(ang), jnp.sin(ang)
    first = (np.arange(DIFF_QK_DIM) % 32) < 16
    sin_a = jnp.where(first, -sin, 0.0)
    sin_b = jnp.where(first, 0.0, sin)
    tile = lambda a: jnp.tile(a, (1, 2))
    return tile(cos), tile(sin_a), tile(sin_b)


def _dot_nt(a, b):
    return lax.dot_general(a, b, (((1,), (1,)), ((), ())), preferred_element_type=F32)


SOFTMAX_ROWS = 128


def _diff_attn_kernel(q_ref, kc_ref, vc_ref, k_ref, v_ref, lq_ref, lk_ref, g_ref, o_ref,
                      sc, pc, s0, s1, p0, p1, al0, al1, m_scr, acc_scr, *, n_main, tk, lambda_init):
    tq = q_ref.shape[1]
    t_first = kc_ref.shape[1]
    q = q_ref[0]
    lo = lax.broadcasted_iota(jnp.int32, (1, 128), 1) < DIFF_QK_DIM
    zero = jnp.zeros_like(q)
    qq = jnp.concatenate([jnp.where(lo, q, zero), jnp.where(lo, zero, q)], axis=0)

    n_sub = 2 * tq // SOFTMAX_ROWS
    sub = lambda i: slice(i * SOFTMAX_ROWS, (i + 1) * SOFTMAX_ROWS)

    def scores(k_chunk, s_scr):
        s_scr[:, :k_chunk.shape[0]] = _dot_nt(qq, k_chunk)

    def softmax(size, first, s_scr, p_scr, al_scr):
        w = min(128, size)
        cols = [slice(c * w, (c + 1) * w) for c in range(size // w)]
        for i in range(n_sub):
            mx = s_scr[sub(i), cols[0]]
            for c in cols[1:]:
                mx = jnp.maximum(mx, s_scr[sub(i), c])
            mn = jnp.broadcast_to(jnp.max(mx, axis=-1, keepdims=True), (SOFTMAX_ROWS, 128))
            if not first:
                mo = m_scr[sub(i), :]
                mn = jnp.maximum(mo, mn)
                al_scr[sub(i), :] = jnp.exp2(mo - mn)
            m_scr[sub(i), :] = mn
        for i in range(n_sub):
            mn = m_scr[sub(i), :w]
            for c in cols:
                p_scr[sub(i), c] = jnp.exp2(s_scr[sub(i), c] - mn).astype(BF16)

    def accumulate(v, first, p_scr, al_scr):
        size = v.shape[0]
        v_ext = jnp.concatenate([v, jnp.ones((size, 128), BF16)], axis=1)
        pv = jnp.dot(p_scr[:, :size], v_ext, preferred_element_type=F32)
        if first:
            acc_scr[...] = pv
        else:
            al = al_scr[...]
            acc_scr[...] = acc_scr[...] * jnp.concatenate([al, al], axis=1) + pv

    scores(kc_ref[0], sc)
    if n_main == 0:
        softmax(t_first, True, sc, pc, None)
        accumulate(vc_ref[0], True, pc, None)
    else:
        def chunk_of(ref, j):
            start = j * tk if isinstance(j, int) else pl.multiple_of(j * tk, tk)
            return ref[0, pl.ds(start, tk), :]

        k_at = functools.partial(chunk_of, k_ref)
        v_at = functools.partial(chunk_of, v_ref)
        scores(k_at(0), s0)
        softmax(t_first, True, sc, pc, None)

        def pair(j, first):
            scores(k_at(j + 1), s1)
            softmax(tk, False, s0, p0, al0)
            if first:
                accumulate(vc_ref[0], True, pc, None)
            else:
                accumulate(v_at(j - 1), False, p1, al1)
            nxt = min(j + 2, n_main - 1) if isinstance(j, int) else jnp.minimum(j + 2, n_main - 1)
            scores(k_at(nxt), s0)
            softmax(tk, False, s1, p1, al1)
            accumulate(v_at(j), False, p0, al0)

        pair(0, True)

        def body(t, _):
            pair(2 * t, False)
            return 0

        lax.fori_loop(1, n_main // 2, body, 0)
        accumulate(v_at(n_main - 1), False, p1, al1)
    lqk = lq_ref[...] * lk_ref[...]
    e = jnp.exp(jnp.sum(lqk, axis=-1, keepdims=True))
    lam = e[0:1] - e[1:2] + lambda_init
    o_maps = acc_scr[:, :DIFF_V_DIM] / acc_scr[:, DIFF_V_DIM:]
    o = o_maps[:tq] - lam * o_maps[tq:]
    o = o * lax.rsqrt(jnp.mean(o * o, axis=-1, keepdims=True) + SUBLN_EPS) * g_ref[...]
    o_ref[0] = o * (1.0 - lambda_init)


def _diff_attn(qkv, qkv_c, with_latent_keys, lam_q, lam_k, subln_g, lambda_init, tq, tk):
    B, T, _ = qkv.shape
    C = qkv_c.shape[1]
    n_main = T // tk if with_latent_keys else 0
    assert n_main % 2 == 0 and (2 * tq) % SOFTMAX_ROWS == 0
    wmax = tk if n_main else 128
    H = DIFF_HEADS
    kv = lambda t, j: pl.BlockSpec((1, t, 128), lambda b, h, m: (b, 0, j * H + h))
    small = lambda r, c: pl.BlockSpec((r, c), lambda b, h, m: (0, 0))
    stat = pltpu.VMEM((2 * tq, 128), F32)
    s_buf = pltpu.VMEM((2 * tq, wmax), F32)
    p_buf = pltpu.VMEM((2 * tq, wmax), BF16)
    return pl.pallas_call(
        functools.partial(_diff_attn_kernel, n_main=n_main, tk=tk, lambda_init=lambda_init),
        grid=(B, H, T // tq),
        in_specs=[pl.BlockSpec((1, tq, 128), lambda b, h, m: (b, m, h)), kv(C, 1), kv(C, 2), kv(T, 1), kv(T, 2),
                  small(2, DIFF_QK_DIM), small(2, DIFF_QK_DIM), small(1, DIFF_V_DIM)],
        out_specs=pl.BlockSpec((1, tq, 128), lambda b, h, m: (b, m, h)),
        out_shape=jax.ShapeDtypeStruct((B, T, H * DIFF_V_DIM), F32),
        scratch_shapes=[pltpu.VMEM((2 * tq, C), F32), pltpu.VMEM((2 * tq, C), BF16), s_buf, s_buf, p_buf, p_buf,
                        stat, stat, stat, pltpu.VMEM((2 * tq, 2 * DIFF_V_DIM), F32)],
        compiler_params=_cparams(("parallel", "parallel", "parallel")),
        name="diff_attn",
    )(qkv, qkv_c, qkv_c, qkv, qkv, lam_q, lam_k, subln_g.reshape(1, DIFF_V_DIM))


def _na_bias_tables(rpb):
    c_idx = np.arange(GRID_W)
    c_start = np.clip(c_idx - NA_WIN_C // 2, 0, GRID_W - NA_WIN_C)
    col_ok = (c_idx[None, :] >= c_start[:, None]) & (c_idx[None, :] < c_start[:, None] + NA_WIN_C)
    dc = np.clip(c_idx[None, :] - c_idx[:, None], -(NA_WIN_C - 1), NA_WIN_C - 1) + (NA_WIN_C - 1)
    H = rpb.shape[0]
    e = jnp.take(rpb, jnp.asarray(dc.reshape(-1)), axis=2).reshape(H, 2 * NA_WIN_R - 1, GRID_W, GRID_W)
    e = jnp.where(col_ok, e, NEG_INF)
    tabs = [e[:, NA_WIN_R - 1 - off:2 * NA_WIN_R - 1 - off].transpose(0, 2, 1, 3).reshape(H, GRID_W, NA_WIN_R * GRID_W)
            for off in range(NA_WIN_R)]
    return jnp.stack(tabs, axis=1)


def _na_kernel(q_ref, k_ref, v_ref, kc_ref, vc_ref, bias_ref, o_ref, *, rq, rows):
    i = pl.program_id(2)
    lo = lax.broadcasted_iota(jnp.int32, (1, 128), 1) < HEAD_DIM
    win = NA_WIN_R * GRID_W
    G2 = 2 * GRID_W
    q = q_ref[0]
    zero = jnp.zeros_like(q)
    q_lo, q_hi = jnp.where(lo, q, zero), jnp.where(lo, zero, q)
    qq = jnp.concatenate([x[rr * GRID_W:(rr + 1) * GRID_W] for rr in range(rq) for x in (q_lo, q_hi)], axis=0)
    kc = kc_ref[0]
    ones = lambda n: jnp.ones((n, 128), BF16)
    s_c = _dot_nt(qq, kc)

    starts, s_nb = [], []
    for rr in range(rq):
        r = i * rq + rr
        r_start = jnp.clip(r - NA_WIN_R // 2, 0, rows - NA_WIN_R)
        off = r - r_start
        start = pl.multiple_of(r_start * GRID_W, GRID_W)
        starts.append(start)
        bias = jnp.concatenate([bias_ref[0, off], bias_ref[1, off]], axis=0)
        s_nb.append(_dot_nt(qq[rr * G2:(rr + 1) * G2], k_ref[0, pl.ds(start, win), :]) + bias)

    def lane_blocks(t):
        w = min(128, t.shape[-1])
        return [t[:, c * w:(c + 1) * w] for c in range(t.shape[-1] // w)]

    p_nb, p_c = [], []
    for rr in range(rq):
        sc = s_c[rr * G2:(rr + 1) * G2]
        blocks = lane_blocks(s_nb[rr])
        mx = blocks[0]
        for b in blocks[1:]:
            mx = jnp.maximum(mx, b)
        m = jnp.maximum(jnp.max(mx, axis=-1, keepdims=True), jnp.max(sc, axis=-1, keepdims=True))
        p_nb.append(jnp.exp2(s_nb[rr] - m).astype(BF16))
        p_c.append(jnp.exp2(sc - m).astype(BF16))

    vc_ext = jnp.concatenate([vc_ref[0], ones(kc.shape[0])], axis=1)
    o_c = jnp.dot(jnp.concatenate(p_c, axis=0), vc_ext, preferred_element_type=F32)
    for rr in range(rq):
        vw_ext = jnp.concatenate([v_ref[0, pl.ds(starts[rr], win), :], ones(win)], axis=1)
        o = jnp.dot(p_nb[rr], vw_ext, preferred_element_type=F32) + o_c[rr * G2:(rr + 1) * G2]
        o = o[:, :128] / o[:, 128:]
        o_ref[0, rr * GRID_W:(rr + 1) * GRID_W, :] = jnp.where(lo, o[:GRID_W], o[GRID_W:])


def _na_attn(qkv, qkv_c, bias, rq):
    B, S, _ = qkv.shape
    C = qkv_c.shape[1]
    rows = S // GRID_W
    npair = NA_HEADS // 2
    full = lambda t, j: pl.BlockSpec((1, t, 128), lambda b, p, m: (b, 0, j * npair + p))
    return pl.pallas_call(
        functools.partial(_na_kernel, rq=rq, rows=rows),
        grid=(B, npair, rows // rq),
        in_specs=[pl.BlockSpec((1, rq * GRID_W, 128), lambda b, p, m: (b, m, p)),
                  full(S, 1), full(S, 2), full(C, 1), full(C, 2),
                  pl.BlockSpec((2, NA_WIN_R, GRID_W, NA_WIN_R * GRID_W), lambda b, p, m: (p, 0, 0, 0))],
        out_specs=pl.BlockSpec((1, rq * GRID_W, 128), lambda b, p, m: (b, m, p)),
        out_shape=jax.ShapeDtypeStruct((B, S, NA_HEADS * HEAD_DIM), F32),
        compiler_params=_cparams(("parallel", "parallel", "parallel")),
        name="na_attn",
    )(qkv, qkv, qkv, qkv_c, qkv_c, bias)


def _ctx_attn_kernel(q_ref, k_ref, v_ref, o_ref):
    lo = lax.broadcasted_iota(jnp.int32, (1, 128), 1) < HEAD_DIM
    q, k, v = q_ref[0], k_ref[0], v_ref[0]
    zero = jnp.zeros_like(q)
    outs = []
    for hl in range(2):
        qh = jnp.where(lo, q, zero) if hl == 0 else jnp.where(lo, zero, q)
        s = _dot_nt(qh, k)
        p = jnp.exp2(s - jnp.max(s, axis=-1, keepdims=True))
        o = jnp.dot(p.astype(BF16), v, preferred_element_type=F32)
        outs.append(o / jnp.sum(p, axis=-1, keepdims=True))
    o_ref[0] = jnp.where(lo, outs[0], outs[1])


def _ctx_attn(qkv):
    B, C, _ = qkv.shape
    npair = NA_HEADS // 2
    blk = lambda j: pl.BlockSpec((1, C, 128), lambda b, p: (b, 0, j * npair + p))
    return pl.pallas_call(
        _ctx_attn_kernel,
        grid=(B, npair),
        in_specs=[blk(0), blk(1), blk(2)],
        out_specs=blk(0),
        out_shape=jax.ShapeDtypeStruct((B, C, NA_HEADS * HEAD_DIM), F32),
        compiler_params=_cparams(("parallel", "parallel")),
        name="ctx_attn",
    )(qkv, qkv, qkv)


PL_R, PL_V, PL_KK = 0, 1, 2
PL_LOGW, PL_KDIR, PL_B = 3, 4, 5
N_PLANES = 9
RWKV_CHUNK = 64


def _split3(x):
    hi = x.astype(BF16)
    r1 = x - hi.astype(F32)
    mid = r1.astype(BF16)
    lo = (r1 - mid.astype(F32)).astype(BF16)
    return hi, mid, lo


def _dot_exact_rhs(m, x):
    hi, mid, lo = _split3(x)
    mb = m.astype(BF16)
    d = lambda t: jnp.dot(mb, t, preferred_element_type=F32)
    return d(hi) + d(mid) + d(lo)


def _dot_hilo(x, m):
    hi = x.astype(BF16)
    lo = (x - hi.astype(F32)).astype(BF16)
    mb = m.astype(BF16)
    d = lambda t: jnp.dot(t, mb, preferred_element_type=F32)
    return d(hi) + d(lo)


def _head_ones(n):
    r = lax.broadcasted_iota(jnp.int32, (n, n), 0) // HEAD_DIM
    c = lax.broadcasted_iota(jnp.int32, (n, n), 1) // HEAD_DIM
    return (r == c).astype(F32)


def _rwkv_prep_kernel(zc_ref, zp_ref, zn_ref, mu_ref, kk_ref, ka_ref, rk_ref, w0_ref, a0_ref, wup_ref, aup_ref,
                      o_ref, bv_ref):
    tm = zc_ref.shape[1]
    m = pl.program_id(1)
    u = zc_ref[0]
    prev_row = jnp.where(m > 0, zp_ref[0, 7:8, :], 0.0)
    next_row = jnp.where(m < pl.num_programs(1) - 1, zn_ref[0, 0:1, :], 0.0)
    rows = lax.broadcasted_iota(jnp.int32, (tm, 1), 0)
    u_prev = jnp.where(rows == 0, prev_row, pltpu.roll(u, 1, 0))
    u_next = jnp.where(rows == tm - 1, next_row, pltpu.roll(u, tm - 1, 0))
    u = u + mu_ref[0:1, :] * (u_prev - u) + mu_ref[1:2, :] * (u_next - u)

    W = RWKV_WIDTH
    r, k, v = u[:, 0:W], u[:, W:2 * W], u[:, 2 * W:3 * W]
    lw = jnp.tanh(u[:, 3 * W:3 * W + 2 * LORA]).astype(BF16)
    la = u[:, 3 * W + 2 * LORA:3 * W + 4 * LORA].astype(BF16)
    ones = _head_ones(W)
    kk = k * kk_ref[...]
    ss = _dot_hilo(kk * kk, ones)
    kk = kk / jnp.maximum(jnp.sqrt(ss), 1e-12)
    o_ref[0, :, PL_R * W:(PL_R + 1) * W] = r
    o_ref[0, :, PL_V * W:(PL_V + 1) * W] = v
    o_ref[0, :, PL_KK * W:(PL_KK + 1) * W] = kk
    k_both = None
    for d in range(2):
        xw = w0_ref[d:d + 1, :] + jnp.dot(lw, wup_ref[d], preferred_element_type=F32)
        logw = -math.exp(-0.5) * _sigmoid(xw)
        a = _sigmoid(a0_ref[d:d + 1, :] + jnp.dot(la, aup_ref[d], preferred_element_type=F32))
        kdir = k * (1.0 + (a - 1.0) * ka_ref[...])
        base = 3 * d
        o_ref[0, :, (PL_LOGW + base) * W:(PL_LOGW + base + 1) * W] = logw
        o_ref[0, :, (PL_KDIR + base) * W:(PL_KDIR + base + 1) * W] = kdir
        o_ref[0, :, (PL_B + base) * W:(PL_B + base + 1) * W] = kk * a
        k_both = kdir if k_both is None else k_both + kdir
    bv_ref[0] = _dot_hilo(r * k_both * rk_ref[...], ones) * v


def _rwkv_prep(z, mu_pad, k_k, k_a, r_k, w0, a0, wup2, aup2, tm):
    B, T, wz = z.shape
    nb = tm // 8
    W = RWKV_WIDTH
    const = lambda shape: pl.BlockSpec(shape, lambda b, m: (0,) * len(shape))
    return pl.pallas_call(
        _rwkv_prep_kernel,
        grid=(B, T // tm),
        in_specs=[pl.BlockSpec((1, tm, wz), lambda b, m: (b, m, 0)),
                  pl.BlockSpec((1, 8, wz), lambda b, m: (b, jnp.maximum(m * nb - 1, 0), 0)),
                  pl.BlockSpec((1, 8, wz), lambda b, m: (b, jnp.minimum((m + 1) * nb, T // 8 - 1), 0)),
                  const((2, wz)), const((1, W)), const((1, W)), const((1, W)), const((2, W)), const((2, W)),
                  const((2, 2 * LORA, W)), const((2, 2 * LORA, W))],
        out_specs=[pl.BlockSpec((1, tm, N_PLANES * W), lambda b, m: (b, m, 0)),
                   pl.BlockSpec((1, tm, W), lambda b, m: (b, m, 0))],
        out_shape=[jax.ShapeDtypeStruct((B, T, N_PLANES * W), F32), jax.ShapeDtypeStruct((B, T, W), F32)],
        compiler_params=_cparams(("parallel", "parallel")),
        name="rwkv_prep",
    )(z, z, z, mu_pad, k_k.reshape(1, W), k_a.reshape(1, W), r_k.reshape(1, W), w0, a0, wup2, aup2)


def _rwkv_scan_kernel(pf_ref, pb_ref, s0_ref, yf_ref, yb_ref, s_ref):
    C = RWKV_CHUNK
    W = RWKV_WIDTH
    C2 = 2 * C

    @pl.when(pl.program_id(1) == 0)
    def _():
        s_ref[...] = s0_ref[...]

    lo = lax.broadcasted_iota(jnp.int32, (1, 128), 1) < HEAD_DIM
    row = lax.broadcasted_iota(jnp.int32, (C2, C2), 0)
    col = lax.broadcasted_iota(jnp.int32, (C2, C2), 1)
    same = (row // C) == (col // C)
    eye = (row == col).astype(F32)
    tri_r = lax.broadcasted_iota(jnp.int32, (C, C), 0)
    tri_c = lax.broadcasted_iota(jnp.int32, (C, C), 1)
    levels = [((row >> k) == (col >> k)) & ((row >> (k - 1)) != (col >> (k - 1))) for k in range(1, 7)]
    p_refs, y_refs = (pf_ref, pb_ref), (yf_ref, yb_ref)
    nb = pf_ref.shape[0]
    groups = [(n, d, p) for n in range(nb) for d in range(2) for p in range(W // 128)]
    ng = len(groups)
    bf = lambda t: t.astype(BF16)

    def stack(x):
        return jnp.concatenate([jnp.where(lo, x, 0.0), jnp.where(lo, 0.0, x)], axis=0)

    c_dir = {}
    for d in range(2):
        cum = ((tri_c <= tri_r) if d == 0 else (tri_c >= tri_r)).astype(F32)
        for n in range(nb):
            logw_all = p_refs[d][n, :, (PL_LOGW + 3 * d) * W:(PL_LOGW + 3 * d + 1) * W]
            c_dir[n, d] = _dot_exact_rhs(cum, logw_all)

    ar, bk, ends, vs, decay_end = [], [], [], [], []
    for n, d, p in groups:
        get = lambda plane: p_refs[d][n, :, plane * W + p * 128:plane * W + (p + 1) * 128]
        r, v, kk = get(PL_R), get(PL_V), get(PL_KK)
        logw, kdir, bb = get(PL_LOGW + 3 * d), get(PL_KDIR + 3 * d), get(PL_B + 3 * d)
        c = c_dir[n, d][:, p * 128:(p + 1) * 128]
        last = C - 1 if d == 0 else 0
        c_last = c[last:last + 1, :]
        e_neg = jnp.exp(-c)
        e_end = jnp.exp(c_last - c)
        ar.append(bf(jnp.concatenate([stack(-kk * jnp.exp(c - logw)), stack(r * jnp.exp(c))], axis=0)))
        bk.append(bf(jnp.concatenate([stack(bb * e_neg), stack(kdir * e_neg)], axis=0)))
        ends.append(bf(jnp.concatenate([stack(bb * e_end), stack(kdir * e_end)], axis=0)))
        vs.append(bf(stack(v)))
        decay_end.append(jnp.exp(c_last))

    amat = [_dot_nt(ar[g], bk[g]) for g in range(ng)]
    a_ab, a_kr, a_rb = [], [], []
    for g, (n, d, p) in enumerate(groups):
        before = (col < row) if d == 0 else (col > row)
        strict = same & before
        incl = same & (before | (row == col))
        m = amat[g]
        a_ab.append(jnp.where(strict, m[:C2, :C2], 0.0))
        a_kr.append(bf(jnp.concatenate([jnp.where(strict, m[:C2, C2:], 0.0),
                                        jnp.where(incl, m[C2:, C2:], 0.0)], axis=0)))
        a_rb.append(bf(jnp.where(incl, m[C2:, :C2], 0.0)))

    t = [eye + jnp.where(levels[0], a, 0.0) for a in a_ab]
    for lvl in levels[1:]:
        ta = [jnp.dot(bf(t[g]), bf(jnp.where(lvl, a_ab[g], 0.0)), preferred_element_type=F32) for g in range(ng)]
        t = [t[g] + jnp.dot(bf(ta[g]), bf(t[g]), preferred_element_type=F32) for g in range(ng)]

    s_old = [s_ref[n, d, p] for n, d, p in groups]
    from_s = [_dot_nt(ar[g], bf(s_old[g])) for g in range(ng)]
    from_v = [jnp.dot(a_kr[g], vs[g], preferred_element_type=F32) for g in range(ng)]
    u = [jnp.dot(bf(t[g]), bf(from_s[g][:C2] + from_v[g][:C2]), preferred_element_type=F32) for g in range(ng)]
    for g, (n, d, p) in enumerate(groups):
        y_s = from_s[g][C2:] + from_v[g][C2:] + jnp.dot(a_rb[g], bf(u[g]), preferred_element_type=F32)
        y_refs[d][n, :, p * 128:(p + 1) * 128] = y_s[:C] + y_s[C:]
    for g, (n, d, p) in enumerate(groups):
        uv = jnp.concatenate([bf(u[g]), vs[g]], axis=0)
        s_ref[n, d, p] = s_old[g] * decay_end[g] + lax.dot_general(
            uv, ends[g], (((0,), (0,)), ((), ())), preferred_element_type=F32)


def _rwkv_scan(planes, s0):
    B, T, _ = planes.shape
    nc = T // RWKV_CHUNK
    W = RWKV_WIDTH
    nb = SCAN_BATCH_PER_STEP if B % SCAN_BATCH_PER_STEP == 0 else 1
    st = pl.BlockSpec((nb, 2, W // 128, 128, 128), lambda b, i: (b, 0, 0, 0, 0))
    y = jax.ShapeDtypeStruct((B, T, W), F32)
    return pl.pallas_call(
        _rwkv_scan_kernel,
        grid=(B // nb, nc),
        in_specs=[pl.BlockSpec((nb, RWKV_CHUNK, N_PLANES * W), lambda b, i: (b, i, 0)),
                  pl.BlockSpec((nb, RWKV_CHUNK, N_PLANES * W), lambda b, i: (b, nc - 1 - i, 0)),
                  st],
        out_specs=[pl.BlockSpec((nb, RWKV_CHUNK, W), lambda b, i: (b, i, 0)),
                   pl.BlockSpec((nb, RWKV_CHUNK, W), lambda b, i: (b, nc - 1 - i, 0)),
                   st],
        out_shape=[y, y, jax.ShapeDtypeStruct(s0.shape, F32)],
        compiler_params=_cparams(("parallel", "arbitrary")),
        name="rwkv_scan",
    )(planes, planes, s0)


def _layer_weights(l, w_in, shift_mu, w_up, a_up, w_branch, w_out):
    rw, rg, na, ng, df, dg, mg = jnp.split(w_in[l], [int(i) for i in np.cumsum(IN_SIZES)[:-1]], axis=-1)
    pad = jnp.zeros((w_in.shape[1], RWKV_PAD_WIDTH - RWKV_SHIFT_WIDTH), w_in.dtype)
    w_perm = jnp.concatenate([mg, rg, ng, dg, rw, pad, na, df], axis=-1).astype(BF16)
    mu_pad = jnp.pad(shift_mu[l], ((0, 0), (0, RWKV_PAD_WIDTH - RWKV_SHIFT_WIDTH)))
    zl = jnp.zeros((LORA, RWKV_WIDTH), F32)
    wup2 = jnp.stack([jnp.concatenate([w_up[l, 0], zl]), jnp.concatenate([zl, w_up[l, 1]])]).astype(BF16)
    aup2 = jnp.stack([jnp.concatenate([a_up[l, 0], zl]), jnp.concatenate([zl, a_up[l, 1]])]).astype(BF16)
    return w_perm, mu_pad, wup2, aup2, w_branch[l].astype(BF16), w_out[l].astype(BF16)


def kernel(x, c, ctx, c_ctx, w_mod, b_mod, g_pre, g_post, w_in, shift_mu, k_k, k_a, r_k, w0, w_up, a0, a_up,
           ln_x_g, ln_x_b, rpb, lam_q, lam_k, diff_subln, w_branch, w_out):
    B, S, D = x.shape
    C = ctx.shape[1]
    depth = w_in.shape[0]
    tables = _rope_tables(S)
    rows_pad = -(-(B + 1) // 16) * 16
    cvec = jnp.zeros((rows_pad, D), F32).at[:B].set(c).at[B].set(c_ctx)
    mod = _modulation(cvec, w_mod, b_mod)
    hc = ctx
    tm_x = min(INPROJ_ROWS, S)
    for l in range(depth):
        last = l == depth - 1
        lambda_init = 0.8 - 0.6 * math.exp(-0.3 * l)
        w_perm, mu_pad, wup2, aup2, wb, wo = _layer_weights(l, w_in, shift_mu, w_up, a_up, w_branch, w_out)
        mod_x = mod[l, :B].reshape(B, 3, D)
        mod_c = mod[l, B:B + 1].reshape(1, 3, D)
        zg_x, rw_x, na_x, df_x = _inproj(x, mod_x, g_pre[l], w_perm, tables, tm_x)
        ctx_out = _inproj(hc.reshape(1, B * C, D), mod_c, g_pre[l], w_perm, None, B * C)
        zg_c, rw_c, na_c, df_c = (t.reshape(B, C, t.shape[-1]) for t in ctx_out)

        r_k_flat = r_k[l].reshape(RWKV_WIDTH)
        pl_c, bv_c = _rwkv_prep(rw_c, mu_pad, k_k[l], k_a[l], r_k_flat, w0[l], a0[l], wup2, aup2, min(CTX_ROWS, C))
        pl_x, bv_x = _rwkv_prep(rw_x, mu_pad, k_k[l], k_a[l], r_k_flat, w0[l], a0[l], wup2, aup2, PREP_ROWS)
        s_zero = jnp.zeros((B, 2, RWKV_WIDTH // 128, 128, 128), F32)
        yf_c, yb_c, s_ctx = _rwkv_scan(pl_c, s_zero)
        yf_x, yb_x, _ = _rwkv_scan(pl_x, s_ctx)

        o_na_x = _na_attn(na_x, na_c, _na_bias_tables(rpb[l]) * LOG2E, min(NA_ROWS_PER_STEP, S // GRID_W))
        o_df_x = _diff_attn(df_x, df_c, True, lam_q[l], lam_k[l], diff_subln[l], lambda_init,
                            min(DIFF_Q_ROWS, S), min(DIFF_KEY_CHUNK, S // 2))

        x = _merge(x, mod_x, g_post[l], yf_x, yb_x, bv_x, ln_x_g[l], ln_x_b[l], o_na_x, o_df_x, zg_x, wb, wo,
                   MERGE_ROWS)
        if not last:
            o_na_c = _ctx_attn(na_c)
            o_df_c = _diff_attn(df_c, df_c, False, lam_q[l], lam_k[l], diff_subln[l], lambda_init, C, C)
            hc = _merge(hc, mod_c, g_post[l], yf_c, yb_c, bv_c, ln_x_g[l], ln_x_b[l], o_na_c, o_df_c, zg_c, wb, wo,
                        min(CTX_ROWS, C))
    return x
```

```python
import functools
import math

import jax
import jax.numpy as jnp
import numpy as np
from jax import lax
from jax.experimental import pallas as pl
from jax.experimental.pallas import tpu as pltpu

F32 = jnp.float32
BF16 = jnp.bfloat16

DEPTH = 4
GRID_W = 64
RWKV_HEADS = 8
HEAD_DIM = 64
RWKV_WIDTH = 512
LORA = 64
GN_EPS = 64e-5
NA_HEADS = 8
NA_WIN_R = 8
NA_WIN_C = 16
DIFF_HEADS = 4
DIFF_QK_DIM = 64
DIFF_V_DIM = 128
ROPE_THETA = 10000.0
SUBLN_EPS = 1e-5
RMS_EPS = 1e-6
NEG_INF = -1e30
LOG2E = math.log2(math.e)
BRANCH_WIDTH = 512
N_BRANCH = 3

RWKV_SHIFT_WIDTH = 3 * RWKV_WIDTH + 4 * LORA
RWKV_PAD_WIDTH = 2048
TILE_W = 512
TILE_RW, TILE_NA, TILE_DF = 9, 13, 16
Z_WIDTH = 19 * TILE_W
IN_SIZES = (RWKV_SHIFT_WIDTH, 512, 1536, 512, 1536, 512, 3072)

VMEM_LIMIT = 56 * 1024 * 1024
INPROJ_ROWS = 2048
PREP_ROWS = 512
MERGE_ROWS = 512
CTX_ROWS = 256
DIFF_Q_ROWS = 512
DIFF_KEY_CHUNK = 2048
NA_ROWS_PER_STEP = 32
SCAN_BATCH_PER_STEP = 4


def _cparams(sem):
    return pltpu.CompilerParams(dimension_semantics=sem, vmem_limit_bytes=VMEM_LIMIT)


def _sigmoid(x):
    return 1.0 / (1.0 + jnp.exp(-x))


def _silu(x):
    return x * _sigmoid(x)


def _bdot(a, b):
    return jnp.dot(a.astype(BF16), b.astype(BF16), preferred_element_type=F32)


def _mod_kernel(c_ref, w_ref, b_ref, o_ref):
    o_ref[0] = _bdot(_silu(c_ref[...]), w_ref[0]) + b_ref[0]


def _modulation(cvec, w_mod, b_mod):
    L, D, N = w_mod.shape
    R = cvec.shape[0]
    tn = 1024
    return pl.pallas_call(
        _mod_kernel,
        grid=(L, N // tn),
        in_specs=[pl.BlockSpec((R, D), lambda l, n: (0, 0)),
                  pl.BlockSpec((1, D, tn), lambda l, n: (l, 0, n)),
                  pl.BlockSpec((1, 1, tn), lambda l, n: (l, 0, n))],
        out_specs=pl.BlockSpec((1, R, tn), lambda l, n: (l, 0, n)),
        out_shape=jax.ShapeDtypeStruct((L, R, N), F32),
        compiler_params=_cparams(("parallel", "parallel")),
        name="modulation",
    )(cvec, w_mod, b_mod.reshape(L, 1, N))


def _inproj_kernel(*refs, rope, q_scale):
    if rope:
        x_ref, mod_ref, g_ref, w_ref, cos_ref, sa_ref, sb_ref, zg_ref, rw_ref, na_ref, df_ref, h_ref = refs
    else:
        x_ref, mod_ref, g_ref, w_ref, zg_ref, rw_ref, na_ref, df_ref, h_ref = refs
    n = pl.program_id(2)

    @pl.when(n == 0)
    def _():
        x = x_ref[0]
        y = x * lax.rsqrt(jnp.mean(x * x, axis=-1, keepdims=True) + RMS_EPS) * g_ref[...]
        h_ref[...] = (y * (1.0 + mod_ref[0, 1:2, :]) + mod_ref[0, 0:1, :]).astype(BF16)

    mm = lambda: jnp.dot(h_ref[...], w_ref[...], preferred_element_type=F32)

    def rot(t):
        if not rope:
            return t
        w = t.shape[-1]
        tile = lambda r: jnp.concatenate([r[...]] * (w // r.shape[-1]), axis=1)
        return t * tile(cos_ref) + pltpu.roll(t, w - 16, 1) * tile(sa_ref) + pltpu.roll(t, 16, 1) * tile(sb_ref)

    @pl.when(n < TILE_RW)
    def _():
        zg_ref[0] = mm().astype(BF16)

    @pl.when((n >= TILE_RW) & (n < TILE_NA))
    def _():
        rw_ref[0] = mm()

    @pl.when(n == TILE_NA)
    def _():
        na_ref[0] = (mm() * q_scale).astype(BF16)

    @pl.when((n > TILE_NA) & (n < TILE_DF))
    def _():
        na_ref[0] = mm().astype(BF16)

    @pl.when(n == TILE_DF)
    def _():
        df_ref[0] = (rot(mm()) * q_scale).astype(BF16)

    @pl.when(n == TILE_DF + 1)
    def _():
        df_ref[0] = rot(mm()).astype(BF16)

    @pl.when(n == TILE_DF + 2)
    def _():
        df_ref[0] = mm().astype(BF16)


def _inproj(x, mod, g_pre, w_perm, tables, tm):
    B, T, D = x.shape
    tn = TILE_W
    per_batch = mod.shape[0] > 1
    in_specs = [pl.BlockSpec((1, tm, D), lambda b, m, n: (b, m, 0)),
                pl.BlockSpec((1, 3, D), (lambda b, m, n: (b, 0, 0)) if per_batch else (lambda b, m, n: (0, 0, 0))),
                pl.BlockSpec((1, D), lambda b, m, n: (0, 0)),
                pl.BlockSpec((D, tn), lambda b, m, n: (0, n))]
    args = [x, mod, g_pre.reshape(1, D), w_perm]
    if tables is not None:
        in_specs += [pl.BlockSpec((tm, 128), lambda b, m, n: (m, 0))] * 3
        args += list(tables)
    seg = lambda first, count: pl.BlockSpec((1, tm, tn), lambda b, m, n: (b, m, jnp.clip(n - first, 0, count - 1)))
    out = lambda count, dtype: jax.ShapeDtypeStruct((B, T, count * tn), dtype)
    return pl.pallas_call(
        functools.partial(_inproj_kernel, rope=tables is not None, q_scale=HEAD_DIM ** -0.5 * LOG2E),
        grid=(B, T // tm, Z_WIDTH // tn),
        in_specs=in_specs,
        out_specs=[seg(0, TILE_RW), seg(TILE_RW, TILE_NA - TILE_RW), seg(TILE_NA, TILE_DF - TILE_NA),
                   seg(TILE_DF, Z_WIDTH // tn - TILE_DF)],
        out_shape=[out(TILE_RW, BF16), out(TILE_NA - TILE_RW, F32), out(TILE_DF - TILE_NA, BF16),
                   out(Z_WIDTH // tn - TILE_DF, BF16)],
        scratch_shapes=[pltpu.VMEM((tm, D), BF16)],
        compiler_params=_cparams(("parallel", "parallel", "arbitrary")),
        name="inproj",
    )(*args)


def _merge_kernel(x_ref, mod_ref, gpost_ref, yf_ref, yb_ref, bv_ref, lng_ref, lnb_ref, ona_ref, odf_ref,
                  rg_ref, ng_ref, dg_ref, mg_ref, wb_ref, wo_ref, o_ref):
    D = x_ref.shape[-1]
    ones = _head_ones(RWKV_WIDTH)
    y = yf_ref[0] + yb_ref[0]
    mu = _dot_hilo(y, ones) * (1.0 / HEAD_DIM)
    yc = y - mu
    var = _dot_hilo(yc * yc, ones) * (1.0 / HEAD_DIM)
    o_rw = yc * lax.rsqrt(var + GN_EPS) * lng_ref[...] + lnb_ref[...] + bv_ref[0]
    acc = None
    for n, (ob, gate_ref) in enumerate(((o_rw, rg_ref), (ona_ref[0], ng_ref), (odf_ref[0], dg_ref))):
        yb = _bdot(ob * _silu(gate_ref[0].astype(F32)), wb_ref[n])
        term = _sigmoid(mg_ref[0, :, n * D:(n + 1) * D].astype(F32)) * yb
        acc = term if acc is None else acc + term
    y = _bdot(acc, wo_ref[...])
    yn = y * lax.rsqrt(jnp.mean(y * y, axis=-1, keepdims=True) + RMS_EPS) * gpost_ref[...]
    o_ref[0] = x_ref[0] + mod_ref[0, 2:3, :] * yn


def _merge(x, mod, g_post, y_f, y_b, bonus, ln_g, ln_b, o_na, o_df, zg, w_branch, w_out, tm):
    B, T, D = x.shape
    per_batch = mod.shape[0] > 1
    bw = BRANCH_WIDTH
    row = lambda b, m: (b, m, 0)
    col = lambda c: (lambda b, m: (b, m, c))
    return pl.pallas_call(
        _merge_kernel,
        grid=(B, T // tm),
        in_specs=[pl.BlockSpec((1, tm, D), row),
                  pl.BlockSpec((1, 3, D), (lambda b, m: (b, 0, 0)) if per_batch else (lambda b, m: (0, 0, 0))),
                  pl.BlockSpec((1, D), lambda b, m: (0, 0)),
                  pl.BlockSpec((1, tm, bw), row), pl.BlockSpec((1, tm, bw), row), pl.BlockSpec((1, tm, bw), row),
                  pl.BlockSpec((1, bw), lambda b, m: (0, 0)), pl.BlockSpec((1, bw), lambda b, m: (0, 0)),
                  pl.BlockSpec((1, tm, bw), row), pl.BlockSpec((1, tm, bw), row),
                  pl.BlockSpec((1, tm, bw), col(N_BRANCH * D // bw)),
                  pl.BlockSpec((1, tm, bw), col(N_BRANCH * D // bw + 1)),
                  pl.BlockSpec((1, tm, bw), col(N_BRANCH * D // bw + 2)),
                  pl.BlockSpec((1, tm, N_BRANCH * D), col(0)),
                  pl.BlockSpec((N_BRANCH, bw, D), lambda b, m: (0, 0, 0)),
                  pl.BlockSpec((D, D), lambda b, m: (0, 0))],
        out_specs=pl.BlockSpec((1, tm, D), row),
        out_shape=jax.ShapeDtypeStruct((B, T, D), F32),
        compiler_params=_cparams(("parallel", "parallel")),
        name="merge",
    )(x, mod, g_post.reshape(1, D), y_f, y_b, bonus, ln_g.reshape(1, bw), ln_b.reshape(1, bw), o_na, o_df,
      zg, zg, zg, zg, w_branch, w_out)


def _rope_tables(n_tokens):
    t = np.arange(n_tokens)
    axis_dim = DIFF_QK_DIM // 2
    inv = ROPE_THETA ** (-np.arange(0, axis_dim, 2, dtype=np.float32) / axis_dim)
    ar = (t // GRID_W).astype(np.float32)[:, None] * inv
    ac = (t % GRID_W).astype(np.float32)[:, None] * inv
    ang = jnp.asarray(np.concatenate([ar, ar, ac, ac], axis=-1).astype(np.float32))
    cos, sin = jnp.cos(ang), jnp.sin(ang)
    first = (np.arange(DIFF_QK_DIM) % 32) < 16
    sin_a = jnp.where(first, -sin, 0.0)
    sin_b = jnp.where(first, 0.0, sin)
    tile = lambda a: jnp.tile(a, (1, 2))
    return tile(cos), tile(sin_a), tile(sin_b)


def _dot_nt(a, b):
    return lax.dot_general(a, b, (((1,), (1,)), ((), ())), preferred_element_type=F32)


SOFTMAX_ROWS = 128


def _diff_attn_kernel(q_ref, kc_ref, vc_ref, k_ref, v_ref, lq_ref, lk_ref, g_ref, o_ref,
                      sc, pc, s0, s1, p0, p1, al0, al1, m_scr, acc_scr, *, n_main, tk, lambda_init):
    tq = q_ref.shape[1]
    t_first = kc_ref.shape[1]
    q = q_ref[0]
    lo = lax.broadcasted_iota(jnp.int32, (1, 128), 1) < DIFF_QK_DIM
    zero = jnp.zeros_like(q)
    qq = jnp.concatenate([jnp.where(lo, q, zero), jnp.where(lo, zero, q)], axis=0)

    n_sub = 2 * tq // SOFTMAX_ROWS
    sub = lambda i: slice(i * SOFTMAX_ROWS, (i + 1) * SOFTMAX_ROWS)

    def scores(k_chunk, s_scr):
        s_scr[:, :k_chunk.shape[0]] = _dot_nt(qq, k_chunk)

    def softmax(size, first, s_scr, p_scr, al_scr):
        w = min(128, size)
        cols = [slice(c * w, (c + 1) * w) for c in range(size // w)]
        for i in range(n_sub):
            mx = s_scr[sub(i), cols[0]]
            for c in cols[1:]:
                mx = jnp.maximum(mx, s_scr[sub(i), c])
            mn = jnp.broadcast_to(jnp.max(mx, axis=-1, keepdims=True), (SOFTMAX_ROWS, 128))
            if not first:
                mo = m_scr[sub(i), :]
                mn = jnp.maximum(mo, mn)
                al_scr[sub(i), :] = jnp.exp2(mo - mn)
            m_scr[sub(i), :] = mn
        for i in range(n_sub):
            mn = m_scr[sub(i), :w]
            for c in cols:
                p_scr[sub(i), c] = jnp.exp2(s_scr[sub(i), c] - mn).astype(BF16)

    def accumulate(v, first, p_scr, al_scr):
        size = v.shape[0]
        v_ext = jnp.concatenate([v, jnp.ones((size, 128), BF16)], axis=1)
        pv = jnp.dot(p_scr[:, :size], v_ext, preferred_element_type=F32)
        if first:
            acc_scr[...] = pv
        else:
            al = al_scr[...]
            acc_scr[...] = acc_scr[...] * jnp.concatenate([al, al], axis=1) + pv

    scores(kc_ref[0], sc)
    if n_main == 0:
        softmax(t_first, True, sc, pc, None)
        accumulate(vc_ref[0], True, pc, None)
    else:
        def chunk_of(ref, j):
            start = j * tk if isinstance(j, int) else pl.multiple_of(j * tk, tk)
            return ref[0, pl.ds(start, tk), :]

        k_at = functools.partial(chunk_of, k_ref)
        v_at = functools.partial(chunk_of, v_ref)
        scores(k_at(0), s0)
        softmax(t_first, True, sc, pc, None)

        def pair(j, first):
            scores(k_at(j + 1), s1)
            softmax(tk, False, s0, p0, al0)
            if first:
                accumulate(vc_ref[0], True, pc, None)
            else:
                accumulate(v_at(j - 1), False, p1, al1)
            nxt = min(j + 2, n_main - 1) if isinstance(j, int) else jnp.minimum(j + 2, n_main - 1)
            scores(k_at(nxt), s0)
            softmax(tk, False, s1, p1, al1)
            accumulate(v_at(j), False, p0, al0)

        pair(0, True)

        def body(t, _):
            pair(2 * t, False)
            return 0

        lax.fori_loop(1, n_main // 2, body, 0)
        accumulate(v_at(n_main - 1), False, p1, al1)
    lqk = lq_ref[...] * lk_ref[...]
    e = jnp.exp(jnp.sum(lqk, axis=-1, keepdims=True))
    lam = e[0:1] - e[1:2] + lambda_init
    o_maps = acc_scr[:, :DIFF_V_DIM] / acc_scr[:, DIFF_V_DIM:]
    o = o_maps[:tq] - lam * o_maps[tq:]
    o = o * lax.rsqrt(jnp.mean(o * o, axis=-1, keepdims=True) + SUBLN_EPS) * g_ref[...]
    o_ref[0] = o * (1.0 - lambda_init)


def _diff_attn(qkv, qkv_c, with_latent_keys, lam_q, lam_k, subln_g, lambda_init, tq, tk):
    B, T, _ = qkv.shape
    C = qkv_c.shape[1]
    n_main = T // tk if with_latent_keys else 0
    assert n_main % 2 == 0 and (2 * tq) % SOFTMAX_ROWS == 0
    wmax = tk if n_main else 128
    H = DIFF_HEADS
    kv = lambda t, j: pl.BlockSpec((1, t, 128), lambda b, h, m: (b, 0, j * H + h))
    small = lambda r, c: pl.BlockSpec((r, c), lambda b, h, m: (0, 0))
    stat = pltpu.VMEM((2 * tq, 128), F32)
    s_buf = pltpu.VMEM((2 * tq, wmax), F32)
    p_buf = pltpu.VMEM((2 * tq, wmax), BF16)
    return pl.pallas_call(
        functools.partial(_diff_attn_kernel, n_main=n_main, tk=tk, lambda_init=lambda_init),
        grid=(B, H, T // tq),
        in_specs=[pl.BlockSpec((1, tq, 128), lambda b, h, m: (b, m, h)), kv(C, 1), kv(C, 2), kv(T, 1), kv(T, 2),
                  small(2, DIFF_QK_DIM), small(2, DIFF_QK_DIM), small(1, DIFF_V_DIM)],
        out_specs=pl.BlockSpec((1, tq, 128), lambda b, h, m: (b, m, h)),
        out_shape=jax.ShapeDtypeStruct((B, T, H * DIFF_V_DIM), F32),
        scratch_shapes=[pltpu.VMEM((2 * tq, C), F32), pltpu.VMEM((2 * tq, C), BF16), s_buf, s_buf, p_buf, p_buf,
                        stat, stat, stat, pltpu.VMEM((2 * tq, 2 * DIFF_V_DIM), F32)],
        compiler_params=_cparams(("parallel", "parallel", "parallel")),
        name="diff_attn",
    )(qkv, qkv_c, qkv_c, qkv, qkv, lam_q, lam_k, subln_g.reshape(1, DIFF_V_DIM))


def _na_bias_tables(rpb):
    c_idx = np.arange(GRID_W)
    c_start = np.clip(c_idx - NA_WIN_C // 2, 0, GRID_W - NA_WIN_C)
    col_ok = (c_idx[None, :] >= c_start[:, None]) & (c_idx[None, :] < c_start[:, None] + NA_WIN_C)
    dc = np.clip(c_idx[None, :] - c_idx[:, None], -(NA_WIN_C - 1), NA_WIN_C - 1) + (NA_WIN_C - 1)
    H = rpb.shape[0]
    e = jnp.take(rpb, jnp.asarray(dc.reshape(-1)), axis=2).reshape(H, 2 * NA_WIN_R - 1, GRID_W, GRID_W)
    e = jnp.where(col_ok, e, NEG_INF)
    tabs = [e[:, NA_WIN_R - 1 - off:2 * NA_WIN_R - 1 - off].transpose(0, 2, 1, 3).reshape(H, GRID_W, NA_WIN_R * GRID_W)
            for off in range(NA_WIN_R)]
    return jnp.stack(tabs, axis=1)


def _na_kernel(q_ref, k_ref, v_ref, kc_ref, vc_ref, bias_ref, o_ref, *, rq, rows):
    i = pl.program_id(2)
    lo = lax.broadcasted_iota(jnp.int32, (1, 128), 1) < HEAD_DIM
    win = NA_WIN_R * GRID_W
    G2 = 2 * GRID_W
    q = q_ref[0]
    zero = jnp.zeros_like(q)
    q_lo, q_hi = jnp.where(lo, q, zero), jnp.where(lo, zero, q)
    qq = jnp.concatenate([x[rr * GRID_W:(rr + 1) * GRID_W] for rr in range(rq) for x in (q_lo, q_hi)], axis=0)
    kc = kc_ref[0]
    ones = lambda n: jnp.ones((n, 128), BF16)
    s_c = _dot_nt(qq, kc)

    starts, s_nb = [], []
    for rr in range(rq):
        r = i * rq + rr
        r_start = jnp.clip(r - NA_WIN_R // 2, 0, rows - NA_WIN_R)
        off = r - r_start
        start = pl.multiple_of(r_start * GRID_W, GRID_W)
        starts.append(start)
        bias = jnp.concatenate([bias_ref[0, off], bias_ref[1, off]], axis=0)
        s_nb.append(_dot_nt(qq[rr * G2:(rr + 1) * G2], k_ref[0, pl.ds(start, win), :]) + bias)

    def lane_blocks(t):
        w = min(128, t.shape[-1])
        return [t[:, c * w:(c + 1) * w] for c in range(t.shape[-1] // w)]

    p_nb, p_c = [], []
    for rr in range(rq):
        sc = s_c[rr * G2:(rr + 1) * G2]
        blocks = lane_blocks(s_nb[rr])
        mx = blocks[0]
        for b in blocks[1:]:
            mx = jnp.maximum(mx, b)
        m = jnp.maximum(jnp.max(mx, axis=-1, keepdims=True), jnp.max(sc, axis=-1, keepdims=True))
        p_nb.append(jnp.exp2(s_nb[rr] - m).astype(BF16))
        p_c.append(jnp.exp2(sc - m).astype(BF16))

    vc_ext = jnp.concatenate([vc_ref[0], ones(kc.shape[0])], axis=1)
    o_c = jnp.dot(jnp.concatenate(p_c, axis=0), vc_ext, preferred_element_type=F32)
    for rr in range(rq):
        vw_ext = jnp.concatenate([v_ref[0, pl.ds(starts[rr], win), :], ones(win)], axis=1)
        o = jnp.dot(p_nb[rr], vw_ext, preferred_element_type=F32) + o_c[rr * G2:(rr + 1) * G2]
        o = o[:, :128] / o[:, 128:]
        o_ref[0, rr * GRID_W:(rr + 1) * GRID_W, :] = jnp.where(lo, o[:GRID_W], o[GRID_W:])


def _na_attn(qkv, qkv_c, bias, rq):
    B, S, _ = qkv.shape
    C = qkv_c.shape[1]
    rows = S // GRID_W
    npair = NA_HEADS // 2
    full = lambda t, j: pl.BlockSpec((1, t, 128), lambda b, p, m: (b, 0, j * npair + p))
    return pl.pallas_call(
        functools.partial(_na_kernel, rq=rq, rows=rows),
        grid=(B, npair, rows // rq),
        in_specs=[pl.BlockSpec((1, rq * GRID_W, 128), lambda b, p, m: (b, m, p)),
                  full(S, 1), full(S, 2), full(C, 1), full(C, 2),
                  pl.BlockSpec((2, NA_WIN_R, GRID_W, NA_WIN_R * GRID_W), lambda b, p, m: (p, 0, 0, 0))],
        out_specs=pl.BlockSpec((1, rq * GRID_W, 128), lambda b, p, m: (b, m, p)),
        out_shape=jax.ShapeDtypeStruct((B, S, NA_HEADS * HEAD_DIM), F32),
        compiler_params=_cparams(("parallel", "parallel", "parallel")),
        name="na_attn",
    )(qkv, qkv, qkv, qkv_c, qkv_c, bias)


def _ctx_attn_kernel(q_ref, k_ref, v_ref, o_ref):
    lo = lax.broadcasted_iota(jnp.int32, (1, 128), 1) < HEAD_DIM
    q, k, v = q_ref[0], k_ref[0], v_ref[0]
    zero = jnp.zeros_like(q)
    outs = []
    for hl in range(2):
        qh = jnp.where(lo, q, zero) if hl == 0 else jnp.where(lo, zero, q)
        s = _dot_nt(qh, k)
        p = jnp.exp2(s - jnp.max(s, axis=-1, keepdims=True))
        o = jnp.dot(p.astype(BF16), v, preferred_element_type=F32)
        outs.append(o / jnp.sum(p, axis=-1, keepdims=True))
    o_ref[0] = jnp.where(lo, outs[0], outs[1])


def _ctx_attn(qkv):
    B, C, _ = qkv.shape
    npair = NA_HEADS // 2
    blk = lambda j: pl.BlockSpec((1, C, 128), lambda b, p: (b, 0, j * npair + p))
    return pl.pallas_call(
        _ctx_attn_kernel,
        grid=(B, npair),
        in_specs=[blk(0), blk(1), blk(2)],
        out_specs=blk(0),
        out_shape=jax.ShapeDtypeStruct((B, C, NA_HEADS * HEAD_DIM), F32),
        compiler_params=_cparams(("parallel", "parallel")),
        name="ctx_attn",
    )(qkv, qkv, qkv)


PL_R, PL_V, PL_KK = 0, 1, 2
PL_LOGW, PL_KDIR, PL_B = 3, 4, 5
N_PLANES = 9
RWKV_CHUNK = 64


def _split3(x):
    hi = x.astype(BF16)
    r1 = x - hi.astype(F32)
    mid = r1.astype(BF16)
    lo = (r1 - mid.astype(F32)).astype(BF16)
    return hi, mid, lo


def _dot_exact_rhs(m, x):
    hi, mid, lo = _split3(x)
    mb = m.astype(BF16)
    d = lambda t: jnp.dot(mb, t, preferred_element_type=F32)
    return d(hi) + d(mid) + d(lo)


def _dot_hilo(x, m):
    hi = x.astype(BF16)
    lo = (x - hi.astype(F32)).astype(BF16)
    mb = m.astype(BF16)
    d = lambda t: jnp.dot(t, mb, preferred_element_type=F32)
    return d(hi) + d(lo)


def _head_ones(n):
    r = lax.broadcasted_iota(jnp.int32, (n, n), 0) // HEAD_DIM
    c = lax.broadcasted_iota(jnp.int32, (n, n), 1) // HEAD_DIM
    return (r == c).astype(F32)


def _rwkv_prep_kernel(zc_ref, zp_ref, zn_ref, mu_ref, kk_ref, ka_ref, rk_ref, w0_ref, a0_ref, wup_ref, aup_ref,
                      o_ref, bv_ref):
    tm = zc_ref.shape[1]
    m = pl.program_id(1)
    u = zc_ref[0]
    prev_row = jnp.where(m > 0, zp_ref[0, 7:8, :], 0.0)
    next_row = jnp.where(m < pl.num_programs(1) - 1, zn_ref[0, 0:1, :], 0.0)
    rows = lax.broadcasted_iota(jnp.int32, (tm, 1), 0)
    u_prev = jnp.where(rows == 0, prev_row, pltpu.roll(u, 1, 0))
    u_next = jnp.where(rows == tm - 1, next_row, pltpu.roll(u, tm - 1, 0))
    u = u + mu_ref[0:1, :] * (u_prev - u) + mu_ref[1:2, :] * (u_next - u)

    W = RWKV_WIDTH
    r, k, v = u[:, 0:W], u[:, W:2 * W], u[:, 2 * W:3 * W]
    lw = jnp.tanh(u[:, 3 * W:3 * W + 2 * LORA]).astype(BF16)
    la = u[:, 3 * W + 2 * LORA:3 * W + 4 * LORA].astype(BF16)
    ones = _head_ones(W)
    kk = k * kk_ref[...]
    ss = _dot_hilo(kk * kk, ones)
    kk = kk / jnp.maximum(jnp.sqrt(ss), 1e-12)
    o_ref[0, :, PL_R * W:(PL_R + 1) * W] = r
    o_ref[0, :, PL_V * W:(PL_V + 1) * W] = v
    o_ref[0, :, PL_KK * W:(PL_KK + 1) * W] = kk
    k_both = None
    for d in range(2):
        xw = w0_ref[d:d + 1, :] + jnp.dot(lw, wup_ref[d], preferred_element_type=F32)
        logw = -math.exp(-0.5) * _sigmoid(xw)
        a = _sigmoid(a0_ref[d:d + 1, :] + jnp.dot(la, aup_ref[d], preferred_element_type=F32))
        kdir = k * (1.0 + (a - 1.0) * ka_ref[...])
        base = 3 * d
        o_ref[0, :, (PL_LOGW + base) * W:(PL_LOGW + base + 1) * W] = logw
        o_ref[0, :, (PL_KDIR + base) * W:(PL_KDIR + base + 1) * W] = kdir
        o_ref[0, :, (PL_B + base) * W:(PL_B + base + 1) * W] = kk * a
        k_both = kdir if k_both is None else k_both + kdir
    bv_ref[0] = _dot_hilo(r * k_both * rk_ref[...], ones) * v


def _rwkv_prep(z, mu_pad, k_k, k_a, r_k, w0, a0, wup2, aup2, tm):
    B, T, wz = z.shape
    nb = tm // 8
    W = RWKV_WIDTH
    const = lambda shape: pl.BlockSpec(shape, lambda b, m: (0,) * len(shape))
    return pl.pallas_call(
        _rwkv_prep_kernel,
        grid=(B, T // tm),
        in_specs=[pl.BlockSpec((1, tm, wz), lambda b, m: (b, m, 0)),
                  pl.BlockSpec((1, 8, wz), lambda b, m: (b, jnp.maximum(m * nb - 1, 0), 0)),
                  pl.BlockSpec((1, 8, wz), lambda b, m: (b, jnp.minimum((m + 1) * nb, T // 8 - 1), 0)),
                  const((2, wz)), const((1, W)), const((1, W)), const((1, W)), const((2, W)), const((2, W)),
                  const((2, 2 * LORA, W)), const((2, 2 * LORA, W))],
        out_specs=[pl.BlockSpec((1, tm, N_PLANES * W), lambda b, m: (b, m, 0)),
                   pl.BlockSpec((1, tm, W), lambda b, m: (b, m, 0))],
        out_shape=[jax.ShapeDtypeStruct((B, T, N_PLANES * W), F32), jax.ShapeDtypeStruct((B, T, W), F32)],
        compiler_params=_cparams(("parallel", "parallel")),
        name="rwkv_prep",
    )(z, z, z, mu_pad, k_k.reshape(1, W), k_a.reshape(1, W), r_k.reshape(1, W), w0, a0, wup2, aup2)


def _rwkv_scan_kernel(pf_ref, pb_ref, s0_ref, yf_ref, yb_ref, s_ref):
    C = RWKV_CHUNK
    W = RWKV_WIDTH
    C2 = 2 * C

    @pl.when(pl.program_id(1) == 0)
    def _():
        s_ref[...] = s0_ref[...]

    lo = lax.broadcasted_iota(jnp.int32, (1, 128), 1) < HEAD_DIM
    row = lax.broadcasted_iota(jnp.int32, (C2, C2), 0)
    col = lax.broadcasted_iota(jnp.int32, (C2, C2), 1)
    same = (row // C) == (col // C)
    eye = (row == col).astype(F32)
    tri_r = lax.broadcasted_iota(jnp.int32, (C, C), 0)
    tri_c = lax.broadcasted_iota(jnp.int32, (C, C), 1)
    levels = [((row >> k) == (col >> k)) & ((row >> (k - 1)) != (col >> (k - 1))) for k in range(1, 7)]
    p_refs, y_refs = (pf_ref, pb_ref), (yf_ref, yb_ref)
    nb = pf_ref.shape[0]
    groups = [(n, d, p) for n in range(nb) for d in range(2) for p in range(W // 128)]
    ng = len(groups)
    bf = lambda t: t.astype(BF16)

    def stack(x):
        return jnp.concatenate([jnp.where(lo, x, 0.0), jnp.where(lo, 0.0, x)], axis=0)

    c_dir = {}
    for d in range(2):
        cum = ((tri_c <= tri_r) if d == 0 else (tri_c >= tri_r)).astype(F32)
        for n in range(nb):
            logw_all = p_refs[d][n, :, (PL_LOGW + 3 * d) * W:(PL_LOGW + 3 * d + 1) * W]
            c_dir[n, d] = _dot_exact_rhs(cum, logw_all)

    ar, bk, ends, vs, decay_end = [], [], [], [], []
    for n, d, p in groups:
        get = lambda plane: p_refs[d][n, :, plane * W + p * 128:plane * W + (p + 1) * 128]
        r, v, kk = get(PL_R), get(PL_V), get(PL_KK)
        logw, kdir, bb = get(PL_LOGW + 3 * d), get(PL_KDIR + 3 * d), get(PL_B + 3 * d)
        c = c_dir[n, d][:, p * 128:(p + 1) * 128]
        last = C - 1 if d == 0 else 0
        c_last = c[last:last + 1, :]
        e_neg = jnp.exp(-c)
        e_end = jnp.exp(c_last - c)
        ar.append(bf(jnp.concatenate([stack(-kk * jnp.exp(c - logw)), stack(r * jnp.exp(c))], axis=0)))
        bk.append(bf(jnp.concatenate([stack(bb * e_neg), stack(kdir * e_neg)], axis=0)))
        ends.append(bf(jnp.concatenate([stack(bb * e_end), stack(kdir * e_end)], axis=0)))
        vs.append(bf(stack(v)))
        decay_end.append(jnp.exp(c_last))

    amat = [_dot_nt(ar[g], bk[g]) for g in range(ng)]
    a_ab, a_kr, a_rb = [], [], []
    for g, (n, d, p) in enumerate(groups):
        before = (col < row) if d == 0 else (col > row)
        strict = same & before
        incl = same & (before | (row == col))
        m = amat[g]
        a_ab.append(jnp.where(strict, m[:C2, :C2], 0.0))
        a_kr.append(bf(jnp.concatenate([jnp.where(strict, m[:C2, C2:], 0.0),
                                        jnp.where(incl, m[C2:, C2:], 0.0)], axis=0)))
        a_rb.append(bf(jnp.where(incl, m[C2:, :C2], 0.0)))

    t = [eye + jnp.where(levels[0], a, 0.0) for a in a_ab]
    for lvl in levels[1:]:
        ta = [jnp.dot(bf(t[g]), bf(jnp.where(lvl, a_ab[g], 0.0)), preferred_element_type=F32) for g in range(ng)]
        t = [t[g] + jnp.dot(bf(ta[g]), bf(t[g]), preferred_element_type=F32) for g in range(ng)]

    s_old = [s_ref[n, d, p] for n, d, p in groups]
    from_s = [_dot_nt(ar[g], bf(s_old[g])) for g in range(ng)]
    from_v = [jnp.dot(a_kr[g], vs[g], preferred_element_type=F32) for g in range(ng)]
    u = [jnp.dot(bf(t[g]), bf(from_s[g][:C2] + from_v[g][:C2]), preferred_element_type=F32) for g in range(ng)]
    for g, (n, d, p) in enumerate(groups):
        y_s = from_s[g][C2:] + from_v[g][C2:] + jnp.dot(a_rb[g], bf(u[g]), preferred_element_type=F32)
        y_refs[d][n, :, p * 128:(p + 1) * 128] = y_s[:C] + y_s[C:]
    for g, (n, d, p) in enumerate(groups):
        uv = jnp.concatenate([bf(u[g]), vs[g]], axis=0)
        s_ref[n, d, p] = s_old[g] * decay_end[g] + lax.dot_general(
            uv, ends[g], (((0,), (0,)), ((), ())), preferred_element_type=F32)


def _rwkv_scan(planes, s0):
    B, T, _ = planes.shape
    nc = T // RWKV_CHUNK
    W = RWKV_WIDTH
    nb = SCAN_BATCH_PER_STEP if B % SCAN_BATCH_PER_STEP == 0 else 1
    st = pl.BlockSpec((nb, 2, W // 128, 128, 128), lambda b, i: (b, 0, 0, 0, 0))
    y = jax.ShapeDtypeStruct((B, T, W), F32)
    return pl.pallas_call(
        _rwkv_scan_kernel,
        grid=(B // nb, nc),
        in_specs=[pl.BlockSpec((nb, RWKV_CHUNK, N_PLANES * W), lambda b, i: (b, i, 0)),
                  pl.BlockSpec((nb, RWKV_CHUNK, N_PLANES * W), lambda b, i: (b, nc - 1 - i, 0)),
                  st],
        out_specs=[pl.BlockSpec((nb, RWKV_CHUNK, W), lambda b, i: (b, i, 0)),
                   pl.BlockSpec((nb, RWKV_CHUNK, W), lambda b, i: (b, nc - 1 - i, 0)),
                   st],
        out_shape=[y, y, jax.ShapeDtypeStruct(s0.shape, F32)],
        compiler_params=_cparams(("parallel", "arbitrary")),
        name="rwkv_scan",
    )(planes, planes, s0)


def _layer_weights(l, w_in, shift_mu, w_up, a_up, w_branch, w_out):
    rw, rg, na, ng, df, dg, mg = jnp.split(w_in[l], [int(i) for i in np.cumsum(IN_SIZES)[:-1]], axis=-1)
    pad = jnp.zeros((w_in.shape[1], RWKV_PAD_WIDTH - RWKV_SHIFT_WIDTH), w_in.dtype)
    w_perm = jnp.concatenate([mg, rg, ng, dg, rw, pad, na, df], axis=-1).astype(BF16)
    mu_pad = jnp.pad(shift_mu[l], ((0, 0), (0, RWKV_PAD_WIDTH - RWKV_SHIFT_WIDTH)))
    zl = jnp.zeros((LORA, RWKV_WIDTH), F32)
    wup2 = jnp.stack([jnp.concatenate([w_up[l, 0], zl]), jnp.concatenate([zl, w_up[l, 1]])]).astype(BF16)
    aup2 = jnp.stack([jnp.concatenate([a_up[l, 0], zl]), jnp.concatenate([zl, a_up[l, 1]])]).astype(BF16)
    return w_perm, mu_pad, wup2, aup2, w_branch[l].astype(BF16), w_out[l].astype(BF16)


def kernel(x, c, ctx, c_ctx, w_mod, b_mod, g_pre, g_post, w_in, shift_mu, k_k, k_a, r_k, w0, w_up, a0, a_up,
           ln_x_g, ln_x_b, rpb, lam_q, lam_k, diff_subln, w_branch, w_out):
    B, S, D = x.shape
    C = ctx.shape[1]
    depth = w_in.shape[0]
    tables = _rope_tables(S)
    rows_pad = -(-(B + 1) // 16) * 16
    cvec = jnp.zeros((rows_pad, D), F32).at[:B].set(c).at[B].set(c_ctx)
    mod = _modulation(cvec, w_mod, b_mod)
    hc = ctx
    tm_x = min(INPROJ_ROWS, S)
    for l in range(depth):
        last = l == depth - 1
        lambda_init = 0.8 - 0.6 * math.exp(-0.3 * l)
        w_perm, mu_pad, wup2, aup2, wb, wo = _layer_weights(l, w_in, shift_mu, w_up, a_up, w_branch, w_out)
        mod_x = mod[l, :B].reshape(B, 3, D)
        mod_c = mod[l, B:B + 1].reshape(1, 3, D)
        zg_x, rw_x, na_x, df_x = _inproj(x, mod_x, g_pre[l], w_perm, tables, tm_x)
        ctx_out = _inproj(hc.reshape(1, B * C, D), mod_c, g_pre[l], w_perm, None, B * C)
        zg_c, rw_c, na_c, df_c = (t.reshape(B, C, t.shape[-1]) for t in ctx_out)

        r_k_flat = r_k[l].reshape(RWKV_WIDTH)
        pl_c, bv_c = _rwkv_prep(rw_c, mu_pad, k_k[l], k_a[l], r_k_flat, w0[l], a0[l], wup2, aup2, min(CTX_ROWS, C))
        pl_x, bv_x = _rwkv_prep(rw_x, mu_pad, k_k[l], k_a[l], r_k_flat, w0[l], a0[l], wup2, aup2, PREP_ROWS)
        s_zero = jnp.zeros((B, 2, RWKV_WIDTH // 128, 128, 128), F32)
        yf_c, yb_c, s_ctx = _rwkv_scan(pl_c, s_zero)
        yf_x, yb_x, _ = _rwkv_scan(pl_x, s_ctx)

        o_na_x = _na_attn(na_x, na_c, _na_bias_tables(rpb[l]) * LOG2E, min(NA_ROWS_PER_STEP, S // GRID_W))
        o_df_x = _diff_attn(df_x, df_c, True, lam_q[l], lam_k[l], diff_subln[l], lambda_init,
                            min(DIFF_Q_ROWS, S), min(DIFF_KEY_CHUNK, S // 2))

        x = _merge(x, mod_x, g_post[l], yf_x, yb_x, bv_x, ln_x_g[l], ln_x_b[l], o_na_x, o_df_x, zg_x, wb, wo,
                   MERGE_ROWS)
        if not last:
            o_na_c = _ctx_attn(na_c)
            o_df_c = _diff_attn(df_c, df_c, False, lam_q[l], lam_k[l], diff_subln[l], lambda_init, C, C)
            hc = _merge(hc, mod_c, g_post[l], yf_c, yb_c, bv_c, ln_x_g[l], ln_x_b[l], o_na_c, o_df_c, zg_c, wb, wo,
                        min(CTX_ROWS, C))
    return x
```

```python
import functools
import math

import jax
import jax.numpy as jnp
import numpy as np
from jax import lax
from jax.experimental import pallas as pl
from jax.experimental.pallas import tpu as pltpu

F32 = jnp.float32
BF16 = jnp.bfloat16

DEPTH = 4
GRID_W = 64
RWKV_HEADS = 8
HEAD_DIM = 64
RWKV_WIDTH = 512
LORA = 64
GN_EPS = 64e-5
NA_HEADS = 8
NA_WIN_R = 8
NA_WIN_C = 16
DIFF_HEADS = 4
DIFF_QK_DIM = 64
DIFF_V_DIM = 128
ROPE_THETA = 10000.0
SUBLN_EPS = 1e-5
RMS_EPS = 1e-6
NEG_INF = -1e30
LOG2E = math.log2(math.e)
BRANCH_WIDTH = 512
N_BRANCH = 3

RWKV_SHIFT_WIDTH = 3 * RWKV_WIDTH + 4 * LORA
RWKV_PAD_WIDTH = 2048
TILE_W = 512
TILE_RW, TILE_NA, TILE_DF = 9, 13, 16
Z_WIDTH = 19 * TILE_W
IN_SIZES = (RWKV_SHIFT_WIDTH, 512, 1536, 512, 1536, 512, 3072)

VMEM_LIMIT = 56 * 1024 * 1024
INPROJ_ROWS = 2048
PREP_ROWS = 512
MERGE_ROWS = 512
CTX_ROWS = 256
DIFF_Q_ROWS = 1024
DIFF_KEY_CHUNK = 1024
NA_ROWS_PER_STEP = 32
SCAN_BATCH_PER_STEP = 4


def _cparams(sem):
    return pltpu.CompilerParams(dimension_semantics=sem, vmem_limit_bytes=VMEM_LIMIT)


def _sigmoid(x):
    return 1.0 / (1.0 + jnp.exp(-x))


def _silu(x):
    return x * _sigmoid(x)


def _bdot(a, b):
    return jnp.dot(a.astype(BF16), b.astype(BF16), preferred_element_type=F32)


def _mod_kernel(c_ref, w_ref, b_ref, o_ref):
    o_ref[0] = _bdot(_silu(c_ref[...]), w_ref[0]) + b_ref[0]


def _modulation(cvec, w_mod, b_mod):
    L, D, N = w_mod.shape
    R = cvec.shape[0]
    tn = 1024
    return pl.pallas_call(
        _mod_kernel,
        grid=(L, N // tn),
        in_specs=[pl.BlockSpec((R, D), lambda l, n: (0, 0)),
                  pl.BlockSpec((1, D, tn), lambda l, n: (l, 0, n)),
                  pl.BlockSpec((1, 1, tn), lambda l, n: (l, 0, n))],
        out_specs=pl.BlockSpec((1, R, tn), lambda l, n: (l, 0, n)),
        out_shape=jax.ShapeDtypeStruct((L, R, N), F32),
        compiler_params=_cparams(("parallel", "parallel")),
        name="modulation",
    )(cvec, w_mod, b_mod.reshape(L, 1, N))


def _inproj_kernel(*refs, rope, q_scale):
    if rope:
        x_ref, mod_ref, g_ref, w_ref, cos_ref, sa_ref, sb_ref, zg_ref, rw_ref, na_ref, df_ref, h_ref = refs
    else:
        x_ref, mod_ref, g_ref, w_ref, zg_ref, rw_ref, na_ref, df_ref, h_ref = refs
    n = pl.program_id(2)

    @pl.when(n == 0)
    def _():
        x = x_ref[0]
        y = x * lax.rsqrt(jnp.mean(x * x, axis=-1, keepdims=True) + RMS_EPS) * g_ref[...]
        h_ref[...] = (y * (1.0 + mod_ref[0, 1:2, :]) + mod_ref[0, 0:1, :]).astype(BF16)

    mm = lambda: jnp.dot(h_ref[...], w_ref[...], preferred_element_type=F32)

    def rot(t):
        if not rope:
            return t
        w = t.shape[-1]
        tile = lambda r: jnp.concatenate([r[...]] * (w // r.shape[-1]), axis=1)
        return t * tile(cos_ref) + pltpu.roll(t, w - 16, 1) * tile(sa_ref) + pltpu.roll(t, 16, 1) * tile(sb_ref)

    @pl.when(n < TILE_RW)
    def _():
        zg_ref[0] = mm().astype(BF16)

    @pl.when((n >= TILE_RW) & (n < TILE_NA))
    def _():
        rw_ref[0] = mm()

    @pl.when(n == TILE_NA)
    def _():
        na_ref[0] = (mm() * q_scale).astype(BF16)

    @pl.when((n > TILE_NA) & (n < TILE_DF))
    def _():
        na_ref[0] = mm().astype(BF16)

    @pl.when(n == TILE_DF)
    def _():
        df_ref[0] = (rot(mm()) * q_scale).astype(BF16)

    @pl.when(n == TILE_DF + 1)
    def _():
        df_ref[0] = rot(mm()).astype(BF16)

    @pl.when(n == TILE_DF + 2)
    def _():
        df_ref[0] = mm().astype(BF16)


def _inproj(x, mod, g_pre, w_perm, tables, tm):
    B, T, D = x.shape
    tn = TILE_W
    per_batch = mod.shape[0] > 1
    in_specs = [pl.BlockSpec((1, tm, D), lambda b, m, n: (b, m, 0)),
                pl.BlockSpec((1, 3, D), (lambda b, m, n: (b, 0, 0)) if per_batch else (lambda b, m, n: (0, 0, 0))),
                pl.BlockSpec((1, D), lambda b, m, n: (0, 0)),
                pl.BlockSpec((D, tn), lambda b, m, n: (0, n))]
    args = [x, mod, g_pre.reshape(1, D), w_perm]
    if tables is not None:
        in_specs += [pl.BlockSpec((tm, 128), lambda b, m, n: (m, 0))] * 3
        args += list(tables)
    seg = lambda first, count: pl.BlockSpec((1, tm, tn), lambda b, m, n: (b, m, jnp.clip(n - first, 0, count - 1)))
    out = lambda count, dtype: jax.ShapeDtypeStruct((B, T, count * tn), dtype)
    return pl.pallas_call(
        functools.partial(_inproj_kernel, rope=tables is not None, q_scale=HEAD_DIM ** -0.5 * LOG2E),
        grid=(B, T // tm, Z_WIDTH // tn),
        in_specs=in_specs,
        out_specs=[seg(0, TILE_RW), seg(TILE_RW, TILE_NA - TILE_RW), seg(TILE_NA, TILE_DF - TILE_NA),
                   seg(TILE_DF, Z_WIDTH // tn - TILE_DF)],
        out_shape=[out(TILE_RW, BF16), out(TILE_NA - TILE_RW, F32), out(TILE_DF - TILE_NA, BF16),
                   out(Z_WIDTH // tn - TILE_DF, BF16)],
        scratch_shapes=[pltpu.VMEM((tm, D), BF16)],
        compiler_params=_cparams(("parallel", "parallel", "arbitrary")),
        name="inproj",
    )(*args)


def _merge_kernel(x_ref, mod_ref, gpost_ref, yf_ref, yb_ref, bv_ref, lng_ref, lnb_ref, ona_ref, odf_ref,
                  rg_ref, ng_ref, dg_ref, mg_ref, wb_ref, wo_ref, o_ref):
    D = x_ref.shape[-1]
    ones = _head_ones(RWKV_WIDTH)
    y = yf_ref[0] + yb_ref[0]
    mu = _dot_hilo(y, ones) * (1.0 / HEAD_DIM)
    yc = y - mu
    var = _dot_hilo(yc * yc, ones) * (1.0 / HEAD_DIM)
    o_rw = yc * lax.rsqrt(var + GN_EPS) * lng_ref[...] + lnb_ref[...] + bv_ref[0]
    acc = None
    for n, (ob, gate_ref) in enumerate(((o_rw, rg_ref), (ona_ref[0], ng_ref), (odf_ref[0], dg_ref))):
        yb = _bdot(ob * _silu(gate_ref[0].astype(F32)), wb_ref[n])
        term = _sigmoid(mg_ref[0, :, n * D:(n + 1) * D].astype(F32)) * yb
        acc = term if acc is None else acc + term
    y = _bdot(acc, wo_ref[...])
    yn = y * lax.rsqrt(jnp.mean(y * y, axis=-1, keepdims=True) + RMS_EPS) * gpost_ref[...]
    o_ref[0] = x_ref[0] + mod_ref[0, 2:3, :] * yn


def _merge(x, mod, g_post, y_f, y_b, bonus, ln_g, ln_b, o_na, o_df, zg, w_branch, w_out, tm):
    B, T, D = x.shape
    per_batch = mod.shape[0] > 1
    bw = BRANCH_WIDTH
    row = lambda b, m: (b, m, 0)
    col = lambda c: (lambda b, m: (b, m, c))
    return pl.pallas_call(
        _merge_kernel,
        grid=(B, T // tm),
        in_specs=[pl.BlockSpec((1, tm, D), row),
                  pl.BlockSpec((1, 3, D), (lambda b, m: (b, 0, 0)) if per_batch else (lambda b, m: (0, 0, 0))),
                  pl.BlockSpec((1, D), lambda b, m: (0, 0)),
                  pl.BlockSpec((1, tm, bw), row), pl.BlockSpec((1, tm, bw), row), pl.BlockSpec((1, tm, bw), row),
                  pl.BlockSpec((1, bw), lambda b, m: (0, 0)), pl.BlockSpec((1, bw), lambda b, m: (0, 0)),
                  pl.BlockSpec((1, tm, bw), row), pl.BlockSpec((1, tm, bw), row),
                  pl.BlockSpec((1, tm, bw), col(N_BRANCH * D // bw)),
                  pl.BlockSpec((1, tm, bw), col(N_BRANCH * D // bw + 1)),
                  pl.BlockSpec((1, tm, bw), col(N_BRANCH * D // bw + 2)),
                  pl.BlockSpec((1, tm, N_BRANCH * D), col(0)),
                  pl.BlockSpec((N_BRANCH, bw, D), lambda b, m: (0, 0, 0)),
                  pl.BlockSpec((D, D), lambda b, m: (0, 0))],
        out_specs=pl.BlockSpec((1, tm, D), row),
        out_shape=jax.ShapeDtypeStruct((B, T, D), F32),
        compiler_params=_cparams(("parallel", "parallel")),
        name="merge",
    )(x, mod, g_post.reshape(1, D), y_f, y_b, bonus, ln_g.reshape(1, bw), ln_b.reshape(1, bw), o_na, o_df,
      zg, zg, zg, zg, w_branch, w_out)


def _rope_tables(n_tokens):
    t = np.arange(n_tokens)
    axis_dim = DIFF_QK_DIM // 2
    inv = ROPE_THETA ** (-np.arange(0, axis_dim, 2, dtype=np.float32) / axis_dim)
    ar = (t // GRID_W).astype(np.float32)[:, None] * inv
    ac = (t % GRID_W).astype(np.float32)[:, None] * inv
    ang = jnp.asarray(np.concatenate([ar, ar, ac, ac], axis=-1).astype(np.float32))
    cos, sin = jnp.cos(ang), jnp.sin(ang)
    first = (np.arange(DIFF_QK_DIM) % 32) < 16
    sin_a = jnp.where(first, -sin, 0.0)
    sin_b = jnp.where(first, 0.0, sin)
    tile = lambda a: jnp.tile(a, (1, 2))
    return tile(cos), tile(sin_a), tile(sin_b)


def _dot_nt(a, b):
    return lax.dot_general(a, b, (((1,), (1,)), ((), ())), preferred_element_type=F32)


SOFTMAX_ROWS = 128


def _diff_attn_kernel(q_ref, kc_ref, vc_ref, k_ref, v_ref, lq_ref, lk_ref, g_ref, o_ref,
                      sc, pc, s0, s1, p0, p1, al0, al1, m_scr, acc_scr, *, n_main, tk, lambda_init):
    tq = q_ref.shape[1]
    t_first = kc_ref.shape[1]
    q = q_ref[0]
    lo = lax.broadcasted_iota(jnp.int32, (1, 128), 1) < DIFF_QK_DIM
    zero = jnp.zeros_like(q)
    qq = jnp.concatenate([jnp.where(lo, q, zero), jnp.where(lo, zero, q)], axis=0)

    n_sub = 2 * tq // SOFTMAX_ROWS
    sub = lambda i: slice(i * SOFTMAX_ROWS, (i + 1) * SOFTMAX_ROWS)

    def scores(k_chunk, s_scr, col=0):
        s_scr[:, col:col + k_chunk.shape[0]] = _dot_nt(qq, k_chunk)

    def softmax(size, first, s_scr, p_scr, al_scr):
        w = 128 if size % 128 == 0 else 64
        cols = [slice(c * w, (c + 1) * w) for c in range(size // w)]
        for i in range(n_sub):
            mx = s_scr[sub(i), cols[0]]
            for c in cols[1:]:
                mx = jnp.maximum(mx, s_scr[sub(i), c])
            mn = jnp.broadcast_to(jnp.max(mx, axis=-1, keepdims=True), (SOFTMAX_ROWS, 128))
            if not first:
                mo = m_scr[sub(i), :]
                mn = jnp.maximum(mo, mn)
                al_scr[sub(i), :] = jnp.exp2(mo - mn)
            m_scr[sub(i), :] = mn
        for i in range(n_sub):
            mn = m_scr[sub(i), :w]
            for c in cols:
                p_scr[sub(i), c] = jnp.exp2(s_scr[sub(i), c] - mn).astype(BF16)

    def accumulate(v, first, p_scr, al_scr):
        size = v.shape[0]
        v_ext = jnp.concatenate([v, jnp.ones((size, 128), BF16)], axis=1)
        pv = jnp.dot(p_scr[:, :size], v_ext, preferred_element_type=F32)
        if first:
            acc_scr[...] = pv
        else:
            al = al_scr[...]
            acc_scr[...] = acc_scr[...] * jnp.concatenate([al, al], axis=1) + pv

    if n_main == 0:
        scores(kc_ref[0], sc)
        softmax(t_first, True, sc, pc, None)
        accumulate(vc_ref[0], True, pc, None)
    else:
        def chunk_of(ref, j):
            start = j * tk if isinstance(j, int) else pl.multiple_of(j * tk, tk)
            return ref[0, pl.ds(start, tk), :]

        k_at = functools.partial(chunk_of, k_ref)
        v_at = functools.partial(chunk_of, v_ref)
        scores(kc_ref[0], s0)
        scores(k_at(0), s0, t_first)
        scores(k_at(1), s1)
        softmax(t_first + tk, True, s0, p0, None)
        scores(k_at(min(2, n_main - 1)), s0)
        softmax(tk, False, s1, p1, al1)
        accumulate(jnp.concatenate([vc_ref[0], v_at(0)], axis=0), True, p0, None)

        def pair(j):
            scores(k_at(j + 1), s1)
            softmax(tk, False, s0, p0, al0)
            accumulate(v_at(j - 1), False, p1, al1)
            scores(k_at(jnp.minimum(j + 2, n_main - 1)), s0)
            softmax(tk, False, s1, p1, al1)
            accumulate(v_at(j), False, p0, al0)

        def body(t, _):
            pair(2 * t)
            return 0

        lax.fori_loop(1, n_main // 2, body, 0)
        accumulate(v_at(n_main - 1), False, p1, al1)
    lqk = lq_ref[...] * lk_ref[...]
    e = jnp.exp(jnp.sum(lqk, axis=-1, keepdims=True))
    lam = e[0:1] - e[1:2] + lambda_init
    o_maps = acc_scr[:, :DIFF_V_DIM] / acc_scr[:, DIFF_V_DIM:]
    o = o_maps[:tq] - lam * o_maps[tq:]
    o = o * lax.rsqrt(jnp.mean(o * o, axis=-1, keepdims=True) + SUBLN_EPS) * g_ref[...]
    o_ref[0] = o * (1.0 - lambda_init)


def _diff_attn(qkv, qkv_c, with_latent_keys, lam_q, lam_k, subln_g, lambda_init, tq, tk):
    B, T, _ = qkv.shape
    C = qkv_c.shape[1]
    n_main = T // tk if with_latent_keys else 0
    assert n_main % 2 == 0 and (2 * tq) % SOFTMAX_ROWS == 0
    w0, w1 = (C + tk, tk) if n_main else (128, 128)
    H = DIFF_HEADS
    kv = lambda t, j: pl.BlockSpec((1, t, 128), lambda b, h, m: (b, 0, j * H + h))
    small = lambda r, c: pl.BlockSpec((r, c), lambda b, h, m: (0, 0))
    stat = pltpu.VMEM((2 * tq, 128), F32)
    s_buf = lambda w: pltpu.VMEM((2 * tq, w), F32)
    p_buf = lambda w: pltpu.VMEM((2 * tq, w), BF16)
    return pl.pallas_call(
        functools.partial(_diff_attn_kernel, n_main=n_main, tk=tk, lambda_init=lambda_init),
        grid=(B, H, T // tq),
        in_specs=[pl.BlockSpec((1, tq, 128), lambda b, h, m: (b, m, h)), kv(C, 1), kv(C, 2), kv(T, 1), kv(T, 2),
                  small(2, DIFF_QK_DIM), small(2, DIFF_QK_DIM), small(1, DIFF_V_DIM)],
        out_specs=pl.BlockSpec((1, tq, 128), lambda b, h, m: (b, m, h)),
        out_shape=jax.ShapeDtypeStruct((B, T, H * DIFF_V_DIM), F32),
        scratch_shapes=[s_buf(C), p_buf(C), s_buf(w0), s_buf(w1), p_buf(w0), p_buf(w1),
                        stat, stat, stat, pltpu.VMEM((2 * tq, 2 * DIFF_V_DIM), F32)],
        compiler_params=_cparams(("parallel", "parallel", "parallel")),
        name="diff_attn",
    )(qkv, qkv_c, qkv_c, qkv, qkv, lam_q, lam_k, subln_g.reshape(1, DIFF_V_DIM))


def _na_bias_tables(rpb):
    c_idx = np.arange(GRID_W)
    c_start = np.clip(c_idx - NA_WIN_C // 2, 0, GRID_W - NA_WIN_C)
    col_ok = (c_idx[None, :] >= c_start[:, None]) & (c_idx[None, :] < c_start[:, None] + NA_WIN_C)
    dc = np.clip(c_idx[None, :] - c_idx[:, None], -(NA_WIN_C - 1), NA_WIN_C - 1) + (NA_WIN_C - 1)
    H = rpb.shape[0]
    e = jnp.take(rpb, jnp.asarray(dc.reshape(-1)), axis=2).reshape(H, 2 * NA_WIN_R - 1, GRID_W, GRID_W)
    e = jnp.where(col_ok, e, NEG_INF)
    tabs = [e[:, NA_WIN_R - 1 - off:2 * NA_WIN_R - 1 - off].transpose(0, 2, 1, 3).reshape(H, GRID_W, NA_WIN_R * GRID_W)
            for off in range(NA_WIN_R)]
    return jnp.stack(tabs, axis=1)


def _na_kernel(q_ref, k_ref, v_ref, kc_ref, vc_ref, bias_ref, o_ref, *, rq, rows):
    i = pl.program_id(2)
    lo = lax.broadcasted_iota(jnp.int32, (1, 128), 1) < HEAD_DIM
    win = NA_WIN_R * GRID_W
    G2 = 2 * GRID_W
    q = q_ref[0]
    zero = jnp.zeros_like(q)
    q_lo, q_hi = jnp.where(lo, q, zero), jnp.where(lo, zero, q)
    qq = jnp.concatenate([x[rr * GRID_W:(rr + 1) * GRID_W] for rr in range(rq) for x in (q_lo, q_hi)], axis=0)
    kc = kc_ref[0]
    ones = lambda n: jnp.ones((n, 128), BF16)
    s_c = _dot_nt(qq, kc)

    starts, s_nb = [], []
    for rr in range(rq):
        r = i * rq + rr
        r_start = jnp.clip(r - NA_WIN_R // 2, 0, rows - NA_WIN_R)
        off = r - r_start
        start = pl.multiple_of(r_start * GRID_W, GRID_W)
        starts.append(start)
        bias = jnp.concatenate([bias_ref[0, off], bias_ref[1, off]], axis=0)
        s_nb.append(_dot_nt(qq[rr * G2:(rr + 1) * G2], k_ref[0, pl.ds(start, win), :]) + bias)

    def lane_blocks(t):
        w = min(128, t.shape[-1])
        return [t[:, c * w:(c + 1) * w] for c in range(t.shape[-1] // w)]

    p_nb, p_c = [], []
    for rr in range(rq):
        sc = s_c[rr * G2:(rr + 1) * G2]
        blocks = lane_blocks(s_nb[rr])
        mx = blocks[0]
        for b in blocks[1:]:
            mx = jnp.maximum(mx, b)
        m = jnp.maximum(jnp.max(mx, axis=-1, keepdims=True), jnp.max(sc, axis=-1, keepdims=True))
        p_nb.append(jnp.exp2(s_nb[rr] - m).astype(BF16))
        p_c.append(jnp.exp2(sc - m).astype(BF16))

    vc_ext = jnp.concatenate([vc_ref[0], ones(kc.shape[0])], axis=1)
    o_c = jnp.dot(jnp.concatenate(p_c, axis=0), vc_ext, preferred_element_type=F32)
    for rr in range(rq):
        vw_ext = jnp.concatenate([v_ref[0, pl.ds(starts[rr], win), :], ones(win)], axis=1)
        o = jnp.dot(p_nb[rr], vw_ext, preferred_element_type=F32) + o_c[rr * G2:(rr + 1) * G2]
        o = o[:, :128] / o[:, 128:]
        o_ref[0, rr * GRID_W:(rr + 1) * GRID_W, :] = jnp.where(lo, o[:GRID_W], o[GRID_W:])


def _na_attn(qkv, qkv_c, bias, rq):
    B, S, _ = qkv.shape
    C = qkv_c.shape[1]
    rows = S // GRID_W
    npair = NA_HEADS // 2
    full = lambda t, j: pl.BlockSpec((1, t, 128), lambda b, p, m: (b, 0, j * npair + p))
    return pl.pallas_call(
        functools.partial(_na_kernel, rq=rq, rows=rows),
        grid=(B, npair, rows // rq),
        in_specs=[pl.BlockSpec((1, rq * GRID_W, 128), lambda b, p, m: (b, m, p)),
                  full(S, 1), full(S, 2), full(C, 1), full(C, 2),
                  pl.BlockSpec((2, NA_WIN_R, GRID_W, NA_WIN_R * GRID_W), lambda b, p, m: (p, 0, 0, 0))],
        out_specs=pl.BlockSpec((1, rq * GRID_W, 128), lambda b, p, m: (b, m, p)),
        out_shape=jax.ShapeDtypeStruct((B, S, NA_HEADS * HEAD_DIM), F32),
        compiler_params=_cparams(("parallel", "parallel", "parallel")),
        name="na_attn",
    )(qkv, qkv, qkv, qkv_c, qkv_c, bias)


def _ctx_attn_kernel(q_ref, k_ref, v_ref, o_ref):
    lo = lax.broadcasted_iota(jnp.int32, (1, 128), 1) < HEAD_DIM
    q, k, v = q_ref[0], k_ref[0], v_ref[0]
    zero = jnp.zeros_like(q)
    outs = []
    for hl in range(2):
        qh = jnp.where(lo, q, zero) if hl == 0 else jnp.where(lo, zero, q)
        s = _dot_nt(qh, k)
        p = jnp.exp2(s - jnp.max(s, axis=-1, keepdims=True))
        o = jnp.dot(p.astype(BF16), v, preferred_element_type=F32)
        outs.append(o / jnp.sum(p, axis=-1, keepdims=True))
    o_ref[0] = jnp.where(lo, outs[0], outs[1])


def _ctx_attn(qkv):
    B, C, _ = qkv.shape
    npair = NA_HEADS // 2
    blk = lambda j: pl.BlockSpec((1, C, 128), lambda b, p: (b, 0, j * npair + p))
    return pl.pallas_call(
        _ctx_attn_kernel,
        grid=(B, npair),
        in_specs=[blk(0), blk(1), blk(2)],
        out_specs=blk(0),
        out_shape=jax.ShapeDtypeStruct((B, C, NA_HEADS * HEAD_DIM), F32),
        compiler_params=_cparams(("parallel", "parallel")),
        name="ctx_attn",
    )(qkv, qkv, qkv)


PL_R, PL_V, PL_KK = 0, 1, 2
PL_LOGW, PL_KDIR, PL_B = 3, 4, 5
N_PLANES = 9
RWKV_CHUNK = 64


def _split3(x):
    hi = x.astype(BF16)
    r1 = x - hi.astype(F32)
    mid = r1.astype(BF16)
    lo = (r1 - mid.astype(F32)).astype(BF16)
    return hi, mid, lo


def _dot_exact_rhs(m, x):
    hi, mid, lo = _split3(x)
    mb = m.astype(BF16)
    d = lambda t: jnp.dot(mb, t, preferred_element_type=F32)
    return d(hi) + d(mid) + d(lo)


def _dot_hilo(x, m):
    hi = x.astype(BF16)
    lo = (x - hi.astype(F32)).astype(BF16)
    mb = m.astype(BF16)
    d = lambda t: jnp.dot(t, mb, preferred_element_type=F32)
    return d(hi) + d(lo)


def _head_ones(n):
    r = lax.broadcasted_iota(jnp.int32, (n, n), 0) // HEAD_DIM
    c = lax.broadcasted_iota(jnp.int32, (n, n), 1) // HEAD_DIM
    return (r == c).astype(F32)


def _rwkv_prep_kernel(zc_ref, zp_ref, zn_ref, mu_ref, kk_ref, ka_ref, rk_ref, w0_ref, a0_ref, wup_ref, aup_ref,
                      o_ref, bv_ref):
    tm = zc_ref.shape[1]
    m = pl.program_id(1)
    u = zc_ref[0]
    prev_row = jnp.where(m > 0, zp_ref[0, 7:8, :], 0.0)
    next_row = jnp.where(m < pl.num_programs(1) - 1, zn_ref[0, 0:1, :], 0.0)
    rows = lax.broadcasted_iota(jnp.int32, (tm, 1), 0)
    u_prev = jnp.where(rows == 0, prev_row, pltpu.roll(u, 1, 0))
    u_next = jnp.where(rows == tm - 1, next_row, pltpu.roll(u, tm - 1, 0))
    u = u + mu_ref[0:1, :] * (u_prev - u) + mu_ref[1:2, :] * (u_next - u)

    W = RWKV_WIDTH
    r, k, v = u[:, 0:W], u[:, W:2 * W], u[:, 2 * W:3 * W]
    lw = jnp.tanh(u[:, 3 * W:3 * W + 2 * LORA]).astype(BF16)
    la = u[:, 3 * W + 2 * LORA:3 * W + 4 * LORA].astype(BF16)
    ones = _head_ones(W)
    kk = k * kk_ref[...]
    ss = _dot_hilo(kk * kk, ones)
    kk = kk / jnp.maximum(jnp.sqrt(ss), 1e-12)
    o_ref[0, :, PL_R * W:(PL_R + 1) * W] = r
    o_ref[0, :, PL_V * W:(PL_V + 1) * W] = v
    o_ref[0, :, PL_KK * W:(PL_KK + 1) * W] = kk
    k_both = None
    for d in range(2):
        xw = w0_ref[d:d + 1, :] + jnp.dot(lw, wup_ref[d], preferred_element_type=F32)
        logw = -math.exp(-0.5) * _sigmoid(xw)
        a = _sigmoid(a0_ref[d:d + 1, :] + jnp.dot(la, aup_ref[d], preferred_element_type=F32))
        kdir = k * (1.0 + (a - 1.0) * ka_ref[...])
        base = 3 * d
        o_ref[0, :, (PL_LOGW + base) * W:(PL_LOGW + base + 1) * W] = logw
        o_ref[0, :, (PL_KDIR + base) * W:(PL_KDIR + base + 1) * W] = kdir
        o_ref[0, :, (PL_B + base) * W:(PL_B + base + 1) * W] = kk * a
        k_both = kdir if k_both is None else k_both + kdir
    bv_ref[0] = _dot_hilo(r * k_both * rk_ref[...], ones) * v


def _rwkv_prep(z, mu_pad, k_k, k_a, r_k, w0, a0, wup2, aup2, tm):
    B, T, wz = z.shape
    nb = tm // 8
    W = RWKV_WIDTH
    const = lambda shape: pl.BlockSpec(shape, lambda b, m: (0,) * len(shape))
    return pl.pallas_call(
        _rwkv_prep_kernel,
        grid=(B, T // tm),
        in_specs=[pl.BlockSpec((1, tm, wz), lambda b, m: (b, m, 0)),
                  pl.BlockSpec((1, 8, wz), lambda b, m: (b, jnp.maximum(m * nb - 1, 0), 0)),
                  pl.BlockSpec((1, 8, wz), lambda b, m: (b, jnp.minimum((m + 1) * nb, T // 8 - 1), 0)),
                  const((2, wz)), const((1, W)), const((1, W)), const((1, W)), const((2, W)), const((2, W)),
                  const((2, 2 * LORA, W)), const((2, 2 * LORA, W))],
        out_specs=[pl.BlockSpec((1, tm, N_PLANES * W), lambda b, m: (b, m, 0)),
                   pl.BlockSpec((1, tm, W), lambda b, m: (b, m, 0))],
        out_shape=[jax.ShapeDtypeStruct((B, T, N_PLANES * W), F32), jax.ShapeDtypeStruct((B, T, W), F32)],
        compiler_params=_cparams(("parallel", "parallel")),
        name="rwkv_prep",
    )(z, z, z, mu_pad, k_k.reshape(1, W), k_a.reshape(1, W), r_k.reshape(1, W), w0, a0, wup2, aup2)


def _rwkv_scan_kernel(pf_ref, pb_ref, s0_ref, yf_ref, yb_ref, s_ref):
    C = RWKV_CHUNK
    W = RWKV_WIDTH
    C2 = 2 * C

    @pl.when(pl.program_id(1) == 0)
    def _():
        s_ref[...] = s0_ref[...]

    lo = lax.broadcasted_iota(jnp.int32, (1, 128), 1) < HEAD_DIM
    row = lax.broadcasted_iota(jnp.int32, (C2, C2), 0)
    col = lax.broadcasted_iota(jnp.int32, (C2, C2), 1)
    same = (row // C) == (col // C)
    eye = (row == col).astype(F32)
    tri_r = lax.broadcasted_iota(jnp.int32, (C, C), 0)
    tri_c = lax.broadcasted_iota(jnp.int32, (C, C), 1)
    levels = [((row >> k) == (col >> k)) & ((row >> (k - 1)) != (col >> (k - 1))) for k in range(1, 7)]
    p_refs, y_refs = (pf_ref, pb_ref), (yf_ref, yb_ref)
    nb = pf_ref.shape[0]
    groups = [(n, d, p) for n in range(nb) for d in range(2) for p in range(W // 128)]
    ng = len(groups)
    bf = lambda t: t.astype(BF16)

    def stack(x):
        return jnp.concatenate([jnp.where(lo, x, 0.0), jnp.where(lo, 0.0, x)], axis=0)

    c_dir = {}
    for d in range(2):
        cum = ((tri_c <= tri_r) if d == 0 else (tri_c >= tri_r)).astype(F32)
        for n in range(nb):
            logw_all = p_refs[d][n, :, (PL_LOGW + 3 * d) * W:(PL_LOGW + 3 * d + 1) * W]
            c_dir[n, d] = _dot_exact_rhs(cum, logw_all)

    ar, bk, ends, vs, decay_end = [], [], [], [], []
    for n, d, p in groups:
        get = lambda plane: p_refs[d][n, :, plane * W + p * 128:plane * W + (p + 1) * 128]
        r, v, kk = get(PL_R), get(PL_V), get(PL_KK)
        logw, kdir, bb = get(PL_LOGW + 3 * d), get(PL_KDIR + 3 * d), get(PL_B + 3 * d)
        c = c_dir[n, d][:, p * 128:(p + 1) * 128]
        last = C - 1 if d == 0 else 0
        c_last = c[last:last + 1, :]
        e_neg = jnp.exp(-c)
        e_end = jnp.exp(c_last - c)
        ar.append(bf(jnp.concatenate([stack(-kk * jnp.exp(c - logw)), stack(r * jnp.exp(c))], axis=0)))
        bk.append(bf(jnp.concatenate([stack(bb * e_neg), stack(kdir * e_neg)], axis=0)))
        ends.append(bf(jnp.concatenate([stack(bb * e_end), stack(kdir * e_end)], axis=0)))
        vs.append(bf(stack(v)))
        decay_end.append(jnp.exp(c_last))

    amat = [_dot_nt(ar[g], bk[g]) for g in range(ng)]
    a_ab, a_kr, a_rb = [], [], []
    for g, (n, d, p) in enumerate(groups):
        before = (col < row) if d == 0 else (col > row)
        strict = same & before
        incl = same & (before | (row == col))
        m = amat[g]
        a_ab.append(jnp.where(strict, m[:C2, :C2], 0.0))
        a_kr.append(bf(jnp.concatenate([jnp.where(strict, m[:C2, C2:], 0.0),
                                        jnp.where(incl, m[C2:, C2:], 0.0)], axis=0)))
        a_rb.append(bf(jnp.where(incl, m[C2:, :C2], 0.0)))

    t = [eye + jnp.where(levels[0], a, 0.0) for a in a_ab]
    for lvl in levels[1:]:
        ta = [jnp.dot(bf(t[g]), bf(jnp.where(lvl, a_ab[g], 0.0)), preferred_element_type=F32) for g in range(ng)]
        t = [t[g] + jnp.dot(bf(ta[g]), bf(t[g]), preferred_element_type=F32) for g in range(ng)]

    s_old = [s_ref[n, d, p] for n, d, p in groups]
    from_s = [_dot_nt(ar[g], bf(s_old[g])) for g in range(ng)]
    from_v = [jnp.dot(a_kr[g], vs[g], preferred_element_type=F32) for g in range(ng)]
    u = [jnp.dot(bf(t[g]), bf(from_s[g][:C2] + from_v[g][:C2]), preferred_element_type=F32) for g in range(ng)]
    for g, (n, d, p) in enumerate(groups):
        y_s = from_s[g][C2:] + from_v[g][C2:] + jnp.dot(a_rb[g], bf(u[g]), preferred_element_type=F32)
        y_refs[d][n, :, p * 128:(p + 1) * 128] = y_s[:C] + y_s[C:]
    for g, (n, d, p) in enumerate(groups):
        uv = jnp.concatenate([bf(u[g]), vs[g]], axis=0)
        s_ref[n, d, p] = s_old[g] * decay_end[g] + lax.dot_general(
            uv, ends[g], (((0,), (0,)), ((), ())), preferred_element_type=F32)


def _rwkv_scan(planes, s0):
    B, T, _ = planes.shape
    nc = T // RWKV_CHUNK
    W = RWKV_WIDTH
    nb = SCAN_BATCH_PER_STEP if B % SCAN_BATCH_PER_STEP == 0 else 1
    st = pl.BlockSpec((nb, 2, W // 128, 128, 128), lambda b, i: (b, 0, 0, 0, 0))
    y = jax.ShapeDtypeStruct((B, T, W), F32)
    return pl.pallas_call(
        _rwkv_scan_kernel,
        grid=(B // nb, nc),
        in_specs=[pl.BlockSpec((nb, RWKV_CHUNK, N_PLANES * W), lambda b, i: (b, i, 0)),
                  pl.BlockSpec((nb, RWKV_CHUNK, N_PLANES * W), lambda b, i: (b, nc - 1 - i, 0)),
                  st],
        out_specs=[pl.BlockSpec((nb, RWKV_CHUNK, W), lambda b, i: (b, i, 0)),
                   pl.BlockSpec((nb, RWKV_CHUNK, W), lambda b, i: (b, nc - 1 - i, 0)),
                   st],
        out_shape=[y, y, jax.ShapeDtypeStruct(s0.shape, F32)],
        compiler_params=_cparams(("parallel", "arbitrary")),
        name="rwkv_scan",
    )(planes, planes, s0)


def _layer_weights(l, w_in, shift_mu, w_up, a_up, w_branch, w_out):
    rw, rg, na, ng, df, dg, mg = jnp.split(w_in[l], [int(i) for i in np.cumsum(IN_SIZES)[:-1]], axis=-1)
    pad = jnp.zeros((w_in.shape[1], RWKV_PAD_WIDTH - RWKV_SHIFT_WIDTH), w_in.dtype)
    w_perm = jnp.concatenate([mg, rg, ng, dg, rw, pad, na, df], axis=-1).astype(BF16)
    mu_pad = jnp.pad(shift_mu[l], ((0, 0), (0, RWKV_PAD_WIDTH - RWKV_SHIFT_WIDTH)))
    zl = jnp.zeros((LORA, RWKV_WIDTH), F32)
    wup2 = jnp.stack([jnp.concatenate([w_up[l, 0], zl]), jnp.concatenate([zl, w_up[l, 1]])]).astype(BF16)
    aup2 = jnp.stack([jnp.concatenate([a_up[l, 0], zl]), jnp.concatenate([zl, a_up[l, 1]])]).astype(BF16)
    return w_perm, mu_pad, wup2, aup2, w_branch[l].astype(BF16), w_out[l].astype(BF16)


def kernel(x, c, ctx, c_ctx, w_mod, b_mod, g_pre, g_post, w_in, shift_mu, k_k, k_a, r_k, w0, w_up, a0, a_up,
           ln_x_g, ln_x_b, rpb, lam_q, lam_k, diff_subln, w_branch, w_out):
    B, S, D = x.shape
    C = ctx.shape[1]
    depth = w_in.shape[0]
    tables = _rope_tables(S)
    rows_pad = -(-(B + 1) // 16) * 16
    cvec = jnp.zeros((rows_pad, D), F32).at[:B].set(c).at[B].set(c_ctx)
    mod = _modulation(cvec, w_mod, b_mod)
    hc = ctx
    tm_x = min(INPROJ_ROWS, S)
    for l in range(depth):
        last = l == depth - 1
        lambda_init = 0.8 - 0.6 * math.exp(-0.3 * l)
        w_perm, mu_pad, wup2, aup2, wb, wo = _layer_weights(l, w_in, shift_mu, w_up, a_up, w_branch, w_out)
        mod_x = mod[l, :B].reshape(B, 3, D)
        mod_c = mod[l, B:B + 1].reshape(1, 3, D)
        zg_x, rw_x, na_x, df_x = _inproj(x, mod_x, g_pre[l], w_perm, tables, tm_x)
        ctx_out = _inproj(hc.reshape(1, B * C, D), mod_c, g_pre[l], w_perm, None, B * C)
        zg_c, rw_c, na_c, df_c = (t.reshape(B, C, t.shape[-1]) for t in ctx_out)

        r_k_flat = r_k[l].reshape(RWKV_WIDTH)
        pl_c, bv_c = _rwkv_prep(rw_c, mu_pad, k_k[l], k_a[l], r_k_flat, w0[l], a0[l], wup2, aup2, min(CTX_ROWS, C))
        pl_x, bv_x = _rwkv_prep(rw_x, mu_pad, k_k[l], k_a[l], r_k_flat, w0[l], a0[l], wup2, aup2, PREP_ROWS)
        s_zero = jnp.zeros((B, 2, RWKV_WIDTH // 128, 128, 128), F32)
        yf_c, yb_c, s_ctx = _rwkv_scan(pl_c, s_zero)
        yf_x, yb_x, _ = _rwkv_scan(pl_x, s_ctx)

        o_na_x = _na_attn(na_x, na_c, _na_bias_tables(rpb[l]) * LOG2E, min(NA_ROWS_PER_STEP, S // GRID_W))
        o_df_x = _diff_attn(df_x, df_c, True, lam_q[l], lam_k[l], diff_subln[l], lambda_init,
                            min(DIFF_Q_ROWS, S), min(DIFF_KEY_CHUNK, S // 2))

        x = _merge(x, mod_x, g_post[l], yf_x, yb_x, bv_x, ln_x_g[l], ln_x_b[l], o_na_x, o_df_x, zg_x, wb, wo,
                   MERGE_ROWS)
        if not last:
            o_na_c = _ctx_attn(na_c)
            o_df_c = _diff_attn(df_c, df_c, False, lam_q[l], lam_k[l], diff_subln[l], lambda_init, C, C)
            hc = _merge(hc, mod_c, g_post[l], yf_c, yb_c, bv_c, ln_x_g[l], ln_x_b[l], o_na_c, o_df_c, zg_c, wb, wo,
                        min(CTX_ROWS, C))
    return x
```

```python
import functools
import math

import jax
import jax.numpy as jnp
import numpy as np
from jax import lax
from jax.experimental import pallas as pl
from jax.experimental.pallas import tpu as pltpu

F32 = jnp.float32
BF16 = jnp.bfloat16

GRID_W = 64
HEAD_DIM = 64
RWKV_WIDTH = 512
LORA = 64
GN_EPS = 64e-5
NA_HEADS = 8
NA_WIN_R = 8
NA_WIN_C = 16
DIFF_HEADS = 4
DIFF_QK_DIM = 64
DIFF_V_DIM = 128
ROPE_THETA = 10000.0
SUBLN_EPS = 1e-5
RMS_EPS = 1e-6
NEG_INF = -1e30
LOG2E = math.log2(math.e)
BRANCH_WIDTH = 512
N_BRANCH = 3

RWKV_SHIFT_WIDTH = 3 * RWKV_WIDTH + 4 * LORA
RWKV_PAD_WIDTH = 2048
TILE_W = 512
TILE_RW, TILE_NA, TILE_DF = 9, 13, 16
Z_WIDTH = 19 * TILE_W
IN_SIZES = (RWKV_SHIFT_WIDTH, 512, 1536, 512, 1536, 512, 3072)

VMEM_LIMIT = 56 * 1024 * 1024
INPROJ_ROWS = 2048
PREP_ROWS = 512
MERGE_ROWS = 512
CTX_ROWS = 256
DIFF_Q_ROWS = 1024
DIFF_KEY_CHUNK = 1024
NA_ROWS_PER_STEP = 32
SCAN_BATCH_PER_STEP = 4


def _cparams(sem):
    return pltpu.CompilerParams(dimension_semantics=sem, vmem_limit_bytes=VMEM_LIMIT)


def _sigmoid(x):
    return 1.0 / (1.0 + jnp.exp(-x))


def _silu(x):
    return x * _sigmoid(x)


def _bdot(a, b):
    return jnp.dot(a.astype(BF16), b.astype(BF16), preferred_element_type=F32)


def _mod_kernel(c_ref, w_ref, b_ref, o_ref):
    o_ref[0] = _bdot(_silu(c_ref[...]), w_ref[0]) + b_ref[0]


def _modulation(cvec, w_mod, b_mod):
    L, D, N = w_mod.shape
    R = cvec.shape[0]
    tn = 1024
    return pl.pallas_call(
        _mod_kernel,
        grid=(L, N // tn),
        in_specs=[pl.BlockSpec((R, D), lambda l, n: (0, 0)),
                  pl.BlockSpec((1, D, tn), lambda l, n: (l, 0, n)),
                  pl.BlockSpec((1, 1, tn), lambda l, n: (l, 0, n))],
        out_specs=pl.BlockSpec((1, R, tn), lambda l, n: (l, 0, n)),
        out_shape=jax.ShapeDtypeStruct((L, R, N), F32),
        compiler_params=_cparams(("parallel", "parallel")),
        name="modulation",
    )(cvec, w_mod, b_mod.reshape(L, 1, N))


def _inproj_kernel(*refs, rope, q_scale):
    if rope:
        x_ref, mod_ref, g_ref, w_ref, cos_ref, sa_ref, sb_ref, zg_ref, rw_ref, na_ref, df_ref, h_ref = refs
    else:
        x_ref, mod_ref, g_ref, w_ref, zg_ref, rw_ref, na_ref, df_ref, h_ref = refs
    n = pl.program_id(2)

    @pl.when(n == 0)
    def _():
        x = x_ref[0]
        y = x * lax.rsqrt(jnp.mean(x * x, axis=-1, keepdims=True) + RMS_EPS) * g_ref[...]
        h_ref[...] = (y * (1.0 + mod_ref[0, 1:2, :]) + mod_ref[0, 0:1, :]).astype(BF16)

    mm = lambda: jnp.dot(h_ref[...], w_ref[...], preferred_element_type=F32)

    def rot(t):
        if not rope:
            return t
        w = t.shape[-1]
        tile = lambda r: jnp.concatenate([r[...]] * (w // r.shape[-1]), axis=1)
        return t * tile(cos_ref) + pltpu.roll(t, w - 16, 1) * tile(sa_ref) + pltpu.roll(t, 16, 1) * tile(sb_ref)

    @pl.when(n < TILE_RW)
    def _():
        zg_ref[0] = mm().astype(BF16)

    @pl.when((n >= TILE_RW) & (n < TILE_NA))
    def _():
        rw_ref[0] = mm()

    @pl.when(n == TILE_NA)
    def _():
        na_ref[0] = (mm() * q_scale).astype(BF16)

    @pl.when((n > TILE_NA) & (n < TILE_DF))
    def _():
        na_ref[0] = mm().astype(BF16)

    @pl.when(n == TILE_DF)
    def _():
        df_ref[0] = (rot(mm()) * q_scale).astype(BF16)

    @pl.when(n == TILE_DF + 1)
    def _():
        df_ref[0] = rot(mm()).astype(BF16)

    @pl.when(n == TILE_DF + 2)
    def _():
        df_ref[0] = mm().astype(BF16)


def _inproj(x, mod, g_pre, w_perm, tables, tm):
    B, T, D = x.shape
    tn = TILE_W
    per_batch = mod.shape[0] > 1
    in_specs = [pl.BlockSpec((1, tm, D), lambda b, m, n: (b, m, 0)),
                pl.BlockSpec((1, 3, D), (lambda b, m, n: (b, 0, 0)) if per_batch else (lambda b, m, n: (0, 0, 0))),
                pl.BlockSpec((1, D), lambda b, m, n: (0, 0)),
                pl.BlockSpec((D, tn), lambda b, m, n: (0, n))]
    args = [x, mod, g_pre.reshape(1, D), w_perm]
    if tables is not None:
        in_specs += [pl.BlockSpec((tm, 128), lambda b, m, n: (m, 0))] * 3
        args += list(tables)
    seg = lambda first, count: pl.BlockSpec((1, tm, tn), lambda b, m, n: (b, m, jnp.clip(n - first, 0, count - 1)))
    out = lambda count, dtype: jax.ShapeDtypeStruct((B, T, count * tn), dtype)
    return pl.pallas_call(
        functools.partial(_inproj_kernel, rope=tables is not None, q_scale=HEAD_DIM ** -0.5 * LOG2E),
        grid=(B, T // tm, Z_WIDTH // tn),
        in_specs=in_specs,
        out_specs=[seg(0, TILE_RW), seg(TILE_RW, TILE_NA - TILE_RW), seg(TILE_NA, TILE_DF - TILE_NA),
                   seg(TILE_DF, Z_WIDTH // tn - TILE_DF)],
        out_shape=[out(TILE_RW, BF16), out(TILE_NA - TILE_RW, F32), out(TILE_DF - TILE_NA, BF16),
                   out(Z_WIDTH // tn - TILE_DF, BF16)],
        scratch_shapes=[pltpu.VMEM((tm, D), BF16)],
        compiler_params=_cparams(("parallel", "parallel", "arbitrary")),
        name="inproj",
    )(*args)


def _merge_kernel(x_ref, mod_ref, gpost_ref, yf_ref, yb_ref, bv_ref, lng_ref, lnb_ref, ona_ref, odf_ref,
                  rg_ref, ng_ref, dg_ref, mg_ref, wb_ref, wo_ref, o_ref):
    D = x_ref.shape[-1]
    ones = _head_ones(RWKV_WIDTH)
    y = yf_ref[0] + yb_ref[0]
    mu = _dot_hilo(y, ones) * (1.0 / HEAD_DIM)
    yc = y - mu
    var = _dot_hilo(yc * yc, ones) * (1.0 / HEAD_DIM)
    o_rw = yc * lax.rsqrt(var + GN_EPS) * lng_ref[...] + lnb_ref[...] + bv_ref[0]
    acc = None
    for n, (ob, gate_ref) in enumerate(((o_rw, rg_ref), (ona_ref[0], ng_ref), (odf_ref[0], dg_ref))):
        yb = _bdot(ob * _silu(gate_ref[0].astype(F32)), wb_ref[n])
        term = _sigmoid(mg_ref[0, :, n * D:(n + 1) * D].astype(F32)) * yb
        acc = term if acc is None else acc + term
    y = _bdot(acc, wo_ref[...])
    yn = y * lax.rsqrt(jnp.mean(y * y, axis=-1, keepdims=True) + RMS_EPS) * gpost_ref[...]
    o_ref[0] = x_ref[0] + mod_ref[0, 2:3, :] * yn


def _merge(x, mod, g_post, y_f, y_b, bonus, ln_g, ln_b, o_na, o_df, zg, w_branch, w_out, tm):
    B, T, D = x.shape
    per_batch = mod.shape[0] > 1
    bw = BRANCH_WIDTH
    row = lambda b, m: (b, m, 0)
    col = lambda c: (lambda b, m: (b, m, c))
    return pl.pallas_call(
        _merge_kernel,
        grid=(B, T // tm),
        in_specs=[pl.BlockSpec((1, tm, D), row),
                  pl.BlockSpec((1, 3, D), (lambda b, m: (b, 0, 0)) if per_batch else (lambda b, m: (0, 0, 0))),
                  pl.BlockSpec((1, D), lambda b, m: (0, 0)),
                  pl.BlockSpec((1, tm, bw), row), pl.BlockSpec((1, tm, bw), row), pl.BlockSpec((1, tm, bw), row),
                  pl.BlockSpec((1, bw), lambda b, m: (0, 0)), pl.BlockSpec((1, bw), lambda b, m: (0, 0)),
                  pl.BlockSpec((1, tm, bw), row), pl.BlockSpec((1, tm, bw), row),
                  pl.BlockSpec((1, tm, bw), col(N_BRANCH * D // bw)),
                  pl.BlockSpec((1, tm, bw), col(N_BRANCH * D // bw + 1)),
                  pl.BlockSpec((1, tm, bw), col(N_BRANCH * D // bw + 2)),
                  pl.BlockSpec((1, tm, N_BRANCH * D), col(0)),
                  pl.BlockSpec((N_BRANCH, bw, D), lambda b, m: (0, 0, 0)),
                  pl.BlockSpec((D, D), lambda b, m: (0, 0))],
        out_specs=pl.BlockSpec((1, tm, D), row),
        out_shape=jax.ShapeDtypeStruct((B, T, D), F32),
        compiler_params=_cparams(("parallel", "parallel")),
        name="merge",
    )(x, mod, g_post.reshape(1, D), y_f, y_b, bonus, ln_g.reshape(1, bw), ln_b.reshape(1, bw), o_na, o_df,
      zg, zg, zg, zg, w_branch, w_out)


def _rope_tables(n_tokens):
    t = np.arange(n_tokens)
    axis_dim = DIFF_QK_DIM // 2
    inv = ROPE_THETA ** (-np.arange(0, axis_dim, 2, dtype=np.float32) / axis_dim)
    ar = (t // GRID_W).astype(np.float32)[:, None] * inv
    ac = (t % GRID_W).astype(np.float32)[:, None] * inv
    ang = jnp.asarray(np.concatenate([ar, ar, ac, ac], axis=-1).astype(np.float32))
    cos, sin = jnp.cos(ang), jnp.sin(ang)
    first = (np.arange(DIFF_QK_DIM) % 32) < 16
    sin_a = jnp.where(first, -sin, 0.0)
    sin_b = jnp.where(first, 0.0, sin)
    tile = lambda a: jnp.tile(a, (1, 2))
    return tile(cos), tile(sin_a), tile(sin_b)


def _dot_nt(a, b):
    return lax.dot_general(a, b, (((1,), (1,)), ((), ())), preferred_element_type=F32)


SOFTMAX_ROWS = 128


def _diff_attn_kernel(q_ref, kc_ref, vc_ref, k_ref, v_ref, lq_ref, lk_ref, g_ref, o_ref,
                      sc, pc, s0, s1, p0, p1, al0, al1, m_scr, acc_scr, *, n_main, tk, lambda_init):
    tq = q_ref.shape[1]
    t_first = kc_ref.shape[1]
    q = q_ref[0]
    lo = lax.broadcasted_iota(jnp.int32, (1, 128), 1) < DIFF_QK_DIM
    zero = jnp.zeros_like(q)
    qq = jnp.concatenate([jnp.where(lo, q, zero), jnp.where(lo, zero, q)], axis=0)

    n_sub = 2 * tq // SOFTMAX_ROWS
    sub = lambda i: slice(i * SOFTMAX_ROWS, (i + 1) * SOFTMAX_ROWS)

    def scores(k_chunk, s_scr, col=0):
        s_scr[:, col:col + k_chunk.shape[0]] = _dot_nt(qq, k_chunk)

    def softmax(size, first, s_scr, p_scr, al_scr):
        w = 128 if size % 128 == 0 else 64
        cols = [slice(c * w, (c + 1) * w) for c in range(size // w)]
        for i in range(n_sub):
            mx = s_scr[sub(i), cols[0]]
            for c in cols[1:]:
                mx = jnp.maximum(mx, s_scr[sub(i), c])
            mn = jnp.broadcast_to(jnp.max(mx, axis=-1, keepdims=True), (SOFTMAX_ROWS, 128))
            if not first:
                mo = m_scr[sub(i), :]
                mn = jnp.maximum(mo, mn)
                al_scr[sub(i), :] = jnp.exp2(mo - mn)
            m_scr[sub(i), :] = mn
        for i in range(n_sub):
            mn = m_scr[sub(i), :w]
            for c in cols:
                p_scr[sub(i), c] = jnp.exp2(s_scr[sub(i), c] - mn).astype(BF16)

    def accumulate(v, first, p_scr, al_scr):
        size = v.shape[0]
        v_ext = jnp.concatenate([v, jnp.ones((size, 128), BF16)], axis=1)
        pv = jnp.dot(p_scr[:, :size], v_ext, preferred_element_type=F32)
        if first:
            acc_scr[...] = pv
        else:
            al = al_scr[...]
            acc_scr[...] = acc_scr[...] * jnp.concatenate([al, al], axis=1) + pv

    if n_main == 0:
        scores(kc_ref[0], sc)
        softmax(t_first, True, sc, pc, None)
        accumulate(vc_ref[0], True, pc, None)
    else:
        def chunk_of(ref, j):
            start = j * tk if isinstance(j, int) else pl.multiple_of(j * tk, tk)
            return ref[0, pl.ds(start, tk), :]

        k_at = functools.partial(chunk_of, k_ref)
        v_at = functools.partial(chunk_of, v_ref)
        scores(kc_ref[0], s0)
        scores(k_at(0), s0, t_first)
        scores(k_at(1), s1)
        softmax(t_first + tk, True, s0, p0, None)
        scores(k_at(min(2, n_main - 1)), s0)
        softmax(tk, False, s1, p1, al1)
        accumulate(jnp.concatenate([vc_ref[0], v_at(0)], axis=0), True, p0, None)

        def pair(j):
            scores(k_at(j + 1), s1)
            softmax(tk, False, s0, p0, al0)
            accumulate(v_at(j - 1), False, p1, al1)
            scores(k_at(jnp.minimum(j + 2, n_main - 1)), s0)
            softmax(tk, False, s1, p1, al1)
            accumulate(v_at(j), False, p0, al0)

        def body(t, _):
            pair(2 * t)
            return 0

        lax.fori_loop(1, n_main // 2, body, 0)
        accumulate(v_at(n_main - 1), False, p1, al1)
    lqk = lq_ref[...] * lk_ref[...]
    e = jnp.exp(jnp.sum(lqk, axis=-1, keepdims=True))
    lam = e[0:1] - e[1:2] + lambda_init
    o_maps = acc_scr[:, :DIFF_V_DIM] / acc_scr[:, DIFF_V_DIM:]
    o = o_maps[:tq] - lam * o_maps[tq:]
    o = o * lax.rsqrt(jnp.mean(o * o, axis=-1, keepdims=True) + SUBLN_EPS) * g_ref[...]
    o_ref[0] = o * (1.0 - lambda_init)


def _diff_attn(qkv, qkv_c, with_latent_keys, lam_q, lam_k, subln_g, lambda_init, tq, tk):
    B, T, _ = qkv.shape
    C = qkv_c.shape[1]
    n_main = T // tk if with_latent_keys else 0
    assert n_main % 2 == 0 and (2 * tq) % SOFTMAX_ROWS == 0
    w0, w1 = (C + tk, tk) if n_main else (128, 128)
    H = DIFF_HEADS
    kv = lambda t, j: pl.BlockSpec((1, t, 128), lambda b, h, m: (b, 0, j * H + h))
    small = lambda r, c: pl.BlockSpec((r, c), lambda b, h, m: (0, 0))
    stat = pltpu.VMEM((2 * tq, 128), F32)
    s_buf = lambda w: pltpu.VMEM((2 * tq, w), F32)
    p_buf = lambda w: pltpu.VMEM((2 * tq, w), BF16)
    return pl.pallas_call(
        functools.partial(_diff_attn_kernel, n_main=n_main, tk=tk, lambda_init=lambda_init),
        grid=(B, H, T // tq),
        in_specs=[pl.BlockSpec((1, tq, 128), lambda b, h, m: (b, m, h)), kv(C, 1), kv(C, 2), kv(T, 1), kv(T, 2),
                  small(2, DIFF_QK_DIM), small(2, DIFF_QK_DIM), small(1, DIFF_V_DIM)],
        out_specs=pl.BlockSpec((1, tq, 128), lambda b, h, m: (b, m, h)),
        out_shape=jax.ShapeDtypeStruct((B, T, H * DIFF_V_DIM), F32),
        scratch_shapes=[s_buf(C), p_buf(C), s_buf(w0), s_buf(w1), p_buf(w0), p_buf(w1),
                        stat, stat, stat, pltpu.VMEM((2 * tq, 2 * DIFF_V_DIM), F32)],
        compiler_params=_cparams(("parallel", "parallel", "parallel")),
        name="diff_attn",
    )(qkv, qkv_c, qkv_c, qkv, qkv, lam_q, lam_k, subln_g.reshape(1, DIFF_V_DIM))


def _na_bias_tables(rpb):
    c_idx = np.arange(GRID_W)
    c_start = np.clip(c_idx - NA_WIN_C // 2, 0, GRID_W - NA_WIN_C)
    col_ok = (c_idx[None, :] >= c_start[:, None]) & (c_idx[None, :] < c_start[:, None] + NA_WIN_C)
    H = rpb.shape[0]
    pad = GRID_W - NA_WIN_C
    ext = jnp.pad(rpb, ((0, 0), (0, 0), (pad, pad)))
    e = jnp.stack([ext[..., GRID_W - 1 - q:2 * GRID_W - 1 - q] for q in range(GRID_W)], axis=2)
    e = jnp.where(col_ok, e, NEG_INF)
    tabs = [e[:, NA_WIN_R - 1 - off:2 * NA_WIN_R - 1 - off].transpose(0, 2, 1, 3).reshape(H, GRID_W, NA_WIN_R * GRID_W)
            for off in range(NA_WIN_R)]
    return jnp.stack(tabs, axis=1)


def _na_kernel(q_ref, k_ref, v_ref, kc_ref, vc_ref, bias_ref, o_ref, *, rq, rows):
    i = pl.program_id(2)
    lo = lax.broadcasted_iota(jnp.int32, (1, 128), 1) < HEAD_DIM
    win = NA_WIN_R * GRID_W
    G2 = 2 * GRID_W
    q = q_ref[0]
    zero = jnp.zeros_like(q)
    q_lo, q_hi = jnp.where(lo, q, zero), jnp.where(lo, zero, q)
    qq = jnp.concatenate([x[rr * GRID_W:(rr + 1) * GRID_W] for rr in range(rq) for x in (q_lo, q_hi)], axis=0)
    kc = kc_ref[0]
    ones = lambda n: jnp.ones((n, 128), BF16)
    s_c = _dot_nt(qq, kc)

    starts, s_nb = [], []
    for rr in range(rq):
        r = i * rq + rr
        r_start = jnp.clip(r - NA_WIN_R // 2, 0, rows - NA_WIN_R)
        off = r - r_start
        start = pl.multiple_of(r_start * GRID_W, GRID_W)
        starts.append(start)
        bias = jnp.concatenate([bias_ref[0, off], bias_ref[1, off]], axis=0)
        s_nb.append(_dot_nt(qq[rr * G2:(rr + 1) * G2], k_ref[0, pl.ds(start, win), :]) + bias)

    def lane_blocks(t):
        w = min(128, t.shape[-1])
        return [t[:, c * w:(c + 1) * w] for c in range(t.shape[-1] // w)]

    p_nb, p_c = [], []
    for rr in range(rq):
        sc = s_c[rr * G2:(rr + 1) * G2]
        blocks = lane_blocks(s_nb[rr])
        mx = blocks[0]
        for b in blocks[1:]:
            mx = jnp.maximum(mx, b)
        m = jnp.maximum(jnp.max(mx, axis=-1, keepdims=True), jnp.max(sc, axis=-1, keepdims=True))
        p_nb.append(jnp.exp2(s_nb[rr] - m).astype(BF16))
        p_c.append(jnp.exp2(sc - m).astype(BF16))

    vc_ext = jnp.concatenate([vc_ref[0], ones(kc.shape[0])], axis=1)
    o_c = jnp.dot(jnp.concatenate(p_c, axis=0), vc_ext, preferred_element_type=F32)
    for rr in range(rq):
        vw_ext = jnp.concatenate([v_ref[0, pl.ds(starts[rr], win), :], ones(win)], axis=1)
        o = jnp.dot(p_nb[rr], vw_ext, preferred_element_type=F32) + o_c[rr * G2:(rr + 1) * G2]
        o = o[:, :128] / o[:, 128:]
        o_ref[0, rr * GRID_W:(rr + 1) * GRID_W, :] = jnp.where(lo, o[:GRID_W], o[GRID_W:])


def _na_attn(qkv, qkv_c, bias, rq):
    B, S, _ = qkv.shape
    C = qkv_c.shape[1]
    rows = S // GRID_W
    npair = NA_HEADS // 2
    full = lambda t, j: pl.BlockSpec((1, t, 128), lambda b, p, m: (b, 0, j * npair + p))
    return pl.pallas_call(
        functools.partial(_na_kernel, rq=rq, rows=rows),
        grid=(B, npair, rows // rq),
        in_specs=[pl.BlockSpec((1, rq * GRID_W, 128), lambda b, p, m: (b, m, p)),
                  full(S, 1), full(S, 2), full(C, 1), full(C, 2),
                  pl.BlockSpec((2, NA_WIN_R, GRID_W, NA_WIN_R * GRID_W), lambda b, p, m: (p, 0, 0, 0))],
        out_specs=pl.BlockSpec((1, rq * GRID_W, 128), lambda b, p, m: (b, m, p)),
        out_shape=jax.ShapeDtypeStruct((B, S, NA_HEADS * HEAD_DIM), F32),
        compiler_params=_cparams(("parallel", "parallel", "parallel")),
        name="na_attn",
    )(qkv, qkv, qkv, qkv_c, qkv_c, bias)


def _ctx_attn_kernel(q_ref, k_ref, v_ref, o_ref):
    lo = lax.broadcasted_iota(jnp.int32, (1, 128), 1) < HEAD_DIM
    q, k, v = q_ref[0], k_ref[0], v_ref[0]
    zero = jnp.zeros_like(q)
    outs = []
    for hl in range(2):
        qh = jnp.where(lo, q, zero) if hl == 0 else jnp.where(lo, zero, q)
        s = _dot_nt(qh, k)
        p = jnp.exp2(s - jnp.max(s, axis=-1, keepdims=True))
        o = jnp.dot(p.astype(BF16), v, preferred_element_type=F32)
        outs.append(o / jnp.sum(p, axis=-1, keepdims=True))
    o_ref[0] = jnp.where(lo, outs[0], outs[1])


def _ctx_attn(qkv):
    B, C, _ = qkv.shape
    npair = NA_HEADS // 2
    blk = lambda j: pl.BlockSpec((1, C, 128), lambda b, p: (b, 0, j * npair + p))
    return pl.pallas_call(
        _ctx_attn_kernel,
        grid=(B, npair),
        in_specs=[blk(0), blk(1), blk(2)],
        out_specs=blk(0),
        out_shape=jax.ShapeDtypeStruct((B, C, NA_HEADS * HEAD_DIM), F32),
        compiler_params=_cparams(("parallel", "parallel")),
        name="ctx_attn",
    )(qkv, qkv, qkv)


PL_R, PL_V, PL_KK = 0, 1, 2
PL_LOGW, PL_KDIR, PL_B = 3, 4, 5
N_PLANES = 9
RWKV_CHUNK = 64


def _split3(x):
    hi = x.astype(BF16)
    r1 = x - hi.astype(F32)
    mid = r1.astype(BF16)
    lo = (r1 - mid.astype(F32)).astype(BF16)
    return hi, mid, lo


def _dot_exact_rhs(m, x):
    hi, mid, lo = _split3(x)
    mb = m.astype(BF16)
    d = lambda t: jnp.dot(mb, t, preferred_element_type=F32)
    return d(hi) + d(mid) + d(lo)


def _dot_hilo(x, m):
    hi = x.astype(BF16)
    lo = (x - hi.astype(F32)).astype(BF16)
    mb = m.astype(BF16)
    d = lambda t: jnp.dot(t, mb, preferred_element_type=F32)
    return d(hi) + d(lo)


def _head_ones(n):
    r = lax.broadcasted_iota(jnp.int32, (n, n), 0) // HEAD_DIM
    c = lax.broadcasted_iota(jnp.int32, (n, n), 1) // HEAD_DIM
    return (r == c).astype(F32)


def _rwkv_prep_kernel(zc_ref, zp_ref, zn_ref, mu_ref, kk_ref, ka_ref, rk_ref, w0_ref, a0_ref, wup_ref, aup_ref,
                      o_ref, bv_ref):
    tm = zc_ref.shape[1]
    m = pl.program_id(1)
    u = zc_ref[0]
    prev_row = jnp.where(m > 0, zp_ref[0, 7:8, :], 0.0)
    next_row = jnp.where(m < pl.num_programs(1) - 1, zn_ref[0, 0:1, :], 0.0)
    rows = lax.broadcasted_iota(jnp.int32, (tm, 1), 0)
    u_prev = jnp.where(rows == 0, prev_row, pltpu.roll(u, 1, 0))
    u_next = jnp.where(rows == tm - 1, next_row, pltpu.roll(u, tm - 1, 0))
    u = u + mu_ref[0:1, :] * (u_prev - u) + mu_ref[1:2, :] * (u_next - u)

    W = RWKV_WIDTH
    r, k, v = u[:, 0:W], u[:, W:2 * W], u[:, 2 * W:3 * W]
    lw = jnp.tanh(u[:, 3 * W:3 * W + 2 * LORA]).astype(BF16)
    la = u[:, 3 * W + 2 * LORA:3 * W + 4 * LORA].astype(BF16)
    ones = _head_ones(W)
    kk = k * kk_ref[...]
    ss = _dot_hilo(kk * kk, ones)
    kk = kk / jnp.maximum(jnp.sqrt(ss), 1e-12)
    o_ref[0, :, PL_R * W:(PL_R + 1) * W] = r
    o_ref[0, :, PL_V * W:(PL_V + 1) * W] = v
    o_ref[0, :, PL_KK * W:(PL_KK + 1) * W] = kk
    k_both = None
    for d in range(2):
        xw = w0_ref[d:d + 1, :] + jnp.dot(lw, wup_ref[d], preferred_element_type=F32)
        logw = -math.exp(-0.5) * _sigmoid(xw)
        a = _sigmoid(a0_ref[d:d + 1, :] + jnp.dot(la, aup_ref[d], preferred_element_type=F32))
        kdir = k * (1.0 + (a - 1.0) * ka_ref[...])
        base = 3 * d
        o_ref[0, :, (PL_LOGW + base) * W:(PL_LOGW + base + 1) * W] = logw
        o_ref[0, :, (PL_KDIR + base) * W:(PL_KDIR + base + 1) * W] = kdir
        o_ref[0, :, (PL_B + base) * W:(PL_B + base + 1) * W] = kk * a
        k_both = kdir if k_both is None else k_both + kdir
    bv_ref[0] = _dot_hilo(r * k_both * rk_ref[...], ones) * v


def _rwkv_prep(z, mu_pad, k_k, k_a, r_k, w0, a0, wup2, aup2, tm):
    B, T, wz = z.shape
    nb = tm // 8
    W = RWKV_WIDTH
    const = lambda shape: pl.BlockSpec(shape, lambda b, m: (0,) * len(shape))
    return pl.pallas_call(
        _rwkv_prep_kernel,
        grid=(B, T // tm),
        in_specs=[pl.BlockSpec((1, tm, wz), lambda b, m: (b, m, 0)),
                  pl.BlockSpec((1, 8, wz), lambda b, m: (b, jnp.maximum(m * nb - 1, 0), 0)),
                  pl.BlockSpec((1, 8, wz), lambda b, m: (b, jnp.minimum((m + 1) * nb, T // 8 - 1), 0)),
                  const((2, wz)), const((1, W)), const((1, W)), const((1, W)), const((2, W)), const((2, W)),
                  const((2, 2 * LORA, W)), const((2, 2 * LORA, W))],
        out_specs=[pl.BlockSpec((1, tm, N_PLANES * W), lambda b, m: (b, m, 0)),
                   pl.BlockSpec((1, tm, W), lambda b, m: (b, m, 0))],
        out_shape=[jax.ShapeDtypeStruct((B, T, N_PLANES * W), F32), jax.ShapeDtypeStruct((B, T, W), F32)],
        compiler_params=_cparams(("parallel", "parallel")),
        name="rwkv_prep",
    )(z, z, z, mu_pad, k_k.reshape(1, W), k_a.reshape(1, W), r_k.reshape(1, W), w0, a0, wup2, aup2)


def _rwkv_scan_kernel(pf_ref, pb_ref, s0_ref, yf_ref, yb_ref, s_ref):
    C = RWKV_CHUNK
    W = RWKV_WIDTH
    C2 = 2 * C

    @pl.when(pl.program_id(1) == 0)
    def _():
        s_ref[...] = s0_ref[...]

    lo = lax.broadcasted_iota(jnp.int32, (1, 128), 1) < HEAD_DIM
    row = lax.broadcasted_iota(jnp.int32, (C2, C2), 0)
    col = lax.broadcasted_iota(jnp.int32, (C2, C2), 1)
    same = (row // C) == (col // C)
    eye = (row == col).astype(F32)
    tri_r = lax.broadcasted_iota(jnp.int32, (C, C), 0)
    tri_c = lax.broadcasted_iota(jnp.int32, (C, C), 1)
    levels = [((row >> k) == (col >> k)) & ((row >> (k - 1)) != (col >> (k - 1))) for k in range(1, 7)]
    p_refs, y_refs = (pf_ref, pb_ref), (yf_ref, yb_ref)
    nb = pf_ref.shape[0]
    groups = [(n, d, p) for n in range(nb) for d in range(2) for p in range(W // 128)]
    ng = len(groups)
    bf = lambda t: t.astype(BF16)

    def stack(x):
        return jnp.concatenate([jnp.where(lo, x, 0.0), jnp.where(lo, 0.0, x)], axis=0)

    c_dir = {}
    for d in range(2):
        cum = ((tri_c <= tri_r) if d == 0 else (tri_c >= tri_r)).astype(F32)
        for n in range(nb):
            logw_all = p_refs[d][n, :, (PL_LOGW + 3 * d) * W:(PL_LOGW + 3 * d + 1) * W]
            c_dir[n, d] = _dot_exact_rhs(cum, logw_all)

    ar, bk, ends, vs, decay_end = [], [], [], [], []
    for n, d, p in groups:
        get = lambda plane: p_refs[d][n, :, plane * W + p * 128:plane * W + (p + 1) * 128]
        r, v, kk = get(PL_R), get(PL_V), get(PL_KK)
        logw, kdir, bb = get(PL_LOGW + 3 * d), get(PL_KDIR + 3 * d), get(PL_B + 3 * d)
        c = c_dir[n, d][:, p * 128:(p + 1) * 128]
        last = C - 1 if d == 0 else 0
        c_last = c[last:last + 1, :]
        e_neg = jnp.exp(-c)
        e_end = jnp.exp(c_last - c)
        ar.append(bf(jnp.concatenate([stack(-kk * jnp.exp(c - logw)), stack(r * jnp.exp(c))], axis=0)))
        bk.append(bf(jnp.concatenate([stack(bb * e_neg), stack(kdir * e_neg)], axis=0)))
        ends.append(bf(jnp.concatenate([stack(bb * e_end), stack(kdir * e_end)], axis=0)))
        vs.append(bf(stack(v)))
        decay_end.append(jnp.exp(c_last))

    amat = [_dot_nt(ar[g], bk[g]) for g in range(ng)]
    a_ab, a_kr, a_rb = [], [], []
    for g, (n, d, p) in enumerate(groups):
        before = (col < row) if d == 0 else (col > row)
        strict = same & before
        incl = same & (before | (row == col))
        m = amat[g]
        a_ab.append(jnp.where(strict, m[:C2, :C2], 0.0))
        a_kr.append(bf(jnp.concatenate([jnp.where(strict, m[:C2, C2:], 0.0),
                                        jnp.where(incl, m[C2:, C2:], 0.0)], axis=0)))
        a_rb.append(bf(jnp.where(incl, m[C2:, :C2], 0.0)))

    t = [eye + jnp.where(levels[0], a, 0.0) for a in a_ab]
    for lvl in levels[1:]:
        ta = [jnp.dot(bf(t[g]), bf(jnp.where(lvl, a_ab[g], 0.0)), preferred_element_type=F32) for g in range(ng)]
        t = [t[g] + jnp.dot(bf(ta[g]), bf(t[g]), preferred_element_type=F32) for g in range(ng)]

    s_old = [s_ref[n, d, p] for n, d, p in groups]
    from_s = [_dot_nt(ar[g], bf(s_old[g])) for g in range(ng)]
    from_v = [jnp.dot(a_kr[g], vs[g], preferred_element_type=F32) for g in range(ng)]
    u = [jnp.dot(bf(t[g]), bf(from_s[g][:C2] + from_v[g][:C2]), preferred_element_type=F32) for g in range(ng)]
    for g, (n, d, p) in enumerate(groups):
        y_s = from_s[g][C2:] + from_v[g][C2:] + jnp.dot(a_rb[g], bf(u[g]), preferred_element_type=F32)
        y_refs[d][n, :, p * 128:(p + 1) * 128] = y_s[:C] + y_s[C:]
    for g, (n, d, p) in enumerate(groups):
        uv = jnp.concatenate([bf(u[g]), vs[g]], axis=0)
        s_ref[n, d, p] = s_old[g] * decay_end[g] + lax.dot_general(
            uv, ends[g], (((0,), (0,)), ((), ())), preferred_element_type=F32)


def _rwkv_scan(planes, s0):
    B, T, _ = planes.shape
    nc = T // RWKV_CHUNK
    W = RWKV_WIDTH
    nb = SCAN_BATCH_PER_STEP if B % SCAN_BATCH_PER_STEP == 0 else 1
    st = pl.BlockSpec((nb, 2, W // 128, 128, 128), lambda b, i: (b, 0, 0, 0, 0))
    y = jax.ShapeDtypeStruct((B, T, W), F32)
    return pl.pallas_call(
        _rwkv_scan_kernel,
        grid=(B // nb, nc),
        in_specs=[pl.BlockSpec((nb, RWKV_CHUNK, N_PLANES * W), lambda b, i: (b, i, 0)),
                  pl.BlockSpec((nb, RWKV_CHUNK, N_PLANES * W), lambda b, i: (b, nc - 1 - i, 0)),
                  st],
        out_specs=[pl.BlockSpec((nb, RWKV_CHUNK, W), lambda b, i: (b, i, 0)),
                   pl.BlockSpec((nb, RWKV_CHUNK, W), lambda b, i: (b, nc - 1 - i, 0)),
                   st],
        out_shape=[y, y, jax.ShapeDtypeStruct(s0.shape, F32)],
        compiler_params=_cparams(("parallel", "arbitrary")),
        name="rwkv_scan",
    )(planes, planes, s0)


def _layer_weights(l, w_in, shift_mu, w_up, a_up, w_branch, w_out):
    rw, rg, na, ng, df, dg, mg = jnp.split(w_in[l], [int(i) for i in np.cumsum(IN_SIZES)[:-1]], axis=-1)
    pad = jnp.zeros((w_in.shape[1], RWKV_PAD_WIDTH - RWKV_SHIFT_WIDTH), w_in.dtype)
    w_perm = jnp.concatenate([mg, rg, ng, dg, rw, pad, na, df], axis=-1).astype(BF16)
    mu_pad = jnp.pad(shift_mu[l], ((0, 0), (0, RWKV_PAD_WIDTH - RWKV_SHIFT_WIDTH)))
    zl = jnp.zeros((LORA, RWKV_WIDTH), F32)
    wup2 = jnp.stack([jnp.concatenate([w_up[l, 0], zl]), jnp.concatenate([zl, w_up[l, 1]])]).astype(BF16)
    aup2 = jnp.stack([jnp.concatenate([a_up[l, 0], zl]), jnp.concatenate([zl, a_up[l, 1]])]).astype(BF16)
    return w_perm, mu_pad, wup2, aup2, w_branch[l].astype(BF16), w_out[l].astype(BF16)


def kernel(x, c, ctx, c_ctx, w_mod, b_mod, g_pre, g_post, w_in, shift_mu, k_k, k_a, r_k, w0, w_up, a0, a_up,
           ln_x_g, ln_x_b, rpb, lam_q, lam_k, diff_subln, w_branch, w_out):
    B, S, D = x.shape
    C = ctx.shape[1]
    depth = w_in.shape[0]
    tables = _rope_tables(S)
    rows_pad = -(-(B + 1) // 16) * 16
    cvec = jnp.zeros((rows_pad, D), F32).at[:B].set(c).at[B].set(c_ctx)
    mod = _modulation(cvec, w_mod, b_mod)
    hc = ctx
    tm_x = min(INPROJ_ROWS, S)
    for l in range(depth):
        last = l == depth - 1
        lambda_init = 0.8 - 0.6 * math.exp(-0.3 * l)
        w_perm, mu_pad, wup2, aup2, wb, wo = _layer_weights(l, w_in, shift_mu, w_up, a_up, w_branch, w_out)
        mod_x = mod[l, :B].reshape(B, 3, D)
        mod_c = mod[l, B:B + 1].reshape(1, 3, D)
        zg_x, rw_x, na_x, df_x = _inproj(x, mod_x, g_pre[l], w_perm, tables, tm_x)
        ctx_out = _inproj(hc.reshape(1, B * C, D), mod_c, g_pre[l], w_perm, None, B * C)
        zg_c, rw_c, na_c, df_c = (t.reshape(B, C, t.shape[-1]) for t in ctx_out)

        r_k_flat = r_k[l].reshape(RWKV_WIDTH)
        pl_c, bv_c = _rwkv_prep(rw_c, mu_pad, k_k[l], k_a[l], r_k_flat, w0[l], a0[l], wup2, aup2, min(CTX_ROWS, C))
        pl_x, bv_x = _rwkv_prep(rw_x, mu_pad, k_k[l], k_a[l], r_k_flat, w0[l], a0[l], wup2, aup2, PREP_ROWS)
        s_zero = jnp.zeros((B, 2, RWKV_WIDTH // 128, 128, 128), F32)
        yf_c, yb_c, s_ctx = _rwkv_scan(pl_c, s_zero)
        yf_x, yb_x, _ = _rwkv_scan(pl_x, s_ctx)

        o_na_x = _na_attn(na_x, na_c, _na_bias_tables(rpb[l]) * LOG2E, min(NA_ROWS_PER_STEP, S // GRID_W))
        o_df_x = _diff_attn(df_x, df_c, True, lam_q[l], lam_k[l], diff_subln[l], lambda_init,
                            min(DIFF_Q_ROWS, S), min(DIFF_KEY_CHUNK, S // 2))

        x = _merge(x, mod_x, g_post[l], yf_x, yb_x, bv_x, ln_x_g[l], ln_x_b[l], o_na_x, o_df_x, zg_x, wb, wo,
                   MERGE_ROWS)
        if not last:
            o_na_c = _ctx_attn(na_c)
            o_df_c = _diff_attn(df_c, df_c, False, lam_q[l], lam_k[l], diff_subln[l], lambda_init, C, C)
            hc = _merge(hc, mod_c, g_post[l], yf_c, yb_c, bv_c, ln_x_g[l], ln_x_b[l], o_na_c, o_df_c, zg_c, wb, wo,
                        min(CTX_ROWS, C))
    return x
```

```python
import functools
import math

import jax
import jax.numpy as jnp
import numpy as np
from jax import lax
from jax.experimental import pallas as pl
from jax.experimental.pallas import tpu as pltpu

F32 = jnp.float32
BF16 = jnp.bfloat16

GRID_W = 64
HEAD_DIM = 64
RWKV_WIDTH = 512
LORA = 64
GN_EPS = 64e-5
NA_HEADS = 8
NA_WIN_R = 8
NA_WIN_C = 16
DIFF_HEADS = 4
DIFF_QK_DIM = 64
DIFF_V_DIM = 128
ROPE_THETA = 10000.0
SUBLN_EPS = 1e-5
RMS_EPS = 1e-6
NEG_INF = -1e30
LOG2E = math.log2(math.e)
BRANCH_WIDTH = 512
N_BRANCH = 3

RWKV_SHIFT_WIDTH = 3 * RWKV_WIDTH + 4 * LORA
RWKV_PAD_WIDTH = 2048
TILE_W = 512
TILE_RW, TILE_NA, TILE_DF = 9, 13, 16
Z_WIDTH = 19 * TILE_W
IN_SIZES = (RWKV_SHIFT_WIDTH, 512, 1536, 512, 1536, 512, 3072)

VMEM_LIMIT = 56 * 1024 * 1024
INPROJ_ROWS = 2048
PREP_ROWS = 512
MERGE_ROWS = 512
CTX_ROWS = 256
DIFF_Q_ROWS = 1024
DIFF_KEY_CHUNK = 1024
NA_ROWS_PER_STEP = 32
SCAN_BATCH_PER_STEP = 4


def _cparams(sem):
    return pltpu.CompilerParams(dimension_semantics=sem, vmem_limit_bytes=VMEM_LIMIT)


def _sigmoid(x):
    return 1.0 / (1.0 + jnp.exp(-x))


def _silu(x):
    return x * _sigmoid(x)


def _bdot(a, b):
    return jnp.dot(a.astype(BF16), b.astype(BF16), preferred_element_type=F32)


def _mod_kernel(c_ref, w_ref, b_ref, o_ref):
    o_ref[0] = _bdot(_silu(c_ref[...]), w_ref[0]) + b_ref[0]


def _modulation(cvec, w_mod, b_mod):
    L, D, N = w_mod.shape
    R = cvec.shape[0]
    tn = 1024
    return pl.pallas_call(
        _mod_kernel,
        grid=(L, N // tn),
        in_specs=[pl.BlockSpec((R, D), lambda l, n: (0, 0)),
                  pl.BlockSpec((1, D, tn), lambda l, n: (l, 0, n)),
                  pl.BlockSpec((1, 1, tn), lambda l, n: (l, 0, n))],
        out_specs=pl.BlockSpec((1, R, tn), lambda l, n: (l, 0, n)),
        out_shape=jax.ShapeDtypeStruct((L, R, N), F32),
        compiler_params=_cparams(("parallel", "parallel")),
        name="modulation",
    )(cvec, w_mod, b_mod.reshape(L, 1, N))


def _inproj_kernel(*refs, rope, q_scale):
    if rope:
        x_ref, mod_ref, g_ref, w_ref, cos_ref, sa_ref, sb_ref, zg_ref, rw_ref, na_ref, df_ref, h_ref = refs
    else:
        x_ref, mod_ref, g_ref, w_ref, zg_ref, rw_ref, na_ref, df_ref, h_ref = refs
    n = pl.program_id(2)

    @pl.when(n == 0)
    def _():
        x = x_ref[0]
        y = x * lax.rsqrt(jnp.mean(x * x, axis=-1, keepdims=True) + RMS_EPS) * g_ref[...]
        h_ref[...] = (y * (1.0 + mod_ref[0, 1:2, :]) + mod_ref[0, 0:1, :]).astype(BF16)

    mm = lambda: jnp.dot(h_ref[...], w_ref[...], preferred_element_type=F32)

    def rot(t):
        if not rope:
            return t
        w = t.shape[-1]
        tile = lambda r: jnp.concatenate([r[...]] * (w // r.shape[-1]), axis=1)
        return t * tile(cos_ref) + pltpu.roll(t, w - 16, 1) * tile(sa_ref) + pltpu.roll(t, 16, 1) * tile(sb_ref)

    @pl.when(n < TILE_RW)
    def _():
        zg_ref[0] = mm().astype(BF16)

    @pl.when((n >= TILE_RW) & (n < TILE_NA))
    def _():
        rw_ref[0] = mm()

    @pl.when(n == TILE_NA)
    def _():
        na_ref[0] = (mm() * q_scale).astype(BF16)

    @pl.when((n > TILE_NA) & (n < TILE_DF))
    def _():
        na_ref[0] = mm().astype(BF16)

    @pl.when(n == TILE_DF)
    def _():
        df_ref[0] = (rot(mm()) * q_scale).astype(BF16)

    @pl.when(n == TILE_DF + 1)
    def _():
        df_ref[0] = rot(mm()).astype(BF16)

    @pl.when(n == TILE_DF + 2)
    def _():
        df_ref[0] = mm().astype(BF16)


def _inproj(x, mod, g_pre, w_perm, tables, tm):
    B, T, D = x.shape
    tn = TILE_W
    per_batch = mod.shape[0] > 1
    in_specs = [pl.BlockSpec((1, tm, D), lambda b, m, n: (b, m, 0)),
                pl.BlockSpec((1, 3, D), (lambda b, m, n: (b, 0, 0)) if per_batch else (lambda b, m, n: (0, 0, 0))),
                pl.BlockSpec((1, D), lambda b, m, n: (0, 0)),
                pl.BlockSpec((D, tn), lambda b, m, n: (0, n))]
    args = [x, mod, g_pre.reshape(1, D), w_perm]
    if tables is not None:
        in_specs += [pl.BlockSpec((tm, 128), lambda b, m, n: (m, 0))] * 3
        args += list(tables)
    seg = lambda first, count: pl.BlockSpec((1, tm, tn), lambda b, m, n: (b, m, jnp.clip(n - first, 0, count - 1)))
    out = lambda count, dtype: jax.ShapeDtypeStruct((B, T, count * tn), dtype)
    return pl.pallas_call(
        functools.partial(_inproj_kernel, rope=tables is not None, q_scale=HEAD_DIM ** -0.5 * LOG2E),
        grid=(B, T // tm, Z_WIDTH // tn),
        in_specs=in_specs,
        out_specs=[seg(0, TILE_RW), seg(TILE_RW, TILE_NA - TILE_RW), seg(TILE_NA, TILE_DF - TILE_NA),
                   seg(TILE_DF, Z_WIDTH // tn - TILE_DF)],
        out_shape=[out(TILE_RW, BF16), out(TILE_NA - TILE_RW, F32), out(TILE_DF - TILE_NA, BF16),
                   out(Z_WIDTH // tn - TILE_DF, BF16)],
        scratch_shapes=[pltpu.VMEM((tm, D), BF16)],
        compiler_params=_cparams(("parallel", "parallel", "arbitrary")),
        name="inproj",
    )(*args)


def _merge_kernel(x_ref, mod_ref, gpost_ref, yf_ref, yb_ref, bv_ref, lng_ref, lnb_ref, ona_ref, odf_ref,
                  rg_ref, ng_ref, dg_ref, mg_ref, wb_ref, wo_ref, o_ref):
    D = x_ref.shape[-1]
    ones = _head_ones(RWKV_WIDTH)
    y = yf_ref[0] + yb_ref[0]
    mu = _dot_hilo(y, ones) * (1.0 / HEAD_DIM)
    yc = y - mu
    var = _dot_hilo(yc * yc, ones) * (1.0 / HEAD_DIM)
    o_rw = yc * lax.rsqrt(var + GN_EPS) * lng_ref[...] + lnb_ref[...] + bv_ref[0]
    acc = None
    for n, (ob, gate_ref) in enumerate(((o_rw, rg_ref), (ona_ref[0], ng_ref), (odf_ref[0], dg_ref))):
        yb = _bdot(ob * _silu(gate_ref[0].astype(F32)), wb_ref[n])
        term = _sigmoid(mg_ref[0, :, n * D:(n + 1) * D].astype(F32)) * yb
        acc = term if acc is None else acc + term
    y = _bdot(acc, wo_ref[...])
    yn = y * lax.rsqrt(jnp.mean(y * y, axis=-1, keepdims=True) + RMS_EPS) * gpost_ref[...]
    o_ref[0] = x_ref[0] + mod_ref[0, 2:3, :] * yn


def _merge(x, mod, g_post, y_f, y_b, bonus, ln_g, ln_b, o_na, o_df, zg, w_branch, w_out, tm):
    B, T, D = x.shape
    per_batch = mod.shape[0] > 1
    bw = BRANCH_WIDTH
    row = lambda b, m: (b, m, 0)
    col = lambda c: (lambda b, m: (b, m, c))
    return pl.pallas_call(
        _merge_kernel,
        grid=(B, T // tm),
        in_specs=[pl.BlockSpec((1, tm, D), row),
                  pl.BlockSpec((1, 3, D), (lambda b, m: (b, 0, 0)) if per_batch else (lambda b, m: (0, 0, 0))),
                  pl.BlockSpec((1, D), lambda b, m: (0, 0)),
                  pl.BlockSpec((1, tm, bw), row), pl.BlockSpec((1, tm, bw), row), pl.BlockSpec((1, tm, bw), row),
                  pl.BlockSpec((1, bw), lambda b, m: (0, 0)), pl.BlockSpec((1, bw), lambda b, m: (0, 0)),
                  pl.BlockSpec((1, tm, bw), row), pl.BlockSpec((1, tm, bw), row),
                  pl.BlockSpec((1, tm, bw), col(N_BRANCH * D // bw)),
                  pl.BlockSpec((1, tm, bw), col(N_BRANCH * D // bw + 1)),
                  pl.BlockSpec((1, tm, bw), col(N_BRANCH * D // bw + 2)),
                  pl.BlockSpec((1, tm, N_BRANCH * D), col(0)),
                  pl.BlockSpec((N_BRANCH, bw, D), lambda b, m: (0, 0, 0)),
                  pl.BlockSpec((D, D), lambda b, m: (0, 0))],
        out_specs=pl.BlockSpec((1, tm, D), row),
        out_shape=jax.ShapeDtypeStruct((B, T, D), F32),
        compiler_params=_cparams(("parallel", "parallel")),
        name="merge",
    )(x, mod, g_post.reshape(1, D), y_f, y_b, bonus, ln_g.reshape(1, bw), ln_b.reshape(1, bw), o_na, o_df,
      zg, zg, zg, zg, w_branch, w_out)


def _rope_tables(n_tokens):
    t = np.arange(n_tokens)
    axis_dim = DIFF_QK_DIM // 2
    inv = ROPE_THETA ** (-np.arange(0, axis_dim, 2, dtype=np.float32) / axis_dim)
    ar = (t // GRID_W).astype(np.float32)[:, None] * inv
    ac = (t % GRID_W).astype(np.float32)[:, None] * inv
    ang = jnp.asarray(np.concatenate([ar, ar, ac, ac], axis=-1).astype(np.float32))
    cos, sin = jnp.cos(ang), jnp.sin(ang)
    first = (np.arange(DIFF_QK_DIM) % 32) < 16
    sin_a = jnp.where(first, -sin, 0.0)
    sin_b = jnp.where(first, 0.0, sin)
    tile = lambda a: jnp.tile(a, (1, 2))
    return tile(cos), tile(sin_a), tile(sin_b)


def _dot_nt(a, b):
    return lax.dot_general(a, b, (((1,), (1,)), ((), ())), preferred_element_type=F32)


SOFTMAX_ROWS = 128


def _diff_attn_kernel(q_ref, kc_ref, vc_ref, k_ref, v_ref, lq_ref, lk_ref, g_ref, o_ref,
                      sc, pc, s0, s1, p0, p1, al0, al1, m_scr, acc_scr, *, n_main, tk, lambda_init):
    tq = q_ref.shape[1]
    t_first = kc_ref.shape[1]
    q = q_ref[0]
    lo = lax.broadcasted_iota(jnp.int32, (1, 128), 1) < DIFF_QK_DIM
    zero = jnp.zeros_like(q)
    qq = jnp.concatenate([jnp.where(lo, q, zero), jnp.where(lo, zero, q)], axis=0)

    n_sub = 2 * tq // SOFTMAX_ROWS
    sub = lambda i: slice(i * SOFTMAX_ROWS, (i + 1) * SOFTMAX_ROWS)

    def scores(k_chunk, s_scr):
        s_scr[:, :k_chunk.shape[0]] = _dot_nt(qq, k_chunk)

    def softmax(size, first, s_scr, p_scr, al_scr):
        w = min(128, size)
        cols = [slice(c * w, (c + 1) * w) for c in range(size // w)]
        for i in range(n_sub):
            mx = s_scr[sub(i), cols[0]]
            for c in cols[1:]:
                mx = jnp.maximum(mx, s_scr[sub(i), c])
            mn = jnp.broadcast_to(jnp.max(mx, axis=-1, keepdims=True), (SOFTMAX_ROWS, 128))
            if not first:
                mo = m_scr[sub(i), :]
                mn = jnp.maximum(mo, mn)
                al_scr[sub(i), :] = jnp.exp2(mo - mn)
            m_scr[sub(i), :] = mn
        for i in range(n_sub):
            mn = m_scr[sub(i), :w]
            for c in cols:
                p_scr[sub(i), c] = jnp.exp2(s_scr[sub(i), c] - mn).astype(BF16)

    def accumulate(v, first, p_scr, al_scr):
        size = v.shape[0]
        v_ext = jnp.concatenate([v, jnp.ones((size, 128), BF16)], axis=1)
        pv = jnp.dot(p_scr[:, :size], v_ext, preferred_element_type=F32)
        if first:
            acc_scr[...] = pv
        else:
            al = al_scr[...]
            acc_scr[...] = acc_scr[...] * jnp.concatenate([al, al], axis=1) + pv

    scores(kc_ref[0], sc)
    if n_main == 0:
        softmax(t_first, True, sc, pc, None)
        accumulate(vc_ref[0], True, pc, None)
    else:
        def chunk_of(ref, j):
            start = j * tk if isinstance(j, int) else pl.multiple_of(j * tk, tk)
            return ref[0, pl.ds(start, tk), :]

        k_at = functools.partial(chunk_of, k_ref)
        v_at = functools.partial(chunk_of, v_ref)
        scores(k_at(0), s0)
        softmax(t_first, True, sc, pc, None)

        def pair(j, first):
            scores(k_at(j + 1), s1)
            softmax(tk, False, s0, p0, al0)
            if first:
                accumulate(vc_ref[0], True, pc, None)
            else:
                accumulate(v_at(j - 1), False, p1, al1)
            nxt = min(j + 2, n_main - 1) if isinstance(j, int) else jnp.minimum(j + 2, n_main - 1)
            scores(k_at(nxt), s0)
            softmax(tk, False, s1, p1, al1)
            accumulate(v_at(j), False, p0, al0)

        pair(0, True)

        def body(t, _):
            pair(2 * t, False)
            return 0

        lax.fori_loop(1, n_main // 2, body, 0)
        accumulate(v_at(n_main - 1), False, p1, al1)
    lqk = lq_ref[...] * lk_ref[...]
    e = jnp.exp(jnp.sum(lqk, axis=-1, keepdims=True))
    lam = e[0:1] - e[1:2] + lambda_init
    o_maps = acc_scr[:, :DIFF_V_DIM] / acc_scr[:, DIFF_V_DIM:]
    o = o_maps[:tq] - lam * o_maps[tq:]
    o = o * lax.rsqrt(jnp.mean(o * o, axis=-1, keepdims=True) + SUBLN_EPS) * g_ref[...]
    o_ref[0] = o * (1.0 - lambda_init)


def _diff_attn(qkv, qkv_c, with_latent_keys, lam_q, lam_k, subln_g, lambda_init, tq, tk):
    B, T, _ = qkv.shape
    C = qkv_c.shape[1]
    n_main = T // tk if with_latent_keys else 0
    assert n_main % 2 == 0 and (2 * tq) % SOFTMAX_ROWS == 0
    wmax = tk if n_main else 128
    H = DIFF_HEADS
    kv = lambda t, j: pl.BlockSpec((1, t, 128), lambda b, h, m: (b, 0, j * H + h))
    small = lambda r, c: pl.BlockSpec((r, c), lambda b, h, m: (0, 0))
    stat = pltpu.VMEM((2 * tq, 128), F32)
    s_buf = pltpu.VMEM((2 * tq, wmax), F32)
    p_buf = pltpu.VMEM((2 * tq, wmax), BF16)
    return pl.pallas_call(
        functools.partial(_diff_attn_kernel, n_main=n_main, tk=tk, lambda_init=lambda_init),
        grid=(B, H, T // tq),
        in_specs=[pl.BlockSpec((1, tq, 128), lambda b, h, m: (b, m, h)), kv(C, 1), kv(C, 2), kv(T, 1), kv(T, 2),
                  small(2, DIFF_QK_DIM), small(2, DIFF_QK_DIM), small(1, DIFF_V_DIM)],
        out_specs=pl.BlockSpec((1, tq, 128), lambda b, h, m: (b, m, h)),
        out_shape=jax.ShapeDtypeStruct((B, T, H * DIFF_V_DIM), F32),
        scratch_shapes=[pltpu.VMEM((2 * tq, C), F32), pltpu.VMEM((2 * tq, C), BF16), s_buf, s_buf, p_buf, p_buf,
                        stat, stat, stat, pltpu.VMEM((2 * tq, 2 * DIFF_V_DIM), F32)],
        compiler_params=_cparams(("parallel", "parallel", "parallel")),
        name="diff_attn",
    )(qkv, qkv_c, qkv_c, qkv, qkv, lam_q, lam_k, subln_g.reshape(1, DIFF_V_DIM))


def _na_bias_tables(rpb):
    c_idx = np.arange(GRID_W)
    c_start = np.clip(c_idx - NA_WIN_C // 2, 0, GRID_W - NA_WIN_C)
    col_ok = (c_idx[None, :] >= c_start[:, None]) & (c_idx[None, :] < c_start[:, None] + NA_WIN_C)
    H = rpb.shape[0]
    pad = GRID_W - NA_WIN_C
    ext = jnp.pad(rpb, ((0, 0), (0, 0), (pad, pad)))
    e = jnp.stack([ext[..., GRID_W - 1 - q:2 * GRID_W - 1 - q] for q in range(GRID_W)], axis=2)
    e = jnp.where(col_ok, e, NEG_INF)
    tabs = [e[:, NA_WIN_R - 1 - off:2 * NA_WIN_R - 1 - off].transpose(0, 2, 1, 3).reshape(H, GRID_W, NA_WIN_R * GRID_W)
            for off in range(NA_WIN_R)]
    return jnp.stack(tabs, axis=1)


def _na_kernel(q_ref, k_ref, v_ref, kc_ref, vc_ref, bias_ref, o_ref, *, rq, rows):
    i = pl.program_id(2)
    lo = lax.broadcasted_iota(jnp.int32, (1, 128), 1) < HEAD_DIM
    win = NA_WIN_R * GRID_W
    G2 = 2 * GRID_W
    q = q_ref[0]
    zero = jnp.zeros_like(q)
    q_lo, q_hi = jnp.where(lo, q, zero), jnp.where(lo, zero, q)
    qq = jnp.concatenate([x[rr * GRID_W:(rr + 1) * GRID_W] for rr in range(rq) for x in (q_lo, q_hi)], axis=0)
    kc = kc_ref[0]
    ones = lambda n: jnp.ones((n, 128), BF16)
    s_c = _dot_nt(qq, kc)

    starts, s_nb = [], []
    for rr in range(rq):
        r = i * rq + rr
        r_start = jnp.clip(r - NA_WIN_R // 2, 0, rows - NA_WIN_R)
        off = r - r_start
        start = pl.multiple_of(r_start * GRID_W, GRID_W)
        starts.append(start)
        bias = jnp.concatenate([bias_ref[0, off], bias_ref[1, off]], axis=0)
        s_nb.append(_dot_nt(qq[rr * G2:(rr + 1) * G2], k_ref[0, pl.ds(start, win), :]) + bias)

    def lane_blocks(t):
        w = min(128, t.shape[-1])
        return [t[:, c * w:(c + 1) * w] for c in range(t.shape[-1] // w)]

    p_nb, p_c = [], []
    for rr in range(rq):
        sc = s_c[rr * G2:(rr + 1) * G2]
        blocks = lane_blocks(s_nb[rr])
        mx = blocks[0]
        for b in blocks[1:]:
            mx = jnp.maximum(mx, b)
        m = jnp.maximum(jnp.max(mx, axis=-1, keepdims=True), jnp.max(sc, axis=-1, keepdims=True))
        p_nb.append(jnp.exp2(s_nb[rr] - m).astype(BF16))
        p_c.append(jnp.exp2(sc - m).astype(BF16))

    vc_ext = jnp.concatenate([vc_ref[0], ones(kc.shape[0])], axis=1)
    o_c = jnp.dot(jnp.concatenate(p_c, axis=0), vc_ext, preferred_element_type=F32)
    for rr in range(rq):
        vw_ext = jnp.concatenate([v_ref[0, pl.ds(starts[rr], win), :], ones(win)], axis=1)
        o = jnp.dot(p_nb[rr], vw_ext, preferred_element_type=F32) + o_c[rr * G2:(rr + 1) * G2]
        o = o[:, :128] / o[:, 128:]
        o_ref[0, rr * GRID_W:(rr + 1) * GRID_W, :] = jnp.where(lo, o[:GRID_W], o[GRID_W:])


def _na_attn(qkv, qkv_c, bias, rq):
    B, S, _ = qkv.shape
    C = qkv_c.shape[1]
    rows = S // GRID_W
    npair = NA_HEADS // 2
    full = lambda t, j: pl.BlockSpec((1, t, 128), lambda b, p, m: (b, 0, j * npair + p))
    return pl.pallas_call(
        functools.partial(_na_kernel, rq=rq, rows=rows),
        grid=(B, npair, rows // rq),
        in_specs=[pl.BlockSpec((1, rq * GRID_W, 128), lambda b, p, m: (b, m, p)),
                  full(S, 1), full(S, 2), full(C, 1), full(C, 2),
                  pl.BlockSpec((2, NA_WIN_R, GRID_W, NA_WIN_R * GRID_W), lambda b, p, m: (p, 0, 0, 0))],
        out_specs=pl.BlockSpec((1, rq * GRID_W, 128), lambda b, p, m: (b, m, p)),
        out_shape=jax.ShapeDtypeStruct((B, S, NA_HEADS * HEAD_DIM), F32),
        compiler_params=_cparams(("parallel", "parallel", "parallel")),
        name="na_attn",
    )(qkv, qkv, qkv, qkv_c, qkv_c, bias)


def _ctx_attn_kernel(q_ref, k_ref, v_ref, o_ref):
    lo = lax.broadcasted_iota(jnp.int32, (1, 128), 1) < HEAD_DIM
    q, k, v = q_ref[0], k_ref[0], v_ref[0]
    zero = jnp.zeros_like(q)
    outs = []
    for hl in range(2):
        qh = jnp.where(lo, q, zero) if hl == 0 else jnp.where(lo, zero, q)
        s = _dot_nt(qh, k)
        p = jnp.exp2(s - jnp.max(s, axis=-1, keepdims=True))
        o = jnp.dot(p.astype(BF16), v, preferred_element_type=F32)
        outs.append(o / jnp.sum(p, axis=-1, keepdims=True))
    o_ref[0] = jnp.where(lo, outs[0], outs[1])


def _ctx_attn(qkv):
    B, C, _ = qkv.shape
    npair = NA_HEADS // 2
    blk = lambda j: pl.BlockSpec((1, C, 128), lambda b, p: (b, 0, j * npair + p))
    return pl.pallas_call(
        _ctx_attn_kernel,
        grid=(B, npair),
        in_specs=[blk(0), blk(1), blk(2)],
        out_specs=blk(0),
        out_shape=jax.ShapeDtypeStruct((B, C, NA_HEADS * HEAD_DIM), F32),
        compiler_params=_cparams(("parallel", "parallel")),
        name="ctx_attn",
    )(qkv, qkv, qkv)


PL_R, PL_V, PL_KK = 0, 1, 2
PL_LOGW, PL_KDIR, PL_B = 3, 4, 5
N_PLANES = 9
RWKV_CHUNK = 64


def _split3(x):
    hi = x.astype(BF16)
    r1 = x - hi.astype(F32)
    mid = r1.astype(BF16)
    lo = (r1 - mid.astype(F32)).astype(BF16)
    return hi, mid, lo


def _dot_exact_rhs(m, x):
    hi, mid, lo = _split3(x)
    mb = m.astype(BF16)
    d = lambda t: jnp.dot(mb, t, preferred_element_type=F32)
    return d(hi) + d(mid) + d(lo)


def _dot_hilo(x, m):
    hi = x.astype(BF16)
    lo = (x - hi.astype(F32)).astype(BF16)
    mb = m.astype(BF16)
    d = lambda t: jnp.dot(t, mb, preferred_element_type=F32)
    return d(hi) + d(lo)


def _head_ones(n):
    r = lax.broadcasted_iota(jnp.int32, (n, n), 0) // HEAD_DIM
    c = lax.broadcasted_iota(jnp.int32, (n, n), 1) // HEAD_DIM
    return (r == c).astype(F32)


def _rwkv_prep_kernel(zc_ref, zp_ref, zn_ref, mu_ref, kk_ref, ka_ref, rk_ref, w0_ref, a0_ref, wup_ref, aup_ref,
                      o_ref, bv_ref):
    tm = zc_ref.shape[1]
    m = pl.program_id(1)
    u = zc_ref[0]
    prev_row = jnp.where(m > 0, zp_ref[0, 7:8, :], 0.0)
    next_row = jnp.where(m < pl.num_programs(1) - 1, zn_ref[0, 0:1, :], 0.0)
    rows = lax.broadcasted_iota(jnp.int32, (tm, 1), 0)
    u_prev = jnp.where(rows == 0, prev_row, pltpu.roll(u, 1, 0))
    u_next = jnp.where(rows == tm - 1, next_row, pltpu.roll(u, tm - 1, 0))
    u = u + mu_ref[0:1, :] * (u_prev - u) + mu_ref[1:2, :] * (u_next - u)

    W = RWKV_WIDTH
    r, k, v = u[:, 0:W], u[:, W:2 * W], u[:, 2 * W:3 * W]
    lw = jnp.tanh(u[:, 3 * W:3 * W + 2 * LORA]).astype(BF16)
    la = u[:, 3 * W + 2 * LORA:3 * W + 4 * LORA].astype(BF16)
    ones = _head_ones(W)
    kk = k * kk_ref[...]
    ss = _dot_hilo(kk * kk, ones)
    kk = kk / jnp.maximum(jnp.sqrt(ss), 1e-12)
    o_ref[0, :, PL_R * W:(PL_R + 1) * W] = r
    o_ref[0, :, PL_V * W:(PL_V + 1) * W] = v
    o_ref[0, :, PL_KK * W:(PL_KK + 1) * W] = kk
    k_both = None
    for d in range(2):
        xw = w0_ref[d:d + 1, :] + jnp.dot(lw, wup_ref[d], preferred_element_type=F32)
        logw = -math.exp(-0.5) * _sigmoid(xw)
        a = _sigmoid(a0_ref[d:d + 1, :] + jnp.dot(la, aup_ref[d], preferred_element_type=F32))
        kdir = k * (1.0 + (a - 1.0) * ka_ref[...])
        base = 3 * d
        o_ref[0, :, (PL_LOGW + base) * W:(PL_LOGW + base + 1) * W] = logw
        o_ref[0, :, (PL_KDIR + base) * W:(PL_KDIR + base + 1) * W] = kdir
        o_ref[0, :, (PL_B + base) * W:(PL_B + base + 1) * W] = kk * a
        k_both = kdir if k_both is None else k_both + kdir
    bv_ref[0] = _dot_hilo(r * k_both * rk_ref[...], ones) * v


def _rwkv_prep(z, mu_pad, k_k, k_a, r_k, w0, a0, wup2, aup2, tm):
    B, T, wz = z.shape
    nb = tm // 8
    W = RWKV_WIDTH
    const = lambda shape: pl.BlockSpec(shape, lambda b, m: (0,) * len(shape))
    return pl.pallas_call(
        _rwkv_prep_kernel,
        grid=(B, T // tm),
        in_specs=[pl.BlockSpec((1, tm, wz), lambda b, m: (b, m, 0)),
                  pl.BlockSpec((1, 8, wz), lambda b, m: (b, jnp.maximum(m * nb - 1, 0), 0)),
                  pl.BlockSpec((1, 8, wz), lambda b, m: (b, jnp.minimum((m + 1) * nb, T // 8 - 1), 0)),
                  const((2, wz)), const((1, W)), const((1, W)), const((1, W)), const((2, W)), const((2, W)),
                  const((2, 2 * LORA, W)), const((2, 2 * LORA, W))],
        out_specs=[pl.BlockSpec((1, tm, N_PLANES * W), lambda b, m: (b, m, 0)),
                   pl.BlockSpec((1, tm, W), lambda b, m: (b, m, 0))],
        out_shape=[jax.ShapeDtypeStruct((B, T, N_PLANES * W), F32), jax.ShapeDtypeStruct((B, T, W), F32)],
        compiler_params=_cparams(("parallel", "parallel")),
        name="rwkv_prep",
    )(z, z, z, mu_pad, k_k.reshape(1, W), k_a.reshape(1, W), r_k.reshape(1, W), w0, a0, wup2, aup2)


def _rwkv_scan_kernel(pf_ref, pb_ref, s0_ref, yf_ref, yb_ref, s_ref):
    C = RWKV_CHUNK
    W = RWKV_WIDTH
    C2 = 2 * C

    @pl.when(pl.program_id(1) == 0)
    def _():
        s_ref[...] = s0_ref[...]

    lo = lax.broadcasted_iota(jnp.int32, (1, 128), 1) < HEAD_DIM
    row = lax.broadcasted_iota(jnp.int32, (C2, C2), 0)
    col = lax.broadcasted_iota(jnp.int32, (C2, C2), 1)
    same = (row // C) == (col // C)
    eye = (row == col).astype(F32)
    tri_r = lax.broadcasted_iota(jnp.int32, (C, C), 0)
    tri_c = lax.broadcasted_iota(jnp.int32, (C, C), 1)
    levels = [((row >> k) == (col >> k)) & ((row >> (k - 1)) != (col >> (k - 1))) for k in range(1, 7)]
    p_refs, y_refs = (pf_ref, pb_ref), (yf_ref, yb_ref)
    nb = pf_ref.shape[0]
    groups = [(n, d, p) for n in range(nb) for d in range(2) for p in range(W // 128)]
    ng = len(groups)
    bf = lambda t: t.astype(BF16)

    def stack(x):
        return jnp.concatenate([jnp.where(lo, x, 0.0), jnp.where(lo, 0.0, x)], axis=0)

    c_dir = {}
    for d in range(2):
        cum = ((tri_c <= tri_r) if d == 0 else (tri_c >= tri_r)).astype(F32)
        for n in range(nb):
            logw_all = p_refs[d][n, :, (PL_LOGW + 3 * d) * W:(PL_LOGW + 3 * d + 1) * W]
            c_dir[n, d] = _dot_exact_rhs(cum, logw_all)

    ar, bk, ends, vs, decay_end = [], [], [], [], []
    for n, d, p in groups:
        get = lambda plane: p_refs[d][n, :, plane * W + p * 128:plane * W + (p + 1) * 128]
        r, v, kk = get(PL_R), get(PL_V), get(PL_KK)
        logw, kdir, bb = get(PL_LOGW + 3 * d), get(PL_KDIR + 3 * d), get(PL_B + 3 * d)
        c = c_dir[n, d][:, p * 128:(p + 1) * 128]
        last = C - 1 if d == 0 else 0
        c_last = c[last:last + 1, :]
        e_neg = jnp.exp(-c)
        e_end = jnp.exp(c_last - c)
        ar.append(bf(jnp.concatenate([stack(-kk * jnp.exp(c - logw)), stack(r * jnp.exp(c))], axis=0)))
        bk.append(bf(jnp.concatenate([stack(bb * e_neg), stack(kdir * e_neg)], axis=0)))
        ends.append(bf(jnp.concatenate([stack(bb * e_end), stack(kdir * e_end)], axis=0)))
        vs.append(bf(stack(v)))
        decay_end.append(jnp.exp(c_last))

    amat = [_dot_nt(ar[g], bk[g]) for g in range(ng)]
    a_ab, a_kr, a_rb = [], [], []
    for g, (n, d, p) in enumerate(groups):
        before = (col < row) if d == 0 else (col > row)
        strict = same & before
        incl = same & (before | (row == col))
        m = amat[g]
        a_ab.append(jnp.where(strict, m[:C2, :C2], 0.0))
        a_kr.append(bf(jnp.concatenate([jnp.where(strict, m[:C2, C2:], 0.0),
                                        jnp.where(incl, m[C2:, C2:], 0.0)], axis=0)))
        a_rb.append(bf(jnp.where(incl, m[C2:, :C2], 0.0)))

    t = [eye + jnp.where(levels[0], a, 0.0) for a in a_ab]
    for lvl in levels[1:]:
        ta = [jnp.dot(bf(t[g]), bf(jnp.where(lvl, a_ab[g], 0.0)), preferred_element_type=F32) for g in range(ng)]
        t = [t[g] + jnp.dot(bf(ta[g]), bf(t[g]), preferred_element_type=F32) for g in range(ng)]

    s_old = [s_ref[n, d, p] for n, d, p in groups]
    from_s = [_dot_nt(ar[g], bf(s_old[g])) for g in range(ng)]
    from_v = [jnp.dot(a_kr[g], vs[g], preferred_element_type=F32) for g in range(ng)]
    u = [jnp.dot(bf(t[g]), bf(from_s[g][:C2] + from_v[g][:C2]), preferred_element_type=F32) for g in range(ng)]
    for g, (n, d, p) in enumerate(groups):
        y_s = from_s[g][C2:] + from_v[g][C2:] + jnp.dot(a_rb[g], bf(u[g]), preferred_element_type=F32)
        y_refs[d][n, :, p * 128:(p + 1) * 128] = y_s[:C] + y_s[C:]
    for g, (n, d, p) in enumerate(groups):
        uv = jnp.concatenate([bf(u[g]), vs[g]], axis=0)
        s_ref[n, d, p] = s_old[g] * decay_end[g] + lax.dot_general(
            uv, ends[g], (((0,), (0,)), ((), ())), preferred_element_type=F32)


def _rwkv_scan(planes, s0):
    B, T, _ = planes.shape
    nc = T // RWKV_CHUNK
    W = RWKV_WIDTH
    nb = SCAN_BATCH_PER_STEP if B % SCAN_BATCH_PER_STEP == 0 else 1
    st = pl.BlockSpec((nb, 2, W // 128, 128, 128), lambda b, i: (b, 0, 0, 0, 0))
    y = jax.ShapeDtypeStruct((B, T, W), F32)
    return pl.pallas_call(
        _rwkv_scan_kernel,
        grid=(B // nb, nc),
        in_specs=[pl.BlockSpec((nb, RWKV_CHUNK, N_PLANES * W), lambda b, i: (b, i, 0)),
                  pl.BlockSpec((nb, RWKV_CHUNK, N_PLANES * W), lambda b, i: (b, nc - 1 - i, 0)),
                  st],
        out_specs=[pl.BlockSpec((nb, RWKV_CHUNK, W), lambda b, i: (b, i, 0)),
                   pl.BlockSpec((nb, RWKV_CHUNK, W), lambda b, i: (b, nc - 1 - i, 0)),
                   st],
        out_shape=[y, y, jax.ShapeDtypeStruct(s0.shape, F32)],
        compiler_params=_cparams(("parallel", "arbitrary")),
        name="rwkv_scan",
    )(planes, planes, s0)


def _layer_weights(l, w_in, shift_mu, w_up, a_up, w_branch, w_out):
    rw, rg, na, ng, df, dg, mg = jnp.split(w_in[l], [int(i) for i in np.cumsum(IN_SIZES)[:-1]], axis=-1)
    pad = jnp.zeros((w_in.shape[1], RWKV_PAD_WIDTH - RWKV_SHIFT_WIDTH), w_in.dtype)
    w_perm = jnp.concatenate([mg, rg, ng, dg, rw, pad, na, df], axis=-1).astype(BF16)
    mu_pad = jnp.pad(shift_mu[l], ((0, 0), (0, RWKV_PAD_WIDTH - RWKV_SHIFT_WIDTH)))
    zl = jnp.zeros((LORA, RWKV_WIDTH), F32)
    wup2 = jnp.stack([jnp.concatenate([w_up[l, 0], zl]), jnp.concatenate([zl, w_up[l, 1]])]).astype(BF16)
    aup2 = jnp.stack([jnp.concatenate([a_up[l, 0], zl]), jnp.concatenate([zl, a_up[l, 1]])]).astype(BF16)
    return w_perm, mu_pad, wup2, aup2, w_branch[l].astype(BF16), w_out[l].astype(BF16)


def kernel(x, c, ctx, c_ctx, w_mod, b_mod, g_pre, g_post, w_in, shift_mu, k_k, k_a, r_k, w0, w_up, a0, a_up,
           ln_x_g, ln_x_b, rpb, lam_q, lam_k, diff_subln, w_branch, w_out):
    B, S, D = x.shape
    C = ctx.shape[1]
    depth = w_in.shape[0]
    tables = _rope_tables(S)
    rows_pad = -(-(B + 1) // 16) * 16
    cvec = jnp.zeros((rows_pad, D), F32).at[:B].set(c).at[B].set(c_ctx)
    mod = _modulation(cvec, w_mod, b_mod)
    hc = ctx
    tm_x = min(INPROJ_ROWS, S)
    for l in range(depth):
        last = l == depth - 1
        lambda_init = 0.8 - 0.6 * math.exp(-0.3 * l)
        w_perm, mu_pad, wup2, aup2, wb, wo = _layer_weights(l, w_in, shift_mu, w_up, a_up, w_branch, w_out)
        mod_x = mod[l, :B].reshape(B, 3, D)
        mod_c = mod[l, B:B + 1].reshape(1, 3, D)
        zg_x, rw_x, na_x, df_x = _inproj(x, mod_x, g_pre[l], w_perm, tables, tm_x)
        ctx_out = _inproj(hc.reshape(1, B * C, D), mod_c, g_pre[l], w_perm, None, B * C)
        zg_c, rw_c, na_c, df_c = (t.reshape(B, C, t.shape[-1]) for t in ctx_out)

        r_k_flat = r_k[l].reshape(RWKV_WIDTH)
        pl_c, bv_c = _rwkv_prep(rw_c, mu_pad, k_k[l], k_a[l], r_k_flat, w0[l], a0[l], wup2, aup2, min(CTX_ROWS, C))
        pl_x, bv_x = _rwkv_prep(rw_x, mu_pad, k_k[l], k_a[l], r_k_flat, w0[l], a0[l], wup2, aup2, PREP_ROWS)
        s_zero = jnp.zeros((B, 2, RWKV_WIDTH // 128, 128, 128), F32)
        yf_c, yb_c, s_ctx = _rwkv_scan(pl_c, s_zero)
        yf_x, yb_x, _ = _rwkv_scan(pl_x, s_ctx)

        o_na_x = _na_attn(na_x, na_c, _na_bias_tables(rpb[l]) * LOG2E, min(NA_ROWS_PER_STEP, S // GRID_W))
        o_df_x = _diff_attn(df_x, df_c, True, lam_q[l], lam_k[l], diff_subln[l], lambda_init,
                            min(DIFF_Q_ROWS, S), min(DIFF_KEY_CHUNK, S // 2))

        x = _merge(x, mod_x, g_post[l], yf_x, yb_x, bv_x, ln_x_g[l], ln_x_b[l], o_na_x, o_df_x, zg_x, wb, wo,
                   MERGE_ROWS)
        if not last:
            o_na_c = _ctx_attn(na_c)
            o_df_c = _diff_attn(df_c, df_c, False, lam_q[l], lam_k[l], diff_subln[l], lambda_init, C, C)
            hc = _merge(hc, mod_c, g_post[l], yf_c, yb_c, bv_c, ln_x_g[l], ln_x_b[l], o_na_c, o_df_c, zg_c, wb, wo,
                        min(CTX_ROWS, C))
    return x
```

```python
import functools
import math

import jax
import jax.numpy as jnp
import numpy as np
from jax import lax
from jax.experimental import pallas as pl
from jax.experimental.pallas import tpu as pltpu

F32 = jnp.float32
BF16 = jnp.bfloat16

GRID_W = 64
HEAD_DIM = 64
RWKV_WIDTH = 512
LORA = 64
GN_EPS = 64e-5
NA_HEADS = 8
NA_WIN_R = 8
NA_WIN_C = 16
DIFF_HEADS = 4
DIFF_QK_DIM = 64
DIFF_V_DIM = 128
ROPE_THETA = 10000.0
SUBLN_EPS = 1e-5
RMS_EPS = 1e-6
NEG_INF = -1e30
LOG2E = math.log2(math.e)
BRANCH_WIDTH = 512
N_BRANCH = 3

RWKV_SHIFT_WIDTH = 3 * RWKV_WIDTH + 4 * LORA
RWKV_PAD_WIDTH = 2048
TILE_W = 512
TILE_RW, TILE_NA, TILE_DF = 9, 13, 16
Z_WIDTH = 19 * TILE_W
IN_SIZES = (RWKV_SHIFT_WIDTH, 512, 1536, 512, 1536, 512, 3072)

VMEM_LIMIT = 56 * 1024 * 1024
INPROJ_ROWS = 2048
PREP_ROWS = 512
MERGE_ROWS = 512
CTX_ROWS = 256
DIFF_Q_ROWS = 1024
DIFF_KEY_CHUNK = 1024
NA_ROWS_PER_STEP = 32
SCAN_BATCH_PER_STEP = 4


def _cparams(sem):
    return pltpu.CompilerParams(dimension_semantics=sem, vmem_limit_bytes=VMEM_LIMIT)


def _sigmoid(x):
    return 1.0 / (1.0 + jnp.exp(-x))


def _silu(x):
    return x * _sigmoid(x)


def _bdot(a, b):
    return jnp.dot(a.astype(BF16), b.astype(BF16), preferred_element_type=F32)


def _mod_kernel(c_ref, w_ref, b_ref, o_ref):
    o_ref[0] = _bdot(_silu(c_ref[...]), w_ref[0]) + b_ref[0]


def _modulation(cvec, w_mod, b_mod):
    L, D, N = w_mod.shape
    R = cvec.shape[0]
    tn = 1024
    return pl.pallas_call(
        _mod_kernel,
        grid=(L, N // tn),
        in_specs=[pl.BlockSpec((R, D), lambda l, n: (0, 0)),
                  pl.BlockSpec((1, D, tn), lambda l, n: (l, 0, n)),
                  pl.BlockSpec((1, 1, tn), lambda l, n: (l, 0, n))],
        out_specs=pl.BlockSpec((1, R, tn), lambda l, n: (l, 0, n)),
        out_shape=jax.ShapeDtypeStruct((L, R, N), F32),
        compiler_params=_cparams(("parallel", "parallel")),
        name="modulation",
    )(cvec, w_mod, b_mod.reshape(L, 1, N))


def _inproj_kernel(*refs, rope, q_scale):
    if rope:
        x_ref, mod_ref, g_ref, w_ref, cos_ref, sa_ref, sb_ref, zg_ref, rw_ref, na_ref, df_ref, h_ref = refs
    else:
        x_ref, mod_ref, g_ref, w_ref, zg_ref, rw_ref, na_ref, df_ref, h_ref = refs
    n = pl.program_id(2)

    @pl.when(n == 0)
    def _():
        x = x_ref[0]
        y = x * lax.rsqrt(jnp.mean(x * x, axis=-1, keepdims=True) + RMS_EPS) * g_ref[...]
        h_ref[...] = (y * (1.0 + mod_ref[0, 1:2, :]) + mod_ref[0, 0:1, :]).astype(BF16)

    mm = lambda: jnp.dot(h_ref[...], w_ref[...], preferred_element_type=F32)

    def rot(t):
        if not rope:
            return t
        w = t.shape[-1]
        tile = lambda r: jnp.concatenate([r[...]] * (w // r.shape[-1]), axis=1)
        return t * tile(cos_ref) + pltpu.roll(t, w - 16, 1) * tile(sa_ref) + pltpu.roll(t, 16, 1) * tile(sb_ref)

    @pl.when(n < TILE_RW)
    def _():
        zg_ref[0] = mm().astype(BF16)

    @pl.when((n >= TILE_RW) & (n < TILE_NA))
    def _():
        rw_ref[0] = mm()

    @pl.when(n == TILE_NA)
    def _():
        na_ref[0] = (mm() * q_scale).astype(BF16)

    @pl.when((n > TILE_NA) & (n < TILE_DF))
    def _():
        na_ref[0] = mm().astype(BF16)

    @pl.when(n == TILE_DF)
    def _():
        df_ref[0] = (rot(mm()) * q_scale).astype(BF16)

    @pl.when(n == TILE_DF + 1)
    def _():
        df_ref[0] = rot(mm()).astype(BF16)

    @pl.when(n == TILE_DF + 2)
    def _():
        df_ref[0] = mm().astype(BF16)


def _inproj(x, mod, g_pre, w_perm, tables, tm):
    B, T, D = x.shape
    tn = TILE_W
    per_batch = mod.shape[0] > 1
    in_specs = [pl.BlockSpec((1, tm, D), lambda b, m, n: (b, m, 0)),
                pl.BlockSpec((1, 3, D), (lambda b, m, n: (b, 0, 0)) if per_batch else (lambda b, m, n: (0, 0, 0))),
                pl.BlockSpec((1, D), lambda b, m, n: (0, 0)),
                pl.BlockSpec((D, tn), lambda b, m, n: (0, n))]
    args = [x, mod, g_pre.reshape(1, D), w_perm]
    if tables is not None:
        in_specs += [pl.BlockSpec((tm, 128), lambda b, m, n: (m, 0))] * 3
        args += list(tables)
    seg = lambda first, count: pl.BlockSpec((1, tm, tn), lambda b, m, n: (b, m, jnp.clip(n - first, 0, count - 1)))
    out = lambda count, dtype: jax.ShapeDtypeStruct((B, T, count * tn), dtype)
    return pl.pallas_call(
        functools.partial(_inproj_kernel, rope=tables is not None, q_scale=HEAD_DIM ** -0.5 * LOG2E),
        grid=(B, T // tm, Z_WIDTH // tn),
        in_specs=in_specs,
        out_specs=[seg(0, TILE_RW), seg(TILE_RW, TILE_NA - TILE_RW), seg(TILE_NA, TILE_DF - TILE_NA),
                   seg(TILE_DF, Z_WIDTH // tn - TILE_DF)],
        out_shape=[out(TILE_RW, BF16), out(TILE_NA - TILE_RW, F32), out(TILE_DF - TILE_NA, BF16),
                   out(Z_WIDTH // tn - TILE_DF, BF16)],
        scratch_shapes=[pltpu.VMEM((tm, D), BF16)],
        compiler_params=_cparams(("parallel", "parallel", "arbitrary")),
        name="inproj",
    )(*args)


def _merge_kernel(x_ref, mod_ref, gpost_ref, yf_ref, yb_ref, bv_ref, lng_ref, lnb_ref, ona_ref, odf_ref,
                  rg_ref, ng_ref, dg_ref, mg_ref, wb_ref, wo_ref, o_ref):
    D = x_ref.shape[-1]
    ones = _head_ones(RWKV_WIDTH)
    y = yf_ref[0] + yb_ref[0]
    mu = _dot_hilo(y, ones) * (1.0 / HEAD_DIM)
    yc = y - mu
    var = _dot_hilo(yc * yc, ones) * (1.0 / HEAD_DIM)
    o_rw = yc * lax.rsqrt(var + GN_EPS) * lng_ref[...] + lnb_ref[...] + bv_ref[0]
    acc = None
    for n, (ob, gate_ref) in enumerate(((o_rw, rg_ref), (ona_ref[0], ng_ref), (odf_ref[0], dg_ref))):
        yb = _bdot(ob * _silu(gate_ref[0].astype(F32)), wb_ref[n])
        term = _sigmoid(mg_ref[0, :, n * D:(n + 1) * D].astype(F32)) * yb
        acc = term if acc is None else acc + term
    y = _bdot(acc, wo_ref[...])
    yn = y * lax.rsqrt(jnp.mean(y * y, axis=-1, keepdims=True) + RMS_EPS) * gpost_ref[...]
    o_ref[0] = x_ref[0] + mod_ref[0, 2:3, :] * yn


def _merge(x, mod, g_post, y_f, y_b, bonus, ln_g, ln_b, o_na, o_df, zg, w_branch, w_out, tm):
    B, T, D = x.shape
    per_batch = mod.shape[0] > 1
    bw = BRANCH_WIDTH
    row = lambda b, m: (b, m, 0)
    col = lambda c: (lambda b, m: (b, m, c))
    return pl.pallas_call(
        _merge_kernel,
        grid=(B, T // tm),
        in_specs=[pl.BlockSpec((1, tm, D), row),
                  pl.BlockSpec((1, 3, D), (lambda b, m: (b, 0, 0)) if per_batch else (lambda b, m: (0, 0, 0))),
                  pl.BlockSpec((1, D), lambda b, m: (0, 0)),
                  pl.BlockSpec((1, tm, bw), row), pl.BlockSpec((1, tm, bw), row), pl.BlockSpec((1, tm, bw), row),
                  pl.BlockSpec((1, bw), lambda b, m: (0, 0)), pl.BlockSpec((1, bw), lambda b, m: (0, 0)),
                  pl.BlockSpec((1, tm, bw), row), pl.BlockSpec((1, tm, bw), row),
                  pl.BlockSpec((1, tm, bw), col(N_BRANCH * D // bw)),
                  pl.BlockSpec((1, tm, bw), col(N_BRANCH * D // bw + 1)),
                  pl.BlockSpec((1, tm, bw), col(N_BRANCH * D // bw + 2)),
                  pl.BlockSpec((1, tm, N_BRANCH * D), col(0)),
                  pl.BlockSpec((N_BRANCH, bw, D), lambda b, m: (0, 0, 0)),
                  pl.BlockSpec((D, D), lambda b, m: (0, 0))],
        out_specs=pl.BlockSpec((1, tm, D), row),
        out_shape=jax.ShapeDtypeStruct((B, T, D), F32),
        compiler_params=_cparams(("parallel", "parallel")),
        name="merge",
    )(x, mod, g_post.reshape(1, D), y_f, y_b, bonus, ln_g.reshape(1, bw), ln_b.reshape(1, bw), o_na, o_df,
      zg, zg, zg, zg, w_branch, w_out)


def _rope_tables(n_tokens):
    t = np.arange(n_tokens)
    axis_dim = DIFF_QK_DIM // 2
    inv = ROPE_THETA ** (-np.arange(0, axis_dim, 2, dtype=np.float32) / axis_dim)
    ar = (t // GRID_W).astype(np.float32)[:, None] * inv
    ac = (t % GRID_W).astype(np.float32)[:, None] * inv
    ang = jnp.asarray(np.concatenate([ar, ar, ac, ac], axis=-1).astype(np.float32))
    cos, sin = jnp.cos(ang), jnp.sin(ang)
    first = (np.arange(DIFF_QK_DIM) % 32) < 16
    sin_a = jnp.where(first, -sin, 0.0)
    sin_b = jnp.where(first, 0.0, sin)
    tile = lambda a: jnp.tile(a, (1, 2))
    return tile(cos), tile(sin_a), tile(sin_b)


def _dot_nt(a, b):
    return lax.dot_general(a, b, (((1,), (1,)), ((), ())), preferred_element_type=F32)


SOFTMAX_ROWS = 128


def _diff_attn_kernel(q_ref, kc_ref, vc_ref, k_ref, v_ref, lq_ref, lk_ref, g_ref, o_ref,
                      sc, pc, s0, s1, p0, p1, al0, al1, m_scr, acc_scr, *, n_main, tk, lambda_init):
    tq = q_ref.shape[1]
    t_first = kc_ref.shape[1]
    q = q_ref[0]
    lo = lax.broadcasted_iota(jnp.int32, (1, 128), 1) < DIFF_QK_DIM
    zero = jnp.zeros_like(q)
    qq = jnp.concatenate([jnp.where(lo, q, zero), jnp.where(lo, zero, q)], axis=0)

    n_sub = 2 * tq // SOFTMAX_ROWS
    sub = lambda i: slice(i * SOFTMAX_ROWS, (i + 1) * SOFTMAX_ROWS)

    def scores(k_chunk, s_scr, col=0):
        s_scr[:, col:col + k_chunk.shape[0]] = _dot_nt(qq, k_chunk)

    def softmax(size, first, s_scr, p_scr, al_scr):
        w = 128 if size % 128 == 0 else 64
        cols = [slice(c * w, (c + 1) * w) for c in range(size // w)]
        for i in range(n_sub):
            mx = s_scr[sub(i), cols[0]]
            for c in cols[1:]:
                mx = jnp.maximum(mx, s_scr[sub(i), c])
            mn = jnp.broadcast_to(jnp.max(mx, axis=-1, keepdims=True), (SOFTMAX_ROWS, 128))
            if not first:
                mo = m_scr[sub(i), :]
                mn = jnp.maximum(mo, mn)
                al_scr[sub(i), :] = jnp.exp2(mo - mn)
            m_scr[sub(i), :] = mn
        for i in range(n_sub):
            mn = m_scr[sub(i), :w]
            for c in cols:
                p_scr[sub(i), c] = jnp.exp2(s_scr[sub(i), c] - mn).astype(BF16)

    def accumulate(v, first, p_scr, al_scr):
        size = v.shape[0]
        v_ext = jnp.concatenate([v, jnp.ones((size, 128), BF16)], axis=1)
        pv = jnp.dot(p_scr[:, :size], v_ext, preferred_element_type=F32)
        if first:
            acc_scr[...] = pv
        else:
            al = al_scr[...]
            acc_scr[...] = acc_scr[...] * jnp.concatenate([al, al], axis=1) + pv

    scores(kc_ref[0], sc)
    if n_main == 0:
        softmax(t_first, True, sc, pc, None)
        accumulate(vc_ref[0], True, pc, None)
    else:
        def chunk_of(ref, j):
            start = j * tk if isinstance(j, int) else pl.multiple_of(j * tk, tk)
            return ref[0, pl.ds(start, tk), :]

        k_at = functools.partial(chunk_of, k_ref)
        v_at = functools.partial(chunk_of, v_ref)
        scores(k_at(0), sc, t_first)
        scores(k_at(1), s1)
        softmax(t_first + tk, True, sc, pc, None)
        scores(k_at(min(2, n_main - 1)), s0)
        softmax(tk, False, s1, p1, al1)
        accumulate(jnp.concatenate([vc_ref[0], v_at(0)], axis=0), True, pc, None)

        def pair(j):
            scores(k_at(j + 1), s1)
            softmax(tk, False, s0, p0, al0)
            accumulate(v_at(j - 1), False, p1, al1)
            scores(k_at(jnp.minimum(j + 2, n_main - 1)), s0)
            softmax(tk, False, s1, p1, al1)
            accumulate(v_at(j), False, p0, al0)

        def body(t, _):
            pair(2 * t)
            return 0

        lax.fori_loop(1, n_main // 2, body, 0)
        accumulate(v_at(n_main - 1), False, p1, al1)
    lqk = lq_ref[...] * lk_ref[...]
    e = jnp.exp(jnp.sum(lqk, axis=-1, keepdims=True))
    lam = e[0:1] - e[1:2] + lambda_init
    o_maps = acc_scr[:, :DIFF_V_DIM] / acc_scr[:, DIFF_V_DIM:]
    o = o_maps[:tq] - lam * o_maps[tq:]
    o = o * lax.rsqrt(jnp.mean(o * o, axis=-1, keepdims=True) + SUBLN_EPS) * g_ref[...]
    o_ref[0] = o * (1.0 - lambda_init)


def _diff_attn(qkv, qkv_c, with_latent_keys, lam_q, lam_k, subln_g, lambda_init, tq, tk):
    B, T, _ = qkv.shape
    C = qkv_c.shape[1]
    n_main = T // tk if with_latent_keys else 0
    assert n_main % 2 == 0 and (2 * tq) % SOFTMAX_ROWS == 0
    wmax = tk if n_main else 128
    wfirst = C + tk if n_main else C
    H = DIFF_HEADS
    kv = lambda t, j: pl.BlockSpec((1, t, 128), lambda b, h, m: (b, 0, j * H + h))
    small = lambda r, c: pl.BlockSpec((r, c), lambda b, h, m: (0, 0))
    stat = pltpu.VMEM((2 * tq, 128), F32)
    s_buf = pltpu.VMEM((2 * tq, wmax), F32)
    p_buf = pltpu.VMEM((2 * tq, wmax), BF16)
    return pl.pallas_call(
        functools.partial(_diff_attn_kernel, n_main=n_main, tk=tk, lambda_init=lambda_init),
        grid=(B, H, T // tq),
        in_specs=[pl.BlockSpec((1, tq, 128), lambda b, h, m: (b, m, h)), kv(C, 1), kv(C, 2), kv(T, 1), kv(T, 2),
                  small(2, DIFF_QK_DIM), small(2, DIFF_QK_DIM), small(1, DIFF_V_DIM)],
        out_specs=pl.BlockSpec((1, tq, 128), lambda b, h, m: (b, m, h)),
        out_shape=jax.ShapeDtypeStruct((B, T, H * DIFF_V_DIM), F32),
        scratch_shapes=[pltpu.VMEM((2 * tq, wfirst), F32), pltpu.VMEM((2 * tq, wfirst), BF16), s_buf, s_buf, p_buf, p_buf,
                        stat, stat, stat, pltpu.VMEM((2 * tq, 2 * DIFF_V_DIM), F32)],
        compiler_params=_cparams(("parallel", "parallel", "parallel")),
        name="diff_attn",
    )(qkv, qkv_c, qkv_c, qkv, qkv, lam_q, lam_k, subln_g.reshape(1, DIFF_V_DIM))


def _na_bias_tables(rpb):
    c_idx = np.arange(GRID_W)
    c_start = np.clip(c_idx - NA_WIN_C // 2, 0, GRID_W - NA_WIN_C)
    col_ok = (c_idx[None, :] >= c_start[:, None]) & (c_idx[None, :] < c_start[:, None] + NA_WIN_C)
    H = rpb.shape[0]
    pad = GRID_W - NA_WIN_C
    ext = jnp.pad(rpb, ((0, 0), (0, 0), (pad, pad)))
    e = jnp.stack([ext[..., GRID_W - 1 - q:2 * GRID_W - 1 - q] for q in range(GRID_W)], axis=2)
    e = jnp.where(col_ok, e, NEG_INF)
    tabs = [e[:, NA_WIN_R - 1 - off:2 * NA_WIN_R - 1 - off].transpose(0, 2, 1, 3).reshape(H, GRID_W, NA_WIN_R * GRID_W)
            for off in range(NA_WIN_R)]
    return jnp.stack(tabs, axis=1)


def _na_kernel(q_ref, k_ref, v_ref, kc_ref, vc_ref, bias_ref, o_ref, *, rq, rows):
    i = pl.program_id(2)
    lo = lax.broadcasted_iota(jnp.int32, (1, 128), 1) < HEAD_DIM
    win = NA_WIN_R * GRID_W
    G2 = 2 * GRID_W
    q = q_ref[0]
    zero = jnp.zeros_like(q)
    q_lo, q_hi = jnp.where(lo, q, zero), jnp.where(lo, zero, q)
    qq = jnp.concatenate([x[rr * GRID_W:(rr + 1) * GRID_W] for rr in range(rq) for x in (q_lo, q_hi)], axis=0)
    kc = kc_ref[0]
    ones = lambda n: jnp.ones((n, 128), BF16)
    s_c = _dot_nt(qq, kc)

    starts, s_nb = [], []
    for rr in range(rq):
        r = i * rq + rr
        r_start = jnp.clip(r - NA_WIN_R // 2, 0, rows - NA_WIN_R)
        off = r - r_start
        start = pl.multiple_of(r_start * GRID_W, GRID_W)
        starts.append(start)
        bias = jnp.concatenate([bias_ref[0, off], bias_ref[1, off]], axis=0)
        s_nb.append(_dot_nt(qq[rr * G2:(rr + 1) * G2], k_ref[0, pl.ds(start, win), :]) + bias)

    def lane_blocks(t):
        w = min(128, t.shape[-1])
        return [t[:, c * w:(c + 1) * w] for c in range(t.shape[-1] // w)]

    p_nb, p_c = [], []
    for rr in range(rq):
        sc = s_c[rr * G2:(rr + 1) * G2]
        blocks = lane_blocks(s_nb[rr])
        mx = blocks[0]
        for b in blocks[1:]:
            mx = jnp.maximum(mx, b)
        m = jnp.maximum(jnp.max(mx, axis=-1, keepdims=True), jnp.max(sc, axis=-1, keepdims=True))
        p_nb.append(jnp.exp2(s_nb[rr] - m).astype(BF16))
        p_c.append(jnp.exp2(sc - m).astype(BF16))

    vc_ext = jnp.concatenate([vc_ref[0], ones(kc.shape[0])], axis=1)
    o_c = jnp.dot(jnp.concatenate(p_c, axis=0), vc_ext, preferred_element_type=F32)
    for rr in range(rq):
        vw_ext = jnp.concatenate([v_ref[0, pl.ds(starts[rr], win), :], ones(win)], axis=1)
        o = jnp.dot(p_nb[rr], vw_ext, preferred_element_type=F32) + o_c[rr * G2:(rr + 1) * G2]
        o = o[:, :128] / o[:, 128:]
        o_ref[0, rr * GRID_W:(rr + 1) * GRID_W, :] = jnp.where(lo, o[:GRID_W], o[GRID_W:])


def _na_attn(qkv, qkv_c, bias, rq):
    B, S, _ = qkv.shape
    C = qkv_c.shape[1]
    rows = S // GRID_W
    npair = NA_HEADS // 2
    full = lambda t, j: pl.BlockSpec((1, t, 128), lambda b, p, m: (b, 0, j * npair + p))
    return pl.pallas_call(
        functools.partial(_na_kernel, rq=rq, rows=rows),
        grid=(B, npair, rows // rq),
        in_specs=[pl.BlockSpec((1, rq * GRID_W, 128), lambda b, p, m: (b, m, p)),
                  full(S, 1), full(S, 2), full(C, 1), full(C, 2),
                  pl.BlockSpec((2, NA_WIN_R, GRID_W, NA_WIN_R * GRID_W), lambda b, p, m: (p, 0, 0, 0))],
        out_specs=pl.BlockSpec((1, rq * GRID_W, 128), lambda b, p, m: (b, m, p)),
        out_shape=jax.ShapeDtypeStruct((B, S, NA_HEADS * HEAD_DIM), F32),
        compiler_params=_cparams(("parallel", "parallel", "parallel")),
        name="na_attn",
    )(qkv, qkv, qkv, qkv_c, qkv_c, bias)


def _ctx_attn_kernel(q_ref, k_ref, v_ref, o_ref):
    lo = lax.broadcasted_iota(jnp.int32, (1, 128), 1) < HEAD_DIM
    q, k, v = q_ref[0], k_ref[0], v_ref[0]
    zero = jnp.zeros_like(q)
    outs = []
    for hl in range(2):
        qh = jnp.where(lo, q, zero) if hl == 0 else jnp.where(lo, zero, q)
        s = _dot_nt(qh, k)
        p = jnp.exp2(s - jnp.max(s, axis=-1, keepdims=True))
        o = jnp.dot(p.astype(BF16), v, preferred_element_type=F32)
        outs.append(o / jnp.sum(p, axis=-1, keepdims=True))
    o_ref[0] = jnp.where(lo, outs[0], outs[1])


def _ctx_attn(qkv):
    B, C, _ = qkv.shape
    npair = NA_HEADS // 2
    blk = lambda j: pl.BlockSpec((1, C, 128), lambda b, p: (b, 0, j * npair + p))
    return pl.pallas_call(
        _ctx_attn_kernel,
        grid=(B, npair),
        in_specs=[blk(0), blk(1), blk(2)],
        out_specs=blk(0),
        out_shape=jax.ShapeDtypeStruct((B, C, NA_HEADS * HEAD_DIM), F32),
        compiler_params=_cparams(("parallel", "parallel")),
        name="ctx_attn",
    )(qkv, qkv, qkv)


PL_R, PL_V, PL_KK = 0, 1, 2
PL_LOGW, PL_KDIR, PL_B = 3, 4, 5
N_PLANES = 9
RWKV_CHUNK = 64


def _split3(x):
    hi = x.astype(BF16)
    r1 = x - hi.astype(F32)
    mid = r1.astype(BF16)
    lo = (r1 - mid.astype(F32)).astype(BF16)
    return hi, mid, lo


def _dot_exact_rhs(m, x):
    hi, mid, lo = _split3(x)
    mb = m.astype(BF16)
    d = lambda t: jnp.dot(mb, t, preferred_element_type=F32)
    return d(hi) + d(mid) + d(lo)


def _dot_hilo(x, m):
    hi = x.astype(BF16)
    lo = (x - hi.astype(F32)).astype(BF16)
    mb = m.astype(BF16)
    d = lambda t: jnp.dot(t, mb, preferred_element_type=F32)
    return d(hi) + d(lo)


def _head_ones(n):
    r = lax.broadcasted_iota(jnp.int32, (n, n), 0) // HEAD_DIM
    c = lax.broadcasted_iota(jnp.int32, (n, n), 1) // HEAD_DIM
    return (r == c).astype(F32)


def _rwkv_prep_kernel(zc_ref, zp_ref, zn_ref, mu_ref, kk_ref, ka_ref, rk_ref, w0_ref, a0_ref, wup_ref, aup_ref,
                      o_ref, bv_ref):
    tm = zc_ref.shape[1]
    m = pl.program_id(1)
    u = zc_ref[0]
    prev_row = jnp.where(m > 0, zp_ref[0, 7:8, :], 0.0)
    next_row = jnp.where(m < pl.num_programs(1) - 1, zn_ref[0, 0:1, :], 0.0)
    rows = lax.broadcasted_iota(jnp.int32, (tm, 1), 0)
    u_prev = jnp.where(rows == 0, prev_row, pltpu.roll(u, 1, 0))
    u_next = jnp.where(rows == tm - 1, next_row, pltpu.roll(u, tm - 1, 0))
    u = u + mu_ref[0:1, :] * (u_prev - u) + mu_ref[1:2, :] * (u_next - u)

    W = RWKV_WIDTH
    r, k, v = u[:, 0:W], u[:, W:2 * W], u[:, 2 * W:3 * W]
    lw = jnp.tanh(u[:, 3 * W:3 * W + 2 * LORA]).astype(BF16)
    la = u[:, 3 * W + 2 * LORA:3 * W + 4 * LORA].astype(BF16)
    ones = _head_ones(W)
    kk = k * kk_ref[...]
    ss = _dot_hilo(kk * kk, ones)
    kk = kk / jnp.maximum(jnp.sqrt(ss), 1e-12)
    o_ref[0, :, PL_R * W:(PL_R + 1) * W] = r
    o_ref[0, :, PL_V * W:(PL_V + 1) * W] = v
    o_ref[0, :, PL_KK * W:(PL_KK + 1) * W] = kk
    k_both = None
    for d in range(2):
        xw = w0_ref[d:d + 1, :] + jnp.dot(lw, wup_ref[d], preferred_element_type=F32)
        logw = -math.exp(-0.5) * _sigmoid(xw)
        a = _sigmoid(a0_ref[d:d + 1, :] + jnp.dot(la, aup_ref[d], preferred_element_type=F32))
        kdir = k * (1.0 + (a - 1.0) * ka_ref[...])
        base = 3 * d
        o_ref[0, :, (PL_LOGW + base) * W:(PL_LOGW + base + 1) * W] = logw
        o_ref[0, :, (PL_KDIR + base) * W:(PL_KDIR + base + 1) * W] = kdir
        o_ref[0, :, (PL_B + base) * W:(PL_B + base + 1) * W] = kk * a
        k_both = kdir if k_both is None else k_both + kdir
    bv_ref[0] = _dot_hilo(r * k_both * rk_ref[...], ones) * v


def _rwkv_prep(z, mu_pad, k_k, k_a, r_k, w0, a0, wup2, aup2, tm):
    B, T, wz = z.shape
    nb = tm // 8
    W = RWKV_WIDTH
    const = lambda shape: pl.BlockSpec(shape, lambda b, m: (0,) * len(shape))
    return pl.pallas_call(
        _rwkv_prep_kernel,
        grid=(B, T // tm),
        in_specs=[pl.BlockSpec((1, tm, wz), lambda b, m: (b, m, 0)),
                  pl.BlockSpec((1, 8, wz), lambda b, m: (b, jnp.maximum(m * nb - 1, 0), 0)),
                  pl.BlockSpec((1, 8, wz), lambda b, m: (b, jnp.minimum((m + 1) * nb, T // 8 - 1), 0)),
                  const((2, wz)), const((1, W)), const((1, W)), const((1, W)), const((2, W)), const((2, W)),
                  const((2, 2 * LORA, W)), const((2, 2 * LORA, W))],
        out_specs=[pl.BlockSpec((1, tm, N_PLANES * W), lambda b, m: (b, m, 0)),
                   pl.BlockSpec((1, tm, W), lambda b, m: (b, m, 0))],
        out_shape=[jax.ShapeDtypeStruct((B, T, N_PLANES * W), F32), jax.ShapeDtypeStruct((B, T, W), F32)],
        compiler_params=_cparams(("parallel", "parallel")),
        name="rwkv_prep",
    )(z, z, z, mu_pad, k_k.reshape(1, W), k_a.reshape(1, W), r_k.reshape(1, W), w0, a0, wup2, aup2)


def _rwkv_scan_kernel(pf_ref, pb_ref, s0_ref, yf_ref, yb_ref, s_ref):
    C = RWKV_CHUNK
    W = RWKV_WIDTH
    C2 = 2 * C

    @pl.when(pl.program_id(1) == 0)
    def _():
        s_ref[...] = s0_ref[...]

    lo = lax.broadcasted_iota(jnp.int32, (1, 128), 1) < HEAD_DIM
    row = lax.broadcasted_iota(jnp.int32, (C2, C2), 0)
    col = lax.broadcasted_iota(jnp.int32, (C2, C2), 1)
    same = (row // C) == (col // C)
    eye = (row == col).astype(F32)
    tri_r = lax.broadcasted_iota(jnp.int32, (C, C), 0)
    tri_c = lax.broadcasted_iota(jnp.int32, (C, C), 1)
    levels = [((row >> k) == (col >> k)) & ((row >> (k - 1)) != (col >> (k - 1))) for k in range(1, 7)]
    p_refs, y_refs = (pf_ref, pb_ref), (yf_ref, yb_ref)
    nb = pf_ref.shape[0]
    groups = [(n, d, p) for n in range(nb) for d in range(2) for p in range(W // 128)]
    ng = len(groups)
    bf = lambda t: t.astype(BF16)

    def stack(x):
        return jnp.concatenate([jnp.where(lo, x, 0.0), jnp.where(lo, 0.0, x)], axis=0)

    c_dir = {}
    for d in range(2):
        cum = ((tri_c <= tri_r) if d == 0 else (tri_c >= tri_r)).astype(F32)
        for n in range(nb):
            logw_all = p_refs[d][n, :, (PL_LOGW + 3 * d) * W:(PL_LOGW + 3 * d + 1) * W]
            c_dir[n, d] = _dot_exact_rhs(cum, logw_all)

    ar, bk, ends, vs, decay_end = [], [], [], [], []
    for n, d, p in groups:
        get = lambda plane: p_refs[d][n, :, plane * W + p * 128:plane * W + (p + 1) * 128]
        r, v, kk = get(PL_R), get(PL_V), get(PL_KK)
        logw, kdir, bb = get(PL_LOGW + 3 * d), get(PL_KDIR + 3 * d), get(PL_B + 3 * d)
        c = c_dir[n, d][:, p * 128:(p + 1) * 128]
        last = C - 1 if d == 0 else 0
        c_last = c[last:last + 1, :]
        e_neg = jnp.exp(-c)
        e_end = jnp.exp(c_last - c)
        ar.append(bf(jnp.concatenate([stack(-kk * jnp.exp(c - logw)), stack(r * jnp.exp(c))], axis=0)))
        bk.append(bf(jnp.concatenate([stack(bb * e_neg), stack(kdir * e_neg)], axis=0)))
        ends.append(bf(jnp.concatenate([stack(bb * e_end), stack(kdir * e_end)], axis=0)))
        vs.append(bf(stack(v)))
        decay_end.append(jnp.exp(c_last))

    amat = [_dot_nt(ar[g], bk[g]) for g in range(ng)]
    a_ab, a_kr, a_rb = [], [], []
    for g, (n, d, p) in enumerate(groups):
        before = (col < row) if d == 0 else (col > row)
        strict = same & before
        incl = same & (before | (row == col))
        m = amat[g]
        a_ab.append(jnp.where(strict, m[:C2, :C2], 0.0))
        a_kr.append(bf(jnp.concatenate([jnp.where(strict, m[:C2, C2:], 0.0),
                                        jnp.where(incl, m[C2:, C2:], 0.0)], axis=0)))
        a_rb.append(bf(jnp.where(incl, m[C2:, :C2], 0.0)))

    t = [eye + jnp.where(levels[0], a, 0.0) for a in a_ab]
    for lvl in levels[1:]:
        ta = [jnp.dot(bf(t[g]), bf(jnp.where(lvl, a_ab[g], 0.0)), preferred_element_type=F32) for g in range(ng)]
        t = [t[g] + jnp.dot(bf(ta[g]), bf(t[g]), preferred_element_type=F32) for g in range(ng)]

    s_old = [s_ref[n, d, p] for n, d, p in groups]
    from_s = [_dot_nt(ar[g], bf(s_old[g])) for g in range(ng)]
    from_v = [jnp.dot(a_kr[g], vs[g], preferred_element_type=F32) for g in range(ng)]
    u = [jnp.dot(bf(t[g]), bf(from_s[g][:C2] + from_v[g][:C2]), preferred_element_type=F32) for g in range(ng)]
    for g, (n, d, p) in enumerate(groups):
        y_s = from_s[g][C2:] + from_v[g][C2:] + jnp.dot(a_rb[g], bf(u[g]), preferred_element_type=F32)
        y_refs[d][n, :, p * 128:(p + 1) * 128] = y_s[:C] + y_s[C:]
    for g, (n, d, p) in enumerate(groups):
        uv = jnp.concatenate([bf(u[g]), vs[g]], axis=0)
        s_ref[n, d, p] = s_old[g] * decay_end[g] + lax.dot_general(
            uv, ends[g], (((0,), (0,)), ((), ())), preferred_element_type=F32)


def _rwkv_scan(planes, s0):
    B, T, _ = planes.shape
    nc = T // RWKV_CHUNK
    W = RWKV_WIDTH
    nb = SCAN_BATCH_PER_STEP if B % SCAN_BATCH_PER_STEP == 0 else 1
    st = pl.BlockSpec((nb, 2, W // 128, 128, 128), lambda b, i: (b, 0, 0, 0, 0))
    y = jax.ShapeDtypeStruct((B, T, W), F32)
    return pl.pallas_call(
        _rwkv_scan_kernel,
        grid=(B // nb, nc),
        in_specs=[pl.BlockSpec((nb, RWKV_CHUNK, N_PLANES * W), lambda b, i: (b, i, 0)),
                  pl.BlockSpec((nb, RWKV_CHUNK, N_PLANES * W), lambda b, i: (b, nc - 1 - i, 0)),
                  st],
        out_specs=[pl.BlockSpec((nb, RWKV_CHUNK, W), lambda b, i: (b, i, 0)),
                   pl.BlockSpec((nb, RWKV_CHUNK, W), lambda b, i: (b, nc - 1 - i, 0)),
                   st],
        out_shape=[y, y, jax.ShapeDtypeStruct(s0.shape, F32)],
        compiler_params=_cparams(("parallel", "arbitrary")),
        name="rwkv_scan",
    )(planes, planes, s0)


def _layer_weights(l, w_in, shift_mu, w_up, a_up, w_branch, w_out):
    rw, rg, na, ng, df, dg, mg = jnp.split(w_in[l], [int(i) for i in np.cumsum(IN_SIZES)[:-1]], axis=-1)
    pad = jnp.zeros((w_in.shape[1], RWKV_PAD_WIDTH - RWKV_SHIFT_WIDTH), w_in.dtype)
    w_perm = jnp.concatenate([mg, rg, ng, dg, rw, pad, na, df], axis=-1).astype(BF16)
    mu_pad = jnp.pad(shift_mu[l], ((0, 0), (0, RWKV_PAD_WIDTH - RWKV_SHIFT_WIDTH)))
    zl = jnp.zeros((LORA, RWKV_WIDTH), F32)
    wup2 = jnp.stack([jnp.concatenate([w_up[l, 0], zl]), jnp.concatenate([zl, w_up[l, 1]])]).astype(BF16)
    aup2 = jnp.stack([jnp.concatenate([a_up[l, 0], zl]), jnp.concatenate([zl, a_up[l, 1]])]).astype(BF16)
    return w_perm, mu_pad, wup2, aup2, w_branch[l].astype(BF16), w_out[l].astype(BF16)


def kernel(x, c, ctx, c_ctx, w_mod, b_mod, g_pre, g_post, w_in, shift_mu, k_k, k_a, r_k, w0, w_up, a0, a_up,
           ln_x_g, ln_x_b, rpb, lam_q, lam_k, diff_subln, w_branch, w_out):
    B, S, D = x.shape
    C = ctx.shape[1]
    depth = w_in.shape[0]
    tables = _rope_tables(S)
    rows_pad = -(-(B + 1) // 16) * 16
    cvec = jnp.zeros((rows_pad, D), F32).at[:B].set(c).at[B].set(c_ctx)
    mod = _modulation(cvec, w_mod, b_mod)
    hc = ctx
    tm_x = min(INPROJ_ROWS, S)
    for l in range(depth):
        last = l == depth - 1
        lambda_init = 0.8 - 0.6 * math.exp(-0.3 * l)
        w_perm, mu_pad, wup2, aup2, wb, wo = _layer_weights(l, w_in, shift_mu, w_up, a_up, w_branch, w_out)
        mod_x = mod[l, :B].reshape(B, 3, D)
        mod_c = mod[l, B:B + 1].reshape(1, 3, D)
        zg_x, rw_x, na_x, df_x = _inproj(x, mod_x, g_pre[l], w_perm, tables, tm_x)
        ctx_out = _inproj(hc.reshape(1, B * C, D), mod_c, g_pre[l], w_perm, None, B * C)
        zg_c, rw_c, na_c, df_c = (t.reshape(B, C, t.shape[-1]) for t in ctx_out)

        r_k_flat = r_k[l].reshape(RWKV_WIDTH)
        pl_c, bv_c = _rwkv_prep(rw_c, mu_pad, k_k[l], k_a[l], r_k_flat, w0[l], a0[l], wup2, aup2, min(CTX_ROWS, C))
        pl_x, bv_x = _rwkv_prep(rw_x, mu_pad, k_k[l], k_a[l], r_k_flat, w0[l], a0[l], wup2, aup2, PREP_ROWS)
        s_zero = jnp.zeros((B, 2, RWKV_WIDTH // 128, 128, 128), F32)
        yf_c, yb_c, s_ctx = _rwkv_scan(pl_c, s_zero)
        yf_x, yb_x, _ = _rwkv_scan(pl_x, s_ctx)

        o_na_x = _na_attn(na_x, na_c, _na_bias_tables(rpb[l]) * LOG2E, min(NA_ROWS_PER_STEP, S // GRID_W))
        o_df_x = _diff_attn(df_x, df_c, True, lam_q[l], lam_k[l], diff_subln[l], lambda_init,
                            min(DIFF_Q_ROWS, S), min(DIFF_KEY_CHUNK, S // 2))

        x = _merge(x, mod_x, g_post[l], yf_x, yb_x, bv_x, ln_x_g[l], ln_x_b[l], o_na_x, o_df_x, zg_x, wb, wo,
                   MERGE_ROWS)
        if not last:
            o_na_c = _ctx_attn(na_c)
            o_df_c = _diff_attn(df_c, df_c, False, lam_q[l], lam_k[l], diff_subln[l], lambda_init, C, C)
            hc = _merge(hc, mod_c, g_post[l], yf_c, yb_c, bv_c, ln_x_g[l], ln_x_b[l], o_na_c, o_df_c, zg_c, wb, wo,
                        min(CTX_ROWS, C))
    return x
```
